```python
import math
import jax, jax.numpy as jnp
from jax import lax
import numpy as np

D_MODEL = 2048
BATCH = 8
SEQ = 2048
DEPTH = 4

NUM_MIXERS = 2
N_SSD_LAYERS = (DEPTH + 1) // 2
N_POOL_LAYERS = DEPTH // 2
N_META = 16
EPS = 1e-6

SSD_EXPAND = 2
D_INNER = SSD_EXPAND * D_MODEL
SSD_HEAD_DIM = 64
SSD_HEADS = D_INNER // SSD_HEAD_DIM
D_STATE = 128
SSD_GROUPS = 8
HEADS_PER_GROUP = SSD_HEADS // SSD_GROUPS
D_CONV = 4
CHUNK = 256
CONV_DIM = D_INNER + 2 * SSD_GROUPS * D_STATE
D_IN_PROJ = D_INNER + CONV_DIM + SSD_HEADS
DT_MIN = 0.001
DT_MAX = 0.1
A_INIT_MAX = 16.0

POOL_WINDOWS = (2, 4, 8, 16)
POOL_GROUPS = len(POOL_WINDOWS)
POOL_GROUP_DIM = D_MODEL // POOL_GROUPS

FFN_HIDDEN = -(-8 * D_MODEL // (3 * 256)) * 256

kernel_name = 'hybrid_ssd_pool_trunk'


def rmsnorm(x, w):
    xf = x.astype(jnp.float32)
    y = xf * lax.rsqrt(jnp.mean(xf * xf, axis=-1, keepdims=True) + EPS)
    return (y * w).astype(x.dtype)


def causal_depthwise_conv(u, w, b):
    c = u.shape[-1]
    out = lax.conv_general_dilated(
        u, w[:, None, :].astype(u.dtype), window_strides=(1,),
        padding=[(w.shape[0] - 1, 0)], dimension_numbers=('NWC', 'WIO', 'NWC'),
        feature_group_count=c)
    return out + b.astype(u.dtype)


def ssd_chunked(X, dt, A, Bm, Cm):
    bsz, l, _, p = X.shape
    nc = l // CHUNK
    G, R, N = SSD_GROUPS, HEADS_PER_GROUP, D_STATE
    Xc = (X * dt[..., None]).reshape(bsz, nc, CHUNK, G, R, p)
    dA = jnp.moveaxis((dt * A).reshape(bsz, nc, CHUNK, G, R), 2, -1)
    dA_cs = jnp.cumsum(dA, axis=-1)
    Bc = Bm.reshape(bsz, nc, CHUNK, G, N)
    Cc = Cm.reshape(bsz, nc, CHUNK, G, N)
    causal = jnp.tril(jnp.ones((CHUNK, CHUNK), dtype=bool))
    seg = dA_cs[..., :, None] - dA_cs[..., None, :]
    Lmat = jnp.exp(jnp.where(causal, seg, -jnp.inf))
    CB = jnp.einsum('bclgn,bcsgn->bcgls', Cc, Bc)
    y_diag = jnp.einsum('bcgls,bcgrls,bcsgrp->bclgrp', CB, Lmat, Xc)
    decay_states = jnp.exp(dA_cs[..., -1:] - dA_cs)
    states = jnp.einsum('bclgn,bcgrl,bclgrp->bcgrpn', Bc, decay_states, Xc)
    chunk_decay = jnp.exp(dA_cs[..., -1])

    def step(carry, inp):
        st, dec = inp
        return carry * dec[..., None, None] + st, carry

    init = jnp.zeros_like(states[:, 0])
    _, prev_states = lax.scan(step, init, (jnp.moveaxis(states, 1, 0), jnp.moveaxis(chunk_decay, 1, 0)))
    prev_states = jnp.moveaxis(prev_states, 0, 1)
    y_off = jnp.einsum('bclgn,bcgrpn,bcgrl->bclgrp', Cc, prev_states, jnp.exp(dA_cs))
    return (y_diag + y_off).reshape(bsz, l, SSD_HEADS, p)


def ssd_mixer(u, w_in, conv_w, conv_b, dt_bias, a_log, d_skip, norm_w, w_out):
    bsz, L, _ = u.shape
    zxbcdt = u @ w_in
    z, xBC, dt_raw = jnp.split(zxbcdt, [D_INNER, D_INNER + CONV_DIM], axis=-1)
    xBC = jax.nn.silu(causal_depthwise_conv(xBC, conv_w, conv_b))
    xs, Bs, Cs = jnp.split(xBC, [D_INNER, D_INNER + SSD_GROUPS * D_STATE], axis=-1)
    dt = jax.nn.softplus(dt_raw.astype(jnp.float32) + dt_bias.astype(jnp.float32))
    A = -jnp.exp(a_log.astype(jnp.float32))
    Xf = xs.astype(jnp.float32).reshape(bsz, L, SSD_HEADS, SSD_HEAD_DIM)
    Bf = Bs.astype(jnp.float32).reshape(bsz, L, SSD_GROUPS, D_STATE)
    Cf = Cs.astype(jnp.float32).reshape(bsz, L, SSD_GROUPS, D_STATE)
    pad_front = (-N_META) % CHUNK
    pad_back = (-(pad_front + L)) % CHUNK

    def pad(t):
        return jnp.pad(t, ((0, 0), (pad_front, pad_back)) + ((0, 0),) * (t.ndim - 2))

    y = ssd_chunked(pad(Xf), pad(dt), A, pad(Bf), pad(Cf))[:, pad_front:pad_front + L]
    y = y + Xf * d_skip.astype(jnp.float32)[:, None]
    y = y.reshape(bsz, L, D_INNER) * jax.nn.silu(z.astype(jnp.float32))
    yg = y.reshape(bsz, L, SSD_GROUPS, D_INNER // SSD_GROUPS)
    yg = yg * lax.rsqrt(jnp.mean(yg * yg, axis=-1, keepdims=True) + EPS)
    y = (yg.reshape(bsz, L, D_INNER) * norm_w).astype(u.dtype)
    return y @ w_out


def pool_mixer(u, w_group, b, scale):
    bsz, L, _ = u.shape
    uf = u.astype(jnp.float32)
    cs = jnp.concatenate([jnp.zeros((bsz, 1, D_MODEL), jnp.float32), jnp.cumsum(uf, axis=1)], axis=1)
    t = jnp.arange(L)
    pooled = []
    for g, win in enumerate(POOL_WINDOWS):
        cs_g = cs[..., g * POOL_GROUP_DIM:(g + 1) * POOL_GROUP_DIM]
        start = jnp.maximum(t + 1 - win, 0)
        count = (t + 1 - start).astype(jnp.float32)
        pooled.append((cs_g[:, 1:] - cs_g[:, start]) / count[None, :, None])
    pooled = jnp.stack(pooled, axis=2)
    mixed = (pooled - uf.reshape(bsz, L, POOL_GROUPS, POOL_GROUP_DIM)).astype(u.dtype)
    out = jnp.einsum('blgc,gcd->blgd', mixed, w_group) + b.reshape(POOL_GROUPS, POOL_GROUP_DIM)
    return out.reshape(bsz, L, D_MODEL) * scale


def swiglu(u, w_gate, w_up, w_down):
    return (jax.nn.silu(u @ w_gate) * (u @ w_up)) @ w_down


def _fwd_setup_inputs(seed: int = 0) -> dict:
    key = jax.random.key(seed)
    ks = jax.random.split(key, 17)
    f32 = jnp.float32

    def nrm(k, shape, scale):
        return scale * jax.random.normal(k, shape, f32)

    x = jax.random.normal(ks[0], (BATCH, SEQ, D_MODEL), f32)
    meta_tokens = nrm(ks[1], (N_META, D_MODEL), 1.0)
    norm_w = 1.0 + nrm(ks[2], (DEPTH, 4, D_MODEL), 0.05)
    ssd_w_in = nrm(ks[3], (N_SSD_LAYERS, D_MODEL, D_IN_PROJ), D_MODEL ** -0.5)
    ssd_conv_w = nrm(ks[4], (N_SSD_LAYERS, D_CONV, CONV_DIM), D_CONV ** -0.5)
    ssd_conv_b = nrm(ks[5], (N_SSD_LAYERS, CONV_DIM), 0.01)
    dt0 = jnp.exp(jax.random.uniform(ks[6], (N_SSD_LAYERS, SSD_HEADS), f32,
                                     math.log(DT_MIN), math.log(DT_MAX)))
    ssd_dt_bias = dt0 + jnp.log(-jnp.expm1(-dt0))
    ssd_a_log = jnp.log(jax.random.uniform(ks[7], (N_SSD_LAYERS, SSD_HEADS), f32, 1.0, A_INIT_MAX))
    ssd_d = 1.0 + nrm(ks[8], (N_SSD_LAYERS, SSD_HEADS), 0.1)
    ssd_norm_w = 1.0 + nrm(ks[9], (N_SSD_LAYERS, D_INNER), 0.05)
    ssd_w_out = nrm(ks[10], (N_SSD_LAYERS, D_INNER, D_MODEL), D_INNER ** -0.5)
    pool_w = nrm(ks[11], (N_POOL_LAYERS, POOL_GROUPS, POOL_GROUP_DIM, POOL_GROUP_DIM), POOL_GROUP_DIM ** -0.5)
    pool_b = nrm(ks[12], (N_POOL_LAYERS, D_MODEL), 0.01)
    pool_scale = 1.0 + nrm(ks[13], (N_POOL_LAYERS, D_MODEL), 0.1)
    ffn_w_gate = nrm(ks[14], (DEPTH, D_MODEL, FFN_HIDDEN), D_MODEL ** -0.5)
    ffn_w_up = nrm(ks[15], (DEPTH, D_MODEL, FFN_HIDDEN), D_MODEL ** -0.5)
    ffn_w_down = nrm(ks[16], (DEPTH, FFN_HIDDEN, D_MODEL), FFN_HIDDEN ** -0.5)
    return {'x': x, 'meta_tokens': meta_tokens, 'norm_w': norm_w,
            'ssd_w_in': ssd_w_in, 'ssd_conv_w': ssd_conv_w, 'ssd_conv_b': ssd_conv_b,
            'ssd_dt_bias': ssd_dt_bias, 'ssd_a_log': ssd_a_log, 'ssd_d': ssd_d,
            'ssd_norm_w': ssd_norm_w, 'ssd_w_out': ssd_w_out,
            'pool_w': pool_w, 'pool_b': pool_b, 'pool_scale': pool_scale,
            'ffn_w_gate': ffn_w_gate, 'ffn_w_up': ffn_w_up, 'ffn_w_down': ffn_w_down}


def _fwd_reference(x, meta_tokens, norm_w, ssd_w_in, ssd_conv_w, ssd_conv_b, ssd_dt_bias,
              ssd_a_log, ssd_d, ssd_norm_w, ssd_w_out, pool_w, pool_b, pool_scale,
              ffn_w_gate, ffn_w_up, ffn_w_down):
    bsz = x.shape[0]
    meta = jnp.broadcast_to(meta_tokens[None].astype(x.dtype), (bsz, N_META, D_MODEL))
    h = jnp.concatenate([meta, x], axis=1)
    for i in range(DEPTH):
        j = i // NUM_MIXERS
        u = rmsnorm(h, norm_w[i, 0])
        if i % NUM_MIXERS == 0:
            mix = ssd_mixer(u, ssd_w_in[j], ssd_conv_w[j], ssd_conv_b[j], ssd_dt_bias[j],
                            ssd_a_log[j], ssd_d[j], ssd_norm_w[j], ssd_w_out[j])
        else:
            mix = pool_mixer(u, pool_w[j], pool_b[j], pool_scale[j])
        h = h + rmsnorm(mix, norm_w[i, 1])
        f = swiglu(rmsnorm(h, norm_w[i, 2]), ffn_w_gate[i], ffn_w_up[i], ffn_w_down[i])
        h = h + rmsnorm(f, norm_w[i, 3])
    return h[:, N_META:]


import jax as _jax
import jax.numpy as _jnp

TWIN_FORMAT = 'train_step'
FWD_PARAMS = ['x', 'meta_tokens', 'norm_w', 'ssd_w_in', 'ssd_conv_w', 'ssd_conv_b', 'ssd_dt_bias', 'ssd_a_log', 'ssd_d', 'ssd_norm_w', 'ssd_w_out', 'pool_w', 'pool_b', 'pool_scale', 'ffn_w_gate', 'ffn_w_up', 'ffn_w_down']
TWIN_WEIGHTS = ['meta_tokens', 'norm_w', 'ssd_w_in', 'ssd_conv_w', 'ssd_conv_b', 'ssd_dt_bias', 'ssd_a_log', 'ssd_d', 'ssd_norm_w', 'ssd_w_out', 'pool_w', 'pool_b', 'pool_scale', 'ffn_w_gate', 'ffn_w_up', 'ffn_w_down']
TWIN_DIFF_INPUT = 'x'
TWIN_INPUTS = ['x', 'meta_tokens', 'norm_w', 'ssd_w_in', 'ssd_conv_w', 'ssd_conv_b', 'ssd_dt_bias', 'ssd_a_log', 'ssd_d', 'ssd_norm_w', 'ssd_w_out', 'pool_w', 'pool_b', 'pool_scale', 'ffn_w_gate', 'ffn_w_up', 'ffn_w_down', 'loss_target', 'm_meta_tokens', 'm_norm_w', 'm_ssd_w_in', 'm_ssd_conv_w', 'm_ssd_conv_b', 'm_ssd_dt_bias', 'm_ssd_a_log', 'm_ssd_d', 'm_ssd_norm_w', 'm_ssd_w_out', 'm_pool_w', 'm_pool_b', 'm_pool_scale', 'm_ffn_w_gate', 'm_ffn_w_up', 'm_ffn_w_down', 'v_meta_tokens', 'v_norm_w', 'v_ssd_w_in', 'v_ssd_conv_w', 'v_ssd_conv_b', 'v_ssd_dt_bias', 'v_ssd_a_log', 'v_ssd_d', 'v_ssd_norm_w', 'v_ssd_w_out', 'v_pool_w', 'v_pool_b', 'v_pool_scale', 'v_ffn_w_gate', 'v_ffn_w_up', 'v_ffn_w_down']
TWIN_OUTPUTS = ['loss', 'grad_x', 'grad_meta_tokens', 'grad_norm_w', 'grad_ssd_w_in', 'grad_ssd_conv_w', 'grad_ssd_conv_b', 'grad_ssd_dt_bias', 'grad_ssd_a_log', 'grad_ssd_d', 'grad_ssd_norm_w', 'grad_ssd_w_out', 'grad_pool_w', 'grad_pool_b', 'grad_pool_scale', 'grad_ffn_w_gate', 'grad_ffn_w_up', 'grad_ffn_w_down', 'delta_meta_tokens', 'delta_norm_w', 'delta_ssd_w_in', 'delta_ssd_conv_w', 'delta_ssd_conv_b', 'delta_ssd_dt_bias', 'delta_ssd_a_log', 'delta_ssd_d', 'delta_ssd_norm_w', 'delta_ssd_w_out', 'delta_pool_w', 'delta_pool_b', 'delta_pool_scale', 'delta_ffn_w_gate', 'delta_ffn_w_up', 'delta_ffn_w_down', 'new_m_meta_tokens', 'new_m_norm_w', 'new_m_ssd_w_in', 'new_m_ssd_conv_w', 'new_m_ssd_conv_b', 'new_m_ssd_dt_bias', 'new_m_ssd_a_log', 'new_m_ssd_d', 'new_m_ssd_norm_w', 'new_m_ssd_w_out', 'new_m_pool_w', 'new_m_pool_b', 'new_m_pool_scale', 'new_m_ffn_w_gate', 'new_m_ffn_w_up', 'new_m_ffn_w_down', 'new_v_meta_tokens', 'new_v_norm_w', 'new_v_ssd_w_in', 'new_v_ssd_conv_w', 'new_v_ssd_conv_b', 'new_v_ssd_dt_bias', 'new_v_ssd_a_log', 'new_v_ssd_d', 'new_v_ssd_norm_w', 'new_v_ssd_w_out', 'new_v_pool_w', 'new_v_pool_b', 'new_v_pool_scale', 'new_v_ffn_w_gate', 'new_v_ffn_w_up', 'new_v_ffn_w_down']
TWIN_LEAF_KINDS = {'loss': 'loss', 'grad_x': 'grad_x', 'grad_meta_tokens': 'grad_w', 'grad_norm_w': 'grad_w', 'grad_ssd_w_in': 'grad_w', 'grad_ssd_conv_w': 'grad_w', 'grad_ssd_conv_b': 'grad_w', 'grad_ssd_dt_bias': 'grad_w', 'grad_ssd_a_log': 'grad_w', 'grad_ssd_d': 'grad_w', 'grad_ssd_norm_w': 'grad_w', 'grad_ssd_w_out': 'grad_w', 'grad_pool_w': 'grad_w', 'grad_pool_b': 'grad_w', 'grad_pool_scale': 'grad_w', 'grad_ffn_w_gate': 'grad_w', 'grad_ffn_w_up': 'grad_w', 'grad_ffn_w_down': 'grad_w', 'delta_meta_tokens': 'delta_w', 'delta_norm_w': 'delta_w', 'delta_ssd_w_in': 'delta_w', 'delta_ssd_conv_w': 'delta_w', 'delta_ssd_conv_b': 'delta_w', 'delta_ssd_dt_bias': 'delta_w', 'delta_ssd_a_log': 'delta_w', 'delta_ssd_d': 'delta_w', 'delta_ssd_norm_w': 'delta_w', 'delta_ssd_w_out': 'delta_w', 'delta_pool_w': 'delta_w', 'delta_pool_b': 'delta_w', 'delta_pool_scale': 'delta_w', 'delta_ffn_w_gate': 'delta_w', 'delta_ffn_w_up': 'delta_w', 'delta_ffn_w_down': 'delta_w', 'new_m_meta_tokens': 'new_m', 'new_m_norm_w': 'new_m', 'new_m_ssd_w_in': 'new_m', 'new_m_ssd_conv_w': 'new_m', 'new_m_ssd_conv_b': 'new_m', 'new_m_ssd_dt_bias': 'new_m', 'new_m_ssd_a_log': 'new_m', 'new_m_ssd_d': 'new_m', 'new_m_ssd_norm_w': 'new_m', 'new_m_ssd_w_out': 'new_m', 'new_m_pool_w': 'new_m', 'new_m_pool_b': 'new_m', 'new_m_pool_scale': 'new_m', 'new_m_ffn_w_gate': 'new_m', 'new_m_ffn_w_up': 'new_m', 'new_m_ffn_w_down': 'new_m', 'new_v_meta_tokens': 'new_v', 'new_v_norm_w': 'new_v', 'new_v_ssd_w_in': 'new_v', 'new_v_ssd_conv_w': 'new_v', 'new_v_ssd_conv_b': 'new_v', 'new_v_ssd_dt_bias': 'new_v', 'new_v_ssd_a_log': 'new_v', 'new_v_ssd_d': 'new_v', 'new_v_ssd_norm_w': 'new_v', 'new_v_ssd_w_out': 'new_v', 'new_v_pool_w': 'new_v', 'new_v_pool_b': 'new_v', 'new_v_pool_scale': 'new_v', 'new_v_ffn_w_gate': 'new_v', 'new_v_ffn_w_up': 'new_v', 'new_v_ffn_w_down': 'new_v'}


def _forward(args):
    return _fwd_reference(*[args[k] for k in FWD_PARAMS])


def _output_shape():
    out = _jax.eval_shape(lambda: _forward(_fwd_setup_inputs(0)))
    return out.shape, out.dtype

N_MICROBATCH = 1
ADAM_LR = 0.001
ADAM_B1 = 0.9
ADAM_B2 = 0.999
ADAM_EPS = 1e-08
ADAM_WD = 0.01
ADAM_STEP = 10
PER_EXAMPLE_BATCH_AXIS = {'x': 0, 'loss_target': 0}
SHARED_INPUTS = []
_WEIGHT_DTYPES = {'meta_tokens': _jnp.float32, 'norm_w': _jnp.float32, 'ssd_w_in': _jnp.float32, 'ssd_conv_w': _jnp.float32, 'ssd_conv_b': _jnp.float32, 'ssd_dt_bias': _jnp.float32, 'ssd_a_log': _jnp.float32, 'ssd_d': _jnp.float32, 'ssd_norm_w': _jnp.float32, 'ssd_w_out': _jnp.float32, 'pool_w': _jnp.float32, 'pool_b': _jnp.float32, 'pool_scale': _jnp.float32, 'ffn_w_gate': _jnp.float32, 'ffn_w_up': _jnp.float32, 'ffn_w_down': _jnp.float32}
MOMENT_SCALE = {'meta_tokens': 2.269419e-02, 'norm_w': 5.633351e+00, 'ssd_w_in': 3.348280e-01, 'ssd_conv_w': 3.252538e-01, 'ssd_conv_b': 6.057320e-01, 'ssd_dt_bias': 9.271688e-01, 'ssd_a_log': 1.548945e+00, 'ssd_d': 1.937874e+00, 'ssd_norm_w': 4.386204e-01, 'ssd_w_out': 6.010959e-01, 'pool_w': 7.274572e-01, 'pool_b': 2.806669e+00, 'pool_scale': 1.060643e+00, 'ffn_w_gate': 1.972175e-01, 'ffn_w_up': 2.111421e-01, 'ffn_w_down': 3.550244e-01}


def _to_microbatches(a, axis):
    t = _jnp.moveaxis(a, axis, 0)
    t = t.reshape((N_MICROBATCH, t.shape[0] // N_MICROBATCH) + t.shape[1:])
    return _jnp.moveaxis(t, 1, axis + 1)


def setup_inputs(seed: int = 0) -> dict:
    inp = _fwd_setup_inputs(seed)
    key = _jax.random.fold_in(_jax.random.key(seed), 7919)
    shape, _ = _output_shape()
    out = dict(inp)
    out["loss_target"] = _jax.random.normal(_jax.random.fold_in(key, 0), shape, _jnp.float32)
    for i, name in enumerate(TWIN_WEIGHTS):
        w = inp[name].astype(_jnp.float32)
        if MOMENT_SCALE is None:
            s = _jnp.sqrt(_jnp.mean(_jnp.square(w)) + 1e-30)
        else:
            s = MOMENT_SCALE[name]
        km, kv = _jax.random.split(_jax.random.fold_in(key, i + 1))
        out[name] = w
        out["m_" + name] = s * _jax.random.normal(km, w.shape, _jnp.float32)
        out["v_" + name] = (s * s) * _jax.random.uniform(kv, w.shape, _jnp.float32, 0.5, 1.5)
    if N_MICROBATCH > 1:
        for name, axis in PER_EXAMPLE_BATCH_AXIS.items():
            out[name] = _to_microbatches(out[name], axis)
    return {'x': out['x'], 'meta_tokens': out['meta_tokens'], 'norm_w': out['norm_w'], 'ssd_w_in': out['ssd_w_in'], 'ssd_conv_w': out['ssd_conv_w'], 'ssd_conv_b': out['ssd_conv_b'], 'ssd_dt_bias': out['ssd_dt_bias'], 'ssd_a_log': out['ssd_a_log'], 'ssd_d': out['ssd_d'], 'ssd_norm_w': out['ssd_norm_w'], 'ssd_w_out': out['ssd_w_out'], 'pool_w': out['pool_w'], 'pool_b': out['pool_b'], 'pool_scale': out['pool_scale'], 'ffn_w_gate': out['ffn_w_gate'], 'ffn_w_up': out['ffn_w_up'], 'ffn_w_down': out['ffn_w_down'], 'loss_target': out['loss_target'], 'm_meta_tokens': out['m_meta_tokens'], 'm_norm_w': out['m_norm_w'], 'm_ssd_w_in': out['m_ssd_w_in'], 'm_ssd_conv_w': out['m_ssd_conv_w'], 'm_ssd_conv_b': out['m_ssd_conv_b'], 'm_ssd_dt_bias': out['m_ssd_dt_bias'], 'm_ssd_a_log': out['m_ssd_a_log'], 'm_ssd_d': out['m_ssd_d'], 'm_ssd_norm_w': out['m_ssd_norm_w'], 'm_ssd_w_out': out['m_ssd_w_out'], 'm_pool_w': out['m_pool_w'], 'm_pool_b': out['m_pool_b'], 'm_pool_scale': out['m_pool_scale'], 'm_ffn_w_gate': out['m_ffn_w_gate'], 'm_ffn_w_up': out['m_ffn_w_up'], 'm_ffn_w_down': out['m_ffn_w_down'], 'v_meta_tokens': out['v_meta_tokens'], 'v_norm_w': out['v_norm_w'], 'v_ssd_w_in': out['v_ssd_w_in'], 'v_ssd_conv_w': out['v_ssd_conv_w'], 'v_ssd_conv_b': out['v_ssd_conv_b'], 'v_ssd_dt_bias': out['v_ssd_dt_bias'], 'v_ssd_a_log': out['v_ssd_a_log'], 'v_ssd_d': out['v_ssd_d'], 'v_ssd_norm_w': out['v_ssd_norm_w'], 'v_ssd_w_out': out['v_ssd_w_out'], 'v_pool_w': out['v_pool_w'], 'v_pool_b': out['v_pool_b'], 'v_pool_scale': out['v_pool_scale'], 'v_ffn_w_gate': out['v_ffn_w_gate'], 'v_ffn_w_up': out['v_ffn_w_up'], 'v_ffn_w_down': out['v_ffn_w_down']}


def _loss(weights, diff, rest, loss_target):
    with _jax.named_scope("forward"):
        args = {**rest, TWIN_DIFF_INPUT: diff, **{k: w.astype(_WEIGHT_DTYPES[k]) for k, w in weights.items()}}
        y = _forward(args)
    with _jax.named_scope("loss_head"):
        err = _jnp.square(y.astype(_jnp.float32) - loss_target)
        return 0.5 * _jnp.sum(_jnp.mean(err, axis=-1)) if err.ndim else 0.5 * err


def _adamw(w, g, m, v):
    m = ADAM_B1 * m + (1.0 - ADAM_B1) * g
    v = ADAM_B2 * v + (1.0 - ADAM_B2) * _jnp.square(g)
    m_hat = m / (1.0 - ADAM_B1 ** ADAM_STEP)
    v_hat = v / (1.0 - ADAM_B2 ** ADAM_STEP)
    delta = -ADAM_LR * (m_hat / (_jnp.sqrt(v_hat) + ADAM_EPS) + ADAM_WD * w)
    return delta, m, v


def reference(x, meta_tokens, norm_w, ssd_w_in, ssd_conv_w, ssd_conv_b, ssd_dt_bias, ssd_a_log, ssd_d, ssd_norm_w, ssd_w_out, pool_w, pool_b, pool_scale, ffn_w_gate, ffn_w_up, ffn_w_down, loss_target, m_meta_tokens, m_norm_w, m_ssd_w_in, m_ssd_conv_w, m_ssd_conv_b, m_ssd_dt_bias, m_ssd_a_log, m_ssd_d, m_ssd_norm_w, m_ssd_w_out, m_pool_w, m_pool_b, m_pool_scale, m_ffn_w_gate, m_ffn_w_up, m_ffn_w_down, v_meta_tokens, v_norm_w, v_ssd_w_in, v_ssd_conv_w, v_ssd_conv_b, v_ssd_dt_bias, v_ssd_a_log, v_ssd_d, v_ssd_norm_w, v_ssd_w_out, v_pool_w, v_pool_b, v_pool_scale, v_ffn_w_gate, v_ffn_w_up, v_ffn_w_down):
    given = dict(x=x, meta_tokens=meta_tokens, norm_w=norm_w, ssd_w_in=ssd_w_in, ssd_conv_w=ssd_conv_w, ssd_conv_b=ssd_conv_b, ssd_dt_bias=ssd_dt_bias, ssd_a_log=ssd_a_log, ssd_d=ssd_d, ssd_norm_w=ssd_norm_w, ssd_w_out=ssd_w_out, pool_w=pool_w, pool_b=pool_b, pool_scale=pool_scale, ffn_w_gate=ffn_w_gate, ffn_w_up=ffn_w_up, ffn_w_down=ffn_w_down, loss_target=loss_target, m_meta_tokens=m_meta_tokens, m_norm_w=m_norm_w, m_ssd_w_in=m_ssd_w_in, m_ssd_conv_w=m_ssd_conv_w, m_ssd_conv_b=m_ssd_conv_b, m_ssd_dt_bias=m_ssd_dt_bias, m_ssd_a_log=m_ssd_a_log, m_ssd_d=m_ssd_d, m_ssd_norm_w=m_ssd_norm_w, m_ssd_w_out=m_ssd_w_out, m_pool_w=m_pool_w, m_pool_b=m_pool_b, m_pool_scale=m_pool_scale, m_ffn_w_gate=m_ffn_w_gate, m_ffn_w_up=m_ffn_w_up, m_ffn_w_down=m_ffn_w_down, v_meta_tokens=v_meta_tokens, v_norm_w=v_norm_w, v_ssd_w_in=v_ssd_w_in, v_ssd_conv_w=v_ssd_conv_w, v_ssd_conv_b=v_ssd_conv_b, v_ssd_dt_bias=v_ssd_dt_bias, v_ssd_a_log=v_ssd_a_log, v_ssd_d=v_ssd_d, v_ssd_norm_w=v_ssd_norm_w, v_ssd_w_out=v_ssd_w_out, v_pool_w=v_pool_w, v_pool_b=v_pool_b, v_pool_scale=v_pool_scale, v_ffn_w_gate=v_ffn_w_gate, v_ffn_w_up=v_ffn_w_up, v_ffn_w_down=v_ffn_w_down)
    weights = {n: given[n] for n in TWIN_WEIGHTS}
    shared = {n: given[n] for n in SHARED_INPUTS}
    per_example = {n: given[n] for n in ['x']}
    grad_fn = _jax.value_and_grad(_loss, argnums=(0, 1))

    def one_microbatch(ex, loss_target):
        ex = dict(ex)
        diff = ex.pop(TWIN_DIFF_INPUT)
        return grad_fn(weights, diff, {**shared, **ex}, loss_target)

    if N_MICROBATCH == 1:
        loss, (grad_w, grad_x) = one_microbatch(per_example, given["loss_target"])
    else:
        def body(carry, xs):
            loss_sum, grad_sum = carry
            l_k, (gw_k, gx_k) = one_microbatch(xs[0], xs[1])
            with _jax.named_scope("update"):
                return (loss_sum + l_k, _jax.tree.map(_jnp.add, grad_sum, gw_k)), gx_k

        init = (_jnp.zeros((), _jnp.float32), _jax.tree.map(_jnp.zeros_like, weights))
        (loss, grad_w), grad_x = _jax.lax.scan(body, init, (per_example, given["loss_target"]))
    with _jax.named_scope("update"):
        delta_w, new_m, new_v = {}, {}, {}
        for n in TWIN_WEIGHTS:
            delta_w[n], new_m[n], new_v[n] = _adamw(weights[n], grad_w[n], given["m_" + n], given["v_" + n])
    return (loss, grad_x, *[grad_w[n] for n in TWIN_WEIGHTS], *[delta_w[n] for n in TWIN_WEIGHTS],
            *[new_m[n] for n in TWIN_WEIGHTS], *[new_v[n] for n in TWIN_WEIGHTS])
```

```python
import functools

import jax
import jax.numpy as jnp
from jax import lax
from jax.experimental import pallas as pl
from jax.experimental.pallas import tpu as pltpu

F32 = jnp.float32
BF16 = jnp.bfloat16
MESH = pl.DeviceIdType.MESH
AXES = ("x", "y", "c")
N_DEV = 8

N_META = 16
EPS = 1e-6
HEAD_DIM = 64
D_STATE = 128
SSD_GROUPS = 8
D_CONV = 4
CHUNK = 256
POOL_WINDOWS = (2, 4, 8, 16)
ADAM_LR, ADAM_B1, ADAM_B2, ADAM_EPS, ADAM_WD, ADAM_STEP = 0.001, 0.9, 0.999, 1e-08, 0.01, 10

PAD_FRONT = (-N_META) % CHUNK
LANES = 128
SUBLANES = 8
ROW_TILE = 256
CONV_TILE = 256
POOL_TILE = 128
VMEM_LIMIT = 56 * 1024 * 1024


def _params(sem=None):
    return pltpu.CompilerParams(dimension_semantics=sem, vmem_limit_bytes=VMEM_LIMIT)


def _tile(n, target, mult):
    if n <= target:
        return n
    best = None
    for t in range(mult, target + 1, mult):
        if n % t == 0:
            best = t
    assert best is not None, (n, target, mult)
    return best


def _dot(a, b, ca, cb):
    return lax.dot_general(a, b, (((ca,), (cb,)), ((), ())), preferred_element_type=F32)


def _sigmoid(x):
    return 1.0 / (1.0 + jnp.exp(-x))


def _row_mask(shape, first_row):
    rows = lax.broadcasted_iota(jnp.int32, shape, 0) + first_row
    return rows >= PAD_FRONT


def matmul(name, a, b, *, ta=False, tb=False, out_dtype=F32, tm=768, tn=512, tk=2048):
    if ta:
        kdim, m = a.shape
    else:
        m, kdim = a.shape
    if tb:
        n, k2 = b.shape
    else:
        k2, n = b.shape
    assert kdim == k2, (a.shape, b.shape, ta, tb)
    tm = _tile(m, tm, LANES if ta else 16)
    tn = _tile(n, tn, LANES)
    tk = _tile(kdim, tk, LANES)
    nk = kdim // tk
    a_spec = pl.BlockSpec((tk, tm), lambda i, j, k: (k, i)) if ta else pl.BlockSpec((tm, tk), lambda i, j, k: (i, k))
    b_spec = pl.BlockSpec((tn, tk), lambda i, j, k: (j, k)) if tb else pl.BlockSpec((tk, tn), lambda i, j, k: (k, j))

    def body(a_ref, b_ref, o_ref, acc_ref):
        k = pl.program_id(2)

        @pl.when(k == 0)
        def _():
            acc_ref[...] = jnp.zeros_like(acc_ref)

        acc_ref[...] += _dot(a_ref[...], b_ref[...], 0 if ta else 1, 1 if tb else 0)

        @pl.when(k == nk - 1)
        def _():
            o_ref[...] = acc_ref[...].astype(o_ref.dtype)

    return pl.pallas_call(
        body, name=name, grid=(m // tm, n // tn, nk),
        in_specs=[a_spec, b_spec], out_specs=pl.BlockSpec((tm, tn), lambda i, j, k: (i, j)),
        out_shape=jax.ShapeDtypeStruct((m, n), out_dtype),
        scratch_shapes=[pltpu.VMEM((tm, tn), F32)],
        compiler_params=_params(("parallel", "parallel", "arbitrary")),
    )(a, b)


def rmsnorm_fwd(name, x, w, res=None, out_dtype=F32):
    rows, d = x.shape
    tr = ROW_TILE
    row_spec = pl.BlockSpec((tr, d), lambda i: (i, 0))
    w_spec = pl.BlockSpec((1, d), lambda i: (0, 0))

    def body(*refs):
        if res is None:
            x_ref, w_ref, o_ref = refs
        else:
            x_ref, w_ref, r_ref, o_ref = refs
        xv = x_ref[...]
        y = xv * lax.rsqrt(jnp.mean(xv * xv, axis=-1, keepdims=True) + EPS) * w_ref[...]
        if res is not None:
            y = r_ref[...] + y
        o_ref[...] = y.astype(o_ref.dtype)

    args = [x, w.reshape(1, d)] + ([] if res is None else [res])
    specs = [row_spec, w_spec] + ([] if res is None else [row_spec])
    return pl.pallas_call(
        body, name=name, grid=(rows // tr,), in_specs=specs, out_specs=row_spec,
        out_shape=jax.ShapeDtypeStruct((rows, d), out_dtype), compiler_params=_params(("parallel",)),
    )(*args)


def rmsnorm_bwd(name, x, w, dy, add=None):
    rows, d = x.shape
    tr = ROW_TILE
    row_spec = pl.BlockSpec((tr, d), lambda i: (i, 0))
    w_spec = pl.BlockSpec((1, d), lambda i: (0, 0))

    def body(*refs):
        if add is None:
            x_ref, w_ref, dy_ref, dx_ref, dw_ref = refs
        else:
            x_ref, w_ref, dy_ref, add_ref, dx_ref, dw_ref = refs
        xv = x_ref[...]
        dyv = dy_ref[...].astype(F32)
        r = lax.rsqrt(jnp.mean(xv * xv, axis=-1, keepdims=True) + EPS)
        xh = xv * r
        dxh = dyv * w_ref[...]
        dx = r * (dxh - xh * jnp.mean(dxh * xh, axis=-1, keepdims=True))
        if add is not None:
            dx = dx + add_ref[...]
        dx_ref[...] = dx

        @pl.when(pl.program_id(0) == 0)
        def _():
            dw_ref[...] = jnp.zeros_like(dw_ref)

        dw_ref[...] += jnp.sum(dyv * xh, axis=0, keepdims=True)

    args = [x, w.reshape(1, d), dy] + ([] if add is None else [add])
    specs = [row_spec, w_spec, row_spec] + ([] if add is None else [row_spec])
    dx, dw = pl.pallas_call(
        body, name=name, grid=(rows // tr,), in_specs=specs, out_specs=[row_spec, w_spec],
        out_shape=[jax.ShapeDtypeStruct((rows, d), F32), jax.ShapeDtypeStruct((1, d), F32)],
        compiler_params=_params(("arbitrary",)),
    )(*args)
    return dx, dw.reshape(d)


def loss_head(name, h, target):
    rows, d = h.shape
    tr = ROW_TILE
    first = (PAD_FRONT + N_META) // tr
    assert (PAD_FRONT + N_META) % tr == 0 and target.shape[0] == rows - first * tr

    def body(h_ref, t_ref, dh_ref, loss_ref):
        i = pl.program_id(0)

        @pl.when(i == 0)
        def _():
            loss_ref[...] = jnp.zeros_like(loss_ref)

        keep = (i >= first).astype(F32)
        diff = (h_ref[...] - t_ref[...]) * keep
        dh_ref[...] = diff / d
        loss_ref[...] += 0.5 * jnp.sum(diff * diff) / d

    dh, loss = pl.pallas_call(
        body, name=name, grid=(rows // tr,),
        in_specs=[pl.BlockSpec((tr, d), lambda i: (i, 0)), pl.BlockSpec((tr, d), lambda i: (jnp.maximum(i - first, 0), 0))],
        out_specs=[pl.BlockSpec((tr, d), lambda i: (i, 0)), pl.BlockSpec((SUBLANES, LANES), lambda i: (0, 0))],
        out_shape=[jax.ShapeDtypeStruct((rows, d), F32), jax.ShapeDtypeStruct((SUBLANES, LANES), F32)],
        compiler_params=_params(("arbitrary",)),
    )(h, target)
    return dh, loss[0, 0]


def swiglu_fwd(name, gp):
    rows, h2 = gp.shape
    hid = h2 // 2
    tr, tc = ROW_TILE, _tile(hid, 1024, LANES)
    nh = hid // tc

    def body(g_ref, p_ref, a_ref):
        g = g_ref[...]
        a_ref[...] = (g * _sigmoid(g) * p_ref[...]).astype(a_ref.dtype)

    return pl.pallas_call(
        body, name=name, grid=(rows // tr, nh),
        in_specs=[pl.BlockSpec((tr, tc), lambda i, j: (i, j)), pl.BlockSpec((tr, tc), lambda i, j: (i, j + nh))],
        out_specs=pl.BlockSpec((tr, tc), lambda i, j: (i, j)),
        out_shape=jax.ShapeDtypeStruct((rows, hid), BF16), compiler_params=_params(("parallel", "parallel")),
    )(gp, gp)


def swiglu_bwd(name, gp, da):
    rows, h2 = gp.shape
    hid = h2 // 2
    tr, tc = ROW_TILE, _tile(hid, 1024, LANES)
    nh = hid // tc

    def body(g_ref, p_ref, da_ref, o_ref):
        g = g_ref[...]
        s = _sigmoid(g)
        dav = da_ref[...]
        is_gate = pl.program_id(1) < nh
        d_gate = dav * p_ref[...] * (s * (1.0 + g * (1.0 - s)))
        d_up = dav * (g * s)
        o_ref[...] = jnp.where(is_gate, d_gate, d_up).astype(o_ref.dtype)

    return pl.pallas_call(
        body, name=name, grid=(rows // tr, 2 * nh),
        in_specs=[pl.BlockSpec((tr, tc), lambda i, j: (i, j % nh)), pl.BlockSpec((tr, tc), lambda i, j: (i, nh + j % nh)),
                  pl.BlockSpec((tr, tc), lambda i, j: (i, j % nh))],
        out_specs=pl.BlockSpec((tr, tc), lambda i, j: (i, j)),
        out_shape=jax.ShapeDtypeStruct((rows, h2), BF16), compiler_params=_params(("parallel", "parallel")),
    )(gp, gp, da)


def conv_fwd(name, zx, conv_w, conv_b, d_inner, conv_dim):
    rows = zx.shape[0]
    tc = CONV_TILE
    off = d_inner // tc
    assert d_inner % tc == 0 and conv_dim % tc == 0

    def body(u_ref, w_ref, b_ref, o_ref):
        u = u_ref[...]
        acc = u * w_ref[D_CONV - 1:D_CONV, :] + b_ref[...]
        for s in range(1, D_CONV):
            acc = acc + pltpu.roll(u, s, axis=0) * w_ref[D_CONV - 1 - s:D_CONV - s, :]
        y = acc * _sigmoid(acc)
        o_ref[...] = jnp.where(_row_mask(y.shape, 0), y, 0.0)

    return pl.pallas_call(
        body, name=name, grid=(conv_dim // tc,),
        in_specs=[pl.BlockSpec((rows, tc), lambda j: (0, off + j)), pl.BlockSpec((D_CONV, tc), lambda j: (0, j)),
                  pl.BlockSpec((1, tc), lambda j: (0, j))],
        out_specs=pl.BlockSpec((rows, tc), lambda j: (0, j)),
        out_shape=jax.ShapeDtypeStruct((rows, conv_dim), F32), compiler_params=_params(("parallel",)),
    )(zx, conv_w, conv_b.reshape(1, conv_dim))


def conv_bwd(name, zx, conv_w, conv_b, dxbc, d_inner, conv_dim):
    rows = zx.shape[0]
    tc = CONV_TILE
    off = d_inner // tc

    def body(u_ref, w_ref, b_ref, dy_ref, du_ref, dw_ref, db_ref):
        u = u_ref[...]
        wk = [w_ref[D_CONV - 1 - s:D_CONV - s, :] for s in range(D_CONV)]
        shifted = [u] + [pltpu.roll(u, s, axis=0) for s in range(1, D_CONV)]
        acc = u * wk[0] + b_ref[...]
        for s in range(1, D_CONV):
            acc = acc + shifted[s] * wk[s]
        sg = _sigmoid(acc)
        mask = _row_mask(acc.shape, 0)
        dpre = jnp.where(mask, dy_ref[...] * (sg * (1.0 + acc * (1.0 - sg))), 0.0)
        db_ref[...] = jnp.sum(dpre, axis=0, keepdims=True)
        du = dpre * wk[0]
        dw_ref[D_CONV - 1:D_CONV, :] = jnp.sum(dpre * u, axis=0, keepdims=True)
        for s in range(1, D_CONV):
            du = du + pltpu.roll(dpre, rows - s, axis=0) * wk[s]
            dw_ref[D_CONV - 1 - s:D_CONV - s, :] = jnp.sum(dpre * shifted[s], axis=0, keepdims=True)
        du_ref[...] = jnp.where(mask, du, 0.0).astype(du_ref.dtype)

    return pl.pallas_call(
        body, name=name, grid=(conv_dim // tc,),
        in_specs=[pl.BlockSpec((rows, tc), lambda j: (0, off + j)), pl.BlockSpec((D_CONV, tc), lambda j: (0, j)),
                  pl.BlockSpec((1, tc), lambda j: (0, j)), pl.BlockSpec((rows, tc), lambda j: (0, j))],
        out_specs=[pl.BlockSpec((rows, tc), lambda j: (0, j)), pl.BlockSpec((D_CONV, tc), lambda j: (0, j)),
                   pl.BlockSpec((1, tc), lambda j: (0, j))],
        out_shape=[jax.ShapeDtypeStruct((rows, conv_dim), BF16), jax.ShapeDtypeStruct((D_CONV, conv_dim), F32),
                   jax.ShapeDtypeStruct((1, conv_dim), F32)],
        compiler_params=_params(("parallel",)),
    )(zx, conv_w, conv_b.reshape(1, conv_dim), dxbc)


def dt_fwd(name, zx, bias_pad, zx_cols):
    rows = zx.shape[0]
    tr = ROW_TILE
    off = zx_cols // LANES

    def body(r_ref, b_ref, o_ref):
        v = r_ref[...] + b_ref[...]
        sp = jnp.maximum(v, 0.0) + jnp.log1p(jnp.exp(-jnp.abs(v)))
        o_ref[...] = jnp.where(_row_mask(v.shape, pl.program_id(0) * tr), sp, 0.0)

    return pl.pallas_call(
        body, name=name, grid=(rows // tr,),
        in_specs=[pl.BlockSpec((tr, LANES), lambda i: (i, off)), pl.BlockSpec((1, LANES), lambda i: (0, 0))],
        out_specs=pl.BlockSpec((tr, LANES), lambda i: (i, 0)),
        out_shape=jax.ShapeDtypeStruct((rows, LANES), F32), compiler_params=_params(("parallel",)),
    )(zx, bias_pad)


def dt_bwd(name, zx, bias_pad, ddt, zx_cols):
    rows = zx.shape[0]
    tr = ROW_TILE
    off = zx_cols // LANES

    def body(r_ref, b_ref, d_ref, o_ref, db_ref):
        v = r_ref[...] + b_ref[...]
        g = jnp.where(_row_mask(v.shape, pl.program_id(0) * tr), d_ref[...] * _sigmoid(v), 0.0)
        o_ref[...] = g.astype(o_ref.dtype)

        @pl.when(pl.program_id(0) == 0)
        def _():
            db_ref[...] = jnp.zeros_like(db_ref)

        db_ref[...] += jnp.sum(g, axis=0, keepdims=True)

    return pl.pallas_call(
        body, name=name, grid=(rows // tr,),
        in_specs=[pl.BlockSpec((tr, LANES), lambda i: (i, off)), pl.BlockSpec((1, LANES), lambda i: (0, 0)),
                  pl.BlockSpec((tr, LANES), lambda i: (i, 0))],
        out_specs=[pl.BlockSpec((tr, LANES), lambda i: (i, 0)), pl.BlockSpec((1, LANES), lambda i: (0, 0))],
        out_shape=[jax.ShapeDtypeStruct((rows, LANES), BF16), jax.ShapeDtypeStruct((1, LANES), F32)],
        compiler_params=_params(("arbitrary",)),
    )(zx, bias_pad, ddt)


def _split3(x):
    h1 = x.astype(BF16)
    r1 = x - h1.astype(F32)
    h2 = r1.astype(BF16)
    h3 = (r1 - h2.astype(F32)).astype(BF16)
    return h1, h2, h3


def _exact_left(ones_b, x):
    h1, h2, h3 = _split3(x)
    return _dot(ones_b, h1, 1, 0) + _dot(ones_b, h2, 1, 0) + _dot(ones_b, h3, 1, 0)


def _exact_right_t(x, ones_b):
    h1, h2, h3 = _split3(x)
    return _dot(h1, ones_b, 1, 1) + _dot(h2, ones_b, 1, 1) + _dot(h3, ones_b, 1, 1)


class _ScanCommon:
    def __init__(self, b_ref, c_ref, dtc_ref, dtr_ref, arow_ref, acol_ref, drow_ref):
        q = CHUNK
        self.bb = b_ref[...].astype(BF16)
        self.cb = c_ref[...].astype(BF16)
        ri = lax.broadcasted_iota(jnp.int32, (q, q), 0)
        cj = lax.broadcasted_iota(jnp.int32, (q, q), 1)
        self.lower = ri >= cj
        self.upper = cj >= ri
        self.dtc = dtc_ref[...]
        self.dtr = dtr_ref[...]
        self.arow = arow_ref[...]
        self.acol = acol_ref[...]
        self.drow = drow_ref[...]
        da_col = self.dtc * self.arow
        self.a_col = _exact_left(self.lower.astype(BF16), da_col)
        self.a_row = _exact_right_t(self.dtr * self.acol, self.lower.astype(BF16))
        self.a_last = jnp.sum(da_col, axis=0, keepdims=True)
        self.lane_q = lax.broadcasted_iota(jnp.int32, (q, LANES), 1)
        self.lane_1 = lax.broadcasted_iota(jnp.int32, (1, LANES), 1)
        self.sub_8 = lax.broadcasted_iota(jnp.int32, (SUBLANES, q), 0)
        self.first_half = self.lane_q < HEAD_DIM
        self.first_rows = lax.broadcasted_iota(jnp.int32, (LANES, 1), 0) < HEAD_DIM

    def col(self, v, r):
        return jnp.sum(jnp.where(self.lane_q == r, v, 0.0), axis=1, keepdims=True)

    def row(self, v, r):
        return jnp.sum(jnp.where(self.sub_8 == r, v, 0.0), axis=0, keepdims=True)

    def scalar(self, v, r):
        return jnp.sum(jnp.where(self.lane_1 == r, v, 0.0), axis=1, keepdims=True)

    def pair(self, v0, v1):
        return jnp.where(self.first_half, v0, v1)

    def half_rowsum(self, t, h):
        keep = self.first_half if h == 0 else jnp.logical_not(self.first_half)
        return jnp.sum(jnp.where(keep, t, 0.0), axis=1, keepdims=True)


def ssd_scan_fwd(name, xbc, dt_col, dt_row, a_row, a_col, d_row, d_inner):
    rows = xbc.shape[0]
    q, n, g_cnt = CHUNK, D_STATE, SSD_GROUPS
    nc = rows // q
    rp = d_inner // g_cnt
    n_pairs = rp // LANES
    b_off = d_inner // n

    def body(x_ref, b_ref, c_ref, dtc_ref, dtr_ref, arow_ref, acol_ref, drow_ref, y_ref, sprev_ref, s_ref):
        @pl.when(pl.program_id(1) == 0)
        def _():
            s_ref[...] = jnp.zeros_like(s_ref)

        sprev_ref[...] = s_ref[...]
        k = _ScanCommon(b_ref, c_ref, dtc_ref, dtr_ref, arow_ref, acol_ref, drow_ref)
        cb_mat = _dot(k.cb, k.bb, 1, 1)
        for pr in range(n_pairs):
            sl = slice(pr * LANES, (pr + 1) * LANES)
            heads = (2 * pr, 2 * pr + 1)
            xp = x_ref[:, sl]
            ac = [k.col(k.a_col, r) for r in heads]
            ar = [k.row(k.a_row, r) for r in heads]
            al = [k.scalar(k.a_last, r) for r in heads]
            xd = xp * k.pair(k.col(k.dtc, heads[0]), k.col(k.dtc, heads[1]))
            xdb = xd.astype(BF16)
            ys = []
            for h in range(2):
                lm = jnp.exp(jnp.where(k.lower, ac[h] - ar[h], -jnp.inf))
                ys.append(_dot((cb_mat * lm).astype(BF16), xdb, 1, 0))
            y = jnp.where(k.first_half, ys[0], ys[1])
            sp = s_ref[sl, :]
            y = y + k.pair(jnp.exp(ac[0]), jnp.exp(ac[1])) * _dot(k.cb, sp.astype(BF16), 1, 1)
            y = y + k.pair(k.scalar(k.drow, heads[0]), k.scalar(k.drow, heads[1])) * xp
            y_ref[:, sl] = y
            wb = (xd * k.pair(jnp.exp(al[0] - ac[0]), jnp.exp(al[1] - ac[1]))).astype(BF16)
            decay = jnp.where(k.first_rows, jnp.exp(al[0]), jnp.exp(al[1]))
            s_ref[sl, :] = decay * sp + _dot(wb, k.bb, 0, 0)

    return pl.pallas_call(
        body, name=name, grid=(g_cnt, nc),
        in_specs=[
            pl.BlockSpec((q, rp), lambda g, c: (c, g)),
            pl.BlockSpec((q, n), lambda g, c: (c, b_off + g)),
            pl.BlockSpec((q, n), lambda g, c: (c, b_off + g_cnt + g)),
            pl.BlockSpec((None, q, LANES), lambda g, c: (g, c, 0)),
            pl.BlockSpec((None, SUBLANES, q), lambda g, c: (g, 0, c)),
            pl.BlockSpec((None, 1, LANES), lambda g, c: (g, 0, 0)),
            pl.BlockSpec((None, SUBLANES, 1), lambda g, c: (g, 0, 0)),
            pl.BlockSpec((None, 1, LANES), lambda g, c: (g, 0, 0)),
        ],
        out_specs=[pl.BlockSpec((q, rp), lambda g, c: (c, g)),
                   pl.BlockSpec((None, None, rp, n), lambda g, c: (c, g, 0, 0))],
        out_shape=[jax.ShapeDtypeStruct((rows, d_inner), F32), jax.ShapeDtypeStruct((nc, g_cnt, rp, n), F32)],
        scratch_shapes=[pltpu.VMEM((rp, n), F32)],
        compiler_params=_params(("parallel", "arbitrary")),
    )(xbc, xbc, xbc, dt_col, dt_row, a_row, a_col, d_row)


def ssd_scan_bwd(name, xbc, dt_col, dt_row, a_row, a_col, d_row, sprev, dy, d_inner):
    rows = xbc.shape[0]
    q, n, g_cnt = CHUNK, D_STATE, SSD_GROUPS
    nc = rows // q
    rp = d_inner // g_cnt
    n_pairs = rp // LANES
    b_off = d_inner // n

    def body(x_ref, b_ref, c_ref, dtc_ref, dtr_ref, arow_ref, acol_ref, drow_ref, sprev_ref, dy_ref,
             dx_ref, db_ref, dc_ref, ddtc_ref, ddtr_ref, dar_ref, dac_ref, dd_ref, ds_ref):
        @pl.when(pl.program_id(1) == 0)
        def _():
            ds_ref[...] = jnp.zeros_like(ds_ref)
            dar_ref[...] = jnp.zeros_like(dar_ref)
            dac_ref[...] = jnp.zeros_like(dac_ref)
            dd_ref[...] = jnp.zeros_like(dd_ref)

        k = _ScanCommon(b_ref, c_ref, dtc_ref, dtr_ref, arow_ref, acol_ref, drow_ref)
        cb_mat = _dot(k.cb, k.bb, 1, 1)
        cbt_mat = _dot(k.bb, k.cb, 1, 1)
        d_cb = jnp.zeros((q, q), F32)
        d_b = jnp.zeros((q, n), F32)
        d_c = jnp.zeros((q, n), F32)
        da_col = jnp.zeros((q, LANES), F32)
        da_row = jnp.zeros((SUBLANES, q), F32)
        ddt_x = jnp.zeros((q, LANES), F32)
        d_alast = jnp.zeros((1, LANES), F32)
        d_dskip = jnp.zeros((1, LANES), F32)
        for pr in range(n_pairs):
            sl = slice(pr * LANES, (pr + 1) * LANES)
            heads = (2 * pr, 2 * pr + 1)
            xp = x_ref[:, sl]
            dyp = dy_ref[:, sl]
            dyb = dyp.astype(BF16)
            ac = [k.col(k.a_col, r) for r in heads]
            ar = [k.row(k.a_row, r) for r in heads]
            al = [k.scalar(k.a_last, r) for r in heads]
            dt_p = k.pair(k.col(k.dtc, heads[0]), k.col(k.dtc, heads[1]))
            xd = xp * dt_p
            xdb = xd.astype(BF16)
            sp = sprev_ref[sl, :]
            spb = sp.astype(BF16)
            dsp = ds_ref[sl, :]
            dspb = dsp.astype(BF16)
            dskip_p = k.pair(k.scalar(k.drow, heads[0]), k.scalar(k.drow, heads[1]))
            dxp = dskip_p * dyp
            dd_lane = jnp.sum(dyp * xp, axis=0, keepdims=True)
            e_p = k.pair(jnp.exp(ac[0]), jnp.exp(ac[1]))
            t_off = dyp * (e_p * _dot(k.cb, spb, 1, 1))
            dzb = (e_p * dyp).astype(BF16)
            d_c = d_c + _dot(dzb, spb, 1, 0)
            ds_in = _dot(dzb, k.cb, 0, 0)
            decay = jnp.where(k.first_rows, jnp.exp(al[0]), jnp.exp(al[1]))
            ds_in = ds_in + decay * dsp
            t_state = jnp.sum(dsp * sp, axis=1, keepdims=True) * decay
            dec_p = k.pair(jnp.exp(al[0] - ac[0]), jnp.exp(al[1] - ac[1]))
            dw = _dot(k.bb, dspb, 1, 1)
            d_b = d_b + _dot((xd * dec_p).astype(BF16), dspb, 1, 0)
            dxd = dw * dec_p
            t_dec = dw * xd * dec_p
            dxd_h = []
            for h in range(2):
                r = heads[h]
                keep = k.first_half if h == 0 else jnp.logical_not(k.first_half)
                lm = jnp.exp(jnp.where(k.lower, ac[h] - ar[h], -jnp.inf))
                m_mat = cb_mat * lm
                dm = _dot(jnp.where(keep, dyp, 0.0).astype(BF16), xdb, 1, 1)
                dseg = dm * m_mat
                d_cb = d_cb + dm * lm
                lmt = jnp.exp(jnp.where(k.upper, ar[h] - ac[h], -jnp.inf))
                dxd_h.append(_dot((cbt_mat * lmt).astype(BF16), dyb, 1, 0))
                tdec_h = k.half_rowsum(t_dec, h)
                da_h = k.half_rowsum(t_off, h) - tdec_h + jnp.sum(dseg, axis=1, keepdims=True)
                da_col = da_col + jnp.where(k.lane_q == r, da_h, 0.0)
                da_row = da_row - jnp.where(k.sub_8 == r, jnp.sum(dseg, axis=0, keepdims=True), 0.0)
                keep_rows = k.first_rows if h == 0 else jnp.logical_not(k.first_rows)
                dal_h = jnp.sum(tdec_h, axis=0, keepdims=True) + jnp.sum(jnp.where(keep_rows, t_state, 0.0), axis=0, keepdims=True)
                d_alast = d_alast + jnp.where(k.lane_1 == r, dal_h, 0.0)
                keep_1 = k.lane_1 < HEAD_DIM if h == 0 else k.lane_1 >= HEAD_DIM
                dd_h = jnp.sum(jnp.where(keep_1, dd_lane, 0.0), axis=1, keepdims=True)
                d_dskip = d_dskip + jnp.where(k.lane_1 == r, dd_h, 0.0)
            dxd = dxd + jnp.where(k.first_half, dxd_h[0], dxd_h[1])
            dx_ref[:, sl] = dxp + dt_p * dxd
            t_dt = dxd * xp
            for h in range(2):
                ddt_x = ddt_x + jnp.where(k.lane_q == heads[h], k.half_rowsum(t_dt, h), 0.0)
            ds_ref[sl, :] = ds_in
        d_cb_b = d_cb.astype(BF16)
        db_ref[...] = d_b + _dot(d_cb_b, k.cb, 0, 0)
        dc_ref[...] = d_c + _dot(d_cb_b, k.bb, 1, 0)
        rc_col = _exact_left(k.upper.astype(BF16), da_col) + d_alast
        rc_row = _exact_right_t(da_row, k.upper.astype(BF16))
        ddtc_ref[...] = ddt_x + k.arow * rc_col
        ddtr_ref[...] = k.acol * rc_row
        dar_ref[...] += jnp.sum(rc_col * k.dtc, axis=0, keepdims=True)
        dac_ref[...] += jnp.sum(rc_row * k.dtr, axis=1, keepdims=True)
        dd_ref[...] += d_dskip

    rc = lambda c: nc - 1 - c
    return pl.pallas_call(
        body, name=name, grid=(g_cnt, nc),
        in_specs=[
            pl.BlockSpec((q, rp), lambda g, c: (rc(c), g)),
            pl.BlockSpec((q, n), lambda g, c: (rc(c), b_off + g)),
            pl.BlockSpec((q, n), lambda g, c: (rc(c), b_off + g_cnt + g)),
            pl.BlockSpec((None, q, LANES), lambda g, c: (g, rc(c), 0)),
            pl.BlockSpec((None, SUBLANES, q), lambda g, c: (g, 0, rc(c))),
            pl.BlockSpec((None, 1, LANES), lambda g, c: (g, 0, 0)),
            pl.BlockSpec((None, SUBLANES, 1), lambda g, c: (g, 0, 0)),
            pl.BlockSpec((None, 1, LANES), lambda g, c: (g, 0, 0)),
            pl.BlockSpec((None, None, rp, n), lambda g, c: (rc(c), g, 0, 0)),
            pl.BlockSpec((q, rp), lambda g, c: (rc(c), g)),
        ],
        out_specs=[
            pl.BlockSpec((q, rp), lambda g, c: (rc(c), g)),
            pl.BlockSpec((q, n), lambda g, c: (rc(c), g)),
            pl.BlockSpec((q, n), lambda g, c: (rc(c), g)),
            pl.BlockSpec((None, q, LANES), lambda g, c: (g, rc(c), 0)),
            pl.BlockSpec((None, SUBLANES, q), lambda g, c: (g, 0, rc(c))),
            pl.BlockSpec((None, 1, LANES), lambda g, c: (g, 0, 0)),
            pl.BlockSpec((None, SUBLANES, 1), lambda g, c: (g, 0, 0)),
            pl.BlockSpec((None, 1, LANES), lambda g, c: (g, 0, 0)),
        ],
        out_shape=[
            jax.ShapeDtypeStruct((rows, d_inner), F32),
            jax.ShapeDtypeStruct((rows, g_cnt * n), F32),
            jax.ShapeDtypeStruct((rows, g_cnt * n), F32),
            jax.ShapeDtypeStruct((g_cnt, rows, LANES), F32),
            jax.ShapeDtypeStruct((g_cnt, SUBLANES, rows), F32),
            jax.ShapeDtypeStruct((g_cnt, 1, LANES), F32),
            jax.ShapeDtypeStruct((g_cnt, SUBLANES, 1), F32),
            jax.ShapeDtypeStruct((g_cnt, 1, LANES), F32),
        ],
        scratch_shapes=[pltpu.VMEM((rp, n), F32)],
        compiler_params=_params(("parallel", "arbitrary")),
    )(xbc, xbc, xbc, dt_col, dt_row, a_row, a_col, d_row, sprev, dy)


def gatenorm_fwd(name, y, zx, w, d_inner):
    rows = y.shape[0]
    tr, gw = ROW_TILE, d_inner // SSD_GROUPS

    def body(y_ref, z_ref, w_ref, o_ref):
        z = z_ref[...]
        v = y_ref[...] * (z * _sigmoid(z))
        o_ref[...] = (v * lax.rsqrt(jnp.mean(v * v, axis=-1, keepdims=True) + EPS) * w_ref[...]).astype(o_ref.dtype)

    blk = pl.BlockSpec((tr, gw), lambda i, j: (i, j))
    return pl.pallas_call(
        body, name=name, grid=(rows // tr, SSD_GROUPS),
        in_specs=[blk, blk, pl.BlockSpec((1, gw), lambda i, j: (0, j))], out_specs=blk,
        out_shape=jax.ShapeDtypeStruct((rows, d_inner), BF16), compiler_params=_params(("parallel", "parallel")),
    )(y, zx, w.reshape(1, d_inner))


def gatenorm_bwd(name, y, zx, w, dyn, d_inner):
    rows = y.shape[0]
    tr, gw = ROW_TILE, d_inner // SSD_GROUPS

    def body(y_ref, z_ref, w_ref, dn_ref, dy_ref, dz_ref, dw_ref):
        z = z_ref[...]
        yv = y_ref[...]
        s = _sigmoid(z)
        gate = z * s
        v = yv * gate
        r = lax.rsqrt(jnp.mean(v * v, axis=-1, keepdims=True) + EPS)
        vh = v * r
        dn = dn_ref[...]
        dvh = dn * w_ref[...]
        dv = r * (dvh - vh * jnp.mean(dvh * vh, axis=-1, keepdims=True))
        dy_ref[...] = dv * gate
        dz_ref[...] = (dv * yv * (s * (1.0 + z * (1.0 - s)))).astype(dz_ref.dtype)

        @pl.when(pl.program_id(1) == 0)
        def _():
            dw_ref[...] = jnp.zeros_like(dw_ref)

        dw_ref[...] += jnp.sum(dn * vh, axis=0, keepdims=True)

    blk = pl.BlockSpec((tr, gw), lambda j, i: (i, j))
    wblk = pl.BlockSpec((1, gw), lambda j, i: (0, j))
    dy, dz, dw = pl.pallas_call(
        body, name=name, grid=(SSD_GROUPS, rows // tr),
        in_specs=[blk, blk, wblk, blk], out_specs=[blk, blk, wblk],
        out_shape=[jax.ShapeDtypeStruct((rows, d_inner), F32), jax.ShapeDtypeStruct((rows, d_inner), BF16),
                   jax.ShapeDtypeStruct((1, d_inner), F32)],
        compiler_params=_params(("parallel", "arbitrary")),
    )(y, zx, w.reshape(1, d_inner), dyn)
    return dy, dz, dw.reshape(d_inner)


def _pool_count(rows, g):
    t1 = lax.broadcasted_iota(jnp.int32, (rows, 1), 0) - (PAD_FRONT - 1)
    win = jnp.left_shift(jnp.int32(POOL_WINDOWS[0]), g)
    return jnp.clip(t1, 1, win).astype(F32)


def _pool_select(levels, g):
    out = levels[-1]
    for i in range(len(levels) - 2, -1, -1):
        out = jnp.where(g == i, levels[i], out)
    return out


def pool_sub(name, u, transpose):
    rows, d = u.shape
    gd = d // len(POOL_WINDOWS)
    assert all(w == POOL_WINDOWS[0] << i for i, w in enumerate(POOL_WINDOWS))

    def body(u_ref, o_ref):
        g = pl.program_id(0)
        v = u_ref[...].astype(F32)
        cnt = _pool_count(rows, g)
        mask = _row_mask(v.shape, 0)
        s = v / cnt if transpose else v
        levels = []
        for i in range(len(POOL_WINDOWS)):
            step = 1 << i
            s = s + pltpu.roll(s, (rows - step) if transpose else step, axis=0)
            levels.append(s)
        sel = _pool_select(levels, g)
        out = (sel - v) if transpose else (sel / cnt - v)
        o_ref[...] = jnp.where(mask, out, 0.0).astype(o_ref.dtype)

    tc = POOL_TILE
    per = gd // tc
    blk = pl.BlockSpec((rows, tc), lambda g, j: (0, g * per + j))
    return pl.pallas_call(
        body, name=name, grid=(len(POOL_WINDOWS), per), in_specs=[blk], out_specs=blk,
        out_shape=jax.ShapeDtypeStruct((rows, d), F32 if transpose else BF16),
        compiler_params=_params(("parallel", "parallel")),
    )(u)


def pool_proj_fwd(name, mixed, w, b, scale):
    rows, d = mixed.shape
    ng = len(POOL_WINDOWS)
    gd = d // ng
    tr = ROW_TILE

    def body(m_ref, w_ref, b_ref, s_ref, pre_ref, mix_ref):
        pre = _dot(m_ref[...], w_ref[...], 1, 0) + b_ref[...]
        pre_ref[...] = pre
        mix_ref[...] = jnp.where(_row_mask(pre.shape, pl.program_id(1) * tr), pre * s_ref[...], 0.0)

    blk = pl.BlockSpec((tr, gd), lambda g, i: (i, g))
    vec = pl.BlockSpec((1, gd), lambda g, i: (0, g))
    return pl.pallas_call(
        body, name=name, grid=(ng, rows // tr),
        in_specs=[blk, pl.BlockSpec((None, gd, gd), lambda g, i: (g, 0, 0)), vec, vec], out_specs=[blk, blk],
        out_shape=[jax.ShapeDtypeStruct((rows, d), F32), jax.ShapeDtypeStruct((rows, d), F32)],
        compiler_params=_params(("parallel", "parallel")),
    )(mixed, w, b.reshape(1, d), scale.reshape(1, d))


def pool_proj_bwd(name, dmix, pre, mixed, w, scale):
    rows, d = dmix.shape
    ng = len(POOL_WINDOWS)
    gd = d // ng
    tr = ROW_TILE

    def body(dm_ref, pre_ref, mx_ref, w_ref, s_ref, dmx_ref, dw_ref, db_ref, ds_ref):
        @pl.when(pl.program_id(1) == 0)
        def _():
            dw_ref[...] = jnp.zeros_like(dw_ref)
            db_ref[...] = jnp.zeros_like(db_ref)
            ds_ref[...] = jnp.zeros_like(ds_ref)

        dmv = jnp.where(_row_mask(dm_ref.shape, pl.program_id(1) * tr), dm_ref[...], 0.0)
        dpre = dmv * s_ref[...]
        dpre_b = dpre.astype(BF16)
        ds_ref[...] += jnp.sum(dmv * pre_ref[...], axis=0, keepdims=True)
        db_ref[...] += jnp.sum(dpre, axis=0, keepdims=True)
        dmx_ref[...] = _dot(dpre_b, w_ref[...], 1, 1)
        dw_ref[...] += _dot(mx_ref[...], dpre_b, 0, 0)

    blk = pl.BlockSpec((tr, gd), lambda g, i: (i, g))
    vec = pl.BlockSpec((1, gd), lambda g, i: (0, g))
    wblk = pl.BlockSpec((None, gd, gd), lambda g, i: (g, 0, 0))
    dmixed, dw, db, ds = pl.pallas_call(
        body, name=name, grid=(ng, rows // tr),
        in_specs=[blk, blk, blk, wblk, vec], out_specs=[blk, wblk, vec, vec],
        out_shape=[jax.ShapeDtypeStruct((rows, d), F32), jax.ShapeDtypeStruct((ng, gd, gd), F32),
                   jax.ShapeDtypeStruct((1, d), F32), jax.ShapeDtypeStruct((1, d), F32)],
        compiler_params=_params(("parallel", "arbitrary")),
    )(dmix, pre, mixed, w, scale.reshape(1, d))
    return dmixed, dw, db.reshape(d), ds.reshape(d)


def _my_place():
    return lax.axis_index("x"), lax.axis_index("y"), lax.axis_index("c")


def _linear(place):
    return 4 * place[0] + 2 * place[1] + place[2]


def all_gather(name, shards):
    n_ops = len(shards)

    def body(*refs):
        ins, outs = refs[:n_ops], refs[n_ops:2 * n_ops]
        send_sems, recv_sems, local_sems = refs[2 * n_ops:]
        x, y, c = _my_place()
        me, sibling = (x, y, c), (x, y, 1 - c)
        chips = [(1 - x, y), (x, 1 - y), (1 - x, 1 - y)]

        def copy(t, k, block, to, src=None):
            dst = outs[t].at[_linear(block)]
            return pltpu.make_async_remote_copy(
                src_ref=dst if src is None else src, dst_ref=dst, send_sem=send_sems.at[t, k], recv_sem=recv_sems.at[t, k],
                device_id=to, device_id_type=MESH)

        mine = [pltpu.make_async_copy(ins[t], outs[t].at[_linear(me)], local_sems.at[t]) for t in range(n_ops)]
        for cp in mine:
            cp.start()
        first = []
        for t in range(n_ops):
            first.append(copy(t, 0, me, sibling, src=ins[t]))
            first += [copy(t, 1 + j, me, (*chip, c), src=ins[t]) for j, chip in enumerate(chips)]
        for cp in first:
            cp.start()
        passed = []
        for j, chip in enumerate(chips):
            for t in range(n_ops):
                copy(t, 1 + j, (*chip, c), me).wait_recv()
                fwd = copy(t, 4 + j, (*chip, c), sibling)
                fwd.start()
                passed.append(fwd)
        for t in range(n_ops):
            copy(t, 0, sibling, me).wait_recv()
            for j, chip in enumerate(chips):
                copy(t, 4 + j, (*chip, 1 - c), me).wait_recv()
        for cp in first + passed:
            cp.wait_send()
        for cp in mine:
            cp.wait()

    any_spec = pl.BlockSpec(memory_space=pl.ANY)
    outs = pl.pallas_call(
        body, name=name,
        in_specs=[any_spec] * n_ops, out_specs=[any_spec] * n_ops,
        out_shape=[jax.ShapeDtypeStruct((N_DEV, *s.shape), s.dtype) for s in shards],
        scratch_shapes=[pltpu.SemaphoreType.DMA((n_ops, 7)), pltpu.SemaphoreType.DMA((n_ops, 7)),
                        pltpu.SemaphoreType.DMA((n_ops,))],
    )(*shards)
    return list(outs)


def scatter_blocks(name, blocked):
    n_ops = len(blocked)

    def body(*refs):
        ins, outs = refs[:n_ops], refs[n_ops:2 * n_ops]
        send_sems, recv_sems, local_sems = refs[2 * n_ops:]
        x, y, c = _my_place()
        me = _linear((x, y, c))
        peers = [(x ^ (j >> 2), y ^ ((j >> 1) & 1), c ^ (j & 1)) for j in range(1, N_DEV)]

        def copy(t, j, peer):
            return pltpu.make_async_remote_copy(
                src_ref=ins[t].at[_linear(peer)], dst_ref=outs[t].at[me], send_sem=send_sems.at[t, j],
                recv_sem=recv_sems.at[t, j], device_id=peer, device_id_type=MESH)

        def arrival(t, j, peer):
            return pltpu.make_async_remote_copy(
                src_ref=ins[t].at[me], dst_ref=outs[t].at[_linear(peer)], send_sem=send_sems.at[t, j],
                recv_sem=recv_sems.at[t, j], device_id=peer, device_id_type=MESH)

        mine = [pltpu.make_async_copy(ins[t].at[me], outs[t].at[me], local_sems.at[t]) for t in range(n_ops)]
        for cp in mine:
            cp.start()
        sends = [copy(t, j, peer) for t in range(n_ops) for j, peer in enumerate(peers)]
        for cp in sends:
            cp.start()
        for t in range(n_ops):
            for j, peer in enumerate(peers):
                arrival(t, j, peer).wait_recv()
        for cp in sends:
            cp.wait_send()
        for cp in mine:
            cp.wait()

    any_spec = pl.BlockSpec(memory_space=pl.ANY)
    outs = pl.pallas_call(
        body, name=name,
        in_specs=[any_spec] * n_ops, out_specs=[any_spec] * n_ops,
        out_shape=[jax.ShapeDtypeStruct(b.shape, b.dtype) for b in blocked],
        scratch_shapes=[pltpu.SemaphoreType.DMA((n_ops, 7)), pltpu.SemaphoreType.DMA((n_ops, 7)),
                        pltpu.SemaphoreType.DMA((n_ops,))],
    )(*blocked)
    return list(outs)


def _as_rows(a, lead=0):
    return a.reshape(a.shape[:lead] + (-1, a.shape[-1]))


def sum_adamw(name, parts, w, m, v):
    shape = w.shape
    p2, w2, m2, v2 = _as_rows(parts, 1), _as_rows(w), _as_rows(m), _as_rows(v)
    rows, cols = w2.shape
    tr = _tile(rows, ROW_TILE, 16)
    c1 = 1.0 - ADAM_B1 ** ADAM_STEP
    c2 = 1.0 - ADAM_B2 ** ADAM_STEP

    def body(p_ref, w_ref, m_ref, v_ref, g_ref, d_ref, nm_ref, nv_ref):
        g = p_ref[0].astype(F32)
        for k in range(1, N_DEV):
            g = g + p_ref[k].astype(F32)
        wv = w_ref[...]
        nm = ADAM_B1 * m_ref[...] + (1.0 - ADAM_B1) * g
        nv = ADAM_B2 * v_ref[...] + (1.0 - ADAM_B2) * (g * g)
        g_ref[...] = g
        nm_ref[...] = nm
        nv_ref[...] = nv
        d_ref[...] = -ADAM_LR * ((nm / c1) / (jnp.sqrt(nv / c2) + ADAM_EPS) + ADAM_WD * wv)

    blk = pl.BlockSpec((tr, cols), lambda i: (i, 0))
    outs = pl.pallas_call(
        body, name=name, grid=(rows // tr,),
        in_specs=[pl.BlockSpec((N_DEV, tr, cols), lambda i: (0, i, 0)), blk, blk, blk], out_specs=[blk] * 4,
        out_shape=[jax.ShapeDtypeStruct((rows, cols), F32)] * 4, compiler_params=_params(("parallel",)),
    )(p2, w2, m2, v2)
    return [o.reshape(shape) for o in outs]


def _unblock_cols(g):
    g = jnp.moveaxis(g, 0, -2)
    return g.reshape(g.shape[:-2] + (g.shape[-2] * g.shape[-1],))


def _block_cols(a):
    r, c = a.shape
    return jnp.moveaxis(a.reshape(r, N_DEV, c // N_DEV), 1, 0)


def _my_cols(a, n):
    me = _linear(_my_place())
    return lax.dynamic_slice_in_dim(a, me * n, n, axis=a.ndim - 1)


def _pack(arrays):
    flat = jnp.concatenate([a.reshape(-1).astype(F32) for a in arrays])
    pad = (-flat.shape[0]) % (ROW_TILE * LANES)
    return jnp.pad(flat, (0, pad)).reshape(-1, LANES)


def _unpack(packed, like):
    flat = packed.reshape(-1)
    out, pos = [], 0
    for a in like:
        out.append(flat[pos:pos + a.size].reshape(a.shape))
        pos += a.size
    return out


def kernel(x, meta_tokens, norm_w, ssd_w_in, ssd_conv_w, ssd_conv_b, ssd_dt_bias, ssd_a_log, ssd_d, ssd_norm_w, ssd_w_out, pool_w, pool_b, pool_scale, ffn_w_gate, ffn_w_up, ffn_w_down, loss_target, m_meta_tokens, m_norm_w, m_ssd_w_in, m_ssd_conv_w, m_ssd_conv_b, m_ssd_dt_bias, m_ssd_a_log, m_ssd_d, m_ssd_norm_w, m_ssd_w_out, m_pool_w, m_pool_b, m_pool_scale, m_ffn_w_gate, m_ffn_w_up, m_ffn_w_down, v_meta_tokens, v_norm_w, v_ssd_w_in, v_ssd_conv_w, v_ssd_conv_b, v_ssd_dt_bias, v_ssd_a_log, v_ssd_d, v_ssd_norm_w, v_ssd_w_out, v_pool_w, v_pool_b, v_pool_scale, v_ffn_w_gate, v_ffn_w_up, v_ffn_w_down):
    seq, d = x.shape[1], x.shape[2]
    depth = norm_w.shape[0]
    n_ssd = ssd_w_in.shape[0]
    d_inner = ssd_norm_w.shape[1]
    heads = d_inner // HEAD_DIM
    rpg = heads // SSD_GROUPS
    conv_dim = ssd_conv_b.shape[1]
    zx_cols = d_inner + conv_dim
    d_in_proj = zx_cols + heads
    rows = PAD_FRONT + N_META + seq
    assert rows % CHUNK == 0 and rpg % 2 == 0 and heads <= LANES and rpg <= SUBLANES
    sd = d // N_DEV

    small = all_gather("gather_small", [meta_tokens, norm_w, ssd_conv_w, pool_b, pool_scale])
    meta_f, norm_f, convw_f, poolb_f, pools_f = [_unblock_cols(s) for s in small]
    big = all_gather("gather_big", [a.astype(BF16) for a in (ssd_w_in, ssd_w_out, pool_w, ffn_w_gate, ffn_w_up, ffn_w_down)])
    win_g, wout_g, poolw_g, wg_g, wu_g, wd_g = big
    w_in = jnp.pad(_unblock_cols(win_g), ((0, 0), (0, 0), (0, LANES - heads)))
    w_out = jnp.moveaxis(wout_g, 0, 1).reshape(n_ssd, d_inner, d)
    w_pool = jnp.moveaxis(poolw_g, 0, 2).reshape(pool_w.shape[0], len(POOL_WINDOWS), d // len(POOL_WINDOWS), -1)
    w_gu = jnp.concatenate([_unblock_cols(wg_g), _unblock_cols(wu_g)], axis=-1)
    w_down = jnp.moveaxis(wd_g, 0, 1).reshape(depth, -1, d)
    hidden = w_down.shape[1]

    pad_h = lambda a: jnp.pad(a.astype(F32), ((0, 0), (0, LANES - heads)))
    bias_pad = pad_h(ssd_dt_bias)

    def head_layouts(vec):
        g = vec.reshape(SSD_GROUPS, rpg)
        row = jnp.pad(g, ((0, 0), (0, LANES - rpg)))[:, None, :]
        col = jnp.pad(g, ((0, 0), (0, SUBLANES - rpg)))[:, :, None]
        return row, col

    def dt_layouts(dt):
        g = dt[:, :heads].reshape(rows, SSD_GROUPS, rpg)
        col = jnp.pad(jnp.moveaxis(g, 1, 0), ((0, 0), (0, 0), (0, LANES - rpg)))
        row = jnp.pad(jnp.transpose(g, (1, 2, 0)), ((0, 0), (0, SUBLANES - rpg), (0, 0)))
        return col, row

    h = jnp.concatenate([jnp.zeros((PAD_FRONT, d), F32), meta_f, x[0]], axis=0)
    saved = []
    for i in range(depth):
        j = i // 2
        s = {"h": h}
        if i % 2 == 0:
            u = rmsnorm_fwd(f"norm_pre_mix_{i}", h, norm_f[i, 0], out_dtype=BF16)
            zx = matmul(f"ssd_in_{i}", u, w_in[j], tn=384 if w_in.shape[2] % 384 == 0 else 512)
            xbc = conv_fwd(f"ssd_conv_{i}", zx, convw_f[j], ssd_conv_b[j], d_inner, conv_dim)
            dt = dt_fwd(f"ssd_dt_{i}", zx, bias_pad[j:j + 1], zx_cols)
            dt_col, dt_row = dt_layouts(dt)
            a_neg = -jnp.exp(ssd_a_log[j].astype(F32))
            a_row, a_col = head_layouts(a_neg)
            d_row, _ = head_layouts(ssd_d[j].astype(F32))
            y, sprev = ssd_scan_fwd(f"ssd_scan_{i}", xbc, dt_col, dt_row, a_row, a_col, d_row, d_inner)
            yn = gatenorm_fwd(f"ssd_gate_{i}", y, zx, ssd_norm_w[j], d_inner)
            mix = matmul(f"ssd_out_{i}", yn, w_out[j])
            s.update(u=u, zx=zx, xbc=xbc, dt_col=dt_col, dt_row=dt_row, a_row=a_row, a_col=a_col, d_row=d_row,
                     a_neg=a_neg, y=y, sprev=sprev, yn=yn)
        else:
            u = rmsnorm_fwd(f"norm_pre_mix_{i}", h, norm_f[i, 0])
            mixed = pool_sub(f"pool_sub_{i}", u, False)
            pre, mix = pool_proj_fwd(f"pool_proj_{i}", mixed, w_pool[j], poolb_f[j], pools_f[j])
            s.update(mixed=mixed, pre=pre)
        h1 = rmsnorm_fwd(f"norm_post_mix_{i}", mix, norm_f[i, 1], res=h)
        u2 = rmsnorm_fwd(f"norm_pre_ffn_{i}", h1, norm_f[i, 2], out_dtype=BF16)
        gp = matmul(f"ffn_in_{i}", u2, w_gu[i])
        act = swiglu_fwd(f"ffn_act_{i}", gp)
        f = matmul(f"ffn_out_{i}", act, w_down[i], tk=1408 if hidden % 1408 == 0 else 2048)
        h = rmsnorm_fwd(f"norm_post_ffn_{i}", f, norm_f[i, 3], res=h1)
        s.update(mix=mix, h1=h1, u2=u2, gp=gp, act=act, f=f)
        saved.append(s)

    dh, loss_local = loss_head("loss_head", h, loss_target[0])
    loss = lax.psum(loss_local, AXES)

    g_norm = [[None] * 4 for _ in range(depth)]
    g_gu, g_down = [None] * depth, [None] * depth
    g_in, g_out, g_convw, g_convb, g_dtb, g_alog, g_dskip, g_ssdnorm = ([None] * n_ssd for _ in range(8))
    g_poolw, g_poolb, g_pools = ([None] * pool_w.shape[0] for _ in range(3))
    for i in reversed(range(depth)):
        j = i // 2
        s = saved[i]
        df, g_norm[i][3] = rmsnorm_bwd(f"norm_post_ffn_bwd_{i}", s["f"], norm_f[i, 3], dh)
        df_b = df.astype(BF16)
        dact = matmul(f"ffn_out_bwd_x_{i}", df_b, w_down[i], tb=True)
        g_down[i] = matmul(f"ffn_out_bwd_w_{i}", s["act"], df_b, ta=True, out_dtype=BF16, tk=1152)
        dgp = swiglu_bwd(f"ffn_act_bwd_{i}", s["gp"], dact)
        g_gu[i] = matmul(f"ffn_in_bwd_w_{i}", s["u2"], dgp, ta=True, out_dtype=BF16, tk=1152)
        du2 = matmul(f"ffn_in_bwd_x_{i}", dgp, w_gu[i], tb=True, tk=1408 if hidden % 1408 == 0 else 2048)
        dh1, g_norm[i][2] = rmsnorm_bwd(f"norm_pre_ffn_bwd_{i}", s["h1"], norm_f[i, 2], du2, add=dh)
        dmix, g_norm[i][1] = rmsnorm_bwd(f"norm_post_mix_bwd_{i}", s["mix"], norm_f[i, 1], dh1)
        if i % 2 == 0:
            dmix_b = dmix.astype(BF16)
            dyn = matmul(f"ssd_out_bwd_x_{i}", dmix_b, w_out[j], tb=True)
            g_out[j] = matmul(f"ssd_out_bwd_w_{i}", s["yn"], dmix_b, ta=True, out_dtype=BF16, tk=1152)
            dy, dz, g_ssdnorm[j] = gatenorm_bwd(f"ssd_gate_bwd_{i}", s["y"], s["zx"], ssd_norm_w[j], dyn, d_inner)
            dx, db, dc, ddt_col, ddt_row, dar, dac, ddsk = ssd_scan_bwd(
                f"ssd_scan_bwd_{i}", s["xbc"], s["dt_col"], s["dt_row"], s["a_row"], s["a_col"], s["d_row"], s["sprev"], dy, d_inner)
            ddt = (jnp.moveaxis(ddt_col[:, :, :rpg], 0, 1).reshape(rows, heads)
                   + jnp.transpose(ddt_row[:, :rpg, :], (2, 0, 1)).reshape(rows, heads))
            d_a = (dar[:, 0, :rpg] + dac[:, :rpg, 0]).reshape(heads)
            g_alog[j] = d_a * s["a_neg"]
            g_dskip[j] = ddsk[:, 0, :rpg].reshape(heads)
            ddtr, dbias = dt_bwd(f"ssd_dt_bwd_{i}", s["zx"], bias_pad[j:j + 1], pad_h(ddt), zx_cols)
            g_dtb[j] = dbias[0, :heads]
            dxbc_raw, g_convw[j], dconvb = conv_bwd(
                f"ssd_conv_bwd_{i}", s["zx"], convw_f[j], ssd_conv_b[j], jnp.concatenate([dx, db, dc], axis=1), d_inner, conv_dim)
            g_convb[j] = dconvb[0]
            dzx = jnp.concatenate([dz, dxbc_raw, ddtr], axis=1)
            g_in[j] = matmul(f"ssd_in_bwd_w_{i}", s["u"], dzx, ta=True, out_dtype=BF16, tk=1152,
                             tn=384 if dzx.shape[1] % 384 == 0 else 512)
            du = matmul(f"ssd_in_bwd_x_{i}", dzx, w_in[j], tb=True, tk=1152 if dzx.shape[1] % 1152 == 0 else 2048)
        else:
            dmixed, g_poolw[j], g_poolb[j], g_pools[j] = pool_proj_bwd(
                f"pool_proj_bwd_{i}", dmix, s["pre"], s["mixed"], w_pool[j], pools_f[j])
            du = pool_sub(f"pool_sub_bwd_{i}", dmixed, True)
        dh, g_norm[i][0] = rmsnorm_bwd(f"norm_pre_mix_bwd_{i}", s["h"], norm_f[i, 0], du, add=dh1)

    grad_x = dh[PAD_FRONT + N_META:][None]
    g_meta = dh[PAD_FRONT:PAD_FRONT + N_META]

    small_grads = [g_meta, jnp.stack([jnp.stack(r) for r in g_norm]), jnp.stack(g_convw), jnp.stack(g_convb), jnp.stack(g_dtb),
                   jnp.stack(g_alog), jnp.stack(g_dskip), jnp.stack(g_ssdnorm), jnp.stack(g_poolb), jnp.stack(g_pools)]
    (small_parts,) = all_gather("gather_small_grads", [_pack(small_grads)])
    mine = lambda a: _my_cols(a, a.shape[-1] // N_DEV)
    small_w = [meta_tokens, norm_w, ssd_conv_w, ssd_conv_b, ssd_dt_bias, ssd_a_log, ssd_d, ssd_norm_w, pool_b, pool_scale]
    small_m = [m_meta_tokens, m_norm_w, m_ssd_conv_w, m_ssd_conv_b, m_ssd_dt_bias, m_ssd_a_log, m_ssd_d, m_ssd_norm_w, m_pool_b, m_pool_scale]
    small_v = [v_meta_tokens, v_norm_w, v_ssd_conv_w, v_ssd_conv_b, v_ssd_dt_bias, v_ssd_a_log, v_ssd_d, v_ssd_norm_w, v_pool_b, v_pool_scale]
    sharded = [True, True, True, False, False, False, False, False, True, True]
    def widen(a, is_sharded, full):
        if not is_sharded:
            return a
        return lax.dynamic_update_slice_in_dim(jnp.zeros(full.shape, F32), a, _linear(_my_place()) * a.shape[-1], axis=a.ndim - 1)
    packed_w = _pack([widen(a, sh, g) for a, sh, g in zip(small_w, sharded, small_grads)])
    packed_m = _pack([widen(a, sh, g) for a, sh, g in zip(small_m, sharded, small_grads)])
    packed_v = _pack([widen(a, sh, g) for a, sh, g in zip(small_v, sharded, small_grads)])
    sm = [_unpack(o, small_grads) for o in sum_adamw("adamw_small", small_parts, packed_w, packed_m, packed_v)]
    sm = [[mine(a) if sh else a for a, sh in zip(group, sharded)] for group in sm]

    hid_s = hidden // N_DEV
    gu = jnp.stack(g_gu)
    blk_gate = jnp.moveaxis(gu[:, :, :hidden].reshape(depth, d, N_DEV, hid_s), 2, 0)
    blk_up = jnp.moveaxis(gu[:, :, hidden:].reshape(depth, d, N_DEV, hid_s), 2, 0)
    blk_down = jnp.moveaxis(jnp.stack(g_down).reshape(depth, N_DEV, hid_s, d), 1, 0)
    gin = jnp.stack(g_in)[:, :, :d_in_proj]
    blk_in = jnp.moveaxis(gin.reshape(n_ssd, d, N_DEV, d_in_proj // N_DEV), 2, 0)
    blk_out = jnp.moveaxis(jnp.stack(g_out).reshape(n_ssd, N_DEV, d_inner // N_DEV, d), 1, 0)
    gpw = jnp.stack(g_poolw).astype(BF16)
    blk_pool = jnp.moveaxis(gpw.reshape(gpw.shape[0], gpw.shape[1], N_DEV, gpw.shape[2] // N_DEV, gpw.shape[3]), 2, 0)
    parts = scatter_blocks("scatter_grads", [blk_in, blk_out, blk_pool, blk_gate, blk_up, blk_down])
    big_w = [ssd_w_in, ssd_w_out, pool_w, ffn_w_gate, ffn_w_up, ffn_w_down]
    big_m = [m_ssd_w_in, m_ssd_w_out, m_pool_w, m_ffn_w_gate, m_ffn_w_up, m_ffn_w_down]
    big_v = [v_ssd_w_in, v_ssd_w_out, v_pool_w, v_ffn_w_gate, v_ffn_w_up, v_ffn_w_down]
    names = ["ssd_w_in", "ssd_w_out", "pool_w", "ffn_w_gate", "ffn_w_up", "ffn_w_down"]
    bg = [sum_adamw(f"adamw_{nm}", p, w, m, v) for nm, p, w, m, v in zip(names, parts, big_w, big_m, big_v)]

    def ordered(kind):
        s_ = sm[kind]
        b_ = [o[kind] for o in bg]
        return [s_[0], s_[1], b_[0], s_[2], s_[3], s_[4], s_[5], s_[6], s_[7], b_[1], b_[2], s_[8], s_[9], b_[3], b_[4], b_[5]]

    return (loss, grad_x, *ordered(0), *ordered(1), *ordered(2), *ordered(3))
```

```python
import functools

import jax
import jax.numpy as jnp
from jax import lax
from jax.experimental import pallas as pl
from jax.experimental.pallas import tpu as pltpu

F32 = jnp.float32
BF16 = jnp.bfloat16
MESH = pl.DeviceIdType.MESH
AXES = ("x", "y", "c")
N_DEV = 8

N_META = 16
EPS = 1e-6
HEAD_DIM = 64
D_STATE = 128
SSD_GROUPS = 8
D_CONV = 4
CHUNK = 256
POOL_WINDOWS = (2, 4, 8, 16)
ADAM_LR, ADAM_B1, ADAM_B2, ADAM_EPS, ADAM_WD, ADAM_STEP = 0.001, 0.9, 0.999, 1e-08, 0.01, 10

PAD_FRONT = (-N_META) % CHUNK
LANES = 128
SUBLANES = 8
ROW_TILE = 256
CONV_TILE = 256
POOL_TILE = 128
VMEM_LIMIT = 56 * 1024 * 1024


def _params(sem=None):
    return pltpu.CompilerParams(dimension_semantics=sem, vmem_limit_bytes=VMEM_LIMIT)


def _tile(n, target, mult):
    if n <= target:
        return n
    best = None
    for t in range(mult, target + 1, mult):
        if n % t == 0:
            best = t
    assert best is not None, (n, target, mult)
    return best


def _dot(a, b, ca, cb):
    return lax.dot_general(a, b, (((ca,), (cb,)), ((), ())), preferred_element_type=F32)


def _sigmoid(x):
    return 1.0 / (1.0 + jnp.exp(-x))


def _row_mask(shape, first_row):
    rows = lax.broadcasted_iota(jnp.int32, shape, 0) + first_row
    return rows >= PAD_FRONT


def matmul(name, a, b, *, tb=False, out_dtype=F32, tm=768, tn=512, tk=2048):
    m, kdim = a.shape
    if tb:
        n, k2 = b.shape
    else:
        k2, n = b.shape
    assert kdim == k2, (a.shape, b.shape, tb)
    tm = _tile(m, tm, 16)
    tn = _tile(n, tn, LANES)
    tk = _tile(kdim, tk, LANES)
    nk = kdim // tk
    a_spec = pl.BlockSpec((tm, tk), lambda i, j, k: (i, k))
    b_spec = pl.BlockSpec((tn, tk), lambda i, j, k: (j, k)) if tb else pl.BlockSpec((tk, tn), lambda i, j, k: (k, j))

    def body_single(a_ref, b_ref, o_ref):
        o_ref[...] = _dot(a_ref[...], b_ref[...], 1, 1 if tb else 0).astype(o_ref.dtype)

    def body_acc(a_ref, b_ref, o_ref, acc_ref):
        k = pl.program_id(2)

        @pl.when(k == 0)
        def _():
            acc_ref[...] = jnp.zeros_like(acc_ref)

        acc_ref[...] += _dot(a_ref[...], b_ref[...], 1, 1 if tb else 0)

        @pl.when(k == nk - 1)
        def _():
            o_ref[...] = acc_ref[...].astype(o_ref.dtype)

    return pl.pallas_call(
        body_single if nk == 1 else body_acc, name=name, grid=(m // tm, n // tn, nk),
        in_specs=[a_spec, b_spec], out_specs=pl.BlockSpec((tm, tn), lambda i, j, k: (i, j)),
        out_shape=jax.ShapeDtypeStruct((m, n), out_dtype),
        scratch_shapes=[] if nk == 1 else [pltpu.VMEM((tm, tn), F32)],
        compiler_params=_params(("parallel", "parallel", "arbitrary")),
    )(a, b)


def rmsnorm_fwd(name, x, w, res=None, out_dtype=F32):
    rows, d = x.shape
    tr = ROW_TILE
    row_spec = pl.BlockSpec((tr, d), lambda i: (i, 0))
    w_spec = pl.BlockSpec((1, d), lambda i: (0, 0))

    def body(*refs):
        if res is None:
            x_ref, w_ref, o_ref = refs
        else:
            x_ref, w_ref, r_ref, o_ref = refs
        xv = x_ref[...]
        y = xv * lax.rsqrt(jnp.mean(xv * xv, axis=-1, keepdims=True) + EPS) * w_ref[...]
        if res is not None:
            y = r_ref[...] + y
        o_ref[...] = y.astype(o_ref.dtype)

    args = [x, w.reshape(1, d)] + ([] if res is None else [res])
    specs = [row_spec, w_spec] + ([] if res is None else [row_spec])
    return pl.pallas_call(
        body, name=name, grid=(rows // tr,), in_specs=specs, out_specs=row_spec,
        out_shape=jax.ShapeDtypeStruct((rows, d), out_dtype), compiler_params=_params(("parallel",)),
    )(*args)


def rmsnorm_bwd(name, x, w, dy, add=None):
    rows, d = x.shape
    tr = ROW_TILE
    row_spec = pl.BlockSpec((tr, d), lambda i: (i, 0))
    w_spec = pl.BlockSpec((1, d), lambda i: (0, 0))

    def body(*refs):
        if add is None:
            x_ref, w_ref, dy_ref, dx_ref, dw_ref = refs
        else:
            x_ref, w_ref, dy_ref, add_ref, dx_ref, dw_ref = refs
        xv = x_ref[...]
        dyv = dy_ref[...].astype(F32)
        r = lax.rsqrt(jnp.mean(xv * xv, axis=-1, keepdims=True) + EPS)
        xh = xv * r
        dxh = dyv * w_ref[...]
        dx = r * (dxh - xh * jnp.mean(dxh * xh, axis=-1, keepdims=True))
        if add is not None:
            dx = dx + add_ref[...]
        dx_ref[...] = dx

        @pl.when(pl.program_id(0) == 0)
        def _():
            dw_ref[...] = jnp.zeros_like(dw_ref)

        dw_ref[...] += jnp.sum(dyv * xh, axis=0, keepdims=True)

    args = [x, w.reshape(1, d), dy] + ([] if add is None else [add])
    specs = [row_spec, w_spec, row_spec] + ([] if add is None else [row_spec])
    dx, dw = pl.pallas_call(
        body, name=name, grid=(rows // tr,), in_specs=specs, out_specs=[row_spec, w_spec],
        out_shape=[jax.ShapeDtypeStruct((rows, d), F32), jax.ShapeDtypeStruct((1, d), F32)],
        compiler_params=_params(("arbitrary",)),
    )(*args)
    return dx, dw.reshape(d)


def loss_head(name, h, target):
    rows, d = h.shape
    tr = ROW_TILE
    first = (PAD_FRONT + N_META) // tr
    assert (PAD_FRONT + N_META) % tr == 0 and target.shape[0] == rows - first * tr

    def body(h_ref, t_ref, dh_ref, loss_ref):
        i = pl.program_id(0)

        @pl.when(i == 0)
        def _():
            loss_ref[...] = jnp.zeros_like(loss_ref)

        keep = (i >= first).astype(F32)
        diff = (h_ref[...] - t_ref[...]) * keep
        dh_ref[...] = diff / d
        loss_ref[...] += 0.5 * jnp.sum(diff * diff) / d

    dh, loss = pl.pallas_call(
        body, name=name, grid=(rows // tr,),
        in_specs=[pl.BlockSpec((tr, d), lambda i: (i, 0)), pl.BlockSpec((tr, d), lambda i: (jnp.maximum(i - first, 0), 0))],
        out_specs=[pl.BlockSpec((tr, d), lambda i: (i, 0)), pl.BlockSpec((SUBLANES, LANES), lambda i: (0, 0))],
        out_shape=[jax.ShapeDtypeStruct((rows, d), F32), jax.ShapeDtypeStruct((SUBLANES, LANES), F32)],
        compiler_params=_params(("arbitrary",)),
    )(h, target)
    return dh, loss[0, 0]


GU_TILE = 512


def interleave_gate_up(gate, up):
    lead, hid = gate.shape[:-1], gate.shape[-1]
    parts = [t.reshape(*lead, hid // GU_TILE, 1, GU_TILE) for t in (gate, up)]
    return jnp.concatenate(parts, axis=-2).reshape(*lead, 2 * hid)


def split_gate_up(gu):
    lead, hid = gu.shape[:-1], gu.shape[-1] // 2
    t = gu.reshape(*lead, hid // GU_TILE, 2, GU_TILE)
    return t[..., 0, :].reshape(*lead, hid), t[..., 1, :].reshape(*lead, hid)


def swiglu_fwd(name, gp):
    rows, h2 = gp.shape
    tr, tc = ROW_TILE, GU_TILE

    def body(gp_ref, a_ref):
        g = gp_ref[:, :tc]
        a_ref[...] = (g * _sigmoid(g) * gp_ref[:, tc:]).astype(a_ref.dtype)

    return pl.pallas_call(
        body, name=name, grid=(rows // tr, h2 // (2 * tc)),
        in_specs=[pl.BlockSpec((tr, 2 * tc), lambda i, j: (i, j))], out_specs=pl.BlockSpec((tr, tc), lambda i, j: (i, j)),
        out_shape=jax.ShapeDtypeStruct((rows, h2 // 2), BF16), compiler_params=_params(("parallel", "parallel")),
    )(gp)


def swiglu_bwd(name, gp, da):
    rows, h2 = gp.shape
    tr, tc = ROW_TILE, GU_TILE

    def body(gp_ref, da_ref, o_ref):
        g = gp_ref[:, :tc]
        s = _sigmoid(g)
        dav = da_ref[...]
        o_ref[:, :tc] = (dav * gp_ref[:, tc:] * (s * (1.0 + g * (1.0 - s)))).astype(o_ref.dtype)
        o_ref[:, tc:] = (dav * (g * s)).astype(o_ref.dtype)

    return pl.pallas_call(
        body, name=name, grid=(rows // tr, h2 // (2 * tc)),
        in_specs=[pl.BlockSpec((tr, 2 * tc), lambda i, j: (i, j)), pl.BlockSpec((tr, tc), lambda i, j: (i, j))],
        out_specs=pl.BlockSpec((tr, 2 * tc), lambda i, j: (i, j)),
        out_shape=jax.ShapeDtypeStruct((rows, h2), BF16), compiler_params=_params(("parallel", "parallel")),
    )(gp, da)


def conv_fwd(name, zx, conv_w, conv_b, d_inner, conv_dim):
    rows = zx.shape[0]
    tc = CONV_TILE
    off = d_inner // tc
    assert d_inner % tc == 0 and conv_dim % tc == 0

    def body(u_ref, w_ref, b_ref, o_ref):
        u = u_ref[...]
        acc = u * w_ref[D_CONV - 1:D_CONV, :] + b_ref[...]
        for s in range(1, D_CONV):
            acc = acc + pltpu.roll(u, s, axis=0) * w_ref[D_CONV - 1 - s:D_CONV - s, :]
        y = acc * _sigmoid(acc)
        o_ref[...] = jnp.where(_row_mask(y.shape, 0), y, 0.0)

    return pl.pallas_call(
        body, name=name, grid=(conv_dim // tc,),
        in_specs=[pl.BlockSpec((rows, tc), lambda j: (0, off + j)), pl.BlockSpec((D_CONV, tc), lambda j: (0, j)),
                  pl.BlockSpec((1, tc), lambda j: (0, j))],
        out_specs=pl.BlockSpec((rows, tc), lambda j: (0, j)),
        out_shape=jax.ShapeDtypeStruct((rows, conv_dim), F32), compiler_params=_params(("parallel",)),
    )(zx, conv_w, conv_b.reshape(1, conv_dim))


def conv_bwd(name, zx, conv_w, conv_b, dxbc, d_inner, conv_dim):
    rows = zx.shape[0]
    tc = CONV_TILE
    off = d_inner // tc

    def body(u_ref, w_ref, b_ref, dy_ref, du_ref, dw_ref, db_ref):
        u = u_ref[...]
        wk = [w_ref[D_CONV - 1 - s:D_CONV - s, :] for s in range(D_CONV)]
        shifted = [u] + [pltpu.roll(u, s, axis=0) for s in range(1, D_CONV)]
        acc = u * wk[0] + b_ref[...]
        for s in range(1, D_CONV):
            acc = acc + shifted[s] * wk[s]
        sg = _sigmoid(acc)
        mask = _row_mask(acc.shape, 0)
        dpre = jnp.where(mask, dy_ref[...] * (sg * (1.0 + acc * (1.0 - sg))), 0.0)
        db_ref[...] = jnp.sum(dpre, axis=0, keepdims=True)
        du = dpre * wk[0]
        dw_ref[D_CONV - 1:D_CONV, :] = jnp.sum(dpre * u, axis=0, keepdims=True)
        for s in range(1, D_CONV):
            du = du + pltpu.roll(dpre, rows - s, axis=0) * wk[s]
            dw_ref[D_CONV - 1 - s:D_CONV - s, :] = jnp.sum(dpre * shifted[s], axis=0, keepdims=True)
        du_ref[...] = jnp.where(mask, du, 0.0).astype(du_ref.dtype)

    return pl.pallas_call(
        body, name=name, grid=(conv_dim // tc,),
        in_specs=[pl.BlockSpec((rows, tc), lambda j: (0, off + j)), pl.BlockSpec((D_CONV, tc), lambda j: (0, j)),
                  pl.BlockSpec((1, tc), lambda j: (0, j)), pl.BlockSpec((rows, tc), lambda j: (0, j))],
        out_specs=[pl.BlockSpec((rows, tc), lambda j: (0, j)), pl.BlockSpec((D_CONV, tc), lambda j: (0, j)),
                   pl.BlockSpec((1, tc), lambda j: (0, j))],
        out_shape=[jax.ShapeDtypeStruct((rows, conv_dim), BF16), jax.ShapeDtypeStruct((D_CONV, conv_dim), F32),
                   jax.ShapeDtypeStruct((1, conv_dim), F32)],
        compiler_params=_params(("parallel",)),
    )(zx, conv_w, conv_b.reshape(1, conv_dim), dxbc)


def dt_fwd(name, zx, bias_pad, zx_cols):
    rows = zx.shape[0]
    tr = ROW_TILE
    off = zx_cols // LANES

    def body(r_ref, b_ref, o_ref):
        v = r_ref[...] + b_ref[...]
        sp = jnp.maximum(v, 0.0) + jnp.log1p(jnp.exp(-jnp.abs(v)))
        o_ref[...] = jnp.where(_row_mask(v.shape, pl.program_id(0) * tr), sp, 0.0)

    return pl.pallas_call(
        body, name=name, grid=(rows // tr,),
        in_specs=[pl.BlockSpec((tr, LANES), lambda i: (i, off)), pl.BlockSpec((1, LANES), lambda i: (0, 0))],
        out_specs=pl.BlockSpec((tr, LANES), lambda i: (i, 0)),
        out_shape=jax.ShapeDtypeStruct((rows, LANES), F32), compiler_params=_params(("parallel",)),
    )(zx, bias_pad)


def dt_bwd(name, zx, bias_pad, ddt, zx_cols):
    rows = zx.shape[0]
    tr = ROW_TILE
    off = zx_cols // LANES

    def body(r_ref, b_ref, d_ref, o_ref, db_ref):
        v = r_ref[...] + b_ref[...]
        g = jnp.where(_row_mask(v.shape, pl.program_id(0) * tr), d_ref[...] * _sigmoid(v), 0.0)
        o_ref[...] = g.astype(o_ref.dtype)

        @pl.when(pl.program_id(0) == 0)
        def _():
            db_ref[...] = jnp.zeros_like(db_ref)

        db_ref[...] += jnp.sum(g, axis=0, keepdims=True)

    return pl.pallas_call(
        body, name=name, grid=(rows // tr,),
        in_specs=[pl.BlockSpec((tr, LANES), lambda i: (i, off)), pl.BlockSpec((1, LANES), lambda i: (0, 0)),
                  pl.BlockSpec((tr, LANES), lambda i: (i, 0))],
        out_specs=[pl.BlockSpec((tr, LANES), lambda i: (i, 0)), pl.BlockSpec((1, LANES), lambda i: (0, 0))],
        out_shape=[jax.ShapeDtypeStruct((rows, LANES), BF16), jax.ShapeDtypeStruct((1, LANES), F32)],
        compiler_params=_params(("arbitrary",)),
    )(zx, bias_pad, ddt)


def _split3(x):
    h1 = x.astype(BF16)
    r1 = x - h1.astype(F32)
    h2 = r1.astype(BF16)
    h3 = (r1 - h2.astype(F32)).astype(BF16)
    return h1, h2, h3


def _exact_left(ones_b, x):
    h1, h2, h3 = _split3(x)
    return _dot(ones_b, h1, 1, 0) + _dot(ones_b, h2, 1, 0) + _dot(ones_b, h3, 1, 0)


def _exact_right_t(x, ones_b):
    h1, h2, h3 = _split3(x)
    return _dot(h1, ones_b, 1, 1) + _dot(h2, ones_b, 1, 1) + _dot(h3, ones_b, 1, 1)


class _ScanCommon:
    def __init__(self, b_ref, c_ref, dtc_ref, dtr_ref, arow_ref, acol_ref, drow_ref):
        q = CHUNK
        self.bb = b_ref[...].astype(BF16)
        self.cb = c_ref[...].astype(BF16)
        ri = lax.broadcasted_iota(jnp.int32, (q, q), 0)
        cj = lax.broadcasted_iota(jnp.int32, (q, q), 1)
        self.lower = ri >= cj
        self.upper = cj >= ri
        self.dtc = dtc_ref[...]
        self.dtr = dtr_ref[...]
        self.arow = arow_ref[...]
        self.acol = acol_ref[...]
        self.drow = drow_ref[...]
        da_col = self.dtc * self.arow
        self.a_col = _exact_left(self.lower.astype(BF16), da_col)
        self.a_row = _exact_right_t(self.dtr * self.acol, self.lower.astype(BF16))
        self.a_last = jnp.sum(da_col, axis=0, keepdims=True)
        self.lane_q = lax.broadcasted_iota(jnp.int32, (q, LANES), 1)
        self.lane_1 = lax.broadcasted_iota(jnp.int32, (1, LANES), 1)
        self.sub_8 = lax.broadcasted_iota(jnp.int32, (SUBLANES, q), 0)
        self.first_half = self.lane_q < HEAD_DIM
        self.first_rows = lax.broadcasted_iota(jnp.int32, (LANES, 1), 0) < HEAD_DIM

    def col(self, v, r):
        return jnp.sum(jnp.where(self.lane_q == r, v, 0.0), axis=1, keepdims=True)

    def row(self, v, r):
        return jnp.sum(jnp.where(self.sub_8 == r, v, 0.0), axis=0, keepdims=True)

    def scalar(self, v, r):
        return jnp.sum(jnp.where(self.lane_1 == r, v, 0.0), axis=1, keepdims=True)

    def pair(self, v0, v1):
        return jnp.where(self.first_half, v0, v1)

    def half_rowsum(self, t, h):
        keep = self.first_half if h == 0 else jnp.logical_not(self.first_half)
        return jnp.sum(jnp.where(keep, t, 0.0), axis=1, keepdims=True)


def ssd_scan_fwd(name, xbc, dt_col, dt_row, a_row, a_col, d_row, d_inner):
    rows = xbc.shape[0]
    q, n, g_cnt = CHUNK, D_STATE, SSD_GROUPS
    nc = rows // q
    rp = d_inner // g_cnt
    n_pairs = rp // LANES
    b_off = d_inner // n

    def body(x_ref, b_ref, c_ref, dtc_ref, dtr_ref, arow_ref, acol_ref, drow_ref, y_ref, sprev_ref, s_ref):
        @pl.when(pl.program_id(1) == 0)
        def _():
            s_ref[...] = jnp.zeros_like(s_ref)

        sprev_ref[...] = s_ref[...]
        k = _ScanCommon(b_ref, c_ref, dtc_ref, dtr_ref, arow_ref, acol_ref, drow_ref)
        cb_mat = _dot(k.cb, k.bb, 1, 1)
        for pr in range(n_pairs):
            sl = slice(pr * LANES, (pr + 1) * LANES)
            heads = (2 * pr, 2 * pr + 1)
            xp = x_ref[:, sl]
            ac = [k.col(k.a_col, r) for r in heads]
            ar = [k.row(k.a_row, r) for r in heads]
            al = [k.scalar(k.a_last, r) for r in heads]
            xd = xp * k.pair(k.col(k.dtc, heads[0]), k.col(k.dtc, heads[1]))
            xdb = xd.astype(BF16)
            ys = []
            for h in range(2):
                lm = jnp.exp(jnp.where(k.lower, ac[h] - ar[h], -jnp.inf))
                ys.append(_dot((cb_mat * lm).astype(BF16), xdb, 1, 0))
            y = jnp.where(k.first_half, ys[0], ys[1])
            sp = s_ref[sl, :]
            y = y + k.pair(jnp.exp(ac[0]), jnp.exp(ac[1])) * _dot(k.cb, sp.astype(BF16), 1, 1)
            y = y + k.pair(k.scalar(k.drow, heads[0]), k.scalar(k.drow, heads[1])) * xp
            y_ref[:, sl] = y
            wb = (xd * k.pair(jnp.exp(al[0] - ac[0]), jnp.exp(al[1] - ac[1]))).astype(BF16)
            decay = jnp.where(k.first_rows, jnp.exp(al[0]), jnp.exp(al[1]))
            s_ref[sl, :] = decay * sp + _dot(wb, k.bb, 0, 0)

    return pl.pallas_call(
        body, name=name, grid=(g_cnt, nc),
        in_specs=[
            pl.BlockSpec((q, rp), lambda g, c: (c, g)),
            pl.BlockSpec((q, n), lambda g, c: (c, b_off + g)),
            pl.BlockSpec((q, n), lambda g, c: (c, b_off + g_cnt + g)),
            pl.BlockSpec((None, q, LANES), lambda g, c: (g, c, 0)),
            pl.BlockSpec((None, SUBLANES, q), lambda g, c: (g, 0, c)),
            pl.BlockSpec((None, 1, LANES), lambda g, c: (g, 0, 0)),
            pl.BlockSpec((None, SUBLANES, 1), lambda g, c: (g, 0, 0)),
            pl.BlockSpec((None, 1, LANES), lambda g, c: (g, 0, 0)),
        ],
        out_specs=[pl.BlockSpec((q, rp), lambda g, c: (c, g)),
                   pl.BlockSpec((None, None, rp, n), lambda g, c: (c, g, 0, 0))],
        out_shape=[jax.ShapeDtypeStruct((rows, d_inner), F32), jax.ShapeDtypeStruct((nc, g_cnt, rp, n), F32)],
        scratch_shapes=[pltpu.VMEM((rp, n), F32)],
        compiler_params=_params(("parallel", "arbitrary")),
    )(xbc, xbc, xbc, dt_col, dt_row, a_row, a_col, d_row)


def ssd_scan_bwd(name, xbc, dt_col, dt_row, a_row, a_col, d_row, sprev, dy, d_inner):
    rows = xbc.shape[0]
    q, n, g_cnt = CHUNK, D_STATE, SSD_GROUPS
    nc = rows // q
    rp = d_inner // g_cnt
    n_pairs = rp // LANES
    b_off = d_inner // n

    def body(x_ref, b_ref, c_ref, dtc_ref, dtr_ref, arow_ref, acol_ref, drow_ref, sprev_ref, dy_ref,
             dx_ref, db_ref, dc_ref, ddtc_ref, ddtr_ref, dar_ref, dac_ref, dd_ref, ds_ref):
        @pl.when(pl.program_id(1) == 0)
        def _():
            ds_ref[...] = jnp.zeros_like(ds_ref)
            dar_ref[...] = jnp.zeros_like(dar_ref)
            dac_ref[...] = jnp.zeros_like(dac_ref)
            dd_ref[...] = jnp.zeros_like(dd_ref)

        k = _ScanCommon(b_ref, c_ref, dtc_ref, dtr_ref, arow_ref, acol_ref, drow_ref)
        cb_mat = _dot(k.cb, k.bb, 1, 1)
        cbt_mat = _dot(k.bb, k.cb, 1, 1)
        d_cb = jnp.zeros((q, q), F32)
        d_b = jnp.zeros((q, n), F32)
        d_c = jnp.zeros((q, n), F32)
        da_col = jnp.zeros((q, LANES), F32)
        da_row = jnp.zeros((SUBLANES, q), F32)
        ddt_x = jnp.zeros((q, LANES), F32)
        d_alast = jnp.zeros((1, LANES), F32)
        d_dskip = jnp.zeros((1, LANES), F32)
        for pr in range(n_pairs):
            sl = slice(pr * LANES, (pr + 1) * LANES)
            heads = (2 * pr, 2 * pr + 1)
            xp = x_ref[:, sl]
            dyp = dy_ref[:, sl]
            dyb = dyp.astype(BF16)
            ac = [k.col(k.a_col, r) for r in heads]
            ar = [k.row(k.a_row, r) for r in heads]
            al = [k.scalar(k.a_last, r) for r in heads]
            dt_p = k.pair(k.col(k.dtc, heads[0]), k.col(k.dtc, heads[1]))
            xd = xp * dt_p
            xdb = xd.astype(BF16)
            sp = sprev_ref[sl, :]
            spb = sp.astype(BF16)
            dsp = ds_ref[sl, :]
            dspb = dsp.astype(BF16)
            dskip_p = k.pair(k.scalar(k.drow, heads[0]), k.scalar(k.drow, heads[1]))
            dxp = dskip_p * dyp
            dd_lane = jnp.sum(dyp * xp, axis=0, keepdims=True)
            e_p = k.pair(jnp.exp(ac[0]), jnp.exp(ac[1]))
            t_off = dyp * (e_p * _dot(k.cb, spb, 1, 1))
            dzb = (e_p * dyp).astype(BF16)
            d_c = d_c + _dot(dzb, spb, 1, 0)
            ds_in = _dot(dzb, k.cb, 0, 0)
            decay = jnp.where(k.first_rows, jnp.exp(al[0]), jnp.exp(al[1]))
            ds_in = ds_in + decay * dsp
            t_state = jnp.sum(dsp * sp, axis=1, keepdims=True) * decay
            dec_p = k.pair(jnp.exp(al[0] - ac[0]), jnp.exp(al[1] - ac[1]))
            dw = _dot(k.bb, dspb, 1, 1)
            d_b = d_b + _dot((xd * dec_p).astype(BF16), dspb, 1, 0)
            dxd = dw * dec_p
            t_dec = dw * xd * dec_p
            dxd_h = []
            for h in range(2):
                r = heads[h]
                keep = k.first_half if h == 0 else jnp.logical_not(k.first_half)
                lm = jnp.exp(jnp.where(k.lower, ac[h] - ar[h], -jnp.inf))
                m_mat = cb_mat * lm
                dm = _dot(jnp.where(keep, dyp, 0.0).astype(BF16), xdb, 1, 1)
                dseg = dm * m_mat
                d_cb = d_cb + dm * lm
                lmt = jnp.exp(jnp.where(k.upper, ar[h] - ac[h], -jnp.inf))
                dxd_h.append(_dot((cbt_mat * lmt).astype(BF16), dyb, 1, 0))
                tdec_h = k.half_rowsum(t_dec, h)
                da_h = k.half_rowsum(t_off, h) - tdec_h + jnp.sum(dseg, axis=1, keepdims=True)
                da_col = da_col + jnp.where(k.lane_q == r, da_h, 0.0)
                da_row = da_row - jnp.where(k.sub_8 == r, jnp.sum(dseg, axis=0, keepdims=True), 0.0)
                keep_rows = k.first_rows if h == 0 else jnp.logical_not(k.first_rows)
                dal_h = jnp.sum(tdec_h, axis=0, keepdims=True) + jnp.sum(jnp.where(keep_rows, t_state, 0.0), axis=0, keepdims=True)
                d_alast = d_alast + jnp.where(k.lane_1 == r, dal_h, 0.0)
                keep_1 = k.lane_1 < HEAD_DIM if h == 0 else k.lane_1 >= HEAD_DIM
                dd_h = jnp.sum(jnp.where(keep_1, dd_lane, 0.0), axis=1, keepdims=True)
                d_dskip = d_dskip + jnp.where(k.lane_1 == r, dd_h, 0.0)
            dxd = dxd + jnp.where(k.first_half, dxd_h[0], dxd_h[1])
            dx_ref[:, sl] = dxp + dt_p * dxd
            t_dt = dxd * xp
            for h in range(2):
                ddt_x = ddt_x + jnp.where(k.lane_q == heads[h], k.half_rowsum(t_dt, h), 0.0)
            ds_ref[sl, :] = ds_in
        d_cb_b = d_cb.astype(BF16)
        db_ref[...] = d_b + _dot(d_cb_b, k.cb, 0, 0)
        dc_ref[...] = d_c + _dot(d_cb_b, k.bb, 1, 0)
        rc_col = _exact_left(k.upper.astype(BF16), da_col) + d_alast
        rc_row = _exact_right_t(da_row, k.upper.astype(BF16))
        ddtc_ref[...] = ddt_x + k.arow * rc_col
        ddtr_ref[...] = k.acol * rc_row
        dar_ref[...] += jnp.sum(rc_col * k.dtc, axis=0, keepdims=True)
        dac_ref[...] += jnp.sum(rc_row * k.dtr, axis=1, keepdims=True)
        dd_ref[...] += d_dskip

    rc = lambda c: nc - 1 - c
    return pl.pallas_call(
        body, name=name, grid=(g_cnt, nc),
        in_specs=[
            pl.BlockSpec((q, rp), lambda g, c: (rc(c), g)),
            pl.BlockSpec((q, n), lambda g, c: (rc(c), b_off + g)),
            pl.BlockSpec((q, n), lambda g, c: (rc(c), b_off + g_cnt + g)),
            pl.BlockSpec((None, q, LANES), lambda g, c: (g, rc(c), 0)),
            pl.BlockSpec((None, SUBLANES, q), lambda g, c: (g, 0, rc(c))),
            pl.BlockSpec((None, 1, LANES), lambda g, c: (g, 0, 0)),
            pl.BlockSpec((None, SUBLANES, 1), lambda g, c: (g, 0, 0)),
            pl.BlockSpec((None, 1, LANES), lambda g, c: (g, 0, 0)),
            pl.BlockSpec((None, None, rp, n), lambda g, c: (rc(c), g, 0, 0)),
            pl.BlockSpec((q, rp), lambda g, c: (rc(c), g)),
        ],
        out_specs=[
            pl.BlockSpec((q, rp), lambda g, c: (rc(c), g)),
            pl.BlockSpec((q, n), lambda g, c: (rc(c), g)),
            pl.BlockSpec((q, n), lambda g, c: (rc(c), g)),
            pl.BlockSpec((None, q, LANES), lambda g, c: (g, rc(c), 0)),
            pl.BlockSpec((None, SUBLANES, q), lambda g, c: (g, 0, rc(c))),
            pl.BlockSpec((None, 1, LANES), lambda g, c: (g, 0, 0)),
            pl.BlockSpec((None, SUBLANES, 1), lambda g, c: (g, 0, 0)),
            pl.BlockSpec((None, 1, LANES), lambda g, c: (g, 0, 0)),
        ],
        out_shape=[
            jax.ShapeDtypeStruct((rows, d_inner), F32),
            jax.ShapeDtypeStruct((rows, g_cnt * n), F32),
            jax.ShapeDtypeStruct((rows, g_cnt * n), F32),
            jax.ShapeDtypeStruct((g_cnt, rows, LANES), F32),
            jax.ShapeDtypeStruct((g_cnt, SUBLANES, rows), F32),
            jax.ShapeDtypeStruct((g_cnt, 1, LANES), F32),
            jax.ShapeDtypeStruct((g_cnt, SUBLANES, 1), F32),
            jax.ShapeDtypeStruct((g_cnt, 1, LANES), F32),
        ],
        scratch_shapes=[pltpu.VMEM((rp, n), F32)],
        compiler_params=_params(("parallel", "arbitrary")),
    )(xbc, xbc, xbc, dt_col, dt_row, a_row, a_col, d_row, sprev, dy)


def gatenorm_fwd(name, y, zx, w, d_inner):
    rows = y.shape[0]
    tr, gw = ROW_TILE, d_inner // SSD_GROUPS

    def body(y_ref, z_ref, w_ref, o_ref):
        z = z_ref[...]
        v = y_ref[...] * (z * _sigmoid(z))
        o_ref[...] = (v * lax.rsqrt(jnp.mean(v * v, axis=-1, keepdims=True) + EPS) * w_ref[...]).astype(o_ref.dtype)

    blk = pl.BlockSpec((tr, gw), lambda i, j: (i, j))
    return pl.pallas_call(
        body, name=name, grid=(rows // tr, SSD_GROUPS),
        in_specs=[blk, blk, pl.BlockSpec((1, gw), lambda i, j: (0, j))], out_specs=blk,
        out_shape=jax.ShapeDtypeStruct((rows, d_inner), BF16), compiler_params=_params(("parallel", "parallel")),
    )(y, zx, w.reshape(1, d_inner))


def gatenorm_bwd(name, y, zx, w, dyn, d_inner):
    rows = y.shape[0]
    tr, gw = ROW_TILE, d_inner // SSD_GROUPS

    def body(y_ref, z_ref, w_ref, dn_ref, dy_ref, dz_ref, dw_ref):
        z = z_ref[...]
        yv = y_ref[...]
        s = _sigmoid(z)
        gate = z * s
        v = yv * gate
        r = lax.rsqrt(jnp.mean(v * v, axis=-1, keepdims=True) + EPS)
        vh = v * r
        dn = dn_ref[...]
        dvh = dn * w_ref[...]
        dv = r * (dvh - vh * jnp.mean(dvh * vh, axis=-1, keepdims=True))
        dy_ref[...] = dv * gate
        dz_ref[...] = (dv * yv * (s * (1.0 + z * (1.0 - s)))).astype(dz_ref.dtype)

        @pl.when(pl.program_id(1) == 0)
        def _():
            dw_ref[...] = jnp.zeros_like(dw_ref)

        dw_ref[...] += jnp.sum(dn * vh, axis=0, keepdims=True)

    blk = pl.BlockSpec((tr, gw), lambda j, i: (i, j))
    wblk = pl.BlockSpec((1, gw), lambda j, i: (0, j))
    dy, dz, dw = pl.pallas_call(
        body, name=name, grid=(SSD_GROUPS, rows // tr),
        in_specs=[blk, blk, wblk, blk], out_specs=[blk, blk, wblk],
        out_shape=[jax.ShapeDtypeStruct((rows, d_inner), F32), jax.ShapeDtypeStruct((rows, d_inner), BF16),
                   jax.ShapeDtypeStruct((1, d_inner), F32)],
        compiler_params=_params(("parallel", "arbitrary")),
    )(y, zx, w.reshape(1, d_inner), dyn)
    return dy, dz, dw.reshape(d_inner)


def _pool_count(rows, g):
    t1 = lax.broadcasted_iota(jnp.int32, (rows, 1), 0) - (PAD_FRONT - 1)
    win = jnp.left_shift(jnp.int32(POOL_WINDOWS[0]), g)
    return jnp.clip(t1, 1, win).astype(F32)


def _pool_select(levels, g):
    out = levels[-1]
    for i in range(len(levels) - 2, -1, -1):
        out = jnp.where(g == i, levels[i], out)
    return out


def pool_sub(name, u, transpose):
    rows, d = u.shape
    gd = d // len(POOL_WINDOWS)
    assert all(w == POOL_WINDOWS[0] << i for i, w in enumerate(POOL_WINDOWS))

    def body(u_ref, o_ref):
        g = pl.program_id(0)
        v = u_ref[...].astype(F32)
        cnt = _pool_count(rows, g)
        mask = _row_mask(v.shape, 0)
        s = v / cnt if transpose else v
        levels = []
        for i in range(len(POOL_WINDOWS)):
            step = 1 << i
            s = s + pltpu.roll(s, (rows - step) if transpose else step, axis=0)
            levels.append(s)
        sel = _pool_select(levels, g)
        out = (sel - v) if transpose else (sel / cnt - v)
        o_ref[...] = jnp.where(mask, out, 0.0).astype(o_ref.dtype)

    tc = POOL_TILE
    per = gd // tc
    blk = pl.BlockSpec((rows, tc), lambda g, j: (0, g * per + j))
    return pl.pallas_call(
        body, name=name, grid=(len(POOL_WINDOWS), per), in_specs=[blk], out_specs=blk,
        out_shape=jax.ShapeDtypeStruct((rows, d), F32 if transpose else BF16),
        compiler_params=_params(("parallel", "parallel")),
    )(u)


def pool_proj_fwd(name, mixed, w, b, scale):
    rows, d = mixed.shape
    ng = len(POOL_WINDOWS)
    gd = d // ng
    tr = ROW_TILE

    def body(m_ref, w_ref, b_ref, s_ref, pre_ref, mix_ref):
        pre = _dot(m_ref[...], w_ref[...], 1, 0) + b_ref[...]
        pre_ref[...] = pre
        mix_ref[...] = jnp.where(_row_mask(pre.shape, pl.program_id(1) * tr), pre * s_ref[...], 0.0)

    blk = pl.BlockSpec((tr, gd), lambda g, i: (i, g))
    vec = pl.BlockSpec((1, gd), lambda g, i: (0, g))
    return pl.pallas_call(
        body, name=name, grid=(ng, rows // tr),
        in_specs=[blk, pl.BlockSpec((None, gd, gd), lambda g, i: (g, 0, 0)), vec, vec], out_specs=[blk, blk],
        out_shape=[jax.ShapeDtypeStruct((rows, d), F32), jax.ShapeDtypeStruct((rows, d), F32)],
        compiler_params=_params(("parallel", "parallel")),
    )(mixed, w, b.reshape(1, d), scale.reshape(1, d))


def pool_proj_bwd(name, dmix, pre, mixed, w, scale):
    rows, d = dmix.shape
    ng = len(POOL_WINDOWS)
    gd = d // ng
    tr = ROW_TILE

    def body(dm_ref, pre_ref, mx_ref, w_ref, s_ref, dmx_ref, dw_ref, db_ref, ds_ref):
        @pl.when(pl.program_id(1) == 0)
        def _():
            dw_ref[...] = jnp.zeros_like(dw_ref)
            db_ref[...] = jnp.zeros_like(db_ref)
            ds_ref[...] = jnp.zeros_like(ds_ref)

        dmv = jnp.where(_row_mask(dm_ref.shape, pl.program_id(1) * tr), dm_ref[...], 0.0)
        dpre = dmv * s_ref[...]
        dpre_b = dpre.astype(BF16)
        ds_ref[...] += jnp.sum(dmv * pre_ref[...], axis=0, keepdims=True)
        db_ref[...] += jnp.sum(dpre, axis=0, keepdims=True)
        dmx_ref[...] = _dot(dpre_b, w_ref[...], 1, 1)
        dw_ref[...] += _dot(mx_ref[...], dpre_b, 0, 0)

    blk = pl.BlockSpec((tr, gd), lambda g, i: (i, g))
    vec = pl.BlockSpec((1, gd), lambda g, i: (0, g))
    wblk = pl.BlockSpec((None, gd, gd), lambda g, i: (g, 0, 0))
    dmixed, dw, db, ds = pl.pallas_call(
        body, name=name, grid=(ng, rows // tr),
        in_specs=[blk, blk, blk, wblk, vec], out_specs=[blk, wblk, vec, vec],
        out_shape=[jax.ShapeDtypeStruct((rows, d), F32), jax.ShapeDtypeStruct((ng, gd, gd), F32),
                   jax.ShapeDtypeStruct((1, d), F32), jax.ShapeDtypeStruct((1, d), F32)],
        compiler_params=_params(("parallel", "arbitrary")),
    )(dmix, pre, mixed, w, scale.reshape(1, d))
    return dmixed, dw, db.reshape(d), ds.reshape(d)


def _my_place():
    return lax.axis_index("x"), lax.axis_index("y"), lax.axis_index("c")


def _linear(place):
    return 4 * place[0] + 2 * place[1] + place[2]


def all_gather(name, shards):
    n_ops = len(shards)

    def body(*refs):
        ins, outs = refs[:n_ops], refs[n_ops:2 * n_ops]
        send_sems, recv_sems, local_sems = refs[2 * n_ops:]
        x, y, c = _my_place()
        me, sibling = (x, y, c), (x, y, 1 - c)
        chips = [(1 - x, y), (x, 1 - y), (1 - x, 1 - y)]

        def copy(t, k, block, to, src=None):
            dst = outs[t].at[_linear(block)]
            return pltpu.make_async_remote_copy(
                src_ref=dst if src is None else src, dst_ref=dst, send_sem=send_sems.at[t, k], recv_sem=recv_sems.at[t, k],
                device_id=to, device_id_type=MESH)

        mine = [pltpu.make_async_copy(ins[t], outs[t].at[_linear(me)], local_sems.at[t]) for t in range(n_ops)]
        for cp in mine:
            cp.start()
        first = []
        for t in range(n_ops):
            first.append(copy(t, 0, me, sibling, src=ins[t]))
            first += [copy(t, 1 + j, me, (*chip, c), src=ins[t]) for j, chip in enumerate(chips)]
        for cp in first:
            cp.start()
        passed = []
        for j, chip in enumerate(chips):
            for t in range(n_ops):
                copy(t, 1 + j, (*chip, c), me).wait_recv()
                fwd = copy(t, 4 + j, (*chip, c), sibling)
                fwd.start()
                passed.append(fwd)
        for t in range(n_ops):
            copy(t, 0, sibling, me).wait_recv()
            for j, chip in enumerate(chips):
                copy(t, 4 + j, (*chip, 1 - c), me).wait_recv()
        for cp in first + passed:
            cp.wait_send()
        for cp in mine:
            cp.wait()

    any_spec = pl.BlockSpec(memory_space=pl.ANY)
    outs = pl.pallas_call(
        body, name=name,
        in_specs=[any_spec] * n_ops, out_specs=[any_spec] * n_ops,
        out_shape=[jax.ShapeDtypeStruct((N_DEV, *s.shape), s.dtype) for s in shards],
        scratch_shapes=[pltpu.SemaphoreType.DMA((n_ops, 7)), pltpu.SemaphoreType.DMA((n_ops, 7)),
                        pltpu.SemaphoreType.DMA((n_ops,))],
    )(*shards)
    return list(outs)


_HBM = pl.BlockSpec(memory_space=pltpu.HBM)
_SEM = pl.BlockSpec(memory_space=pltpu.SEMAPHORE)
_EFFECT = pltpu.SideEffectType.DATAFLOW_SIDE_EFFECTING


def _hbm(a):
    return pltpu.with_memory_space_constraint(a, pltpu.HBM)


def _peers():
    x, y, c = _my_place()
    return [(x ^ (j >> 2), y ^ ((j >> 1) & 1), c ^ (j & 1)) for j in range(1, N_DEV)]


def exchange_start(name, groups):
    flat = [e for g in groups for e in g]
    n = len(flat)

    def body(*refs):
        srcs, lands = refs[:n], refs[n:2 * n]
        outs = refs[2 * n:]
        sends, recvs, token, local_sems = outs[:n], outs[n:2 * n], outs[4 * n], refs[-1]
        me = _linear(_my_place())
        for t, (_, _, src_view, land_view) in enumerate(flat):
            for peer in _peers():
                pltpu.make_async_remote_copy(
                    src_ref=src_view(srcs[t], _linear(peer)), dst_ref=land_view(lands[t], me),
                    send_sem=sends[t], recv_sem=recvs[t], device_id=peer, device_id_type=MESH).start()
        own = [pltpu.make_async_copy(e[2](srcs[t], me), e[3](lands[t], me), local_sems.at[t]) for t, e in enumerate(flat)]
        for cp in own:
            cp.start()
        for cp in own:
            cp.wait()
        token[...] = jnp.zeros_like(token)

    sem_shapes = [pltpu.SemaphoreType.DMA(())] * (2 * n)
    thru = [pltpu.HBM(e[0].shape, e[0].dtype) for e in flat] + [pltpu.HBM(e[1].shape, e[1].dtype) for e in flat]
    outs = pl.pallas_call(
        body, name=name,
        out_shape=(*sem_shapes, *thru, jax.ShapeDtypeStruct((SUBLANES, LANES), F32)),
        in_specs=[_HBM] * (2 * n), out_specs=(*[_SEM] * (2 * n), *[_HBM] * (2 * n), pl.BlockSpec(memory_space=pltpu.VMEM)),
        input_output_aliases={t: 2 * n + t for t in range(2 * n)},
        scratch_shapes=[pltpu.SemaphoreType.DMA((n,))],
        compiler_params=pltpu.CompilerParams(has_side_effects=_EFFECT),
    )(*[_hbm(e[0]) for e in flat], *[_hbm(e[1]) for e in flat])
    records, t = [], 0
    for g in groups:
        k = len(g)
        records.append((list(outs[t:t + k]), list(outs[n + t:n + t + k]), list(outs[2 * n + t:2 * n + t + k]),
                        list(outs[3 * n + t:3 * n + t + k])))
        t += k
    return records, outs[-1]


def exchange_wait(name, records, lands, land_of, seven_of, after):
    srcs = [s for r in records for s in r[2]]
    sends = [s for r in records for s in r[0]]
    recvs = [s for r in records for s in r[1]]
    where = [(ri, k) for ri, r in enumerate(records) for k in range(len(r[2]))]
    ns, nl = len(srcs), len(lands)

    def body(*refs):
        src_refs, land_refs = refs[:ns], refs[ns:ns + nl]
        send_refs, recv_refs = refs[ns + nl:2 * ns + nl], refs[2 * ns + nl:3 * ns + nl]
        for t, (ri, k) in enumerate(where):
            seven = seven_of[ri][k](src_refs[t], land_refs[land_of[ri][k]])
            cp = pltpu.make_async_remote_copy(src_ref=seven, dst_ref=seven, send_sem=send_refs[t], recv_sem=recv_refs[t],
                                              device_id=_my_place(), device_id_type=MESH)
            cp.wait_send()
            cp.wait_recv()

    outs = pl.pallas_call(
        body, name=name,
        out_shape=tuple(pltpu.HBM(a.shape, a.dtype) for a in (*srcs, *lands)),
        in_specs=[*[_HBM] * (ns + nl), *[_SEM] * (2 * ns), pl.BlockSpec(memory_space=pl.ANY)], out_specs=tuple([_HBM] * (ns + nl)),
        input_output_aliases={t: t for t in range(ns + nl)},
        compiler_params=pltpu.CompilerParams(has_side_effects=_EFFECT),
    )(*srcs, *lands, *sends, *recvs, after)
    return list(outs[ns:])


def _seven_slots_of_land(src_ref, land_ref):
    return land_ref.at[pl.ds(0, N_DEV - 1)]


def _seven_blocks_of_src(src_ref, land_ref):
    return src_ref.at[pl.ds(0, N_DEV - 1)]


def _whole(ref, dev):
    return ref


def _slot(ref, dev):
    return ref.at[dev]


def _slot_of_layer(layer):
    return lambda ref, dev: ref.at[dev, layer]


def _as_rows(a, lead=0):
    return a.reshape(a.shape[:lead] + (-1, a.shape[-1]))


def sum_adamw(name, parts, w, m, v):
    shape = w.shape
    p2, w2, m2, v2 = _as_rows(parts, 1), _as_rows(w), _as_rows(m), _as_rows(v)
    rows, cols = w2.shape
    tr = _tile(rows, ROW_TILE, 16)
    c1 = 1.0 - ADAM_B1 ** ADAM_STEP
    c2 = 1.0 - ADAM_B2 ** ADAM_STEP

    def body(p_ref, w_ref, m_ref, v_ref, g_ref, d_ref, nm_ref, nv_ref):
        g = p_ref[0].astype(F32)
        for k in range(1, N_DEV):
            g = g + p_ref[k].astype(F32)
        wv = w_ref[...]
        nm = ADAM_B1 * m_ref[...] + (1.0 - ADAM_B1) * g
        nv = ADAM_B2 * v_ref[...] + (1.0 - ADAM_B2) * (g * g)
        g_ref[...] = g
        nm_ref[...] = nm
        nv_ref[...] = nv
        d_ref[...] = -ADAM_LR * ((nm / c1) / (jnp.sqrt(nv / c2) + ADAM_EPS) + ADAM_WD * wv)

    blk = pl.BlockSpec((tr, cols), lambda i: (i, 0))
    outs = pl.pallas_call(
        body, name=name, grid=(rows // tr,),
        in_specs=[pl.BlockSpec((N_DEV, tr, cols), lambda i: (0, i, 0)), blk, blk, blk], out_specs=[blk] * 4,
        out_shape=[jax.ShapeDtypeStruct((rows, cols), F32)] * 4, compiler_params=_params(("parallel",)),
    )(p2, w2, m2, v2)
    return [o.reshape(shape) for o in outs]


def _unblock_cols(g):
    g = jnp.moveaxis(g, 0, -2)
    return g.reshape(g.shape[:-2] + (g.shape[-2] * g.shape[-1],))


def _block_cols(a):
    r, c = a.shape
    return jnp.moveaxis(a.reshape(r, N_DEV, c // N_DEV), 1, 0)


def _my_cols(a, n):
    me = _linear(_my_place())
    return lax.dynamic_slice_in_dim(a, me * n, n, axis=a.ndim - 1)


def _pack(arrays):
    flat = jnp.concatenate([a.reshape(-1).astype(F32) for a in arrays])
    pad = (-flat.shape[0]) % (ROW_TILE * LANES)
    return jnp.pad(flat, (0, pad)).reshape(-1, LANES)


def _unpack(packed, like):
    flat = packed.reshape(-1)
    out, pos = [], 0
    for a in like:
        out.append(flat[pos:pos + a.size].reshape(a.shape))
        pos += a.size
    return out


def kernel(x, meta_tokens, norm_w, ssd_w_in, ssd_conv_w, ssd_conv_b, ssd_dt_bias, ssd_a_log, ssd_d, ssd_norm_w, ssd_w_out, pool_w, pool_b, pool_scale, ffn_w_gate, ffn_w_up, ffn_w_down, loss_target, m_meta_tokens, m_norm_w, m_ssd_w_in, m_ssd_conv_w, m_ssd_conv_b, m_ssd_dt_bias, m_ssd_a_log, m_ssd_d, m_ssd_norm_w, m_ssd_w_out, m_pool_w, m_pool_b, m_pool_scale, m_ffn_w_gate, m_ffn_w_up, m_ffn_w_down, v_meta_tokens, v_norm_w, v_ssd_w_in, v_ssd_conv_w, v_ssd_conv_b, v_ssd_dt_bias, v_ssd_a_log, v_ssd_d, v_ssd_norm_w, v_ssd_w_out, v_pool_w, v_pool_b, v_pool_scale, v_ffn_w_gate, v_ffn_w_up, v_ffn_w_down):
    seq, d = x.shape[1], x.shape[2]
    depth = norm_w.shape[0]
    n_ssd = ssd_w_in.shape[0]
    d_inner = ssd_norm_w.shape[1]
    heads = d_inner // HEAD_DIM
    rpg = heads // SSD_GROUPS
    conv_dim = ssd_conv_b.shape[1]
    zx_cols = d_inner + conv_dim
    d_in_proj = zx_cols + heads
    rows = PAD_FRONT + N_META + seq
    assert rows % CHUNK == 0 and rpg % 2 == 0 and heads <= LANES and rpg <= SUBLANES
    sd = d // N_DEV

    hidden = ffn_w_down.shape[1] * N_DEV
    n_pool = pool_w.shape[0]

    def gather_entry(shard):
        return (shard.astype(BF16), lax.empty((N_DEV, *shard.shape), BF16), _whole, _slot)

    gather_groups = []
    for i in range(depth):
        mixer = [ssd_w_in[i // 2], ssd_w_out[i // 2]] if i % 2 == 0 else [pool_w[i // 2]]
        gather_groups.append([gather_entry(w) for w in mixer])
        gather_groups.append([gather_entry(w) for w in (ffn_w_gate[i], ffn_w_up[i], ffn_w_down[i])])
    gather_recs, token = exchange_start("gather_start", gather_groups)

    def gathered(name, rec, after):
        n_e = len(rec[2])
        return exchange_wait(name, [rec], rec[3], [list(range(n_e))], [[_seven_slots_of_land] * n_e], after)

    small = all_gather("gather_small", [meta_tokens + token[0, 0], norm_w, ssd_conv_w, pool_b, pool_scale])
    meta_f, norm_f, convw_f, poolb_f, pools_f = [_unblock_cols(s) for s in small]

    pad_h = lambda a: jnp.pad(a.astype(F32), ((0, 0), (0, LANES - heads)))
    bias_pad = pad_h(ssd_dt_bias)

    def head_layouts(vec):
        g = vec.reshape(SSD_GROUPS, rpg)
        row = jnp.pad(g, ((0, 0), (0, LANES - rpg)))[:, None, :]
        col = jnp.pad(g, ((0, 0), (0, SUBLANES - rpg)))[:, :, None]
        return row, col

    def dt_layouts(dt):
        g = dt[:, :heads].reshape(rows, SSD_GROUPS, rpg)
        col = jnp.pad(jnp.moveaxis(g, 1, 0), ((0, 0), (0, 0), (0, LANES - rpg)))
        row = jnp.pad(jnp.transpose(g, (1, 2, 0)), ((0, 0), (0, SUBLANES - rpg), (0, 0)))
        return col, row

    h = jnp.concatenate([jnp.zeros((PAD_FRONT, d), F32), meta_f, x[0]], axis=0)
    saved = []
    for i in range(depth):
        j = i // 2
        s = {"h": h}
        mixer_w = gathered(f"gather_wait_mix_{i}", gather_recs[2 * i], h)
        if i % 2 == 0:
            w_in = jnp.pad(_unblock_cols(mixer_w[0]), ((0, 0), (0, LANES - heads)))
            w_out = mixer_w[1].reshape(d_inner, d)
            s.update(w_in=w_in, w_out=w_out)
            u = rmsnorm_fwd(f"norm_pre_mix_{i}", h, norm_f[i, 0], out_dtype=BF16)
            zx = matmul(f"ssd_in_{i}", u, w_in, tn=384 if w_in.shape[1] % 384 == 0 else 512)
            xbc = conv_fwd(f"ssd_conv_{i}", zx, convw_f[j], ssd_conv_b[j], d_inner, conv_dim)
            dt = dt_fwd(f"ssd_dt_{i}", zx, bias_pad[j:j + 1], zx_cols)
            dt_col, dt_row = dt_layouts(dt)
            a_neg = -jnp.exp(ssd_a_log[j].astype(F32))
            a_row, a_col = head_layouts(a_neg)
            d_row, _ = head_layouts(ssd_d[j].astype(F32))
            y, sprev = ssd_scan_fwd(f"ssd_scan_{i}", xbc, dt_col, dt_row, a_row, a_col, d_row, d_inner)
            yn = gatenorm_fwd(f"ssd_gate_{i}", y, zx, ssd_norm_w[j], d_inner)
            mix = matmul(f"ssd_out_{i}", yn, w_out)
            s.update(u=u, zx=zx, xbc=xbc, dt_col=dt_col, dt_row=dt_row, a_row=a_row, a_col=a_col, d_row=d_row,
                     a_neg=a_neg, y=y, sprev=sprev, yn=yn)
        else:
            w_pool = jnp.moveaxis(mixer_w[0], 0, 1).reshape(len(POOL_WINDOWS), d // len(POOL_WINDOWS), -1)
            u = rmsnorm_fwd(f"norm_pre_mix_{i}", h, norm_f[i, 0])
            mixed = pool_sub(f"pool_sub_{i}", u, False)
            pre, mix = pool_proj_fwd(f"pool_proj_{i}", mixed, w_pool, poolb_f[j], pools_f[j])
            s.update(mixed=mixed, pre=pre, w_pool=w_pool)
        h1 = rmsnorm_fwd(f"norm_post_mix_{i}", mix, norm_f[i, 1], res=h)
        ffn_w = gathered(f"gather_wait_ffn_{i}", gather_recs[2 * i + 1], h1)
        w_gu = interleave_gate_up(_unblock_cols(ffn_w[0]), _unblock_cols(ffn_w[1]))
        w_down = ffn_w[2].reshape(hidden, d)
        u2 = rmsnorm_fwd(f"norm_pre_ffn_{i}", h1, norm_f[i, 2], out_dtype=BF16)
        gp = matmul(f"ffn_in_{i}", u2, w_gu)
        act = swiglu_fwd(f"ffn_act_{i}", gp)
        f = matmul(f"ffn_out_{i}", act, w_down, tk=1408 if hidden % 1408 == 0 else 2048)
        h = rmsnorm_fwd(f"norm_post_ffn_{i}", f, norm_f[i, 3], res=h1)
        s.update(mix=mix, h1=h1, u2=u2, gp=gp, act=act, f=f, w_gu=w_gu, w_down=w_down)
        saved.append(s)

    dh, loss_local = loss_head("loss_head", h, loss_target[0])
    loss = lax.psum(loss_local, AXES)

    g_norm = [[None] * 4 for _ in range(depth)]
    g_convw, g_convb, g_dtb, g_alog, g_dskip, g_ssdnorm = ([None] * n_ssd for _ in range(6))
    g_poolb, g_pools = ([None] * n_pool for _ in range(2))
    hid_s = hidden // N_DEV
    lands = {"in": lax.empty((N_DEV, *ssd_w_in.shape), BF16), "out": lax.empty((N_DEV, *ssd_w_out.shape), BF16),
             "pool": lax.empty((N_DEV, *pool_w.shape), BF16), "gate": lax.empty((N_DEV, *ffn_w_gate.shape), BF16),
             "up": lax.empty((N_DEV, *ffn_w_up.shape), BF16), "down": lax.empty((N_DEV, *ffn_w_down.shape), BF16)}
    scatter_recs, scatter_keys, scatter_views = [], [], []

    def scatter(name, blocks, layer):
        view = _slot_of_layer(layer)
        (rec,), tok = exchange_start(name, [[(b, lands[key], _slot, view) for key, b in blocks]])
        for (key, _), thru in zip(blocks, rec[3]):
            lands[key] = thru
        scatter_recs.append(rec)
        scatter_keys.append([key for key, _ in blocks])
        scatter_views.append([_seven_blocks_of_src] * len(blocks))
        return tok[0, 0]

    tok = jnp.zeros((), F32)
    for i in reversed(range(depth)):
        j = i // 2
        s = saved[i]
        df, g_norm[i][3] = rmsnorm_bwd(f"norm_post_ffn_bwd_{i}", s["f"], norm_f[i, 3] + tok, dh)
        df_b = df.astype(BF16)
        dact = matmul(f"ffn_out_bwd_x_{i}", df_b, s["w_down"], tb=True)
        g_down = matmul(f"ffn_out_bwd_w_{i}", s["act"].T, df_b, out_dtype=BF16, tm=512, tk=1152)
        dgp = swiglu_bwd(f"ffn_act_bwd_{i}", s["gp"], dact)
        g_gu = matmul(f"ffn_in_bwd_w_{i}", s["u2"].T, dgp, out_dtype=BF16, tm=512, tk=1152)
        g_gate, g_up = split_gate_up(g_gu)
        tok = scatter(f"scatter_start_ffn_{i}", [("gate", _block_cols(g_gate)), ("up", _block_cols(g_up)),
                                                 ("down", g_down.reshape(N_DEV, hid_s, d))], i)
        du2 = matmul(f"ffn_in_bwd_x_{i}", dgp, s["w_gu"], tb=True, tk=1408 if hidden % 1408 == 0 else 2048)
        dh1, g_norm[i][2] = rmsnorm_bwd(f"norm_pre_ffn_bwd_{i}", s["h1"], norm_f[i, 2] + tok, du2, add=dh)
        dmix, g_norm[i][1] = rmsnorm_bwd(f"norm_post_mix_bwd_{i}", s["mix"], norm_f[i, 1], dh1)
        if i % 2 == 0:
            dmix_b = dmix.astype(BF16)
            dyn = matmul(f"ssd_out_bwd_x_{i}", dmix_b, s["w_out"], tb=True)
            g_out = matmul(f"ssd_out_bwd_w_{i}", s["yn"].T, dmix_b, out_dtype=BF16, tm=512, tk=1152)
            dy, dz, g_ssdnorm[j] = gatenorm_bwd(f"ssd_gate_bwd_{i}", s["y"], s["zx"], ssd_norm_w[j], dyn, d_inner)
            dx, db, dc, ddt_col, ddt_row, dar, dac, ddsk = ssd_scan_bwd(
                f"ssd_scan_bwd_{i}", s["xbc"], s["dt_col"], s["dt_row"], s["a_row"], s["a_col"], s["d_row"], s["sprev"], dy, d_inner)
            ddt = (jnp.moveaxis(ddt_col[:, :, :rpg], 0, 1).reshape(rows, heads)
                   + jnp.transpose(ddt_row[:, :rpg, :], (2, 0, 1)).reshape(rows, heads))
            d_a = (dar[:, 0, :rpg] + dac[:, :rpg, 0]).reshape(heads)
            g_alog[j] = d_a * s["a_neg"]
            g_dskip[j] = ddsk[:, 0, :rpg].reshape(heads)
            ddtr, dbias = dt_bwd(f"ssd_dt_bwd_{i}", s["zx"], bias_pad[j:j + 1], pad_h(ddt), zx_cols)
            g_dtb[j] = dbias[0, :heads]
            dxbc_raw, g_convw[j], dconvb = conv_bwd(
                f"ssd_conv_bwd_{i}", s["zx"], convw_f[j], ssd_conv_b[j], jnp.concatenate([dx, db, dc], axis=1), d_inner, conv_dim)
            g_convb[j] = dconvb[0]
            dzx = jnp.concatenate([dz, dxbc_raw, ddtr], axis=1)
            g_in = matmul(f"ssd_in_bwd_w_{i}", s["u"].T, dzx, out_dtype=BF16, tm=512, tk=1152,
                          tn=384 if dzx.shape[1] % 384 == 0 else 512)
            tok = scatter(f"scatter_start_mix_{i}", [("in", _block_cols(g_in[:, :d_in_proj])),
                                                     ("out", g_out.reshape(N_DEV, d_inner // N_DEV, d))], j)
            du = matmul(f"ssd_in_bwd_x_{i}", dzx, s["w_in"], tb=True, tk=1152 if dzx.shape[1] % 1152 == 0 else 2048)
        else:
            dmixed, g_poolw, g_poolb[j], g_pools[j] = pool_proj_bwd(
                f"pool_proj_bwd_{i}", dmix, s["pre"], s["mixed"], s["w_pool"], pools_f[j])
            ng, gd = g_poolw.shape[0], g_poolw.shape[1]
            blk_pool = jnp.moveaxis(g_poolw.astype(BF16).reshape(ng, N_DEV, gd // N_DEV, gd), 1, 0)
            tok = scatter(f"scatter_start_mix_{i}", [("pool", blk_pool)], j)
            du = pool_sub(f"pool_sub_bwd_{i}", dmixed, True)
        dh, g_norm[i][0] = rmsnorm_bwd(f"norm_pre_mix_bwd_{i}", s["h"], norm_f[i, 0] + tok, du, add=dh1)

    grad_x = dh[PAD_FRONT + N_META:][None]
    g_meta = dh[PAD_FRONT:PAD_FRONT + N_META]

    small_grads = [g_meta, jnp.stack([jnp.stack(r) for r in g_norm]), jnp.stack(g_convw), jnp.stack(g_convb), jnp.stack(g_dtb),
                   jnp.stack(g_alog), jnp.stack(g_dskip), jnp.stack(g_ssdnorm), jnp.stack(g_poolb), jnp.stack(g_pools)]
    (small_parts,) = all_gather("gather_small_grads", [_pack(small_grads)])
    mine = lambda a: _my_cols(a, a.shape[-1] // N_DEV)
    small_w = [meta_tokens, norm_w, ssd_conv_w, ssd_conv_b, ssd_dt_bias, ssd_a_log, ssd_d, ssd_norm_w, pool_b, pool_scale]
    small_m = [m_meta_tokens, m_norm_w, m_ssd_conv_w, m_ssd_conv_b, m_ssd_dt_bias, m_ssd_a_log, m_ssd_d, m_ssd_norm_w, m_pool_b, m_pool_scale]
    small_v = [v_meta_tokens, v_norm_w, v_ssd_conv_w, v_ssd_conv_b, v_ssd_dt_bias, v_ssd_a_log, v_ssd_d, v_ssd_norm_w, v_pool_b, v_pool_scale]
    sharded = [True, True, True, False, False, False, False, False, True, True]
    def widen(a, is_sharded, full):
        if not is_sharded:
            return a
        return lax.dynamic_update_slice_in_dim(jnp.zeros(full.shape, F32), a, _linear(_my_place()) * a.shape[-1], axis=a.ndim - 1)
    packed_w = _pack([widen(a, sh, g) for a, sh, g in zip(small_w, sharded, small_grads)])
    packed_m = _pack([widen(a, sh, g) for a, sh, g in zip(small_m, sharded, small_grads)])
    packed_v = _pack([widen(a, sh, g) for a, sh, g in zip(small_v, sharded, small_grads)])
    sm = [_unpack(o, small_grads) for o in sum_adamw("adamw_small", small_parts, packed_w, packed_m, packed_v)]
    sm = [[mine(a) if sh else a for a, sh in zip(group, sharded)] for group in sm]

    keys = ["in", "out", "pool", "gate", "up", "down"]
    land_of = [[keys.index(k) for k in ks] for ks in scatter_keys]
    parts = exchange_wait("scatter_wait", scatter_recs, [lands[k] for k in keys], land_of, scatter_views, small_parts)
    big_w = [ssd_w_in, ssd_w_out, pool_w, ffn_w_gate, ffn_w_up, ffn_w_down]
    big_m = [m_ssd_w_in, m_ssd_w_out, m_pool_w, m_ffn_w_gate, m_ffn_w_up, m_ffn_w_down]
    big_v = [v_ssd_w_in, v_ssd_w_out, v_pool_w, v_ffn_w_gate, v_ffn_w_up, v_ffn_w_down]
    names = ["ssd_w_in", "ssd_w_out", "pool_w", "ffn_w_gate", "ffn_w_up", "ffn_w_down"]
    bg = [sum_adamw(f"adamw_{nm}", p, w, m, v) for nm, p, w, m, v in zip(names, parts, big_w, big_m, big_v)]

    def ordered(kind):
        s_ = sm[kind]
        b_ = [o[kind] for o in bg]
        return [s_[0], s_[1], b_[0], s_[2], s_[3], s_[4], s_[5], s_[6], s_[7], b_[1], b_[2], s_[8], s_[9], b_[3], b_[4], b_[5]]

    return (loss, grad_x, *ordered(0), *ordered(1), *ordered(2), *ordered(3))
```

```python
import functools

import jax
import jax.numpy as jnp
from jax import lax
from jax.experimental import pallas as pl
from jax.experimental.pallas import tpu as pltpu

F32 = jnp.float32
BF16 = jnp.bfloat16
MESH = pl.DeviceIdType.MESH
AXES = ("x", "y", "c")
N_DEV = 8

N_META = 16
EPS = 1e-6
HEAD_DIM = 64
D_STATE = 128
SSD_GROUPS = 8
D_CONV = 4
CHUNK = 256
POOL_WINDOWS = (2, 4, 8, 16)
ADAM_LR, ADAM_B1, ADAM_B2, ADAM_EPS, ADAM_WD, ADAM_STEP = 0.001, 0.9, 0.999, 1e-08, 0.01, 10

PAD_FRONT = (-N_META) % CHUNK
LANES = 128
SUBLANES = 8
ROW_TILE = 256
CONV_TILE = 256
POOL_TILE = 128
VMEM_LIMIT = 56 * 1024 * 1024


def _params(sem=None):
    return pltpu.CompilerParams(dimension_semantics=sem, vmem_limit_bytes=VMEM_LIMIT)


def _tile(n, target, mult):
    if n <= target:
        return n
    best = None
    for t in range(mult, target + 1, mult):
        if n % t == 0:
            best = t
    assert best is not None, (n, target, mult)
    return best


def _dot(a, b, ca, cb):
    return lax.dot_general(a, b, (((ca,), (cb,)), ((), ())), preferred_element_type=F32)


def _sigmoid(x):
    return 1.0 / (1.0 + jnp.exp(-x))


def _after(x, dep):
    return lax.optimization_barrier((x, dep))[0]


def _row_mask(shape, first_row):
    rows = lax.broadcasted_iota(jnp.int32, shape, 0) + first_row
    return rows >= PAD_FRONT


MM_ROWS_RESIDENT = dict(tm=2304, tk=2304)
MM_COLS_RESIDENT = dict(tm=512, tn=2304, tk=2304)
MM_DEEP = dict(tm=1152, tn=1024, tk=512)
GATHER_AHEAD = 2


def matmul(name, a, b, *, tb=False, out_dtype=F32, tm=768, tn=512, tk=2048):
    m, kdim = a.shape
    if tb:
        n, k2 = b.shape
    else:
        k2, n = b.shape
    assert kdim == k2, (a.shape, b.shape, tb)
    tm = _tile(m, tm, 16)
    tn = _tile(n, tn, LANES)
    tk = _tile(kdim, tk, LANES)
    nk = kdim // tk
    a_spec = pl.BlockSpec((tm, tk), lambda i, j, k: (i, k))
    b_spec = pl.BlockSpec((tn, tk), lambda i, j, k: (j, k)) if tb else pl.BlockSpec((tk, tn), lambda i, j, k: (k, j))

    def body_single(a_ref, b_ref, o_ref):
        o_ref[...] = _dot(a_ref[...], b_ref[...], 1, 1 if tb else 0).astype(o_ref.dtype)

    def body_acc(a_ref, b_ref, o_ref, acc_ref):
        k = pl.program_id(2)

        @pl.when(k == 0)
        def _():
            acc_ref[...] = jnp.zeros_like(acc_ref)

        acc_ref[...] += _dot(a_ref[...], b_ref[...], 1, 1 if tb else 0)

        @pl.when(k == nk - 1)
        def _():
            o_ref[...] = acc_ref[...].astype(o_ref.dtype)

    return pl.pallas_call(
        body_single if nk == 1 else body_acc, name=name, grid=(m // tm, n // tn, nk),
        in_specs=[a_spec, b_spec], out_specs=pl.BlockSpec((tm, tn), lambda i, j, k: (i, j)),
        out_shape=jax.ShapeDtypeStruct((m, n), out_dtype),
        scratch_shapes=[] if nk == 1 else [pltpu.VMEM((tm, tn), F32)],
        compiler_params=_params(("parallel", "parallel", "arbitrary")),
    )(a, b)


def rmsnorm_fwd(name, x, w, res=None, out_dtype=F32, transposed=False):
    rows, d = x.shape
    tr = ROW_TILE
    row_spec = pl.BlockSpec((tr, d), lambda i: (i, 0))
    w_spec = pl.BlockSpec((1, d), lambda i: (0, 0))

    def body(*refs):
        x_ref, w_ref = refs[:2]
        outs = refs[2 if res is None else 3:]
        xv = x_ref[...]
        y = xv * lax.rsqrt(jnp.mean(xv * xv, axis=-1, keepdims=True) + EPS) * w_ref[...]
        if res is not None:
            y = refs[2][...] + y
        outs[0][...] = y.astype(out_dtype)
        if transposed:
            outs[1][...] = y.T.astype(out_dtype)

    args = [x, w.reshape(1, d)] + ([] if res is None else [res])
    specs = [row_spec, w_spec] + ([] if res is None else [row_spec])
    out_specs, out_shape = [row_spec], [jax.ShapeDtypeStruct((rows, d), out_dtype)]
    if transposed:
        out_specs.append(pl.BlockSpec((d, tr), lambda i: (0, i)))
        out_shape.append(jax.ShapeDtypeStruct((d, rows), out_dtype))
    outs = pl.pallas_call(
        body, name=name, grid=(rows // tr,), in_specs=specs, out_specs=out_specs, out_shape=out_shape,
        compiler_params=_params(("parallel",)),
    )(*args)
    return outs if transposed else outs[0]


def rmsnorm_bwd(name, x, w, dy, add=None):
    rows, d = x.shape
    tr = ROW_TILE
    row_spec = pl.BlockSpec((tr, d), lambda i: (i, 0))
    w_spec = pl.BlockSpec((1, d), lambda i: (0, 0))

    def body(*refs):
        if add is None:
            x_ref, w_ref, dy_ref, dx_ref, dw_ref = refs
        else:
            x_ref, w_ref, dy_ref, add_ref, dx_ref, dw_ref = refs
        xv = x_ref[...]
        dyv = dy_ref[...].astype(F32)
        r = lax.rsqrt(jnp.mean(xv * xv, axis=-1, keepdims=True) + EPS)
        xh = xv * r
        dxh = dyv * w_ref[...]
        dx = r * (dxh - xh * jnp.mean(dxh * xh, axis=-1, keepdims=True))
        if add is not None:
            dx = dx + add_ref[...]
        dx_ref[...] = dx

        @pl.when(pl.program_id(0) == 0)
        def _():
            dw_ref[...] = jnp.zeros_like(dw_ref)

        dw_ref[...] += jnp.sum(dyv * xh, axis=0, keepdims=True)

    args = [x, w.reshape(1, d), dy] + ([] if add is None else [add])
    specs = [row_spec, w_spec, row_spec] + ([] if add is None else [row_spec])
    dx, dw = pl.pallas_call(
        body, name=name, grid=(rows // tr,), in_specs=specs, out_specs=[row_spec, w_spec],
        out_shape=[jax.ShapeDtypeStruct((rows, d), F32), jax.ShapeDtypeStruct((1, d), F32)],
        compiler_params=_params(("arbitrary",)),
    )(*args)
    return dx, dw.reshape(d)


def loss_head(name, h, target):
    rows, d = h.shape
    tr = ROW_TILE
    first = (PAD_FRONT + N_META) // tr
    assert (PAD_FRONT + N_META) % tr == 0 and target.shape[0] == rows - first * tr

    def body(h_ref, t_ref, dh_ref, loss_ref):
        i = pl.program_id(0)

        @pl.when(i == 0)
        def _():
            loss_ref[...] = jnp.zeros_like(loss_ref)

        keep = (i >= first).astype(F32)
        diff = (h_ref[...] - t_ref[...]) * keep
        dh_ref[...] = diff / d
        loss_ref[...] += 0.5 * jnp.sum(diff * diff) / d

    dh, loss = pl.pallas_call(
        body, name=name, grid=(rows // tr,),
        in_specs=[pl.BlockSpec((tr, d), lambda i: (i, 0)), pl.BlockSpec((tr, d), lambda i: (jnp.maximum(i - first, 0), 0))],
        out_specs=[pl.BlockSpec((tr, d), lambda i: (i, 0)), pl.BlockSpec((SUBLANES, LANES), lambda i: (0, 0))],
        out_shape=[jax.ShapeDtypeStruct((rows, d), F32), jax.ShapeDtypeStruct((SUBLANES, LANES), F32)],
        compiler_params=_params(("arbitrary",)),
    )(h, target)
    return dh, loss[0, 0]


def _mm_call(name, a, b, grid, a_spec, b_spec, o_spec, out_sds, ca, cb, red_axis=None):
    n_red = None if red_axis is None else grid[red_axis]

    def body_single(a_ref, b_ref, o_ref):
        o_ref[...] = _dot(a_ref[...], b_ref[...], ca, cb).astype(o_ref.dtype)

    def body_acc(a_ref, b_ref, o_ref, acc_ref):
        k = pl.program_id(red_axis)

        @pl.when(k == 0)
        def _():
            acc_ref[...] = jnp.zeros_like(acc_ref)

        acc_ref[...] += _dot(a_ref[...], b_ref[...], ca, cb)

        @pl.when(k == n_red - 1)
        def _():
            o_ref[...] = acc_ref[...].astype(o_ref.dtype)

    acc_shape = tuple(s for s in o_spec.block_shape if s is not None)
    sem = tuple("arbitrary" if ax == red_axis else "parallel" for ax in range(len(grid)))
    return pl.pallas_call(
        body_single if red_axis is None else body_acc, name=name, grid=grid, in_specs=[a_spec, b_spec], out_specs=o_spec,
        out_shape=out_sds, scratch_shapes=[] if red_axis is None else [pltpu.VMEM(acc_shape, F32)],
        compiler_params=_params(sem),
    )(a, b)


def ffn_in_fwd(name, u2, w_gu):
    rows, d = u2.shape
    hs = w_gu.shape[-1]
    return _mm_call(name, u2, w_gu, (N_DEV, 2), pl.BlockSpec((rows, d), lambda k, t: (0, 0)),
                    pl.BlockSpec((None, None, d, hs), lambda k, t: (k, t, 0, 0)),
                    pl.BlockSpec((None, None, rows, hs), lambda k, t: (k, t, 0, 0)),
                    jax.ShapeDtypeStruct((N_DEV, 2, rows, hs), F32), 1, 0)


def ffn_out_fwd(name, act, w_down):
    _, rows, hs = act.shape
    d = w_down.shape[-1]
    tm, tn = _tile(rows, 1152, 16), _tile(d, 1024, LANES)
    return _mm_call(name, act, w_down, (rows // tm, d // tn, N_DEV), pl.BlockSpec((None, tm, hs), lambda i, j, k: (k, i, 0)),
                    pl.BlockSpec((None, hs, tn), lambda i, j, k: (k, 0, j)), pl.BlockSpec((tm, tn), lambda i, j, k: (i, j)),
                    jax.ShapeDtypeStruct((rows, d), F32), 1, 0, red_axis=2)


def ffn_out_bwd_x(name, df, w_down):
    rows, d = df.shape
    hs = w_down.shape[1]
    return _mm_call(name, df, w_down, (N_DEV,), pl.BlockSpec((rows, d), lambda k: (0, 0)),
                    pl.BlockSpec((None, hs, d), lambda k: (k, 0, 0)), pl.BlockSpec((None, rows, hs), lambda k: (k, 0, 0)),
                    jax.ShapeDtypeStruct((N_DEV, rows, hs), F32), 1, 1)


def ffn_out_bwd_w(name, act_t, df):
    _, hs, rows = act_t.shape
    d = df.shape[1]
    return _mm_call(name, act_t, df, (N_DEV,), pl.BlockSpec((None, hs, rows), lambda k: (k, 0, 0)),
                    pl.BlockSpec((rows, d), lambda k: (0, 0)), pl.BlockSpec((None, hs, d), lambda k: (k, 0, 0)),
                    jax.ShapeDtypeStruct((N_DEV, hs, d), BF16), 1, 0)


def ffn_in_bwd_w(name, u2_t, dgp):
    d, rows = u2_t.shape
    hs = dgp.shape[-1]
    return _mm_call(name, u2_t, dgp, (N_DEV, 2), pl.BlockSpec((d, rows), lambda k, t: (0, 0)),
                    pl.BlockSpec((None, None, rows, hs), lambda k, t: (k, t, 0, 0)),
                    pl.BlockSpec((None, None, d, hs), lambda k, t: (k, t, 0, 0)),
                    jax.ShapeDtypeStruct((N_DEV, 2, d, hs), BF16), 1, 0)


def ffn_in_bwd_x(name, dgp, w_gu):
    _, _, rows, hs = dgp.shape
    d = w_gu.shape[2]
    tm, tn = _tile(rows, 1152, 16), _tile(d, 1024, LANES)
    return _mm_call(name, dgp.reshape(2 * N_DEV, rows, hs), w_gu.reshape(2 * N_DEV, d, hs), (rows // tm, d // tn, 2 * N_DEV),
                    pl.BlockSpec((None, tm, hs), lambda i, j, k: (k, i, 0)), pl.BlockSpec((None, tn, hs), lambda i, j, k: (k, j, 0)),
                    pl.BlockSpec((tm, tn), lambda i, j, k: (i, j)), jax.ShapeDtypeStruct((rows, d), F32), 1, 1, red_axis=2)


def swiglu_fwd(name, gp):
    _, _, rows, hs = gp.shape
    tr = _tile(rows, 768, ROW_TILE)

    def body(gp_ref, a_ref, at_ref):
        g = gp_ref[0]
        act = g * _sigmoid(g) * gp_ref[1]
        a_ref[...] = act.astype(a_ref.dtype)
        at_ref[...] = act.T.astype(at_ref.dtype)

    return pl.pallas_call(
        body, name=name, grid=(N_DEV, rows // tr),
        in_specs=[pl.BlockSpec((None, 2, tr, hs), lambda k, i: (k, 0, i, 0))],
        out_specs=[pl.BlockSpec((None, tr, hs), lambda k, i: (k, i, 0)), pl.BlockSpec((None, hs, tr), lambda k, i: (k, 0, i))],
        out_shape=[jax.ShapeDtypeStruct((N_DEV, rows, hs), BF16), jax.ShapeDtypeStruct((N_DEV, hs, rows), BF16)],
        compiler_params=_params(("parallel", "parallel")),
    )(gp)


def swiglu_bwd(name, gp, dact):
    _, _, rows, hs = gp.shape
    tr = _tile(rows, 768, ROW_TILE)

    def body(gp_ref, da_ref, o_ref):
        g = gp_ref[0]
        s = _sigmoid(g)
        dav = da_ref[...]
        o_ref[0] = (dav * gp_ref[1] * (s * (1.0 + g * (1.0 - s)))).astype(o_ref.dtype)
        o_ref[1] = (dav * (g * s)).astype(o_ref.dtype)

    blk = pl.BlockSpec((None, 2, tr, hs), lambda k, i: (k, 0, i, 0))
    return pl.pallas_call(
        body, name=name, grid=(N_DEV, rows // tr),
        in_specs=[blk, pl.BlockSpec((None, tr, hs), lambda k, i: (k, i, 0))], out_specs=blk,
        out_shape=jax.ShapeDtypeStruct(gp.shape, BF16), compiler_params=_params(("parallel", "parallel")),
    )(gp, dact)


def conv_fwd(name, zx, conv_w, conv_b, d_inner, conv_dim):
    rows = zx.shape[0]
    tc = CONV_TILE
    off = d_inner // tc
    assert d_inner % tc == 0 and conv_dim % tc == 0

    def body(u_ref, w_ref, b_ref, o_ref):
        u = u_ref[...]
        acc = u * w_ref[D_CONV - 1:D_CONV, :] + b_ref[...]
        for s in range(1, D_CONV):
            acc = acc + pltpu.roll(u, s, axis=0) * w_ref[D_CONV - 1 - s:D_CONV - s, :]
        y = acc * _sigmoid(acc)
        o_ref[...] = jnp.where(_row_mask(y.shape, 0), y, 0.0)

    return pl.pallas_call(
        body, name=name, grid=(conv_dim // tc,),
        in_specs=[pl.BlockSpec((rows, tc), lambda j: (0, off + j)), pl.BlockSpec((D_CONV, tc), lambda j: (0, j)),
                  pl.BlockSpec((1, tc), lambda j: (0, j))],
        out_specs=pl.BlockSpec((rows, tc), lambda j: (0, j)),
        out_shape=jax.ShapeDtypeStruct((rows, conv_dim), F32), compiler_params=_params(("parallel",)),
    )(zx, conv_w, conv_b.reshape(1, conv_dim))


def conv_bwd(name, zx, conv_w, conv_b, dxbc, d_inner, conv_dim):
    rows = zx.shape[0]
    tc = CONV_TILE
    off = d_inner // tc

    def body(u_ref, w_ref, b_ref, dy_ref, du_ref, dw_ref, db_ref):
        u = u_ref[...]
        wk = [w_ref[D_CONV - 1 - s:D_CONV - s, :] for s in range(D_CONV)]
        shifted = [u] + [pltpu.roll(u, s, axis=0) for s in range(1, D_CONV)]
        acc = u * wk[0] + b_ref[...]
        for s in range(1, D_CONV):
            acc = acc + shifted[s] * wk[s]
        sg = _sigmoid(acc)
        mask = _row_mask(acc.shape, 0)
        dpre = jnp.where(mask, dy_ref[...] * (sg * (1.0 + acc * (1.0 - sg))), 0.0)
        db_ref[...] = jnp.sum(dpre, axis=0, keepdims=True)
        du = dpre * wk[0]
        dw_ref[D_CONV - 1:D_CONV, :] = jnp.sum(dpre * u, axis=0, keepdims=True)
        for s in range(1, D_CONV):
            du = du + pltpu.roll(dpre, rows - s, axis=0) * wk[s]
            dw_ref[D_CONV - 1 - s:D_CONV - s, :] = jnp.sum(dpre * shifted[s], axis=0, keepdims=True)
        du_ref[...] = jnp.where(mask, du, 0.0).astype(du_ref.dtype)

    return pl.pallas_call(
        body, name=name, grid=(conv_dim // tc,),
        in_specs=[pl.BlockSpec((rows, tc), lambda j: (0, off + j)), pl.BlockSpec((D_CONV, tc), lambda j: (0, j)),
                  pl.BlockSpec((1, tc), lambda j: (0, j)), pl.BlockSpec((rows, tc), lambda j: (0, j))],
        out_specs=[pl.BlockSpec((rows, tc), lambda j: (0, j)), pl.BlockSpec((D_CONV, tc), lambda j: (0, j)),
                   pl.BlockSpec((1, tc), lambda j: (0, j))],
        out_shape=[jax.ShapeDtypeStruct((rows, conv_dim), BF16), jax.ShapeDtypeStruct((D_CONV, conv_dim), F32),
                   jax.ShapeDtypeStruct((1, conv_dim), F32)],
        compiler_params=_params(("parallel",)),
    )(zx, conv_w, conv_b.reshape(1, conv_dim), dxbc)


def dt_fwd(name, zx, bias_pad, zx_cols):
    rows = zx.shape[0]
    tr = ROW_TILE
    off = zx_cols // LANES

    def body(r_ref, b_ref, o_ref):
        v = r_ref[...] + b_ref[...]
        sp = jnp.maximum(v, 0.0) + jnp.log1p(jnp.exp(-jnp.abs(v)))
        o_ref[...] = jnp.where(_row_mask(v.shape, pl.program_id(0) * tr), sp, 0.0)

    return pl.pallas_call(
        body, name=name, grid=(rows // tr,),
        in_specs=[pl.BlockSpec((tr, LANES), lambda i: (i, off)), pl.BlockSpec((1, LANES), lambda i: (0, 0))],
        out_specs=pl.BlockSpec((tr, LANES), lambda i: (i, 0)),
        out_shape=jax.ShapeDtypeStruct((rows, LANES), F32), compiler_params=_params(("parallel",)),
    )(zx, bias_pad)


def dt_bwd(name, zx, bias_pad, ddt, zx_cols):
    rows = zx.shape[0]
    tr = ROW_TILE
    off = zx_cols // LANES

    def body(r_ref, b_ref, d_ref, o_ref, db_ref):
        v = r_ref[...] + b_ref[...]
        g = jnp.where(_row_mask(v.shape, pl.program_id(0) * tr), d_ref[...] * _sigmoid(v), 0.0)
        o_ref[...] = g.astype(o_ref.dtype)

        @pl.when(pl.program_id(0) == 0)
        def _():
            db_ref[...] = jnp.zeros_like(db_ref)

        db_ref[...] += jnp.sum(g, axis=0, keepdims=True)

    return pl.pallas_call(
        body, name=name, grid=(rows // tr,),
        in_specs=[pl.BlockSpec((tr, LANES), lambda i: (i, off)), pl.BlockSpec((1, LANES), lambda i: (0, 0)),
                  pl.BlockSpec((tr, LANES), lambda i: (i, 0))],
        out_specs=[pl.BlockSpec((tr, LANES), lambda i: (i, 0)), pl.BlockSpec((1, LANES), lambda i: (0, 0))],
        out_shape=[jax.ShapeDtypeStruct((rows, LANES), BF16), jax.ShapeDtypeStruct((1, LANES), F32)],
        compiler_params=_params(("arbitrary",)),
    )(zx, bias_pad, ddt)


def _split3(x):
    h1 = x.astype(BF16)
    r1 = x - h1.astype(F32)
    h2 = r1.astype(BF16)
    h3 = (r1 - h2.astype(F32)).astype(BF16)
    return h1, h2, h3


def _exact_left(ones_b, x):
    h1, h2, h3 = _split3(x)
    return _dot(ones_b, h1, 1, 0) + _dot(ones_b, h2, 1, 0) + _dot(ones_b, h3, 1, 0)


def _exact_right_t(x, ones_b):
    h1, h2, h3 = _split3(x)
    return _dot(h1, ones_b, 1, 1) + _dot(h2, ones_b, 1, 1) + _dot(h3, ones_b, 1, 1)


class _ScanCommon:
    def __init__(self, b_ref, c_ref, dtc_ref, dtr_ref, arow_ref, acol_ref, drow_ref):
        q = CHUNK
        self.bb = b_ref[...].astype(BF16)
        self.cb = c_ref[...].astype(BF16)
        ri = lax.broadcasted_iota(jnp.int32, (q, q), 0)
        cj = lax.broadcasted_iota(jnp.int32, (q, q), 1)
        self.lower = ri >= cj
        self.upper = cj >= ri
        self.dtc = dtc_ref[...]
        self.dtr = dtr_ref[...]
        self.arow = arow_ref[...]
        self.acol = acol_ref[...]
        self.drow = drow_ref[...]
        da_col = self.dtc * self.arow
        self.a_col = _exact_left(self.lower.astype(BF16), da_col)
        self.a_row = _exact_right_t(self.dtr * self.acol, self.lower.astype(BF16))
        self.a_last = jnp.sum(da_col, axis=0, keepdims=True)
        self.lane_q = lax.broadcasted_iota(jnp.int32, (q, LANES), 1)
        self.lane_1 = lax.broadcasted_iota(jnp.int32, (1, LANES), 1)
        self.sub_8 = lax.broadcasted_iota(jnp.int32, (SUBLANES, q), 0)
        self.first_half = self.lane_q < HEAD_DIM
        self.first_rows = lax.broadcasted_iota(jnp.int32, (LANES, 1), 0) < HEAD_DIM

    def col(self, v, r):
        return jnp.sum(jnp.where(self.lane_q == r, v, 0.0), axis=1, keepdims=True)

    def row(self, v, r):
        return jnp.sum(jnp.where(self.sub_8 == r, v, 0.0), axis=0, keepdims=True)

    def scalar(self, v, r):
        return jnp.sum(jnp.where(self.lane_1 == r, v, 0.0), axis=1, keepdims=True)

    def pair(self, v0, v1):
        return jnp.where(self.first_half, v0, v1)

    def half_rowsum(self, t, h):
        keep = self.first_half if h == 0 else jnp.logical_not(self.first_half)
        return jnp.sum(jnp.where(keep, t, 0.0), axis=1, keepdims=True)


def ssd_scan_fwd(name, xbc, dt_col, dt_row, a_row, a_col, d_row, d_inner):
    rows = xbc.shape[0]
    q, n, g_cnt = CHUNK, D_STATE, SSD_GROUPS
    nc = rows // q
    rp = d_inner // g_cnt
    n_pairs = rp // LANES
    b_off = d_inner // n

    def body(x_ref, b_ref, c_ref, dtc_ref, dtr_ref, arow_ref, acol_ref, drow_ref, y_ref, sprev_ref, s_ref):
        @pl.when(pl.program_id(1) == 0)
        def _():
            s_ref[...] = jnp.zeros_like(s_ref)

        sprev_ref[...] = s_ref[...]
        k = _ScanCommon(b_ref, c_ref, dtc_ref, dtr_ref, arow_ref, acol_ref, drow_ref)
        cb_mat = _dot(k.cb, k.bb, 1, 1)
        for pr in range(n_pairs):
            sl = slice(pr * LANES, (pr + 1) * LANES)
            heads = (2 * pr, 2 * pr + 1)
            xp = x_ref[:, sl]
            ac = [k.col(k.a_col, r) for r in heads]
            ar = [k.row(k.a_row, r) for r in heads]
            al = [k.scalar(k.a_last, r) for r in heads]
            xd = xp * k.pair(k.col(k.dtc, heads[0]), k.col(k.dtc, heads[1]))
            xdb = xd.astype(BF16)
            ys = []
            for h in range(2):
                lm = jnp.exp(jnp.where(k.lower, ac[h] - ar[h], -jnp.inf))
                ys.append(_dot((cb_mat * lm).astype(BF16), xdb, 1, 0))
            y = jnp.where(k.first_half, ys[0], ys[1])
            sp = s_ref[sl, :]
            y = y + k.pair(jnp.exp(ac[0]), jnp.exp(ac[1])) * _dot(k.cb, sp.astype(BF16), 1, 1)
            y = y + k.pair(k.scalar(k.drow, heads[0]), k.scalar(k.drow, heads[1])) * xp
            y_ref[:, sl] = y
            wb = (xd * k.pair(jnp.exp(al[0] - ac[0]), jnp.exp(al[1] - ac[1]))).astype(BF16)
            decay = jnp.where(k.first_rows, jnp.exp(al[0]), jnp.exp(al[1]))
            s_ref[sl, :] = decay * sp + _dot(wb, k.bb, 0, 0)

    return pl.pallas_call(
        body, name=name, grid=(g_cnt, nc),
        in_specs=[
            pl.BlockSpec((q, rp), lambda g, c: (c, g)),
            pl.BlockSpec((q, n), lambda g, c: (c, b_off + g)),
            pl.BlockSpec((q, n), lambda g, c: (c, b_off + g_cnt + g)),
            pl.BlockSpec((None, q, LANES), lambda g, c: (g, c, 0)),
            pl.BlockSpec((None, SUBLANES, q), lambda g, c: (g, 0, c)),
            pl.BlockSpec((None, 1, LANES), lambda g, c: (g, 0, 0)),
            pl.BlockSpec((None, SUBLANES, 1), lambda g, c: (g, 0, 0)),
            pl.BlockSpec((None, 1, LANES), lambda g, c: (g, 0, 0)),
        ],
        out_specs=[pl.BlockSpec((q, rp), lambda g, c: (c, g)),
                   pl.BlockSpec((None, None, rp, n), lambda g, c: (c, g, 0, 0))],
        out_shape=[jax.ShapeDtypeStruct((rows, d_inner), F32), jax.ShapeDtypeStruct((nc, g_cnt, rp, n), F32)],
        scratch_shapes=[pltpu.VMEM((rp, n), F32)],
        compiler_params=_params(("parallel", "arbitrary")),
    )(xbc, xbc, xbc, dt_col, dt_row, a_row, a_col, d_row)


def ssd_scan_bwd(name, xbc, dt_col, dt_row, a_row, a_col, d_row, sprev, dy, d_inner):
    rows = xbc.shape[0]
    q, n, g_cnt = CHUNK, D_STATE, SSD_GROUPS
    nc = rows // q
    rp = d_inner // g_cnt
    n_pairs = rp // LANES
    b_off = d_inner // n

    def body(x_ref, b_ref, c_ref, dtc_ref, dtr_ref, arow_ref, acol_ref, drow_ref, sprev_ref, dy_ref,
             dx_ref, db_ref, dc_ref, ddtc_ref, ddtr_ref, dar_ref, dac_ref, dd_ref, ds_ref):
        @pl.when(pl.program_id(1) == 0)
        def _():
            ds_ref[...] = jnp.zeros_like(ds_ref)
            dar_ref[...] = jnp.zeros_like(dar_ref)
            dac_ref[...] = jnp.zeros_like(dac_ref)
            dd_ref[...] = jnp.zeros_like(dd_ref)

        k = _ScanCommon(b_ref, c_ref, dtc_ref, dtr_ref, arow_ref, acol_ref, drow_ref)
        cb_mat = _dot(k.cb, k.bb, 1, 1)
        cbt_mat = _dot(k.bb, k.cb, 1, 1)
        d_cb = jnp.zeros((q, q), F32)
        d_b = jnp.zeros((q, n), F32)
        d_c = jnp.zeros((q, n), F32)
        da_col = jnp.zeros((q, LANES), F32)
        da_row = jnp.zeros((SUBLANES, q), F32)
        ddt_x = jnp.zeros((q, LANES), F32)
        d_alast = jnp.zeros((1, LANES), F32)
        d_dskip = jnp.zeros((1, LANES), F32)
        for pr in range(n_pairs):
            sl = slice(pr * LANES, (pr + 1) * LANES)
            heads = (2 * pr, 2 * pr + 1)
            xp = x_ref[:, sl]
            dyp = dy_ref[:, sl]
            dyb = dyp.astype(BF16)
            ac = [k.col(k.a_col, r) for r in heads]
            ar = [k.row(k.a_row, r) for r in heads]
            al = [k.scalar(k.a_last, r) for r in heads]
            dt_p = k.pair(k.col(k.dtc, heads[0]), k.col(k.dtc, heads[1]))
            xd = xp * dt_p
            xdb = xd.astype(BF16)
            sp = sprev_ref[sl, :]
            spb = sp.astype(BF16)
            dsp = ds_ref[sl, :]
            dspb = dsp.astype(BF16)
            dskip_p = k.pair(k.scalar(k.drow, heads[0]), k.scalar(k.drow, heads[1]))
            dxp = dskip_p * dyp
            dd_lane = jnp.sum(dyp * xp, axis=0, keepdims=True)
            e_p = k.pair(jnp.exp(ac[0]), jnp.exp(ac[1]))
            t_off = dyp * (e_p * _dot(k.cb, spb, 1, 1))
            dzb = (e_p * dyp).astype(BF16)
            d_c = d_c + _dot(dzb, spb, 1, 0)
            ds_in = _dot(dzb, k.cb, 0, 0)
            decay = jnp.where(k.first_rows, jnp.exp(al[0]), jnp.exp(al[1]))
            ds_in = ds_in + decay * dsp
            t_state = jnp.sum(dsp * sp, axis=1, keepdims=True) * decay
            dec_p = k.pair(jnp.exp(al[0] - ac[0]), jnp.exp(al[1] - ac[1]))
            dw = _dot(k.bb, dspb, 1, 1)
            d_b = d_b + _dot((xd * dec_p).astype(BF16), dspb, 1, 0)
            dxd = dw * dec_p
            t_dec = dw * xd * dec_p
            dxd_h = []
            for h in range(2):
                r = heads[h]
                keep = k.first_half if h == 0 else jnp.logical_not(k.first_half)
                lm = jnp.exp(jnp.where(k.lower, ac[h] - ar[h], -jnp.inf))
                m_mat = cb_mat * lm
                dm = _dot(jnp.where(keep, dyp, 0.0).astype(BF16), xdb, 1, 1)
                dseg = dm * m_mat
                d_cb = d_cb + dm * lm
                lmt = jnp.exp(jnp.where(k.upper, ar[h] - ac[h], -jnp.inf))
                dxd_h.append(_dot((cbt_mat * lmt).astype(BF16), dyb, 1, 0))
                tdec_h = k.half_rowsum(t_dec, h)
                da_h = k.half_rowsum(t_off, h) - tdec_h + jnp.sum(dseg, axis=1, keepdims=True)
                da_col = da_col + jnp.where(k.lane_q == r, da_h, 0.0)
                da_row = da_row - jnp.where(k.sub_8 == r, jnp.sum(dseg, axis=0, keepdims=True), 0.0)
                keep_rows = k.first_rows if h == 0 else jnp.logical_not(k.first_rows)
                dal_h = jnp.sum(tdec_h, axis=0, keepdims=True) + jnp.sum(jnp.where(keep_rows, t_state, 0.0), axis=0, keepdims=True)
                d_alast = d_alast + jnp.where(k.lane_1 == r, dal_h, 0.0)
                keep_1 = k.lane_1 < HEAD_DIM if h == 0 else k.lane_1 >= HEAD_DIM
                dd_h = jnp.sum(jnp.where(keep_1, dd_lane, 0.0), axis=1, keepdims=True)
                d_dskip = d_dskip + jnp.where(k.lane_1 == r, dd_h, 0.0)
            dxd = dxd + jnp.where(k.first_half, dxd_h[0], dxd_h[1])
            dx_ref[:, sl] = dxp + dt_p * dxd
            t_dt = dxd * xp
            for h in range(2):
                ddt_x = ddt_x + jnp.where(k.lane_q == heads[h], k.half_rowsum(t_dt, h), 0.0)
            ds_ref[sl, :] = ds_in
        d_cb_b = d_cb.astype(BF16)
        db_ref[...] = d_b + _dot(d_cb_b, k.cb, 0, 0)
        dc_ref[...] = d_c + _dot(d_cb_b, k.bb, 1, 0)
        rc_col = _exact_left(k.upper.astype(BF16), da_col) + d_alast
        rc_row = _exact_right_t(da_row, k.upper.astype(BF16))
        ddtc_ref[...] = ddt_x + k.arow * rc_col
        ddtr_ref[...] = k.acol * rc_row
        dar_ref[...] += jnp.sum(rc_col * k.dtc, axis=0, keepdims=True)
        dac_ref[...] += jnp.sum(rc_row * k.dtr, axis=1, keepdims=True)
        dd_ref[...] += d_dskip

    rc = lambda c: nc - 1 - c
    return pl.pallas_call(
        body, name=name, grid=(g_cnt, nc),
        in_specs=[
            pl.BlockSpec((q, rp), lambda g, c: (rc(c), g)),
            pl.BlockSpec((q, n), lambda g, c: (rc(c), b_off + g)),
            pl.BlockSpec((q, n), lambda g, c: (rc(c), b_off + g_cnt + g)),
            pl.BlockSpec((None, q, LANES), lambda g, c: (g, rc(c), 0)),
            pl.BlockSpec((None, SUBLANES, q), lambda g, c: (g, 0, rc(c))),
            pl.BlockSpec((None, 1, LANES), lambda g, c: (g, 0, 0)),
            pl.BlockSpec((None, SUBLANES, 1), lambda g, c: (g, 0, 0)),
            pl.BlockSpec((None, 1, LANES), lambda g, c: (g, 0, 0)),
            pl.BlockSpec((None, None, rp, n), lambda g, c: (rc(c), g, 0, 0)),
            pl.BlockSpec((q, rp), lambda g, c: (rc(c), g)),
        ],
        out_specs=[
            pl.BlockSpec((q, rp), lambda g, c: (rc(c), g)),
            pl.BlockSpec((q, n), lambda g, c: (rc(c), g)),
            pl.BlockSpec((q, n), lambda g, c: (rc(c), g)),
            pl.BlockSpec((None, q, LANES), lambda g, c: (g, rc(c), 0)),
            pl.BlockSpec((None, SUBLANES, q), lambda g, c: (g, 0, rc(c))),
            pl.BlockSpec((None, 1, LANES), lambda g, c: (g, 0, 0)),
            pl.BlockSpec((None, SUBLANES, 1), lambda g, c: (g, 0, 0)),
            pl.BlockSpec((None, 1, LANES), lambda g, c: (g, 0, 0)),
        ],
        out_shape=[
            jax.ShapeDtypeStruct((rows, d_inner), F32),
            jax.ShapeDtypeStruct((rows, g_cnt * n), F32),
            jax.ShapeDtypeStruct((rows, g_cnt * n), F32),
            jax.ShapeDtypeStruct((g_cnt, rows, LANES), F32),
            jax.ShapeDtypeStruct((g_cnt, SUBLANES, rows), F32),
            jax.ShapeDtypeStruct((g_cnt, 1, LANES), F32),
            jax.ShapeDtypeStruct((g_cnt, SUBLANES, 1), F32),
            jax.ShapeDtypeStruct((g_cnt, 1, LANES), F32),
        ],
        scratch_shapes=[pltpu.VMEM((rp, n), F32)],
        compiler_params=_params(("parallel", "arbitrary")),
    )(xbc, xbc, xbc, dt_col, dt_row, a_row, a_col, d_row, sprev, dy)


def gatenorm_fwd(name, y, zx, w, d_inner):
    rows = y.shape[0]
    tr, gw = ROW_TILE, d_inner // SSD_GROUPS

    def body(y_ref, z_ref, w_ref, o_ref, ot_ref):
        z = z_ref[...]
        v = y_ref[...] * (z * _sigmoid(z))
        out = v * lax.rsqrt(jnp.mean(v * v, axis=-1, keepdims=True) + EPS) * w_ref[...]
        o_ref[...] = out.astype(o_ref.dtype)
        ot_ref[...] = out.T.astype(ot_ref.dtype)

    blk = pl.BlockSpec((tr, gw), lambda i, j: (i, j))
    return pl.pallas_call(
        body, name=name, grid=(rows // tr, SSD_GROUPS),
        in_specs=[blk, blk, pl.BlockSpec((1, gw), lambda i, j: (0, j))],
        out_specs=[blk, pl.BlockSpec((gw, tr), lambda i, j: (j, i))],
        out_shape=[jax.ShapeDtypeStruct((rows, d_inner), BF16), jax.ShapeDtypeStruct((d_inner, rows), BF16)],
        compiler_params=_params(("parallel", "parallel")),
    )(y, zx, w.reshape(1, d_inner))


def gatenorm_bwd(name, y, zx, w, dyn, d_inner):
    rows = y.shape[0]
    tr, gw = ROW_TILE, d_inner // SSD_GROUPS

    def body(y_ref, z_ref, w_ref, dn_ref, dy_ref, dz_ref, dw_ref):
        z = z_ref[...]
        yv = y_ref[...]
        s = _sigmoid(z)
        gate = z * s
        v = yv * gate
        r = lax.rsqrt(jnp.mean(v * v, axis=-1, keepdims=True) + EPS)
        vh = v * r
        dn = dn_ref[...]
        dvh = dn * w_ref[...]
        dv = r * (dvh - vh * jnp.mean(dvh * vh, axis=-1, keepdims=True))
        dy_ref[...] = dv * gate
        dz_ref[...] = (dv * yv * (s * (1.0 + z * (1.0 - s)))).astype(dz_ref.dtype)

        @pl.when(pl.program_id(1) == 0)
        def _():
            dw_ref[...] = jnp.zeros_like(dw_ref)

        dw_ref[...] += jnp.sum(dn * vh, axis=0, keepdims=True)

    blk = pl.BlockSpec((tr, gw), lambda j, i: (i, j))
    wblk = pl.BlockSpec((1, gw), lambda j, i: (0, j))
    dy, dz, dw = pl.pallas_call(
        body, name=name, grid=(SSD_GROUPS, rows // tr),
        in_specs=[blk, blk, wblk, blk], out_specs=[blk, blk, wblk],
        out_shape=[jax.ShapeDtypeStruct((rows, d_inner), F32), jax.ShapeDtypeStruct((rows, d_inner), BF16),
                   jax.ShapeDtypeStruct((1, d_inner), F32)],
        compiler_params=_params(("parallel", "arbitrary")),
    )(y, zx, w.reshape(1, d_inner), dyn)
    return dy, dz, dw.reshape(d_inner)


def _pool_count(rows, g):
    t1 = lax.broadcasted_iota(jnp.int32, (rows, 1), 0) - (PAD_FRONT - 1)
    win = jnp.left_shift(jnp.int32(POOL_WINDOWS[0]), g)
    return jnp.clip(t1, 1, win).astype(F32)


def _pool_select(levels, g):
    out = levels[-1]
    for i in range(len(levels) - 2, -1, -1):
        out = jnp.where(g == i, levels[i], out)
    return out


def pool_sub(name, u, transpose):
    rows, d = u.shape
    gd = d // len(POOL_WINDOWS)
    assert all(w == POOL_WINDOWS[0] << i for i, w in enumerate(POOL_WINDOWS))

    def body(u_ref, o_ref):
        g = pl.program_id(0)
        v = u_ref[...].astype(F32)
        cnt = _pool_count(rows, g)
        mask = _row_mask(v.shape, 0)
        s = v / cnt if transpose else v
        levels = []
        for i in range(len(POOL_WINDOWS)):
            step = 1 << i
            s = s + pltpu.roll(s, (rows - step) if transpose else step, axis=0)
            levels.append(s)
        sel = _pool_select(levels, g)
        out = (sel - v) if transpose else (sel / cnt - v)
        o_ref[...] = jnp.where(mask, out, 0.0).astype(o_ref.dtype)

    tc = POOL_TILE
    per = gd // tc
    blk = pl.BlockSpec((rows, tc), lambda g, j: (0, g * per + j))
    return pl.pallas_call(
        body, name=name, grid=(len(POOL_WINDOWS), per), in_specs=[blk], out_specs=blk,
        out_shape=jax.ShapeDtypeStruct((rows, d), F32 if transpose else BF16),
        compiler_params=_params(("parallel", "parallel")),
    )(u)


def pool_proj_fwd(name, mixed, w, b, scale):
    rows, d = mixed.shape
    ng = len(POOL_WINDOWS)
    gd = d // ng
    tr = ROW_TILE

    def body(m_ref, w_ref, b_ref, s_ref, pre_ref, mix_ref):
        pre = _dot(m_ref[...], w_ref[...], 1, 0) + b_ref[...]
        pre_ref[...] = pre
        mix_ref[...] = jnp.where(_row_mask(pre.shape, pl.program_id(1) * tr), pre * s_ref[...], 0.0)

    blk = pl.BlockSpec((tr, gd), lambda g, i: (i, g))
    vec = pl.BlockSpec((1, gd), lambda g, i: (0, g))
    return pl.pallas_call(
        body, name=name, grid=(ng, rows // tr),
        in_specs=[blk, pl.BlockSpec((None, gd, gd), lambda g, i: (g, 0, 0)), vec, vec], out_specs=[blk, blk],
        out_shape=[jax.ShapeDtypeStruct((rows, d), F32), jax.ShapeDtypeStruct((rows, d), F32)],
        compiler_params=_params(("parallel", "parallel")),
    )(mixed, w, b.reshape(1, d), scale.reshape(1, d))


def pool_proj_bwd(name, dmix, pre, mixed, w, scale):
    rows, d = dmix.shape
    ng = len(POOL_WINDOWS)
    gd = d // ng
    tr = ROW_TILE

    def body(dm_ref, pre_ref, mx_ref, w_ref, s_ref, dmx_ref, dw_ref, db_ref, ds_ref):
        @pl.when(pl.program_id(1) == 0)
        def _():
            dw_ref[...] = jnp.zeros_like(dw_ref)
            db_ref[...] = jnp.zeros_like(db_ref)
            ds_ref[...] = jnp.zeros_like(ds_ref)

        dmv = jnp.where(_row_mask(dm_ref.shape, pl.program_id(1) * tr), dm_ref[...], 0.0)
        dpre = dmv * s_ref[...]
        dpre_b = dpre.astype(BF16)
        ds_ref[...] += jnp.sum(dmv * pre_ref[...], axis=0, keepdims=True)
        db_ref[...] += jnp.sum(dpre, axis=0, keepdims=True)
        dmx_ref[...] = _dot(dpre_b, w_ref[...], 1, 1)
        dw_ref[...] += _dot(mx_ref[...], dpre_b, 0, 0)

    blk = pl.BlockSpec((tr, gd), lambda g, i: (i, g))
    vec = pl.BlockSpec((1, gd), lambda g, i: (0, g))
    wblk = pl.BlockSpec((None, gd, gd), lambda g, i: (g, 0, 0))
    dmixed, dw, db, ds = pl.pallas_call(
        body, name=name, grid=(ng, rows // tr),
        in_specs=[blk, blk, blk, wblk, vec], out_specs=[blk, wblk, vec, vec],
        out_shape=[jax.ShapeDtypeStruct((rows, d), F32), jax.ShapeDtypeStruct((ng, gd, gd), F32),
                   jax.ShapeDtypeStruct((1, d), F32), jax.ShapeDtypeStruct((1, d), F32)],
        compiler_params=_params(("parallel", "arbitrary")),
    )(dmix, pre, mixed, w, scale.reshape(1, d))
    return dmixed, dw, db.reshape(d), ds.reshape(d)


def _my_place():
    return lax.axis_index("x"), lax.axis_index("y"), lax.axis_index("c")


def _linear(place):
    return 4 * place[0] + 2 * place[1] + place[2]


def all_gather(name, shards):
    n_ops = len(shards)

    def body(*refs):
        ins, outs = refs[:n_ops], refs[n_ops:2 * n_ops]
        send_sems, recv_sems, local_sems = refs[2 * n_ops:]
        x, y, c = _my_place()
        me, sibling = (x, y, c), (x, y, 1 - c)
        chips = [(1 - x, y), (x, 1 - y), (1 - x, 1 - y)]

        def copy(t, k, block, to, src=None):
            dst = outs[t].at[_linear(block)]
            return pltpu.make_async_remote_copy(
                src_ref=dst if src is None else src, dst_ref=dst, send_sem=send_sems.at[t, k], recv_sem=recv_sems.at[t, k],
                device_id=to, device_id_type=MESH)

        mine = [pltpu.make_async_copy(ins[t], outs[t].at[_linear(me)], local_sems.at[t]) for t in range(n_ops)]
        for cp in mine:
            cp.start()
        first = []
        for t in range(n_ops):
            first.append(copy(t, 0, me, sibling, src=ins[t]))
            first += [copy(t, 1 + j, me, (*chip, c), src=ins[t]) for j, chip in enumerate(chips)]
        for cp in first:
            cp.start()
        passed = []
        for j, chip in enumerate(chips):
            for t in range(n_ops):
                copy(t, 1 + j, (*chip, c), me).wait_recv()
                fwd = copy(t, 4 + j, (*chip, c), sibling)
                fwd.start()
                passed.append(fwd)
        for t in range(n_ops):
            copy(t, 0, sibling, me).wait_recv()
            for j, chip in enumerate(chips):
                copy(t, 4 + j, (*chip, 1 - c), me).wait_recv()
        for cp in first + passed:
            cp.wait_send()
        for cp in mine:
            cp.wait()

    any_spec = pl.BlockSpec(memory_space=pl.ANY)
    outs = pl.pallas_call(
        body, name=name,
        in_specs=[any_spec] * n_ops, out_specs=[any_spec] * n_ops,
        out_shape=[jax.ShapeDtypeStruct((N_DEV, *s.shape), s.dtype) for s in shards],
        scratch_shapes=[pltpu.SemaphoreType.DMA((n_ops, 7)), pltpu.SemaphoreType.DMA((n_ops, 7)),
                        pltpu.SemaphoreType.DMA((n_ops,))],
    )(*shards)
    return list(outs)


_HBM = pl.BlockSpec(memory_space=pltpu.HBM)
_SEM = pl.BlockSpec(memory_space=pltpu.SEMAPHORE)
_EFFECT = pltpu.SideEffectType.DATAFLOW_SIDE_EFFECTING


def _hbm(a):
    return pltpu.with_memory_space_constraint(a, pltpu.HBM)


def _peers():
    x, y, c = _my_place()
    return [(x ^ (j >> 2), y ^ ((j >> 1) & 1), c ^ (j & 1)) for j in range(1, N_DEV)]


def exchange_start(name, groups):
    flat = [e for g in groups for e in g]
    n = len(flat)

    def body(*refs):
        srcs, lands = refs[:n], refs[n:2 * n]
        outs = refs[2 * n:]
        sends, recvs, token = outs[:n], outs[n:2 * n], outs[4 * n]
        me = _linear(_my_place())
        for t, (_, _, src_view, land_view) in enumerate(flat):
            for peer in _peers():
                pltpu.make_async_remote_copy(
                    src_ref=src_view(srcs[t], _linear(peer)), dst_ref=land_view(lands[t], me),
                    send_sem=sends[t], recv_sem=recvs[t], device_id=peer, device_id_type=MESH).start()
        token[...] = jnp.zeros_like(token)

    sem_shapes = [pltpu.SemaphoreType.DMA(())] * (2 * n)
    thru = [pltpu.HBM(e[0].shape, e[0].dtype) for e in flat] + [pltpu.HBM(e[1].shape, e[1].dtype) for e in flat]
    outs = pl.pallas_call(
        body, name=name,
        out_shape=(*sem_shapes, *thru, jax.ShapeDtypeStruct((SUBLANES, LANES), F32)),
        in_specs=[_HBM] * (2 * n), out_specs=(*[_SEM] * (2 * n), *[_HBM] * (2 * n), pl.BlockSpec(memory_space=pltpu.VMEM)),
        input_output_aliases={t: 2 * n + t for t in range(2 * n)},
        compiler_params=pltpu.CompilerParams(has_side_effects=_EFFECT),
    )(*[_hbm(e[0]) for e in flat], *[_hbm(e[1]) for e in flat])
    records, t = [], 0
    for g in groups:
        k = len(g)
        records.append((list(outs[t:t + k]), list(outs[n + t:n + t + k]), list(outs[2 * n + t:2 * n + t + k]),
                        list(outs[3 * n + t:3 * n + t + k])))
        t += k
    return records, outs[-1]


def exchange_wait(name, records, lands, land_of, views, after):
    srcs = [s for r in records for s in r[2]]
    sends = [s for r in records for s in r[0]]
    recvs = [s for r in records for s in r[1]]
    where = [(ri, k) for ri, r in enumerate(records) for k in range(len(r[2]))]
    ns, nl = len(srcs), len(lands)

    def body(*refs):
        src_refs, land_refs = refs[:ns], refs[ns:ns + nl]
        send_refs, recv_refs = refs[ns + nl:2 * ns + nl], refs[2 * ns + nl:3 * ns + nl]
        local_sems = refs[-1]
        me = _linear(_my_place())
        own = []
        for t, (ri, k) in enumerate(where):
            src_view, land_view, seven = views[ri][k]
            land_ref = land_refs[land_of[ri][k]]
            own.append(pltpu.make_async_copy(src_view(src_refs[t], me), land_view(land_ref, me), local_sems.at[t]))
            own[-1].start()
            seven_ref = seven(src_refs[t], land_ref)
            cp = pltpu.make_async_remote_copy(src_ref=seven_ref, dst_ref=seven_ref, send_sem=send_refs[t], recv_sem=recv_refs[t],
                                              device_id=_my_place(), device_id_type=MESH)
            cp.wait_send()
            cp.wait_recv()
        for cp in own:
            cp.wait()

    outs = pl.pallas_call(
        body, name=name,
        out_shape=tuple(pltpu.HBM(a.shape, a.dtype) for a in (*srcs, *lands)),
        in_specs=[*[_HBM] * (ns + nl), *[_SEM] * (2 * ns), pl.BlockSpec(memory_space=pl.ANY)], out_specs=tuple([_HBM] * (ns + nl)),
        input_output_aliases={t: t for t in range(ns + nl)},
        scratch_shapes=[pltpu.SemaphoreType.DMA((ns,))],
        compiler_params=pltpu.CompilerParams(has_side_effects=_EFFECT),
    )(*srcs, *lands, *sends, *recvs, after)
    return list(outs[ns:])


def _seven_slots_of_land(src_ref, land_ref):
    return land_ref.at[pl.ds(0, N_DEV - 1)]


def _seven_blocks_of_src(src_ref, land_ref):
    return src_ref.at[pl.ds(0, N_DEV - 1)]


def _whole(ref, dev):
    return ref


def _slot(ref, dev):
    return ref.at[dev]


def _slot_of_layer(layer):
    return lambda ref, dev: ref.at[dev, layer]


def _as_rows(a, lead=0):
    return a.reshape(a.shape[:lead] + (-1, a.shape[-1]))


def sum_adamw(name, parts, w, m, v, half=None):
    shape = w.shape
    c1 = 1.0 - ADAM_B1 ** ADAM_STEP
    c2 = 1.0 - ADAM_B2 ** ADAM_STEP
    if half is not None:
        layers, r, cols = shape
        tr = _tile(r, ROW_TILE, 16)
        grid = (layers, r // tr)
        p_spec = pl.BlockSpec((N_DEV, None, None, tr, cols), lambda l, i: (0, l, half, i, 0))
        blk = pl.BlockSpec((None, tr, cols), lambda l, i: (l, i, 0))
        args, out_sds, sem = (parts, w, m, v), jax.ShapeDtypeStruct(shape, F32), ("parallel", "parallel")
    else:
        p2, w2, m2, v2 = _as_rows(parts, 1), _as_rows(w), _as_rows(m), _as_rows(v)
        rows, cols = w2.shape
        tr = _tile(rows, ROW_TILE, 16)
        grid = (rows // tr,)
        p_spec = pl.BlockSpec((N_DEV, tr, cols), lambda i: (0, i, 0))
        blk = pl.BlockSpec((tr, cols), lambda i: (i, 0))
        args, out_sds, sem = (p2, w2, m2, v2), jax.ShapeDtypeStruct((rows, cols), F32), ("parallel",)

    def body(p_ref, w_ref, m_ref, v_ref, g_ref, d_ref, nm_ref, nv_ref):
        g = p_ref[0].astype(F32)
        for k in range(1, N_DEV):
            g = g + p_ref[k].astype(F32)
        wv = w_ref[...]
        nm = ADAM_B1 * m_ref[...] + (1.0 - ADAM_B1) * g
        nv = ADAM_B2 * v_ref[...] + (1.0 - ADAM_B2) * (g * g)
        g_ref[...] = g
        nm_ref[...] = nm
        nv_ref[...] = nv
        d_ref[...] = -ADAM_LR * ((nm / c1) / (jnp.sqrt(nv / c2) + ADAM_EPS) + ADAM_WD * wv)

    outs = pl.pallas_call(
        body, name=name, grid=grid, in_specs=[p_spec, blk, blk, blk], out_specs=[blk] * 4,
        out_shape=[out_sds] * 4, compiler_params=_params(sem),
    )(*args)
    return [o.reshape(shape) for o in outs]


def _unblock_cols(g):
    g = jnp.moveaxis(g, 0, -2)
    return g.reshape(g.shape[:-2] + (g.shape[-2] * g.shape[-1],))


def _block_cols(a):
    r, c = a.shape
    return jnp.moveaxis(a.reshape(r, N_DEV, c // N_DEV), 1, 0)


def _my_cols(a, n):
    me = _linear(_my_place())
    return lax.dynamic_slice_in_dim(a, me * n, n, axis=a.ndim - 1)


def _pack(arrays):
    flat = jnp.concatenate([a.reshape(-1).astype(F32) for a in arrays])
    pad = (-flat.shape[0]) % (ROW_TILE * LANES)
    return jnp.pad(flat, (0, pad)).reshape(-1, LANES)


def _unpack(packed, like):
    flat = packed.reshape(-1)
    out, pos = [], 0
    for a in like:
        out.append(flat[pos:pos + a.size].reshape(a.shape))
        pos += a.size
    return out


def kernel(x, meta_tokens, norm_w, ssd_w_in, ssd_conv_w, ssd_conv_b, ssd_dt_bias, ssd_a_log, ssd_d, ssd_norm_w, ssd_w_out, pool_w, pool_b, pool_scale, ffn_w_gate, ffn_w_up, ffn_w_down, loss_target, m_meta_tokens, m_norm_w, m_ssd_w_in, m_ssd_conv_w, m_ssd_conv_b, m_ssd_dt_bias, m_ssd_a_log, m_ssd_d, m_ssd_norm_w, m_ssd_w_out, m_pool_w, m_pool_b, m_pool_scale, m_ffn_w_gate, m_ffn_w_up, m_ffn_w_down, v_meta_tokens, v_norm_w, v_ssd_w_in, v_ssd_conv_w, v_ssd_conv_b, v_ssd_dt_bias, v_ssd_a_log, v_ssd_d, v_ssd_norm_w, v_ssd_w_out, v_pool_w, v_pool_b, v_pool_scale, v_ffn_w_gate, v_ffn_w_up, v_ffn_w_down):
    seq, d = x.shape[1], x.shape[2]
    depth = norm_w.shape[0]
    n_ssd = ssd_w_in.shape[0]
    d_inner = ssd_norm_w.shape[1]
    heads = d_inner // HEAD_DIM
    rpg = heads // SSD_GROUPS
    conv_dim = ssd_conv_b.shape[1]
    zx_cols = d_inner + conv_dim
    d_in_proj = zx_cols + heads
    rows = PAD_FRONT + N_META + seq
    assert rows % CHUNK == 0 and rpg % 2 == 0 and heads <= LANES and rpg <= SUBLANES
    sd = d // N_DEV

    hidden = ffn_w_down.shape[1] * N_DEV
    n_pool = pool_w.shape[0]

    small = all_gather("gather_small", [meta_tokens, norm_w, ssd_conv_w, pool_b, pool_scale])
    meta_f, norm_f, convw_f, poolb_f, pools_f = [_unblock_cols(s) for s in small]

    gather_recs = {}

    def start_gather(i, after):
        def entry(shard):
            return (_after(shard.astype(BF16), after), lax.empty((N_DEV, *shard.shape), BF16), _whole, _slot)
        mixer = [ssd_w_in[i // 2], ssd_w_out[i // 2]] if i % 2 == 0 else [pool_w[i // 2]]
        gate_up = jnp.stack([ffn_w_gate[i], ffn_w_up[i]])
        groups = [[entry(w) for w in mixer], [entry(gate_up), entry(ffn_w_down[i])]]
        gather_recs[i], tok = exchange_start(f"gather_start_{i}", groups)
        return tok[0, 0]

    def gathered(name, rec, after):
        n_e = len(rec[2])
        return exchange_wait(name, [rec], rec[3], [list(range(n_e))], [[(_whole, _slot, _seven_slots_of_land)] * n_e], after)

    started = meta_f
    for i in range(min(GATHER_AHEAD, depth)):
        started = start_gather(i, started)

    pad_h = lambda a: jnp.pad(a.astype(F32), ((0, 0), (0, LANES - heads)))
    bias_pad = pad_h(ssd_dt_bias)

    def head_layouts(vec):
        g = vec.reshape(SSD_GROUPS, rpg)
        row = jnp.pad(g, ((0, 0), (0, LANES - rpg)))[:, None, :]
        col = jnp.pad(g, ((0, 0), (0, SUBLANES - rpg)))[:, :, None]
        return row, col

    def dt_layouts(dt):
        g = dt[:, :heads].reshape(rows, SSD_GROUPS, rpg)
        col = jnp.pad(jnp.moveaxis(g, 1, 0), ((0, 0), (0, 0), (0, LANES - rpg)))
        row = jnp.pad(jnp.transpose(g, (1, 2, 0)), ((0, 0), (0, SUBLANES - rpg), (0, 0)))
        return col, row

    h = jnp.concatenate([jnp.zeros((PAD_FRONT, d), F32), _after(meta_f, started), x[0]], axis=0)
    saved = []
    for i in range(depth):
        j = i // 2
        s = {"h": h}
        w_pre_mix = norm_f[i, 0]
        if i + GATHER_AHEAD < depth:
            w_pre_mix = _after(w_pre_mix, start_gather(i + GATHER_AHEAD, h))
        mixer_w = gathered(f"gather_wait_mix_{i}", gather_recs[i][0], h)
        if i % 2 == 0:
            w_in = jnp.pad(_unblock_cols(mixer_w[0]), ((0, 0), (0, LANES - heads)))
            w_out = mixer_w[1].reshape(d_inner, d)
            s.update(w_in=w_in, w_out=w_out)
            u, u_t = rmsnorm_fwd(f"norm_pre_mix_{i}", h, w_pre_mix, out_dtype=BF16, transposed=True)
            zx = matmul(f"ssd_in_{i}", u, w_in, **MM_ROWS_RESIDENT, tn=384 if w_in.shape[1] % 384 == 0 else 512)
            xbc = conv_fwd(f"ssd_conv_{i}", zx, convw_f[j], ssd_conv_b[j], d_inner, conv_dim)
            dt = dt_fwd(f"ssd_dt_{i}", zx, bias_pad[j:j + 1], zx_cols)
            dt_col, dt_row = dt_layouts(dt)
            a_neg = -jnp.exp(ssd_a_log[j].astype(F32))
            a_row, a_col = head_layouts(a_neg)
            d_row, _ = head_layouts(ssd_d[j].astype(F32))
            y, sprev = ssd_scan_fwd(f"ssd_scan_{i}", xbc, dt_col, dt_row, a_row, a_col, d_row, d_inner)
            yn, yn_t = gatenorm_fwd(f"ssd_gate_{i}", y, zx, ssd_norm_w[j], d_inner)
            mix = matmul(f"ssd_out_{i}", yn, w_out, **MM_DEEP)
            s.update(u_t=u_t, zx=zx, xbc=xbc, dt_col=dt_col, dt_row=dt_row, a_row=a_row, a_col=a_col, d_row=d_row,
                     a_neg=a_neg, y=y, sprev=sprev, yn_t=yn_t)
        else:
            w_pool = jnp.moveaxis(mixer_w[0], 0, 1).reshape(len(POOL_WINDOWS), d // len(POOL_WINDOWS), -1)
            u = rmsnorm_fwd(f"norm_pre_mix_{i}", h, w_pre_mix)
            mixed = pool_sub(f"pool_sub_{i}", u, False)
            pre, mix = pool_proj_fwd(f"pool_proj_{i}", mixed, w_pool, poolb_f[j], pools_f[j])
            s.update(mixed=mixed, pre=pre, w_pool=w_pool)
        h1 = rmsnorm_fwd(f"norm_post_mix_{i}", mix, norm_f[i, 1], res=h)
        w_gu, w_down = gathered(f"gather_wait_ffn_{i}", gather_recs[i][1], h1)
        u2, u2_t = rmsnorm_fwd(f"norm_pre_ffn_{i}", h1, norm_f[i, 2], out_dtype=BF16, transposed=True)
        gp = ffn_in_fwd(f"ffn_in_{i}", u2, w_gu)
        act, act_t = swiglu_fwd(f"ffn_act_{i}", gp)
        f = ffn_out_fwd(f"ffn_out_{i}", act, w_down)
        h = rmsnorm_fwd(f"norm_post_ffn_{i}", f, norm_f[i, 3], res=h1)
        s.update(mix=mix, h1=h1, u2_t=u2_t, gp=gp, act_t=act_t, f=f, w_gu=w_gu, w_down=w_down)
        saved.append(s)

    dh, loss_local = loss_head("loss_head", h, loss_target[0])
    loss = lax.psum(loss_local, AXES)

    g_norm = [[None] * 4 for _ in range(depth)]
    g_convw, g_convb, g_dtb, g_alog, g_dskip, g_ssdnorm = ([None] * n_ssd for _ in range(6))
    g_poolb, g_pools = ([None] * n_pool for _ in range(2))
    hid_s = hidden // N_DEV
    lands = {"in": lax.empty((N_DEV, *ssd_w_in.shape), BF16), "out": lax.empty((N_DEV, *ssd_w_out.shape), BF16),
             "pool": lax.empty((N_DEV, *pool_w.shape), BF16), "down": lax.empty((N_DEV, *ffn_w_down.shape), BF16),
             "gate_up": lax.empty((N_DEV, depth, 2, d, hid_s), BF16)}
    scatter_recs, scatter_keys, scatter_views = [], [], []

    def scatter(name, blocks, layer):
        view = _slot_of_layer(layer)
        (rec,), tok = exchange_start(name, [[(b, lands[key], _slot, view) for key, b in blocks]])
        for (key, _), thru in zip(blocks, rec[3]):
            lands[key] = thru
        scatter_recs.append(rec)
        scatter_keys.append([key for key, _ in blocks])
        scatter_views.append([(_slot, view, _seven_blocks_of_src)] * len(blocks))
        return tok[0, 0]

    tok = jnp.zeros((), F32)
    for i in reversed(range(depth)):
        j = i // 2
        s = saved[i]
        df, g_norm[i][3] = rmsnorm_bwd(f"norm_post_ffn_bwd_{i}", s["f"], _after(norm_f[i, 3], tok), dh)
        df_b = df.astype(BF16)
        dact = ffn_out_bwd_x(f"ffn_out_bwd_x_{i}", df_b, s["w_down"])
        g_down = ffn_out_bwd_w(f"ffn_out_bwd_w_{i}", s["act_t"], df_b)
        dgp = swiglu_bwd(f"ffn_act_bwd_{i}", s["gp"], dact)
        g_gu = ffn_in_bwd_w(f"ffn_in_bwd_w_{i}", s["u2_t"], dgp)
        tok = scatter(f"scatter_start_ffn_{i}", [("gate_up", g_gu), ("down", g_down)], i)
        du2 = ffn_in_bwd_x(f"ffn_in_bwd_x_{i}", dgp, s["w_gu"])
        dh1, g_norm[i][2] = rmsnorm_bwd(f"norm_pre_ffn_bwd_{i}", s["h1"], _after(norm_f[i, 2], tok), du2, add=dh)
        dmix, g_norm[i][1] = rmsnorm_bwd(f"norm_post_mix_bwd_{i}", s["mix"], norm_f[i, 1], dh1)
        if i % 2 == 0:
            dmix_b = dmix.astype(BF16)
            dyn = matmul(f"ssd_out_bwd_x_{i}", dmix_b, s["w_out"], tb=True, **MM_ROWS_RESIDENT)
            g_out = matmul(f"ssd_out_bwd_w_{i}", s["yn_t"], dmix_b, out_dtype=BF16, **MM_COLS_RESIDENT)
            dy, dz, g_ssdnorm[j] = gatenorm_bwd(f"ssd_gate_bwd_{i}", s["y"], s["zx"], ssd_norm_w[j], dyn, d_inner)
            dx, db, dc, ddt_col, ddt_row, dar, dac, ddsk = ssd_scan_bwd(
                f"ssd_scan_bwd_{i}", s["xbc"], s["dt_col"], s["dt_row"], s["a_row"], s["a_col"], s["d_row"], s["sprev"], dy, d_inner)
            ddt = (jnp.moveaxis(ddt_col[:, :, :rpg], 0, 1).reshape(rows, heads)
                   + jnp.transpose(ddt_row[:, :rpg, :], (2, 0, 1)).reshape(rows, heads))
            d_a = (dar[:, 0, :rpg] + dac[:, :rpg, 0]).reshape(heads)
            g_alog[j] = d_a * s["a_neg"]
            g_dskip[j] = ddsk[:, 0, :rpg].reshape(heads)
            ddtr, dbias = dt_bwd(f"ssd_dt_bwd_{i}", s["zx"], bias_pad[j:j + 1], pad_h(ddt), zx_cols)
            g_dtb[j] = dbias[0, :heads]
            dxbc_raw, g_convw[j], dconvb = conv_bwd(
                f"ssd_conv_bwd_{i}", s["zx"], convw_f[j], ssd_conv_b[j], jnp.concatenate([dx, db, dc], axis=1), d_inner, conv_dim)
            g_convb[j] = dconvb[0]
            dzx = jnp.concatenate([dz, dxbc_raw, ddtr], axis=1)
            g_in = matmul(f"ssd_in_bwd_w_{i}", s["u_t"], dzx, out_dtype=BF16, **MM_ROWS_RESIDENT,
                          tn=384 if dzx.shape[1] % 384 == 0 else 512)
            tok = scatter(f"scatter_start_mix_{i}", [("in", _block_cols(g_in[:, :d_in_proj])),
                                                     ("out", g_out.reshape(N_DEV, d_inner // N_DEV, d))], j)
            du = matmul(f"ssd_in_bwd_x_{i}", dzx, s["w_in"], tb=True, **{**MM_DEEP, "tk": 1152 if dzx.shape[1] % 1152 == 0 else 512})
        else:
            dmixed, g_poolw, g_poolb[j], g_pools[j] = pool_proj_bwd(
                f"pool_proj_bwd_{i}", dmix, s["pre"], s["mixed"], s["w_pool"], pools_f[j])
            ng, gd = g_poolw.shape[0], g_poolw.shape[1]
            blk_pool = jnp.moveaxis(g_poolw.astype(BF16).reshape(ng, N_DEV, gd // N_DEV, gd), 1, 0)
            tok = scatter(f"scatter_start_mix_{i}", [("pool", blk_pool)], j)
            du = pool_sub(f"pool_sub_bwd_{i}", dmixed, True)
        dh, g_norm[i][0] = rmsnorm_bwd(f"norm_pre_mix_bwd_{i}", s["h"], _after(norm_f[i, 0], tok), du, add=dh1)

    grad_x = dh[PAD_FRONT + N_META:][None]
    g_meta = dh[PAD_FRONT:PAD_FRONT + N_META]

    small_grads = [g_meta, jnp.stack([jnp.stack(r) for r in g_norm]), jnp.stack(g_convw), jnp.stack(g_convb), jnp.stack(g_dtb),
                   jnp.stack(g_alog), jnp.stack(g_dskip), jnp.stack(g_ssdnorm), jnp.stack(g_poolb), jnp.stack(g_pools)]
    (small_parts,) = all_gather("gather_small_grads", [_pack(small_grads)])
    mine = lambda a: _my_cols(a, a.shape[-1] // N_DEV)
    small_w = [meta_tokens, norm_w, ssd_conv_w, ssd_conv_b, ssd_dt_bias, ssd_a_log, ssd_d, ssd_norm_w, pool_b, pool_scale]
    small_m = [m_meta_tokens, m_norm_w, m_ssd_conv_w, m_ssd_conv_b, m_ssd_dt_bias, m_ssd_a_log, m_ssd_d, m_ssd_norm_w, m_pool_b, m_pool_scale]
    small_v = [v_meta_tokens, v_norm_w, v_ssd_conv_w, v_ssd_conv_b, v_ssd_dt_bias, v_ssd_a_log, v_ssd_d, v_ssd_norm_w, v_pool_b, v_pool_scale]
    sharded = [True, True, True, False, False, False, False, False, True, True]
    def widen(a, is_sharded, full):
        if not is_sharded:
            return a
        return lax.dynamic_update_slice_in_dim(jnp.zeros(full.shape, F32), a, _linear(_my_place()) * a.shape[-1], axis=a.ndim - 1)
    packed_w = _pack([widen(a, sh, g) for a, sh, g in zip(small_w, sharded, small_grads)])
    packed_m = _pack([widen(a, sh, g) for a, sh, g in zip(small_m, sharded, small_grads)])
    packed_v = _pack([widen(a, sh, g) for a, sh, g in zip(small_v, sharded, small_grads)])
    sm = [_unpack(o, small_grads) for o in sum_adamw("adamw_small", small_parts, packed_w, packed_m, packed_v)]
    sm = [[mine(a) if sh else a for a, sh in zip(group, sharded)] for group in sm]

    keys = ["in", "out", "pool", "gate_up", "down"]
    land_of = [[keys.index(k) for k in ks] for ks in scatter_keys]
    p_in, p_out, p_pool, p_gu, p_down = exchange_wait(
        "scatter_wait", scatter_recs, [lands[k] for k in keys], land_of, scatter_views, small_parts)
    bg = [sum_adamw("adamw_ssd_w_in", p_in, ssd_w_in, m_ssd_w_in, v_ssd_w_in),
          sum_adamw("adamw_ssd_w_out", p_out, ssd_w_out, m_ssd_w_out, v_ssd_w_out),
          sum_adamw("adamw_pool_w", p_pool, pool_w, m_pool_w, v_pool_w),
          sum_adamw("adamw_ffn_w_gate", p_gu, ffn_w_gate, m_ffn_w_gate, v_ffn_w_gate, half=0),
          sum_adamw("adamw_ffn_w_up", p_gu, ffn_w_up, m_ffn_w_up, v_ffn_w_up, half=1),
          sum_adamw("adamw_ffn_w_down", p_down, ffn_w_down, m_ffn_w_down, v_ffn_w_down)]

    def ordered(kind):
        s_ = sm[kind]
        b_ = [o[kind] for o in bg]
        return [s_[0], s_[1], b_[0], s_[2], s_[3], s_[4], s_[5], s_[6], s_[7], b_[1], b_[2], s_[8], s_[9], b_[3], b_[4], b_[5]]

    return (loss, grad_x, *ordered(0), *ordered(1), *ordered(2), *ordered(3))
```

```python
import functools

import jax
import jax.numpy as jnp
from jax import lax
from jax.experimental import pallas as pl
from jax.experimental.pallas import tpu as pltpu

F32 = jnp.float32
BF16 = jnp.bfloat16
MESH = pl.DeviceIdType.MESH
AXES = ("x", "y", "c")
N_DEV = 8

N_META = 16
EPS = 1e-6
HEAD_DIM = 64
D_STATE = 128
SSD_GROUPS = 8
D_CONV = 4
CHUNK = 256
POOL_WINDOWS = (2, 4, 8, 16)
ADAM_LR, ADAM_B1, ADAM_B2, ADAM_EPS, ADAM_WD, ADAM_STEP = 0.001, 0.9, 0.999, 1e-08, 0.01, 10

PAD_FRONT = (-N_META) % CHUNK
LANES = 128
SUBLANES = 8
ROW_TILE = 256
CONV_TILE = 256
POOL_TILE = 128
VMEM_LIMIT = 56 * 1024 * 1024


def _params(sem=None):
    return pltpu.CompilerParams(dimension_semantics=sem, vmem_limit_bytes=VMEM_LIMIT)


def _tile(n, target, mult):
    if n <= target:
        return n
    best = None
    for t in range(mult, target + 1, mult):
        if n % t == 0:
            best = t
    assert best is not None, (n, target, mult)
    return best


def _dot(a, b, ca, cb):
    return lax.dot_general(a, b, (((ca,), (cb,)), ((), ())), preferred_element_type=F32)


def _sigmoid(x):
    return 1.0 / (1.0 + jnp.exp(-x))


def _row_mask(shape, first_row):
    rows = lax.broadcasted_iota(jnp.int32, shape, 0) + first_row
    return rows >= PAD_FRONT


MM_ROWS_RESIDENT = dict(tm=2304, tk=2304)
MM_COLS_RESIDENT = dict(tm=512, tn=2304, tk=2304)
MM_DEEP = dict(tm=1152, tn=1024, tk=512)
GATHER_AHEAD = 2
LOCAL_COPY_CHUNKS = 8


def matmul(name, a, b, *, tb=False, out_dtype=F32, tm=768, tn=512, tk=2048):
    m, kdim = a.shape
    if tb:
        n, k2 = b.shape
    else:
        k2, n = b.shape
    assert kdim == k2, (a.shape, b.shape, tb)
    tm = _tile(m, tm, 16)
    tn = _tile(n, tn, LANES)
    tk = _tile(kdim, tk, LANES)
    nk = kdim // tk
    a_spec = pl.BlockSpec((tm, tk), lambda i, j, k: (i, k))
    b_spec = pl.BlockSpec((tn, tk), lambda i, j, k: (j, k)) if tb else pl.BlockSpec((tk, tn), lambda i, j, k: (k, j))

    def body_single(a_ref, b_ref, o_ref):
        o_ref[...] = _dot(a_ref[...], b_ref[...], 1, 1 if tb else 0).astype(o_ref.dtype)

    def body_acc(a_ref, b_ref, o_ref, acc_ref):
        k = pl.program_id(2)

        @pl.when(k == 0)
        def _():
            acc_ref[...] = jnp.zeros_like(acc_ref)

        acc_ref[...] += _dot(a_ref[...], b_ref[...], 1, 1 if tb else 0)

        @pl.when(k == nk - 1)
        def _():
            o_ref[...] = acc_ref[...].astype(o_ref.dtype)

    return pl.pallas_call(
        body_single if nk == 1 else body_acc, name=name, grid=(m // tm, n // tn, nk),
        in_specs=[a_spec, b_spec], out_specs=pl.BlockSpec((tm, tn), lambda i, j, k: (i, j)),
        out_shape=jax.ShapeDtypeStruct((m, n), out_dtype),
        scratch_shapes=[] if nk == 1 else [pltpu.VMEM((tm, tn), F32)],
        compiler_params=_params(("parallel", "parallel", "arbitrary")),
    )(a, b)


_TOKEN_SPEC = pl.BlockSpec((SUBLANES, LANES), lambda i: (0, 0))


def rmsnorm_fwd(name, x, w, res=None, out_dtype=F32, transposed=False, after=None):
    rows, d = x.shape
    tr = ROW_TILE
    row_spec = pl.BlockSpec((tr, d), lambda i: (i, 0))
    w_spec = pl.BlockSpec((1, d), lambda i: (0, 0))
    n_in = 2 + (res is not None) + (after is not None)

    def body(*refs):
        x_ref, w_ref = refs[:2]
        outs = refs[n_in:]
        xv = x_ref[...]
        y = xv * lax.rsqrt(jnp.mean(xv * xv, axis=-1, keepdims=True) + EPS) * w_ref[...]
        if res is not None:
            y = refs[2][...] + y
        outs[0][...] = y.astype(out_dtype)
        if transposed:
            outs[1][...] = y.T.astype(out_dtype)

    args = [x, w.reshape(1, d)] + ([] if res is None else [res]) + ([] if after is None else [after])
    specs = [row_spec, w_spec] + ([] if res is None else [row_spec]) + ([] if after is None else [_TOKEN_SPEC])
    out_specs, out_shape = [row_spec], [jax.ShapeDtypeStruct((rows, d), out_dtype)]
    if transposed:
        out_specs.append(pl.BlockSpec((d, tr), lambda i: (0, i)))
        out_shape.append(jax.ShapeDtypeStruct((d, rows), out_dtype))
    outs = pl.pallas_call(
        body, name=name, grid=(rows // tr,), in_specs=specs, out_specs=out_specs, out_shape=out_shape,
        compiler_params=_params(("parallel",)),
    )(*args)
    return outs if transposed else outs[0]


def rmsnorm_bwd(name, x, w, dy, add=None, after=None):
    rows, d = x.shape
    tr = ROW_TILE
    row_spec = pl.BlockSpec((tr, d), lambda i: (i, 0))
    w_spec = pl.BlockSpec((1, d), lambda i: (0, 0))

    def body(*refs):
        x_ref, w_ref, dy_ref = refs[:3]
        add_ref = None if add is None else refs[3]
        dx_ref, dw_ref = refs[-2:]
        xv = x_ref[...]
        dyv = dy_ref[...].astype(F32)
        r = lax.rsqrt(jnp.mean(xv * xv, axis=-1, keepdims=True) + EPS)
        xh = xv * r
        dxh = dyv * w_ref[...]
        dx = r * (dxh - xh * jnp.mean(dxh * xh, axis=-1, keepdims=True))
        if add is not None:
            dx = dx + add_ref[...]
        dx_ref[...] = dx

        @pl.when(pl.program_id(0) == 0)
        def _():
            dw_ref[...] = jnp.zeros_like(dw_ref)

        dw_ref[...] += jnp.sum(dyv * xh, axis=0, keepdims=True)

    args = [x, w.reshape(1, d), dy] + ([] if add is None else [add]) + ([] if after is None else [after])
    specs = [row_spec, w_spec, row_spec] + ([] if add is None else [row_spec]) + ([] if after is None else [_TOKEN_SPEC])
    dx, dw = pl.pallas_call(
        body, name=name, grid=(rows // tr,), in_specs=specs, out_specs=[row_spec, w_spec],
        out_shape=[jax.ShapeDtypeStruct((rows, d), F32), jax.ShapeDtypeStruct((1, d), F32)],
        compiler_params=_params(("arbitrary",)),
    )(*args)
    return dx, dw.reshape(d)


def loss_head(name, h, target):
    rows, d = h.shape
    tr = ROW_TILE
    first = (PAD_FRONT + N_META) // tr
    assert (PAD_FRONT + N_META) % tr == 0 and target.shape[0] == rows - first * tr

    def body(h_ref, t_ref, dh_ref, loss_ref):
        i = pl.program_id(0)

        @pl.when(i == 0)
        def _():
            loss_ref[...] = jnp.zeros_like(loss_ref)

        keep = (i >= first).astype(F32)
        diff = (h_ref[...] - t_ref[...]) * keep
        dh_ref[...] = diff / d
        loss_ref[...] += 0.5 * jnp.sum(diff * diff) / d

    dh, loss = pl.pallas_call(
        body, name=name, grid=(rows // tr,),
        in_specs=[pl.BlockSpec((tr, d), lambda i: (i, 0)), pl.BlockSpec((tr, d), lambda i: (jnp.maximum(i - first, 0), 0))],
        out_specs=[pl.BlockSpec((tr, d), lambda i: (i, 0)), pl.BlockSpec((SUBLANES, LANES), lambda i: (0, 0))],
        out_shape=[jax.ShapeDtypeStruct((rows, d), F32), jax.ShapeDtypeStruct((SUBLANES, LANES), F32)],
        compiler_params=_params(("arbitrary",)),
    )(h, target)
    return dh, loss[0, 0]


def _mm_call(name, a, b, grid, a_spec, b_spec, o_spec, out_sds, ca, cb, red_axis=None):
    n_red = None if red_axis is None else grid[red_axis]

    def body_single(a_ref, b_ref, o_ref):
        o_ref[...] = _dot(a_ref[...], b_ref[...], ca, cb).astype(o_ref.dtype)

    def body_acc(a_ref, b_ref, o_ref, acc_ref):
        k = pl.program_id(red_axis)

        @pl.when(k == 0)
        def _():
            acc_ref[...] = jnp.zeros_like(acc_ref)

        acc_ref[...] += _dot(a_ref[...], b_ref[...], ca, cb)

        @pl.when(k == n_red - 1)
        def _():
            o_ref[...] = acc_ref[...].astype(o_ref.dtype)

    acc_shape = tuple(s for s in o_spec.block_shape if s is not None)
    sem = tuple("arbitrary" if ax == red_axis else "parallel" for ax in range(len(grid)))
    return pl.pallas_call(
        body_single if red_axis is None else body_acc, name=name, grid=grid, in_specs=[a_spec, b_spec], out_specs=o_spec,
        out_shape=out_sds, scratch_shapes=[] if red_axis is None else [pltpu.VMEM(acc_shape, F32)],
        compiler_params=_params(sem),
    )(a, b)


def ffn_in_fwd(name, u2, w_gu):
    rows, d = u2.shape
    hs = w_gu.shape[-1]
    return _mm_call(name, u2, w_gu, (N_DEV, 2), pl.BlockSpec((rows, d), lambda k, t: (0, 0)),
                    pl.BlockSpec((None, None, d, hs), lambda k, t: (k, t, 0, 0)),
                    pl.BlockSpec((None, None, rows, hs), lambda k, t: (k, t, 0, 0)),
                    jax.ShapeDtypeStruct((N_DEV, 2, rows, hs), F32), 1, 0)


def ffn_out_fwd(name, act, w_down):
    _, rows, hs = act.shape
    d = w_down.shape[-1]
    tm, tn = _tile(rows, 1152, 16), _tile(d, 1024, LANES)
    return _mm_call(name, act, w_down, (rows // tm, d // tn, N_DEV), pl.BlockSpec((None, tm, hs), lambda i, j, k: (k, i, 0)),
                    pl.BlockSpec((None, hs, tn), lambda i, j, k: (k, 0, j)), pl.BlockSpec((tm, tn), lambda i, j, k: (i, j)),
                    jax.ShapeDtypeStruct((rows, d), F32), 1, 0, red_axis=2)


def ffn_out_bwd_x(name, df, w_down):
    rows, d = df.shape
    hs = w_down.shape[1]
    return _mm_call(name, df, w_down, (N_DEV,), pl.BlockSpec((rows, d), lambda k: (0, 0)),
                    pl.BlockSpec((None, hs, d), lambda k: (k, 0, 0)), pl.BlockSpec((None, rows, hs), lambda k: (k, 0, 0)),
                    jax.ShapeDtypeStruct((N_DEV, rows, hs), F32), 1, 1)


def ffn_out_bwd_w(name, act_t, df):
    _, hs, rows = act_t.shape
    d = df.shape[1]
    return _mm_call(name, act_t, df, (N_DEV,), pl.BlockSpec((None, hs, rows), lambda k: (k, 0, 0)),
                    pl.BlockSpec((rows, d), lambda k: (0, 0)), pl.BlockSpec((None, hs, d), lambda k: (k, 0, 0)),
                    jax.ShapeDtypeStruct((N_DEV, hs, d), BF16), 1, 0)


def ffn_in_bwd_w(name, u2_t, dgp):
    d, rows = u2_t.shape
    hs = dgp.shape[-1]
    return _mm_call(name, u2_t, dgp, (N_DEV, 2), pl.BlockSpec((d, rows), lambda k, t: (0, 0)),
                    pl.BlockSpec((None, None, rows, hs), lambda k, t: (k, t, 0, 0)),
                    pl.BlockSpec((None, None, d, hs), lambda k, t: (k, t, 0, 0)),
                    jax.ShapeDtypeStruct((N_DEV, 2, d, hs), BF16), 1, 0)


def ffn_in_bwd_x(name, dgp, w_gu):
    _, _, rows, hs = dgp.shape
    d = w_gu.shape[2]
    tm, tn = _tile(rows, 1152, 16), _tile(d, 1024, LANES)
    return _mm_call(name, dgp.reshape(2 * N_DEV, rows, hs), w_gu.reshape(2 * N_DEV, d, hs), (rows // tm, d // tn, 2 * N_DEV),
                    pl.BlockSpec((None, tm, hs), lambda i, j, k: (k, i, 0)), pl.BlockSpec((None, tn, hs), lambda i, j, k: (k, j, 0)),
                    pl.BlockSpec((tm, tn), lambda i, j, k: (i, j)), jax.ShapeDtypeStruct((rows, d), F32), 1, 1, red_axis=2)


def swiglu_fwd(name, gp):
    _, _, rows, hs = gp.shape
    tr = _tile(rows, 768, ROW_TILE)

    def body(gp_ref, a_ref, at_ref):
        g = gp_ref[0]
        act = g * _sigmoid(g) * gp_ref[1]
        a_ref[...] = act.astype(a_ref.dtype)
        at_ref[...] = act.T.astype(at_ref.dtype)

    return pl.pallas_call(
        body, name=name, grid=(N_DEV, rows // tr),
        in_specs=[pl.BlockSpec((None, 2, tr, hs), lambda k, i: (k, 0, i, 0))],
        out_specs=[pl.BlockSpec((None, tr, hs), lambda k, i: (k, i, 0)), pl.BlockSpec((None, hs, tr), lambda k, i: (k, 0, i))],
        out_shape=[jax.ShapeDtypeStruct((N_DEV, rows, hs), BF16), jax.ShapeDtypeStruct((N_DEV, hs, rows), BF16)],
        compiler_params=_params(("parallel", "parallel")),
    )(gp)


def swiglu_bwd(name, gp, dact):
    _, _, rows, hs = gp.shape
    tr = _tile(rows, 768, ROW_TILE)

    def body(gp_ref, da_ref, o_ref):
        g = gp_ref[0]
        s = _sigmoid(g)
        dav = da_ref[...]
        o_ref[0] = (dav * gp_ref[1] * (s * (1.0 + g * (1.0 - s)))).astype(o_ref.dtype)
        o_ref[1] = (dav * (g * s)).astype(o_ref.dtype)

    blk = pl.BlockSpec((None, 2, tr, hs), lambda k, i: (k, 0, i, 0))
    return pl.pallas_call(
        body, name=name, grid=(N_DEV, rows // tr),
        in_specs=[blk, pl.BlockSpec((None, tr, hs), lambda k, i: (k, i, 0))], out_specs=blk,
        out_shape=jax.ShapeDtypeStruct(gp.shape, BF16), compiler_params=_params(("parallel", "parallel")),
    )(gp, dact)


def conv_fwd(name, zx, conv_w, conv_b, d_inner, conv_dim):
    rows = zx.shape[0]
    tc = CONV_TILE
    off = d_inner // tc
    assert d_inner % tc == 0 and conv_dim % tc == 0

    def body(u_ref, w_ref, b_ref, o_ref):
        u = u_ref[...]
        acc = u * w_ref[D_CONV - 1:D_CONV, :] + b_ref[...]
        for s in range(1, D_CONV):
            acc = acc + pltpu.roll(u, s, axis=0) * w_ref[D_CONV - 1 - s:D_CONV - s, :]
        y = acc * _sigmoid(acc)
        o_ref[...] = jnp.where(_row_mask(y.shape, 0), y, 0.0)

    return pl.pallas_call(
        body, name=name, grid=(conv_dim // tc,),
        in_specs=[pl.BlockSpec((rows, tc), lambda j: (0, off + j)), pl.BlockSpec((D_CONV, tc), lambda j: (0, j)),
                  pl.BlockSpec((1, tc), lambda j: (0, j))],
        out_specs=pl.BlockSpec((rows, tc), lambda j: (0, j)),
        out_shape=jax.ShapeDtypeStruct((rows, conv_dim), F32), compiler_params=_params(("parallel",)),
    )(zx, conv_w, conv_b.reshape(1, conv_dim))


def conv_bwd(name, zx, conv_w, conv_b, dxbc, d_inner, conv_dim):
    rows = zx.shape[0]
    tc = CONV_TILE
    off = d_inner // tc

    def body(u_ref, w_ref, b_ref, dy_ref, du_ref, dw_ref, db_ref):
        u = u_ref[...]
        wk = [w_ref[D_CONV - 1 - s:D_CONV - s, :] for s in range(D_CONV)]
        shifted = [u] + [pltpu.roll(u, s, axis=0) for s in range(1, D_CONV)]
        acc = u * wk[0] + b_ref[...]
        for s in range(1, D_CONV):
            acc = acc + shifted[s] * wk[s]
        sg = _sigmoid(acc)
        mask = _row_mask(acc.shape, 0)
        dpre = jnp.where(mask, dy_ref[...] * (sg * (1.0 + acc * (1.0 - sg))), 0.0)
        db_ref[...] = jnp.sum(dpre, axis=0, keepdims=True)
        du = dpre * wk[0]
        dw_ref[D_CONV - 1:D_CONV, :] = jnp.sum(dpre * u, axis=0, keepdims=True)
        for s in range(1, D_CONV):
            du = du + pltpu.roll(dpre, rows - s, axis=0) * wk[s]
            dw_ref[D_CONV - 1 - s:D_CONV - s, :] = jnp.sum(dpre * shifted[s], axis=0, keepdims=True)
        du_ref[...] = jnp.where(mask, du, 0.0).astype(du_ref.dtype)

    return pl.pallas_call(
        body, name=name, grid=(conv_dim // tc,),
        in_specs=[pl.BlockSpec((rows, tc), lambda j: (0, off + j)), pl.BlockSpec((D_CONV, tc), lambda j: (0, j)),
                  pl.BlockSpec((1, tc), lambda j: (0, j)), pl.BlockSpec((rows, tc), lambda j: (0, j))],
        out_specs=[pl.BlockSpec((rows, tc), lambda j: (0, j)), pl.BlockSpec((D_CONV, tc), lambda j: (0, j)),
                   pl.BlockSpec((1, tc), lambda j: (0, j))],
        out_shape=[jax.ShapeDtypeStruct((rows, conv_dim), BF16), jax.ShapeDtypeStruct((D_CONV, conv_dim), F32),
                   jax.ShapeDtypeStruct((1, conv_dim), F32)],
        compiler_params=_params(("parallel",)),
    )(zx, conv_w, conv_b.reshape(1, conv_dim), dxbc)


def dt_fwd(name, zx, bias_pad, zx_cols):
    rows = zx.shape[0]
    tr = ROW_TILE
    off = zx_cols // LANES

    def body(r_ref, b_ref, o_ref):
        v = r_ref[...] + b_ref[...]
        sp = jnp.maximum(v, 0.0) + jnp.log1p(jnp.exp(-jnp.abs(v)))
        o_ref[...] = jnp.where(_row_mask(v.shape, pl.program_id(0) * tr), sp, 0.0)

    return pl.pallas_call(
        body, name=name, grid=(rows // tr,),
        in_specs=[pl.BlockSpec((tr, LANES), lambda i: (i, off)), pl.BlockSpec((1, LANES), lambda i: (0, 0))],
        out_specs=pl.BlockSpec((tr, LANES), lambda i: (i, 0)),
        out_shape=jax.ShapeDtypeStruct((rows, LANES), F32), compiler_params=_params(("parallel",)),
    )(zx, bias_pad)


def dt_bwd(name, zx, bias_pad, ddt, zx_cols):
    rows = zx.shape[0]
    tr = ROW_TILE
    off = zx_cols // LANES

    def body(r_ref, b_ref, d_ref, o_ref, db_ref):
        v = r_ref[...] + b_ref[...]
        g = jnp.where(_row_mask(v.shape, pl.program_id(0) * tr), d_ref[...] * _sigmoid(v), 0.0)
        o_ref[...] = g.astype(o_ref.dtype)

        @pl.when(pl.program_id(0) == 0)
        def _():
            db_ref[...] = jnp.zeros_like(db_ref)

        db_ref[...] += jnp.sum(g, axis=0, keepdims=True)

    return pl.pallas_call(
        body, name=name, grid=(rows // tr,),
        in_specs=[pl.BlockSpec((tr, LANES), lambda i: (i, off)), pl.BlockSpec((1, LANES), lambda i: (0, 0)),
                  pl.BlockSpec((tr, LANES), lambda i: (i, 0))],
        out_specs=[pl.BlockSpec((tr, LANES), lambda i: (i, 0)), pl.BlockSpec((1, LANES), lambda i: (0, 0))],
        out_shape=[jax.ShapeDtypeStruct((rows, LANES), BF16), jax.ShapeDtypeStruct((1, LANES), F32)],
        compiler_params=_params(("arbitrary",)),
    )(zx, bias_pad, ddt)


def _split3(x):
    h1 = x.astype(BF16)
    r1 = x - h1.astype(F32)
    h2 = r1.astype(BF16)
    h3 = (r1 - h2.astype(F32)).astype(BF16)
    return h1, h2, h3


def _exact_left(ones_b, x):
    h1, h2, h3 = _split3(x)
    return _dot(ones_b, h1, 1, 0) + _dot(ones_b, h2, 1, 0) + _dot(ones_b, h3, 1, 0)


def _exact_right_t(x, ones_b):
    h1, h2, h3 = _split3(x)
    return _dot(h1, ones_b, 1, 1) + _dot(h2, ones_b, 1, 1) + _dot(h3, ones_b, 1, 1)


class _ScanCommon:
    def __init__(self, b_ref, c_ref, dtc_ref, dtr_ref, arow_ref, acol_ref, drow_ref):
        q = CHUNK
        self.bb = b_ref[...].astype(BF16)
        self.cb = c_ref[...].astype(BF16)
        ri = lax.broadcasted_iota(jnp.int32, (q, q), 0)
        cj = lax.broadcasted_iota(jnp.int32, (q, q), 1)
        self.lower = ri >= cj
        self.upper = cj >= ri
        self.dtc = dtc_ref[...]
        self.dtr = dtr_ref[...]
        self.arow = arow_ref[...]
        self.acol = acol_ref[...]
        self.drow = drow_ref[...]
        da_col = self.dtc * self.arow
        self.a_col = _exact_left(self.lower.astype(BF16), da_col)
        self.a_row = _exact_right_t(self.dtr * self.acol, self.lower.astype(BF16))
        self.a_last = jnp.sum(da_col, axis=0, keepdims=True)
        self.lane_q = lax.broadcasted_iota(jnp.int32, (q, LANES), 1)
        self.lane_1 = lax.broadcasted_iota(jnp.int32, (1, LANES), 1)
        self.sub_8 = lax.broadcasted_iota(jnp.int32, (SUBLANES, q), 0)
        self.first_half = self.lane_q < HEAD_DIM
        self.first_rows = lax.broadcasted_iota(jnp.int32, (LANES, 1), 0) < HEAD_DIM

    def col(self, v, r):
        return jnp.sum(jnp.where(self.lane_q == r, v, 0.0), axis=1, keepdims=True)

    def row(self, v, r):
        return jnp.sum(jnp.where(self.sub_8 == r, v, 0.0), axis=0, keepdims=True)

    def scalar(self, v, r):
        return jnp.sum(jnp.where(self.lane_1 == r, v, 0.0), axis=1, keepdims=True)

    def pair(self, v0, v1):
        return jnp.where(self.first_half, v0, v1)

    def half_rowsum(self, t, h):
        keep = self.first_half if h == 0 else jnp.logical_not(self.first_half)
        return jnp.sum(jnp.where(keep, t, 0.0), axis=1, keepdims=True)


def ssd_scan_fwd(name, xbc, dt_col, dt_row, a_row, a_col, d_row, d_inner):
    rows = xbc.shape[0]
    q, n, g_cnt = CHUNK, D_STATE, SSD_GROUPS
    nc = rows // q
    rp = d_inner // g_cnt
    n_pairs = rp // LANES
    b_off = d_inner // n

    def body(x_ref, b_ref, c_ref, dtc_ref, dtr_ref, arow_ref, acol_ref, drow_ref, y_ref, sprev_ref, s_ref):
        @pl.when(pl.program_id(1) == 0)
        def _():
            s_ref[...] = jnp.zeros_like(s_ref)

        sprev_ref[...] = s_ref[...]
        k = _ScanCommon(b_ref, c_ref, dtc_ref, dtr_ref, arow_ref, acol_ref, drow_ref)
        cb_mat = _dot(k.cb, k.bb, 1, 1)
        for pr in range(n_pairs):
            sl = slice(pr * LANES, (pr + 1) * LANES)
            heads = (2 * pr, 2 * pr + 1)
            xp = x_ref[:, sl]
            ac = [k.col(k.a_col, r) for r in heads]
            ar = [k.row(k.a_row, r) for r in heads]
            al = [k.scalar(k.a_last, r) for r in heads]
            xd = xp * k.pair(k.col(k.dtc, heads[0]), k.col(k.dtc, heads[1]))
            xdb = xd.astype(BF16)
            ys = []
            for h in range(2):
                lm = jnp.exp(jnp.where(k.lower, ac[h] - ar[h], -jnp.inf))
                ys.append(_dot((cb_mat * lm).astype(BF16), xdb, 1, 0))
            y = jnp.where(k.first_half, ys[0], ys[1])
            sp = s_ref[sl, :]
            y = y + k.pair(jnp.exp(ac[0]), jnp.exp(ac[1])) * _dot(k.cb, sp.astype(BF16), 1, 1)
            y = y + k.pair(k.scalar(k.drow, heads[0]), k.scalar(k.drow, heads[1])) * xp
            y_ref[:, sl] = y
            wb = (xd * k.pair(jnp.exp(al[0] - ac[0]), jnp.exp(al[1] - ac[1]))).astype(BF16)
            decay = jnp.where(k.first_rows, jnp.exp(al[0]), jnp.exp(al[1]))
            s_ref[sl, :] = decay * sp + _dot(wb, k.bb, 0, 0)

    return pl.pallas_call(
        body, name=name, grid=(g_cnt, nc),
        in_specs=[
            pl.BlockSpec((q, rp), lambda g, c: (c, g)),
            pl.BlockSpec((q, n), lambda g, c: (c, b_off + g)),
            pl.BlockSpec((q, n), lambda g, c: (c, b_off + g_cnt + g)),
            pl.BlockSpec((None, q, LANES), lambda g, c: (g, c, 0)),
            pl.BlockSpec((None, SUBLANES, q), lambda g, c: (g, 0, c)),
            pl.BlockSpec((None, 1, LANES), lambda g, c: (g, 0, 0)),
            pl.BlockSpec((None, SUBLANES, 1), lambda g, c: (g, 0, 0)),
            pl.BlockSpec((None, 1, LANES), lambda g, c: (g, 0, 0)),
        ],
        out_specs=[pl.BlockSpec((q, rp), lambda g, c: (c, g)),
                   pl.BlockSpec((None, None, rp, n), lambda g, c: (c, g, 0, 0))],
        out_shape=[jax.ShapeDtypeStruct((rows, d_inner), F32), jax.ShapeDtypeStruct((nc, g_cnt, rp, n), F32)],
        scratch_shapes=[pltpu.VMEM((rp, n), F32)],
        compiler_params=_params(("parallel", "arbitrary")),
    )(xbc, xbc, xbc, dt_col, dt_row, a_row, a_col, d_row)


def ssd_scan_bwd(name, xbc, dt_col, dt_row, a_row, a_col, d_row, sprev, dy, d_inner):
    rows = xbc.shape[0]
    q, n, g_cnt = CHUNK, D_STATE, SSD_GROUPS
    nc = rows // q
    rp = d_inner // g_cnt
    n_pairs = rp // LANES
    b_off = d_inner // n

    def body(x_ref, b_ref, c_ref, dtc_ref, dtr_ref, arow_ref, acol_ref, drow_ref, sprev_ref, dy_ref,
             dx_ref, db_ref, dc_ref, ddtc_ref, ddtr_ref, dar_ref, dac_ref, dd_ref, ds_ref):
        @pl.when(pl.program_id(1) == 0)
        def _():
            ds_ref[...] = jnp.zeros_like(ds_ref)
            dar_ref[...] = jnp.zeros_like(dar_ref)
            dac_ref[...] = jnp.zeros_like(dac_ref)
            dd_ref[...] = jnp.zeros_like(dd_ref)

        k = _ScanCommon(b_ref, c_ref, dtc_ref, dtr_ref, arow_ref, acol_ref, drow_ref)
        cb_mat = _dot(k.cb, k.bb, 1, 1)
        cbt_mat = _dot(k.bb, k.cb, 1, 1)
        d_cb = jnp.zeros((q, q), F32)
        d_b = jnp.zeros((q, n), F32)
        d_c = jnp.zeros((q, n), F32)
        da_col = jnp.zeros((q, LANES), F32)
        da_row = jnp.zeros((SUBLANES, q), F32)
        ddt_x = jnp.zeros((q, LANES), F32)
        d_alast = jnp.zeros((1, LANES), F32)
        d_dskip = jnp.zeros((1, LANES), F32)
        for pr in range(n_pairs):
            sl = slice(pr * LANES, (pr + 1) * LANES)
            heads = (2 * pr, 2 * pr + 1)
            xp = x_ref[:, sl]
            dyp = dy_ref[:, sl]
            dyb = dyp.astype(BF16)
            ac = [k.col(k.a_col, r) for r in heads]
            ar = [k.row(k.a_row, r) for r in heads]
            al = [k.scalar(k.a_last, r) for r in heads]
            dt_p = k.pair(k.col(k.dtc, heads[0]), k.col(k.dtc, heads[1]))
            xd = xp * dt_p
            xdb = xd.astype(BF16)
            sp = sprev_ref[sl, :]
            spb = sp.astype(BF16)
            dsp = ds_ref[sl, :]
            dspb = dsp.astype(BF16)
            dskip_p = k.pair(k.scalar(k.drow, heads[0]), k.scalar(k.drow, heads[1]))
            dxp = dskip_p * dyp
            dd_lane = jnp.sum(dyp * xp, axis=0, keepdims=True)
            e_p = k.pair(jnp.exp(ac[0]), jnp.exp(ac[1]))
            t_off = dyp * (e_p * _dot(k.cb, spb, 1, 1))
            dzb = (e_p * dyp).astype(BF16)
            d_c = d_c + _dot(dzb, spb, 1, 0)
            ds_in = _dot(dzb, k.cb, 0, 0)
            decay = jnp.where(k.first_rows, jnp.exp(al[0]), jnp.exp(al[1]))
            ds_in = ds_in + decay * dsp
            t_state = jnp.sum(dsp * sp, axis=1, keepdims=True) * decay
            dec_p = k.pair(jnp.exp(al[0] - ac[0]), jnp.exp(al[1] - ac[1]))
            dw = _dot(k.bb, dspb, 1, 1)
            d_b = d_b + _dot((xd * dec_p).astype(BF16), dspb, 1, 0)
            dxd = dw * dec_p
            t_dec = dw * xd * dec_p
            dxd_h = []
            for h in range(2):
                r = heads[h]
                keep = k.first_half if h == 0 else jnp.logical_not(k.first_half)
                lm = jnp.exp(jnp.where(k.lower, ac[h] - ar[h], -jnp.inf))
                m_mat = cb_mat * lm
                dm = _dot(jnp.where(keep, dyp, 0.0).astype(BF16), xdb, 1, 1)
                dseg = dm * m_mat
                d_cb = d_cb + dm * lm
                lmt = jnp.exp(jnp.where(k.upper, ar[h] - ac[h], -jnp.inf))
                dxd_h.append(_dot((cbt_mat * lmt).astype(BF16), dyb, 1, 0))
                tdec_h = k.half_rowsum(t_dec, h)
                da_h = k.half_rowsum(t_off, h) - tdec_h + jnp.sum(dseg, axis=1, keepdims=True)
                da_col = da_col + jnp.where(k.lane_q == r, da_h, 0.0)
                da_row = da_row - jnp.where(k.sub_8 == r, jnp.sum(dseg, axis=0, keepdims=True), 0.0)
                keep_rows = k.first_rows if h == 0 else jnp.logical_not(k.first_rows)
                dal_h = jnp.sum(tdec_h, axis=0, keepdims=True) + jnp.sum(jnp.where(keep_rows, t_state, 0.0), axis=0, keepdims=True)
                d_alast = d_alast + jnp.where(k.lane_1 == r, dal_h, 0.0)
                keep_1 = k.lane_1 < HEAD_DIM if h == 0 else k.lane_1 >= HEAD_DIM
                dd_h = jnp.sum(jnp.where(keep_1, dd_lane, 0.0), axis=1, keepdims=True)
                d_dskip = d_dskip + jnp.where(k.lane_1 == r, dd_h, 0.0)
            dxd = dxd + jnp.where(k.first_half, dxd_h[0], dxd_h[1])
            dx_ref[:, sl] = dxp + dt_p * dxd
            t_dt = dxd * xp
            for h in range(2):
                ddt_x = ddt_x + jnp.where(k.lane_q == heads[h], k.half_rowsum(t_dt, h), 0.0)
            ds_ref[sl, :] = ds_in
        d_cb_b = d_cb.astype(BF16)
        db_ref[...] = d_b + _dot(d_cb_b, k.cb, 0, 0)
        dc_ref[...] = d_c + _dot(d_cb_b, k.bb, 1, 0)
        rc_col = _exact_left(k.upper.astype(BF16), da_col) + d_alast
        rc_row = _exact_right_t(da_row, k.upper.astype(BF16))
        ddtc_ref[...] = ddt_x + k.arow * rc_col
        ddtr_ref[...] = k.acol * rc_row
        dar_ref[...] += jnp.sum(rc_col * k.dtc, axis=0, keepdims=True)
        dac_ref[...] += jnp.sum(rc_row * k.dtr, axis=1, keepdims=True)
        dd_ref[...] += d_dskip

    rc = lambda c: nc - 1 - c
    return pl.pallas_call(
        body, name=name, grid=(g_cnt, nc),
        in_specs=[
            pl.BlockSpec((q, rp), lambda g, c: (rc(c), g)),
            pl.BlockSpec((q, n), lambda g, c: (rc(c), b_off + g)),
            pl.BlockSpec((q, n), lambda g, c: (rc(c), b_off + g_cnt + g)),
            pl.BlockSpec((None, q, LANES), lambda g, c: (g, rc(c), 0)),
            pl.BlockSpec((None, SUBLANES, q), lambda g, c: (g, 0, rc(c))),
            pl.BlockSpec((None, 1, LANES), lambda g, c: (g, 0, 0)),
            pl.BlockSpec((None, SUBLANES, 1), lambda g, c: (g, 0, 0)),
            pl.BlockSpec((None, 1, LANES), lambda g, c: (g, 0, 0)),
            pl.BlockSpec((None, None, rp, n), lambda g, c: (rc(c), g, 0, 0)),
            pl.BlockSpec((q, rp), lambda g, c: (rc(c), g)),
        ],
        out_specs=[
            pl.BlockSpec((q, rp), lambda g, c: (rc(c), g)),
            pl.BlockSpec((q, n), lambda g, c: (rc(c), g)),
            pl.BlockSpec((q, n), lambda g, c: (rc(c), g)),
            pl.BlockSpec((None, q, LANES), lambda g, c: (g, rc(c), 0)),
            pl.BlockSpec((None, SUBLANES, q), lambda g, c: (g, 0, rc(c))),
            pl.BlockSpec((None, 1, LANES), lambda g, c: (g, 0, 0)),
            pl.BlockSpec((None, SUBLANES, 1), lambda g, c: (g, 0, 0)),
            pl.BlockSpec((None, 1, LANES), lambda g, c: (g, 0, 0)),
        ],
        out_shape=[
            jax.ShapeDtypeStruct((rows, d_inner), F32),
            jax.ShapeDtypeStruct((rows, g_cnt * n), F32),
            jax.ShapeDtypeStruct((rows, g_cnt * n), F32),
            jax.ShapeDtypeStruct((g_cnt, rows, LANES), F32),
            jax.ShapeDtypeStruct((g_cnt, SUBLANES, rows), F32),
            jax.ShapeDtypeStruct((g_cnt, 1, LANES), F32),
            jax.ShapeDtypeStruct((g_cnt, SUBLANES, 1), F32),
            jax.ShapeDtypeStruct((g_cnt, 1, LANES), F32),
        ],
        scratch_shapes=[pltpu.VMEM((rp, n), F32)],
        compiler_params=_params(("parallel", "arbitrary")),
    )(xbc, xbc, xbc, dt_col, dt_row, a_row, a_col, d_row, sprev, dy)


def gatenorm_fwd(name, y, zx, w, d_inner):
    rows = y.shape[0]
    tr, gw = ROW_TILE, d_inner // SSD_GROUPS

    def body(y_ref, z_ref, w_ref, o_ref, ot_ref):
        z = z_ref[...]
        v = y_ref[...] * (z * _sigmoid(z))
        out = v * lax.rsqrt(jnp.mean(v * v, axis=-1, keepdims=True) + EPS) * w_ref[...]
        o_ref[...] = out.astype(o_ref.dtype)
        ot_ref[...] = out.T.astype(ot_ref.dtype)

    blk = pl.BlockSpec((tr, gw), lambda i, j: (i, j))
    return pl.pallas_call(
        body, name=name, grid=(rows // tr, SSD_GROUPS),
        in_specs=[blk, blk, pl.BlockSpec((1, gw), lambda i, j: (0, j))],
        out_specs=[blk, pl.BlockSpec((gw, tr), lambda i, j: (j, i))],
        out_shape=[jax.ShapeDtypeStruct((rows, d_inner), BF16), jax.ShapeDtypeStruct((d_inner, rows), BF16)],
        compiler_params=_params(("parallel", "parallel")),
    )(y, zx, w.reshape(1, d_inner))


def gatenorm_bwd(name, y, zx, w, dyn, d_inner, after):
    rows = y.shape[0]
    tr, gw = ROW_TILE, d_inner // SSD_GROUPS

    def body(y_ref, z_ref, w_ref, dn_ref, after_ref, dy_ref, dz_ref, dw_ref):
        z = z_ref[...]
        yv = y_ref[...]
        s = _sigmoid(z)
        gate = z * s
        v = yv * gate
        r = lax.rsqrt(jnp.mean(v * v, axis=-1, keepdims=True) + EPS)
        vh = v * r
        dn = dn_ref[...]
        dvh = dn * w_ref[...]
        dv = r * (dvh - vh * jnp.mean(dvh * vh, axis=-1, keepdims=True))
        dy_ref[...] = dv * gate
        dz_ref[...] = (dv * yv * (s * (1.0 + z * (1.0 - s)))).astype(dz_ref.dtype)

        @pl.when(pl.program_id(1) == 0)
        def _():
            dw_ref[...] = jnp.zeros_like(dw_ref)

        dw_ref[...] += jnp.sum(dn * vh, axis=0, keepdims=True)

    blk = pl.BlockSpec((tr, gw), lambda j, i: (i, j))
    wblk = pl.BlockSpec((1, gw), lambda j, i: (0, j))
    dy, dz, dw = pl.pallas_call(
        body, name=name, grid=(SSD_GROUPS, rows // tr),
        in_specs=[blk, blk, wblk, blk, pl.BlockSpec((SUBLANES, LANES), lambda j, i: (0, 0))], out_specs=[blk, blk, wblk],
        out_shape=[jax.ShapeDtypeStruct((rows, d_inner), F32), jax.ShapeDtypeStruct((rows, d_inner), BF16),
                   jax.ShapeDtypeStruct((1, d_inner), F32)],
        compiler_params=_params(("parallel", "arbitrary")),
    )(y, zx, w.reshape(1, d_inner), dyn, after)
    return dy, dz, dw.reshape(d_inner)


def _pool_count(rows, g):
    t1 = lax.broadcasted_iota(jnp.int32, (rows, 1), 0) - (PAD_FRONT - 1)
    win = jnp.left_shift(jnp.int32(POOL_WINDOWS[0]), g)
    return jnp.clip(t1, 1, win).astype(F32)


def _pool_select(levels, g):
    out = levels[-1]
    for i in range(len(levels) - 2, -1, -1):
        out = jnp.where(g == i, levels[i], out)
    return out


def pool_sub(name, u, transpose):
    rows, d = u.shape
    gd = d // len(POOL_WINDOWS)
    assert all(w == POOL_WINDOWS[0] << i for i, w in enumerate(POOL_WINDOWS))

    def body(u_ref, o_ref):
        g = pl.program_id(0)
        v = u_ref[...].astype(F32)
        cnt = _pool_count(rows, g)
        mask = _row_mask(v.shape, 0)
        s = v / cnt if transpose else v
        levels = []
        for i in range(len(POOL_WINDOWS)):
            step = 1 << i
            s = s + pltpu.roll(s, (rows - step) if transpose else step, axis=0)
            levels.append(s)
        sel = _pool_select(levels, g)
        out = (sel - v) if transpose else (sel / cnt - v)
        o_ref[...] = jnp.where(mask, out, 0.0).astype(o_ref.dtype)

    tc = POOL_TILE
    per = gd // tc
    blk = pl.BlockSpec((rows, tc), lambda g, j: (0, g * per + j))
    return pl.pallas_call(
        body, name=name, grid=(len(POOL_WINDOWS), per), in_specs=[blk], out_specs=blk,
        out_shape=jax.ShapeDtypeStruct((rows, d), F32 if transpose else BF16),
        compiler_params=_params(("parallel", "parallel")),
    )(u)


def pool_proj_fwd(name, mixed, w, b, scale):
    rows, d = mixed.shape
    ng = len(POOL_WINDOWS)
    gd = d // ng
    tr = ROW_TILE

    def body(m_ref, w_ref, b_ref, s_ref, pre_ref, mix_ref):
        pre = _dot(m_ref[...], w_ref[...], 1, 0) + b_ref[...]
        pre_ref[...] = pre
        mix_ref[...] = jnp.where(_row_mask(pre.shape, pl.program_id(1) * tr), pre * s_ref[...], 0.0)

    blk = pl.BlockSpec((tr, gd), lambda g, i: (i, g))
    vec = pl.BlockSpec((1, gd), lambda g, i: (0, g))
    return pl.pallas_call(
        body, name=name, grid=(ng, rows // tr),
        in_specs=[blk, pl.BlockSpec((None, gd, gd), lambda g, i: (g, 0, 0)), vec, vec], out_specs=[blk, blk],
        out_shape=[jax.ShapeDtypeStruct((rows, d), F32), jax.ShapeDtypeStruct((rows, d), F32)],
        compiler_params=_params(("parallel", "parallel")),
    )(mixed, w, b.reshape(1, d), scale.reshape(1, d))


def pool_proj_bwd(name, dmix, pre, mixed, w, scale):
    rows, d = dmix.shape
    ng = len(POOL_WINDOWS)
    gd = d // ng
    tr = ROW_TILE

    def body(dm_ref, pre_ref, mx_ref, w_ref, s_ref, dmx_ref, dw_ref, db_ref, ds_ref):
        @pl.when(pl.program_id(1) == 0)
        def _():
            dw_ref[...] = jnp.zeros_like(dw_ref)
            db_ref[...] = jnp.zeros_like(db_ref)
            ds_ref[...] = jnp.zeros_like(ds_ref)

        dmv = jnp.where(_row_mask(dm_ref.shape, pl.program_id(1) * tr), dm_ref[...], 0.0)
        dpre = dmv * s_ref[...]
        dpre_b = dpre.astype(BF16)
        ds_ref[...] += jnp.sum(dmv * pre_ref[...], axis=0, keepdims=True)
        db_ref[...] += jnp.sum(dpre, axis=0, keepdims=True)
        dmx_ref[...] = _dot(dpre_b, w_ref[...], 1, 1)
        dw_ref[...] += _dot(mx_ref[...], dpre_b, 0, 0)

    blk = pl.BlockSpec((tr, gd), lambda g, i: (i, g))
    vec = pl.BlockSpec((1, gd), lambda g, i: (0, g))
    wblk = pl.BlockSpec((None, gd, gd), lambda g, i: (g, 0, 0))
    dmixed, dw, db, ds = pl.pallas_call(
        body, name=name, grid=(ng, rows // tr),
        in_specs=[blk, blk, blk, wblk, vec], out_specs=[blk, wblk, vec, vec],
        out_shape=[jax.ShapeDtypeStruct((rows, d), F32), jax.ShapeDtypeStruct((ng, gd, gd), F32),
                   jax.ShapeDtypeStruct((1, d), F32), jax.ShapeDtypeStruct((1, d), F32)],
        compiler_params=_params(("parallel", "arbitrary")),
    )(dmix, pre, mixed, w, scale.reshape(1, d))
    return dmixed, dw, db.reshape(d), ds.reshape(d)


def _my_place():
    return lax.axis_index("x"), lax.axis_index("y"), lax.axis_index("c")


def _linear(place):
    return 4 * place[0] + 2 * place[1] + place[2]


def all_gather(name, shards):
    n_ops = len(shards)

    def body(*refs):
        ins, outs = refs[:n_ops], refs[n_ops:2 * n_ops]
        send_sems, recv_sems, local_sems = refs[2 * n_ops:]
        x, y, c = _my_place()
        me, sibling = (x, y, c), (x, y, 1 - c)
        chips = [(1 - x, y), (x, 1 - y), (1 - x, 1 - y)]

        def copy(t, k, block, to, src=None):
            dst = outs[t].at[_linear(block)]
            return pltpu.make_async_remote_copy(
                src_ref=dst if src is None else src, dst_ref=dst, send_sem=send_sems.at[t, k], recv_sem=recv_sems.at[t, k],
                device_id=to, device_id_type=MESH)

        mine = [pltpu.make_async_copy(ins[t], outs[t].at[_linear(me)], local_sems.at[t]) for t in range(n_ops)]
        for cp in mine:
            cp.start()
        first = []
        for t in range(n_ops):
            first.append(copy(t, 0, me, sibling, src=ins[t]))
            first += [copy(t, 1 + j, me, (*chip, c), src=ins[t]) for j, chip in enumerate(chips)]
        for cp in first:
            cp.start()
        passed = []
        for j, chip in enumerate(chips):
            for t in range(n_ops):
                copy(t, 1 + j, (*chip, c), me).wait_recv()
                fwd = copy(t, 4 + j, (*chip, c), sibling)
                fwd.start()
                passed.append(fwd)
        for t in range(n_ops):
            copy(t, 0, sibling, me).wait_recv()
            for j, chip in enumerate(chips):
                copy(t, 4 + j, (*chip, 1 - c), me).wait_recv()
        for cp in first + passed:
            cp.wait_send()
        for cp in mine:
            cp.wait()

    any_spec = pl.BlockSpec(memory_space=pl.ANY)
    outs = pl.pallas_call(
        body, name=name,
        in_specs=[any_spec] * n_ops, out_specs=[any_spec] * n_ops,
        out_shape=[jax.ShapeDtypeStruct((N_DEV, *s.shape), s.dtype) for s in shards],
        scratch_shapes=[pltpu.SemaphoreType.DMA((n_ops, 7)), pltpu.SemaphoreType.DMA((n_ops, 7)),
                        pltpu.SemaphoreType.DMA((n_ops,))],
    )(*shards)
    return list(outs)


_HBM = pl.BlockSpec(memory_space=pltpu.HBM)
_SEM = pl.BlockSpec(memory_space=pltpu.SEMAPHORE)
_EFFECT = pltpu.SideEffectType.DATAFLOW_SIDE_EFFECTING


def _hbm(a):
    return pltpu.with_memory_space_constraint(a, pltpu.HBM)


def _peers():
    x, y, c = _my_place()
    return [(x ^ (j >> 2), y ^ ((j >> 1) & 1), c ^ (j & 1)) for j in range(1, N_DEV)]


def exchange_start(name, groups, after):
    flat = [e for g in groups for e in g]
    n = len(flat)

    def body(*refs):
        srcs, lands = refs[:n], refs[n:2 * n]
        outs = refs[2 * n + 1:]
        sends, recvs, token = outs[:n], outs[n:2 * n], outs[4 * n]
        me = _linear(_my_place())
        for t, (_, _, src_view, land_view) in enumerate(flat):
            for peer in _peers():
                pltpu.make_async_remote_copy(
                    src_ref=src_view(srcs[t], _linear(peer)), dst_ref=land_view(lands[t], me),
                    send_sem=sends[t], recv_sem=recvs[t], device_id=peer, device_id_type=MESH).start()
        token[...] = jnp.zeros_like(token)

    sem_shapes = [pltpu.SemaphoreType.DMA(())] * (2 * n)
    thru = [pltpu.HBM(e[0].shape, e[0].dtype) for e in flat] + [pltpu.HBM(e[1].shape, e[1].dtype) for e in flat]
    outs = pl.pallas_call(
        body, name=name,
        out_shape=(*sem_shapes, *thru, jax.ShapeDtypeStruct((SUBLANES, LANES), F32)),
        in_specs=[*[_HBM] * (2 * n), pl.BlockSpec(memory_space=pl.ANY)],
        out_specs=(*[_SEM] * (2 * n), *[_HBM] * (2 * n), pl.BlockSpec(memory_space=pltpu.VMEM)),
        input_output_aliases={t: 2 * n + t for t in range(2 * n)},
        compiler_params=pltpu.CompilerParams(has_side_effects=_EFFECT),
    )(*[_hbm(e[0]) for e in flat], *[_hbm(e[1]) for e in flat], after)
    records, t = [], 0
    for g in groups:
        k = len(g)
        records.append((list(outs[t:t + k]), list(outs[n + t:n + t + k]), list(outs[2 * n + t:2 * n + t + k]),
                        list(outs[3 * n + t:3 * n + t + k])))
        t += k
    return records, outs[-1]


def exchange_wait(name, records, lands, land_of, views, after):
    srcs = [s for r in records for s in r[2]]
    sends = [s for r in records for s in r[0]]
    recvs = [s for r in records for s in r[1]]
    where = [(ri, k) for ri, r in enumerate(records) for k in range(len(r[2]))]
    ns, nl = len(srcs), len(lands)

    def body(*refs):
        src_refs, land_refs = refs[:ns], refs[ns:ns + nl]
        send_refs, recv_refs = refs[ns + nl:2 * ns + nl], refs[2 * ns + nl:3 * ns + nl]
        local_sems = refs[-1]
        me = _linear(_my_place())
        own = []
        for t, (ri, k) in enumerate(where):
            src_view, land_view, seven = views[ri][k]
            land_ref = land_refs[land_of[ri][k]]
            own_src, own_dst = src_view(src_refs[t], me), land_view(land_ref, me)
            rows_axis = len(own_src.shape) - 2
            n_rows = own_src.shape[rows_axis]
            n_cut = next(c for c in (LOCAL_COPY_CHUNKS, 4, 2, 1) if n_rows % (16 * c) == 0 or c == 1)
            for c in range(n_cut):
                cut = (slice(None),) * rows_axis + (pl.ds(c * (n_rows // n_cut), n_rows // n_cut), slice(None))
                own.append(pltpu.make_async_copy(own_src.at[cut], own_dst.at[cut], local_sems.at[t, c]))
                own[-1].start()
            seven_ref = seven(src_refs[t], land_ref)
            cp = pltpu.make_async_remote_copy(src_ref=seven_ref, dst_ref=seven_ref, send_sem=send_refs[t], recv_sem=recv_refs[t],
                                              device_id=_my_place(), device_id_type=MESH)
            cp.wait_send()
            cp.wait_recv()
        for cp in own:
            cp.wait()

    outs = pl.pallas_call(
        body, name=name,
        out_shape=tuple(pltpu.HBM(a.shape, a.dtype) for a in (*srcs, *lands)),
        in_specs=[*[_HBM] * (ns + nl), *[_SEM] * (2 * ns), pl.BlockSpec(memory_space=pl.ANY)], out_specs=tuple([_HBM] * (ns + nl)),
        input_output_aliases={t: t for t in range(ns + nl)},
        scratch_shapes=[pltpu.SemaphoreType.DMA((ns, LOCAL_COPY_CHUNKS))],
        compiler_params=pltpu.CompilerParams(has_side_effects=_EFFECT),
    )(*srcs, *lands, *sends, *recvs, after)
    return list(outs[ns:])


def _seven_slots_of_land(src_ref, land_ref):
    return land_ref.at[pl.ds(0, N_DEV - 1)]


def _seven_blocks_of_src(src_ref, land_ref):
    return src_ref.at[pl.ds(0, N_DEV - 1)]


def _whole(ref, dev):
    return ref


def _slot(ref, dev):
    return ref.at[dev]


def _slot_of_layer(layer):
    return lambda ref, dev: ref.at[dev, layer]


def _as_rows(a, lead=0):
    return a.reshape(a.shape[:lead] + (-1, a.shape[-1]))


def sum_adamw(name, parts, w, m, v, half=None):
    shape = w.shape
    c1 = 1.0 - ADAM_B1 ** ADAM_STEP
    c2 = 1.0 - ADAM_B2 ** ADAM_STEP
    if half is not None:
        layers, r, cols = shape
        tr = _tile(r, ROW_TILE, 16)
        grid = (layers, r // tr)
        p_spec = pl.BlockSpec((N_DEV, None, None, tr, cols), lambda l, i: (0, l, half, i, 0))
        blk = pl.BlockSpec((None, tr, cols), lambda l, i: (l, i, 0))
        args, out_sds, sem = (parts, w, m, v), jax.ShapeDtypeStruct(shape, F32), ("parallel", "parallel")
    else:
        p2, w2, m2, v2 = _as_rows(parts, 1), _as_rows(w), _as_rows(m), _as_rows(v)
        rows, cols = w2.shape
        tr = _tile(rows, ROW_TILE, 16)
        grid = (rows // tr,)
        p_spec = pl.BlockSpec((N_DEV, tr, cols), lambda i: (0, i, 0))
        blk = pl.BlockSpec((tr, cols), lambda i: (i, 0))
        args, out_sds, sem = (p2, w2, m2, v2), jax.ShapeDtypeStruct((rows, cols), F32), ("parallel",)

    def body(p_ref, w_ref, m_ref, v_ref, g_ref, d_ref, nm_ref, nv_ref):
        g = p_ref[0].astype(F32)
        for k in range(1, N_DEV):
            g = g + p_ref[k].astype(F32)
        wv = w_ref[...]
        nm = ADAM_B1 * m_ref[...] + (1.0 - ADAM_B1) * g
        nv = ADAM_B2 * v_ref[...] + (1.0 - ADAM_B2) * (g * g)
        g_ref[...] = g
        nm_ref[...] = nm
        nv_ref[...] = nv
        d_ref[...] = -ADAM_LR * ((nm / c1) / (jnp.sqrt(nv / c2) + ADAM_EPS) + ADAM_WD * wv)

    outs = pl.pallas_call(
        body, name=name, grid=grid, in_specs=[p_spec, blk, blk, blk], out_specs=[blk] * 4,
        out_shape=[out_sds] * 4, compiler_params=_params(sem),
    )(*args)
    return [o.reshape(shape) for o in outs]


def _unblock_cols(g):
    g = jnp.moveaxis(g, 0, -2)
    return g.reshape(g.shape[:-2] + (g.shape[-2] * g.shape[-1],))


def _block_cols(a):
    r, c = a.shape
    return jnp.moveaxis(a.reshape(r, N_DEV, c // N_DEV), 1, 0)


def _my_cols(a, n):
    me = _linear(_my_place())
    return lax.dynamic_slice_in_dim(a, me * n, n, axis=a.ndim - 1)


def _pack(arrays):
    flat = jnp.concatenate([a.reshape(-1).astype(F32) for a in arrays])
    pad = (-flat.shape[0]) % (ROW_TILE * LANES)
    return jnp.pad(flat, (0, pad)).reshape(-1, LANES)


def _unpack(packed, like):
    flat = packed.reshape(-1)
    out, pos = [], 0
    for a in like:
        out.append(flat[pos:pos + a.size].reshape(a.shape))
        pos += a.size
    return out


def kernel(x, meta_tokens, norm_w, ssd_w_in, ssd_conv_w, ssd_conv_b, ssd_dt_bias, ssd_a_log, ssd_d, ssd_norm_w, ssd_w_out, pool_w, pool_b, pool_scale, ffn_w_gate, ffn_w_up, ffn_w_down, loss_target, m_meta_tokens, m_norm_w, m_ssd_w_in, m_ssd_conv_w, m_ssd_conv_b, m_ssd_dt_bias, m_ssd_a_log, m_ssd_d, m_ssd_norm_w, m_ssd_w_out, m_pool_w, m_pool_b, m_pool_scale, m_ffn_w_gate, m_ffn_w_up, m_ffn_w_down, v_meta_tokens, v_norm_w, v_ssd_w_in, v_ssd_conv_w, v_ssd_conv_b, v_ssd_dt_bias, v_ssd_a_log, v_ssd_d, v_ssd_norm_w, v_ssd_w_out, v_pool_w, v_pool_b, v_pool_scale, v_ffn_w_gate, v_ffn_w_up, v_ffn_w_down):
    seq, d = x.shape[1], x.shape[2]
    depth = norm_w.shape[0]
    n_ssd = ssd_w_in.shape[0]
    d_inner = ssd_norm_w.shape[1]
    heads = d_inner // HEAD_DIM
    rpg = heads // SSD_GROUPS
    conv_dim = ssd_conv_b.shape[1]
    zx_cols = d_inner + conv_dim
    d_in_proj = zx_cols + heads
    rows = PAD_FRONT + N_META + seq
    assert rows % CHUNK == 0 and rpg % 2 == 0 and heads <= LANES and rpg <= SUBLANES

    hidden = ffn_w_down.shape[1] * N_DEV
    n_pool = pool_w.shape[0]

    small = all_gather("gather_small", [meta_tokens, norm_w, ssd_conv_w, pool_b, pool_scale])
    meta_f, norm_f, convw_f, poolb_f, pools_f = [_unblock_cols(s) for s in small]

    gather_recs = {}

    def start_gather(i, after):
        def entry(shard):
            return (shard.astype(BF16), lax.empty((N_DEV, *shard.shape), BF16), _whole, _slot)
        mixer = [ssd_w_in[i // 2], ssd_w_out[i // 2]] if i % 2 == 0 else [pool_w[i // 2]]
        gate_up = jnp.stack([ffn_w_gate[i], ffn_w_up[i]])
        groups = [*[[entry(w)] for w in mixer], [entry(gate_up), entry(ffn_w_down[i])]]
        gather_recs[i], tok = exchange_start(f"gather_start_{i}", groups, after)
        return tok

    def gathered(name, rec, after):
        n_e = len(rec[2])
        return exchange_wait(name, [rec], rec[3], [list(range(n_e))], [[(_whole, _slot, _seven_slots_of_land)] * n_e], after)

    pending = meta_f
    for i in range(min(GATHER_AHEAD, depth)):
        pending = start_gather(i, pending)

    pad_h = lambda a: jnp.pad(a.astype(F32), ((0, 0), (0, LANES - heads)))
    bias_pad = pad_h(ssd_dt_bias)

    def head_layouts(vec):
        g = vec.reshape(SSD_GROUPS, rpg)
        row = jnp.pad(g, ((0, 0), (0, LANES - rpg)))[:, None, :]
        col = jnp.pad(g, ((0, 0), (0, SUBLANES - rpg)))[:, :, None]
        return row, col

    def dt_layouts(dt):
        g = dt[:, :heads].reshape(rows, SSD_GROUPS, rpg)
        col = jnp.pad(jnp.moveaxis(g, 1, 0), ((0, 0), (0, 0), (0, LANES - rpg)))
        row = jnp.pad(jnp.transpose(g, (1, 2, 0)), ((0, 0), (0, SUBLANES - rpg), (0, 0)))
        return col, row

    h = jnp.concatenate([jnp.zeros((PAD_FRONT, d), F32), meta_f, x[0]], axis=0)
    saved = []
    for i in range(depth):
        j = i // 2
        s = {"h": h}
        w_pre_mix = norm_f[i, 0]
        if i + GATHER_AHEAD < depth:
            pending = start_gather(i + GATHER_AHEAD, pending if i == 0 else h)
        (mixer_w,) = gathered(f"gather_wait_mix_{i}", gather_recs[i][0], h)
        if i % 2 == 0:
            w_in = jnp.pad(_unblock_cols(mixer_w), ((0, 0), (0, LANES - heads)))
            u, u_t = rmsnorm_fwd(f"norm_pre_mix_{i}", h, w_pre_mix, out_dtype=BF16, transposed=True, after=pending)
            zx = matmul(f"ssd_in_{i}", u, w_in, **MM_ROWS_RESIDENT, tn=384 if w_in.shape[1] % 384 == 0 else 512)
            xbc = conv_fwd(f"ssd_conv_{i}", zx, convw_f[j], ssd_conv_b[j], d_inner, conv_dim)
            dt = dt_fwd(f"ssd_dt_{i}", zx, bias_pad[j:j + 1], zx_cols)
            dt_col, dt_row = dt_layouts(dt)
            a_neg = -jnp.exp(ssd_a_log[j].astype(F32))
            a_row, a_col = head_layouts(a_neg)
            d_row, _ = head_layouts(ssd_d[j].astype(F32))
            y, sprev = ssd_scan_fwd(f"ssd_scan_{i}", xbc, dt_col, dt_row, a_row, a_col, d_row, d_inner)
            yn, yn_t = gatenorm_fwd(f"ssd_gate_{i}", y, zx, ssd_norm_w[j], d_inner)
            (w_out,) = gathered(f"gather_wait_out_{i}", gather_recs[i][1], yn)
            w_out = w_out.reshape(d_inner, d)
            mix = matmul(f"ssd_out_{i}", yn, w_out, **MM_DEEP)
            s.update(u_t=u_t, zx=zx, xbc=xbc, dt_col=dt_col, dt_row=dt_row, a_row=a_row, a_col=a_col, d_row=d_row,
                     a_neg=a_neg, y=y, sprev=sprev, yn_t=yn_t, w_in=w_in, w_out=w_out)
        else:
            w_pool = jnp.moveaxis(mixer_w, 0, 1).reshape(len(POOL_WINDOWS), d // len(POOL_WINDOWS), -1)
            u = rmsnorm_fwd(f"norm_pre_mix_{i}", h, w_pre_mix, after=pending)
            mixed = pool_sub(f"pool_sub_{i}", u, False)
            pre, mix = pool_proj_fwd(f"pool_proj_{i}", mixed, w_pool, poolb_f[j], pools_f[j])
            s.update(mixed=mixed, pre=pre, w_pool=w_pool)
        pending = None
        h1 = rmsnorm_fwd(f"norm_post_mix_{i}", mix, norm_f[i, 1], res=h)
        w_gu, w_down = gathered(f"gather_wait_ffn_{i}", gather_recs[i][-1], h1)
        u2, u2_t = rmsnorm_fwd(f"norm_pre_ffn_{i}", h1, norm_f[i, 2], out_dtype=BF16, transposed=True)
        gp = ffn_in_fwd(f"ffn_in_{i}", u2, w_gu)
        act, act_t = swiglu_fwd(f"ffn_act_{i}", gp)
        f = ffn_out_fwd(f"ffn_out_{i}", act, w_down)
        h = rmsnorm_fwd(f"norm_post_ffn_{i}", f, norm_f[i, 3], res=h1)
        s.update(mix=mix, h1=h1, u2_t=u2_t, gp=gp, act_t=act_t, f=f, w_gu=w_gu, w_down=w_down)
        saved.append(s)

    dh, loss_local = loss_head("loss_head", h, loss_target[0])
    loss = lax.psum(loss_local, AXES)

    g_norm = [[None] * 4 for _ in range(depth)]
    g_convw, g_convb, g_dtb, g_alog, g_dskip, g_ssdnorm = ([None] * n_ssd for _ in range(6))
    g_poolb, g_pools = ([None] * n_pool for _ in range(2))
    hid_s = hidden // N_DEV
    lands = {"in": lax.empty((N_DEV, *ssd_w_in.shape), BF16), "out": lax.empty((N_DEV, *ssd_w_out.shape), BF16),
             "pool": lax.empty((N_DEV, *pool_w.shape), BF16), "down": lax.empty((N_DEV, *ffn_w_down.shape), BF16),
             "gate_up": lax.empty((N_DEV, depth, 2, d, hid_s), BF16)}
    scatter_recs, scatter_keys, scatter_views = [], [], []

    def scatter(name, blocks, layer, after):
        view = _slot_of_layer(layer)
        (rec,), tok = exchange_start(name, [[(b, lands[key], _slot, view) for key, b in blocks]], after)
        for (key, _), thru in zip(blocks, rec[3]):
            lands[key] = thru
        scatter_recs.append(rec)
        scatter_keys.append([key for key, _ in blocks])
        scatter_views.append([(_slot, view, _seven_blocks_of_src)] * len(blocks))
        return tok

    tok = None
    for i in reversed(range(depth)):
        j = i // 2
        s = saved[i]
        df, g_norm[i][3] = rmsnorm_bwd(f"norm_post_ffn_bwd_{i}", s["f"], norm_f[i, 3], dh, after=tok)
        df_b = df.astype(BF16)
        dact = ffn_out_bwd_x(f"ffn_out_bwd_x_{i}", df_b, s["w_down"])
        g_down = ffn_out_bwd_w(f"ffn_out_bwd_w_{i}", s["act_t"], df_b)
        dgp = swiglu_bwd(f"ffn_act_bwd_{i}", s["gp"], dact)
        g_gu = ffn_in_bwd_w(f"ffn_in_bwd_w_{i}", s["u2_t"], dgp)
        tok = scatter(f"scatter_start_ffn_{i}", [("gate_up", g_gu), ("down", g_down)], i, g_gu)
        du2 = ffn_in_bwd_x(f"ffn_in_bwd_x_{i}", dgp, s["w_gu"])
        dh1, g_norm[i][2] = rmsnorm_bwd(f"norm_pre_ffn_bwd_{i}", s["h1"], norm_f[i, 2], du2, add=dh, after=tok)
        dmix, g_norm[i][1] = rmsnorm_bwd(f"norm_post_mix_bwd_{i}", s["mix"], norm_f[i, 1], dh1)
        if i % 2 == 0:
            dmix_b = dmix.astype(BF16)
            dyn = matmul(f"ssd_out_bwd_x_{i}", dmix_b, s["w_out"], tb=True, **MM_ROWS_RESIDENT)
            g_out = matmul(f"ssd_out_bwd_w_{i}", s["yn_t"], dmix_b, out_dtype=BF16, **MM_COLS_RESIDENT)
            tok = scatter(f"scatter_start_out_{i}", [("out", g_out.reshape(N_DEV, d_inner // N_DEV, d))], j, g_out)
            dy, dz, g_ssdnorm[j] = gatenorm_bwd(f"ssd_gate_bwd_{i}", s["y"], s["zx"], ssd_norm_w[j], dyn, d_inner, after=tok)
            dx, db, dc, ddt_col, ddt_row, dar, dac, ddsk = ssd_scan_bwd(
                f"ssd_scan_bwd_{i}", s["xbc"], s["dt_col"], s["dt_row"], s["a_row"], s["a_col"], s["d_row"], s["sprev"], dy, d_inner)
            ddt = (jnp.moveaxis(ddt_col[:, :, :rpg], 0, 1).reshape(rows, heads)
                   + jnp.transpose(ddt_row[:, :rpg, :], (2, 0, 1)).reshape(rows, heads))
            d_a = (dar[:, 0, :rpg] + dac[:, :rpg, 0]).reshape(heads)
            g_alog[j] = d_a * s["a_neg"]
            g_dskip[j] = ddsk[:, 0, :rpg].reshape(heads)
            ddtr, dbias = dt_bwd(f"ssd_dt_bwd_{i}", s["zx"], bias_pad[j:j + 1], pad_h(ddt), zx_cols)
            g_dtb[j] = dbias[0, :heads]
            dxbc_raw, g_convw[j], dconvb = conv_bwd(
                f"ssd_conv_bwd_{i}", s["zx"], convw_f[j], ssd_conv_b[j], jnp.concatenate([dx, db, dc], axis=1), d_inner, conv_dim)
            g_convb[j] = dconvb[0]
            dzx = jnp.concatenate([dz, dxbc_raw, ddtr], axis=1)
            g_in = matmul(f"ssd_in_bwd_w_{i}", s["u_t"], dzx, out_dtype=BF16, **MM_ROWS_RESIDENT,
                          tn=384 if dzx.shape[1] % 384 == 0 else 512)
            tok = scatter(f"scatter_start_in_{i}", [("in", _block_cols(g_in[:, :d_in_proj]))], j, g_in)
            du = matmul(f"ssd_in_bwd_x_{i}", dzx, s["w_in"], tb=True, **{**MM_DEEP, "tk": 1152 if dzx.shape[1] % 1152 == 0 else 512})
        else:
            dmixed, g_poolw, g_poolb[j], g_pools[j] = pool_proj_bwd(
                f"pool_proj_bwd_{i}", dmix, s["pre"], s["mixed"], s["w_pool"], pools_f[j])
            ng, gd = g_poolw.shape[0], g_poolw.shape[1]
            blk_pool = jnp.moveaxis(g_poolw.astype(BF16).reshape(ng, N_DEV, gd // N_DEV, gd), 1, 0)
            tok = scatter(f"scatter_start_pool_{i}", [("pool", blk_pool)], j, blk_pool)
            du = pool_sub(f"pool_sub_bwd_{i}", dmixed, True)
        dh, g_norm[i][0] = rmsnorm_bwd(f"norm_pre_mix_bwd_{i}", s["h"], norm_f[i, 0], du, add=dh1, after=tok)

    grad_x = dh[PAD_FRONT + N_META:][None]
    g_meta = dh[PAD_FRONT:PAD_FRONT + N_META]

    small_grads = [g_meta, jnp.stack([jnp.stack(r) for r in g_norm]), jnp.stack(g_convw), jnp.stack(g_convb), jnp.stack(g_dtb),
                   jnp.stack(g_alog), jnp.stack(g_dskip), jnp.stack(g_ssdnorm), jnp.stack(g_poolb), jnp.stack(g_pools)]
    packed_g = _pack(small_grads)
    (small_rec,), _ = exchange_start(
        "gather_small_grads_start", [[(packed_g, lax.empty((N_DEV, *packed_g.shape), F32), _whole, _slot)]], packed_g)
    mine = lambda a: _my_cols(a, a.shape[-1] // N_DEV)
    small_w = [meta_tokens, norm_w, ssd_conv_w, ssd_conv_b, ssd_dt_bias, ssd_a_log, ssd_d, ssd_norm_w, pool_b, pool_scale]
    small_m = [m_meta_tokens, m_norm_w, m_ssd_conv_w, m_ssd_conv_b, m_ssd_dt_bias, m_ssd_a_log, m_ssd_d, m_ssd_norm_w, m_pool_b, m_pool_scale]
    small_v = [v_meta_tokens, v_norm_w, v_ssd_conv_w, v_ssd_conv_b, v_ssd_dt_bias, v_ssd_a_log, v_ssd_d, v_ssd_norm_w, v_pool_b, v_pool_scale]
    sharded = [True, True, True, False, False, False, False, False, True, True]
    def widen(a, is_sharded, full):
        if not is_sharded:
            return a
        return lax.dynamic_update_slice_in_dim(jnp.zeros(full.shape, F32), a, _linear(_my_place()) * a.shape[-1], axis=a.ndim - 1)
    packed_w = _pack([widen(a, sh, g) for a, sh, g in zip(small_w, sharded, small_grads)])
    packed_m = _pack([widen(a, sh, g) for a, sh, g in zip(small_m, sharded, small_grads)])
    packed_v = _pack([widen(a, sh, g) for a, sh, g in zip(small_v, sharded, small_grads)])

    def landed(name, keys, after):
        rs = [r for r, ks in enumerate(scatter_keys) if set(ks) <= set(keys)]
        return exchange_wait(name, [scatter_recs[r] for r in rs], [lands[k] for k in keys],
                             [[keys.index(k) for k in scatter_keys[r]] for r in rs], [scatter_views[r] for r in rs], after)

    p_gu, p_down = landed("scatter_wait_ffn", ["gate_up", "down"], packed_g)
    a_gate = sum_adamw("adamw_ffn_w_gate", p_gu, ffn_w_gate, m_ffn_w_gate, v_ffn_w_gate, half=0)
    a_up = sum_adamw("adamw_ffn_w_up", p_gu, ffn_w_up, m_ffn_w_up, v_ffn_w_up, half=1)
    a_down = sum_adamw("adamw_ffn_w_down", p_down, ffn_w_down, m_ffn_w_down, v_ffn_w_down)
    (p_pool,) = landed("scatter_wait_pool", ["pool"], a_down[0])
    a_pool = sum_adamw("adamw_pool_w", p_pool, pool_w, m_pool_w, v_pool_w)
    (p_out,) = landed("scatter_wait_out", ["out"], a_pool[0])
    a_out = sum_adamw("adamw_ssd_w_out", p_out, ssd_w_out, m_ssd_w_out, v_ssd_w_out)
    (p_in,) = landed("scatter_wait_in", ["in"], a_out[0])
    a_in = sum_adamw("adamw_ssd_w_in", p_in, ssd_w_in, m_ssd_w_in, v_ssd_w_in)
    bg = [a_in, a_out, a_pool, a_gate, a_up, a_down]

    (small_parts,) = exchange_wait("gather_small_grads_wait", [small_rec], small_rec[3], [[0]],
                                   [[(_whole, _slot, _seven_slots_of_land)]], a_in[0])
    sm = [_unpack(o, small_grads) for o in sum_adamw("adamw_small", small_parts, packed_w, packed_m, packed_v)]
    sm = [[mine(a) if sh else a for a, sh in zip(group, sharded)] for group in sm]

    def ordered(kind):
        s_ = sm[kind]
        b_ = [o[kind] for o in bg]
        return [s_[0], s_[1], b_[0], s_[2], s_[3], s_[4], s_[5], s_[6], s_[7], b_[1], b_[2], s_[8], s_[9], b_[3], b_[4], b_[5]]

    return (loss, grad_x, *ordered(0), *ordered(1), *ordered(2), *ordered(3))
```

```python
import functools

import jax
import jax.numpy as jnp
from jax import lax
from jax.experimental import pallas as pl
from jax.experimental.pallas import tpu as pltpu

F32 = jnp.float32
BF16 = jnp.bfloat16
MESH = pl.DeviceIdType.MESH
AXES = ("x", "y", "c")
N_DEV = 8

N_META = 16
EPS = 1e-6
HEAD_DIM = 64
D_STATE = 128
SSD_GROUPS = 8
D_CONV = 4
CHUNK = 256
POOL_WINDOWS = (2, 4, 8, 16)
ADAM_LR, ADAM_B1, ADAM_B2, ADAM_EPS, ADAM_WD, ADAM_STEP = 0.001, 0.9, 0.999, 1e-08, 0.01, 10

PAD_FRONT = (-N_META) % CHUNK
LANES = 128
SUBLANES = 8
ROW_TILE = 256
CONV_TILE = 256
POOL_TILE = 128
VMEM_LIMIT = 56 * 1024 * 1024


def _params(sem=None):
    return pltpu.CompilerParams(dimension_semantics=sem, vmem_limit_bytes=VMEM_LIMIT)


def _tile(n, target, mult):
    if n <= target:
        return n
    best = None
    for t in range(mult, target + 1, mult):
        if n % t == 0:
            best = t
    assert best is not None, (n, target, mult)
    return best


def _dot(a, b, ca, cb):
    return lax.dot_general(a, b, (((ca,), (cb,)), ((), ())), preferred_element_type=F32)


def _sigmoid(x):
    return 1.0 / (1.0 + jnp.exp(-x))


def _row_mask(shape, first_row):
    rows = lax.broadcasted_iota(jnp.int32, shape, 0) + first_row
    return rows >= PAD_FRONT


MM_ROWS_RESIDENT = dict(tm=2304, tk=2304)
MM_COLS_RESIDENT = dict(tm=512, tn=2304, tk=2304)
MM_DEEP = dict(tm=1152, tn=1024, tk=512)
GATHER_AHEAD = 2

def matmul(name, a, b, *, tb=False, out_dtype=F32, tm=768, tn=512, tk=2048):
    m, kdim = a.shape
    if tb:
        n, k2 = b.shape
    else:
        k2, n = b.shape
    assert kdim == k2, (a.shape, b.shape, tb)
    tm = _tile(m, tm, 16)
    tn = _tile(n, tn, LANES)
    tk = _tile(kdim, tk, LANES)
    nk = kdim // tk
    a_spec = pl.BlockSpec((tm, tk), lambda i, j, k: (i, k))
    b_spec = pl.BlockSpec((tn, tk), lambda i, j, k: (j, k)) if tb else pl.BlockSpec((tk, tn), lambda i, j, k: (k, j))

    def body_single(a_ref, b_ref, o_ref):
        o_ref[...] = _dot(a_ref[...], b_ref[...], 1, 1 if tb else 0).astype(o_ref.dtype)

    def body_acc(a_ref, b_ref, o_ref, acc_ref):
        k = pl.program_id(2)

        @pl.when(k == 0)
        def _():
            acc_ref[...] = jnp.zeros_like(acc_ref)

        acc_ref[...] += _dot(a_ref[...], b_ref[...], 1, 1 if tb else 0)

        @pl.when(k == nk - 1)
        def _():
            o_ref[...] = acc_ref[...].astype(o_ref.dtype)

    return pl.pallas_call(
        body_single if nk == 1 else body_acc, name=name, grid=(m // tm, n // tn, nk),
        in_specs=[a_spec, b_spec], out_specs=pl.BlockSpec((tm, tn), lambda i, j, k: (i, j)),
        out_shape=jax.ShapeDtypeStruct((m, n), out_dtype),
        scratch_shapes=[] if nk == 1 else [pltpu.VMEM((tm, tn), F32)],
        compiler_params=_params(("parallel", "parallel", "arbitrary")),
    )(a, b)


_TOKEN_SPEC = pl.BlockSpec((SUBLANES, LANES), lambda i: (0, 0))


def rmsnorm_fwd(name, x, w, res=None, out_dtype=F32, transposed=False, after=None):
    rows, d = x.shape
    tr = ROW_TILE
    row_spec = pl.BlockSpec((tr, d), lambda i: (i, 0))
    w_spec = pl.BlockSpec((1, d), lambda i: (0, 0))
    n_in = 2 + (res is not None) + (after is not None)

    def body(*refs):
        x_ref, w_ref = refs[:2]
        outs = refs[n_in:]
        xv = x_ref[...]
        y = xv * lax.rsqrt(jnp.mean(xv * xv, axis=-1, keepdims=True) + EPS) * w_ref[...]
        if res is not None:
            y = refs[2][...] + y
        outs[0][...] = y.astype(out_dtype)
        if transposed:
            outs[1][...] = y.T.astype(out_dtype)

    args = [x, w.reshape(1, d)] + ([] if res is None else [res]) + ([] if after is None else [after])
    specs = [row_spec, w_spec] + ([] if res is None else [row_spec]) + ([] if after is None else [_TOKEN_SPEC])
    out_specs, out_shape = [row_spec], [jax.ShapeDtypeStruct((rows, d), out_dtype)]
    if transposed:
        out_specs.append(pl.BlockSpec((d, tr), lambda i: (0, i)))
        out_shape.append(jax.ShapeDtypeStruct((d, rows), out_dtype))
    outs = pl.pallas_call(
        body, name=name, grid=(rows // tr,), in_specs=specs, out_specs=out_specs, out_shape=out_shape,
        compiler_params=_params(("parallel",)),
    )(*args)
    return outs if transposed else outs[0]


def rmsnorm_bwd(name, x, w, dy, add=None, after=None):
    rows, d = x.shape
    tr = ROW_TILE
    row_spec = pl.BlockSpec((tr, d), lambda i: (i, 0))
    w_spec = pl.BlockSpec((1, d), lambda i: (0, 0))

    def body(*refs):
        x_ref, w_ref, dy_ref = refs[:3]
        add_ref = None if add is None else refs[3]
        dx_ref, dw_ref = refs[-2:]
        xv = x_ref[...]
        dyv = dy_ref[...].astype(F32)
        r = lax.rsqrt(jnp.mean(xv * xv, axis=-1, keepdims=True) + EPS)
        xh = xv * r
        dxh = dyv * w_ref[...]
        dx = r * (dxh - xh * jnp.mean(dxh * xh, axis=-1, keepdims=True))
        if add is not None:
            dx = dx + add_ref[...]
        dx_ref[...] = dx

        @pl.when(pl.program_id(0) == 0)
        def _():
            dw_ref[...] = jnp.zeros_like(dw_ref)

        dw_ref[...] += jnp.sum(dyv * xh, axis=0, keepdims=True)

    args = [x, w.reshape(1, d), dy] + ([] if add is None else [add]) + ([] if after is None else [after])
    specs = [row_spec, w_spec, row_spec] + ([] if add is None else [row_spec]) + ([] if after is None else [_TOKEN_SPEC])
    dx, dw = pl.pallas_call(
        body, name=name, grid=(rows // tr,), in_specs=specs, out_specs=[row_spec, w_spec],
        out_shape=[jax.ShapeDtypeStruct((rows, d), F32), jax.ShapeDtypeStruct((1, d), F32)],
        compiler_params=_params(("arbitrary",)),
    )(*args)
    return dx, dw.reshape(d)


def loss_head(name, h, target):
    rows, d = h.shape
    tr = ROW_TILE
    first = (PAD_FRONT + N_META) // tr
    assert (PAD_FRONT + N_META) % tr == 0 and target.shape[0] == rows - first * tr

    def body(h_ref, t_ref, dh_ref, loss_ref):
        i = pl.program_id(0)

        @pl.when(i == 0)
        def _():
            loss_ref[...] = jnp.zeros_like(loss_ref)

        keep = (i >= first).astype(F32)
        diff = (h_ref[...] - t_ref[...]) * keep
        dh_ref[...] = diff / d
        loss_ref[...] += 0.5 * jnp.sum(diff * diff) / d

    dh, loss = pl.pallas_call(
        body, name=name, grid=(rows // tr,),
        in_specs=[pl.BlockSpec((tr, d), lambda i: (i, 0)), pl.BlockSpec((tr, d), lambda i: (jnp.maximum(i - first, 0), 0))],
        out_specs=[pl.BlockSpec((tr, d), lambda i: (i, 0)), pl.BlockSpec((SUBLANES, LANES), lambda i: (0, 0))],
        out_shape=[jax.ShapeDtypeStruct((rows, d), F32), jax.ShapeDtypeStruct((SUBLANES, LANES), F32)],
        compiler_params=_params(("arbitrary",)),
    )(h, target)
    return dh, loss[0, 0]


def _mm_call(name, a, b, grid, a_spec, b_spec, o_spec, out_sds, ca, cb, red_axis=None):
    n_red = None if red_axis is None else grid[red_axis]

    def body_single(a_ref, b_ref, o_ref):
        o_ref[...] = _dot(a_ref[...], b_ref[...], ca, cb).astype(o_ref.dtype)

    def body_acc(a_ref, b_ref, o_ref, acc_ref):
        k = pl.program_id(red_axis)

        @pl.when(k == 0)
        def _():
            acc_ref[...] = jnp.zeros_like(acc_ref)

        acc_ref[...] += _dot(a_ref[...], b_ref[...], ca, cb)

        @pl.when(k == n_red - 1)
        def _():
            o_ref[...] = acc_ref[...].astype(o_ref.dtype)

    acc_shape = tuple(s for s in o_spec.block_shape if s is not None)
    sem = tuple("arbitrary" if ax == red_axis else "parallel" for ax in range(len(grid)))
    return pl.pallas_call(
        body_single if red_axis is None else body_acc, name=name, grid=grid, in_specs=[a_spec, b_spec], out_specs=o_spec,
        out_shape=out_sds, scratch_shapes=[] if red_axis is None else [pltpu.VMEM(acc_shape, F32)],
        compiler_params=_params(sem),
    )(a, b)


def ffn_in_fwd(name, u2, w_gu):
    rows, d = u2.shape
    hs = w_gu.shape[-1]
    return _mm_call(name, u2, w_gu, (N_DEV, 2), pl.BlockSpec((rows, d), lambda k, t: (0, 0)),
                    pl.BlockSpec((None, None, d, hs), lambda k, t: (k, t, 0, 0)),
                    pl.BlockSpec((None, None, rows, hs), lambda k, t: (k, t, 0, 0)),
                    jax.ShapeDtypeStruct((N_DEV, 2, rows, hs), F32), 1, 0)


def ffn_out_fwd(name, act, w_down):
    _, rows, hs = act.shape
    d = w_down.shape[-1]
    tm, tn = _tile(rows, 1152, 16), _tile(d, 1024, LANES)
    return _mm_call(name, act, w_down, (rows // tm, d // tn, N_DEV), pl.BlockSpec((None, tm, hs), lambda i, j, k: (k, i, 0)),
                    pl.BlockSpec((None, hs, tn), lambda i, j, k: (k, 0, j)), pl.BlockSpec((tm, tn), lambda i, j, k: (i, j)),
                    jax.ShapeDtypeStruct((rows, d), F32), 1, 0, red_axis=2)


def ffn_out_bwd_x(name, df, w_down):
    rows, d = df.shape
    hs = w_down.shape[1]
    return _mm_call(name, df, w_down, (N_DEV,), pl.BlockSpec((rows, d), lambda k: (0, 0)),
                    pl.BlockSpec((None, hs, d), lambda k: (k, 0, 0)), pl.BlockSpec((None, rows, hs), lambda k: (k, 0, 0)),
                    jax.ShapeDtypeStruct((N_DEV, rows, hs), F32), 1, 1)


def ffn_out_bwd_w(name, act_t, df):
    _, hs, rows = act_t.shape
    d = df.shape[1]
    return _mm_call(name, act_t, df, (N_DEV,), pl.BlockSpec((None, hs, rows), lambda k: (k, 0, 0)),
                    pl.BlockSpec((rows, d), lambda k: (0, 0)), pl.BlockSpec((None, hs, d), lambda k: (k, 0, 0)),
                    jax.ShapeDtypeStruct((N_DEV, hs, d), BF16), 1, 0)


def ffn_in_bwd_w(name, u2_t, dgp):
    d, rows = u2_t.shape
    hs = dgp.shape[-1]
    return _mm_call(name, u2_t, dgp, (N_DEV, 2), pl.BlockSpec((d, rows), lambda k, t: (0, 0)),
                    pl.BlockSpec((None, None, rows, hs), lambda k, t: (k, t, 0, 0)),
                    pl.BlockSpec((None, None, d, hs), lambda k, t: (k, t, 0, 0)),
                    jax.ShapeDtypeStruct((N_DEV, 2, d, hs), BF16), 1, 0)


def ffn_in_bwd_x(name, dgp, w_gu):
    _, _, rows, hs = dgp.shape
    d = w_gu.shape[2]
    tm, tn = _tile(rows, 1152, 16), _tile(d, 1024, LANES)
    return _mm_call(name, dgp.reshape(2 * N_DEV, rows, hs), w_gu.reshape(2 * N_DEV, d, hs), (rows // tm, d // tn, 2 * N_DEV),
                    pl.BlockSpec((None, tm, hs), lambda i, j, k: (k, i, 0)), pl.BlockSpec((None, tn, hs), lambda i, j, k: (k, j, 0)),
                    pl.BlockSpec((tm, tn), lambda i, j, k: (i, j)), jax.ShapeDtypeStruct((rows, d), F32), 1, 1, red_axis=2)


def swiglu_fwd(name, gp):
    _, _, rows, hs = gp.shape
    tr = _tile(rows, 768, ROW_TILE)

    def body(gp_ref, a_ref, at_ref):
        g = gp_ref[0]
        act = g * _sigmoid(g) * gp_ref[1]
        a_ref[...] = act.astype(a_ref.dtype)
        at_ref[...] = act.T.astype(at_ref.dtype)

    return pl.pallas_call(
        body, name=name, grid=(N_DEV, rows // tr),
        in_specs=[pl.BlockSpec((None, 2, tr, hs), lambda k, i: (k, 0, i, 0))],
        out_specs=[pl.BlockSpec((None, tr, hs), lambda k, i: (k, i, 0)), pl.BlockSpec((None, hs, tr), lambda k, i: (k, 0, i))],
        out_shape=[jax.ShapeDtypeStruct((N_DEV, rows, hs), BF16), jax.ShapeDtypeStruct((N_DEV, hs, rows), BF16)],
        compiler_params=_params(("parallel", "parallel")),
    )(gp)


def swiglu_bwd(name, gp, dact):
    _, _, rows, hs = gp.shape
    tr = _tile(rows, 768, ROW_TILE)

    def body(gp_ref, da_ref, o_ref):
        g = gp_ref[0]
        s = _sigmoid(g)
        dav = da_ref[...]
        o_ref[0] = (dav * gp_ref[1] * (s * (1.0 + g * (1.0 - s)))).astype(o_ref.dtype)
        o_ref[1] = (dav * (g * s)).astype(o_ref.dtype)

    blk = pl.BlockSpec((None, 2, tr, hs), lambda k, i: (k, 0, i, 0))
    return pl.pallas_call(
        body, name=name, grid=(N_DEV, rows // tr),
        in_specs=[blk, pl.BlockSpec((None, tr, hs), lambda k, i: (k, i, 0))], out_specs=blk,
        out_shape=jax.ShapeDtypeStruct(gp.shape, BF16), compiler_params=_params(("parallel", "parallel")),
    )(gp, dact)


def conv_fwd(name, zx, conv_w, conv_b, d_inner, conv_dim):
    rows = zx.shape[0]
    tc = CONV_TILE
    off = d_inner // tc
    assert d_inner % tc == 0 and conv_dim % tc == 0

    def body(u_ref, w_ref, b_ref, o_ref):
        u = u_ref[...]
        acc = u * w_ref[D_CONV - 1:D_CONV, :] + b_ref[...]
        for s in range(1, D_CONV):
            acc = acc + pltpu.roll(u, s, axis=0) * w_ref[D_CONV - 1 - s:D_CONV - s, :]
        y = acc * _sigmoid(acc)
        o_ref[...] = jnp.where(_row_mask(y.shape, 0), y, 0.0)

    return pl.pallas_call(
        body, name=name, grid=(conv_dim // tc,),
        in_specs=[pl.BlockSpec((rows, tc), lambda j: (0, off + j)), pl.BlockSpec((D_CONV, tc), lambda j: (0, j)),
                  pl.BlockSpec((1, tc), lambda j: (0, j))],
        out_specs=pl.BlockSpec((rows, tc), lambda j: (0, j)),
        out_shape=jax.ShapeDtypeStruct((rows, conv_dim), F32), compiler_params=_params(("parallel",)),
    )(zx, conv_w, conv_b.reshape(1, conv_dim))


def conv_bwd(name, zx, conv_w, conv_b, dxbc, d_inner, conv_dim):
    rows = zx.shape[0]
    tc = CONV_TILE
    off = d_inner // tc

    def body(u_ref, w_ref, b_ref, dy_ref, du_ref, dw_ref, db_ref):
        u = u_ref[...]
        wk = [w_ref[D_CONV - 1 - s:D_CONV - s, :] for s in range(D_CONV)]
        shifted = [u] + [pltpu.roll(u, s, axis=0) for s in range(1, D_CONV)]
        acc = u * wk[0] + b_ref[...]
        for s in range(1, D_CONV):
            acc = acc + shifted[s] * wk[s]
        sg = _sigmoid(acc)
        mask = _row_mask(acc.shape, 0)
        dpre = jnp.where(mask, dy_ref[...] * (sg * (1.0 + acc * (1.0 - sg))), 0.0)
        db_ref[...] = jnp.sum(dpre, axis=0, keepdims=True)
        du = dpre * wk[0]
        dw_ref[D_CONV - 1:D_CONV, :] = jnp.sum(dpre * u, axis=0, keepdims=True)
        for s in range(1, D_CONV):
            du = du + pltpu.roll(dpre, rows - s, axis=0) * wk[s]
            dw_ref[D_CONV - 1 - s:D_CONV - s, :] = jnp.sum(dpre * shifted[s], axis=0, keepdims=True)
        du_ref[...] = jnp.where(mask, du, 0.0).astype(du_ref.dtype)

    return pl.pallas_call(
        body, name=name, grid=(conv_dim // tc,),
        in_specs=[pl.BlockSpec((rows, tc), lambda j: (0, off + j)), pl.BlockSpec((D_CONV, tc), lambda j: (0, j)),
                  pl.BlockSpec((1, tc), lambda j: (0, j)), pl.BlockSpec((rows, tc), lambda j: (0, j))],
        out_specs=[pl.BlockSpec((rows, tc), lambda j: (0, j)), pl.BlockSpec((D_CONV, tc), lambda j: (0, j)),
                   pl.BlockSpec((1, tc), lambda j: (0, j))],
        out_shape=[jax.ShapeDtypeStruct((rows, conv_dim), BF16), jax.ShapeDtypeStruct((D_CONV, conv_dim), F32),
                   jax.ShapeDtypeStruct((1, conv_dim), F32)],
        compiler_params=_params(("parallel",)),
    )(zx, conv_w, conv_b.reshape(1, conv_dim), dxbc)


def dt_fwd(name, zx, bias_pad, zx_cols):
    rows = zx.shape[0]
    tr = ROW_TILE
    off = zx_cols // LANES

    def body(r_ref, b_ref, o_ref):
        v = r_ref[...] + b_ref[...]
        sp = jnp.maximum(v, 0.0) + jnp.log1p(jnp.exp(-jnp.abs(v)))
        o_ref[...] = jnp.where(_row_mask(v.shape, pl.program_id(0) * tr), sp, 0.0)

    return pl.pallas_call(
        body, name=name, grid=(rows // tr,),
        in_specs=[pl.BlockSpec((tr, LANES), lambda i: (i, off)), pl.BlockSpec((1, LANES), lambda i: (0, 0))],
        out_specs=pl.BlockSpec((tr, LANES), lambda i: (i, 0)),
        out_shape=jax.ShapeDtypeStruct((rows, LANES), F32), compiler_params=_params(("parallel",)),
    )(zx, bias_pad)


def dt_bwd(name, zx, bias_pad, ddt, zx_cols):
    rows = zx.shape[0]
    tr = ROW_TILE
    off = zx_cols // LANES

    def body(r_ref, b_ref, d_ref, o_ref, db_ref):
        v = r_ref[...] + b_ref[...]
        g = jnp.where(_row_mask(v.shape, pl.program_id(0) * tr), d_ref[...] * _sigmoid(v), 0.0)
        o_ref[...] = g.astype(o_ref.dtype)

        @pl.when(pl.program_id(0) == 0)
        def _():
            db_ref[...] = jnp.zeros_like(db_ref)

        db_ref[...] += jnp.sum(g, axis=0, keepdims=True)

    return pl.pallas_call(
        body, name=name, grid=(rows // tr,),
        in_specs=[pl.BlockSpec((tr, LANES), lambda i: (i, off)), pl.BlockSpec((1, LANES), lambda i: (0, 0)),
                  pl.BlockSpec((tr, LANES), lambda i: (i, 0))],
        out_specs=[pl.BlockSpec((tr, LANES), lambda i: (i, 0)), pl.BlockSpec((1, LANES), lambda i: (0, 0))],
        out_shape=[jax.ShapeDtypeStruct((rows, LANES), BF16), jax.ShapeDtypeStruct((1, LANES), F32)],
        compiler_params=_params(("arbitrary",)),
    )(zx, bias_pad, ddt)


def _split3(x):
    h1 = x.astype(BF16)
    r1 = x - h1.astype(F32)
    h2 = r1.astype(BF16)
    h3 = (r1 - h2.astype(F32)).astype(BF16)
    return h1, h2, h3


def _exact_left(ones_b, x):
    h1, h2, h3 = _split3(x)
    return _dot(ones_b, h1, 1, 0) + _dot(ones_b, h2, 1, 0) + _dot(ones_b, h3, 1, 0)


def _exact_right_t(x, ones_b):
    h1, h2, h3 = _split3(x)
    return _dot(h1, ones_b, 1, 1) + _dot(h2, ones_b, 1, 1) + _dot(h3, ones_b, 1, 1)


class _ScanCommon:
    def __init__(self, b_ref, c_ref, dtc_ref, dtr_ref, arow_ref, acol_ref, drow_ref):
        q = CHUNK
        self.bb = b_ref[...].astype(BF16)
        self.cb = c_ref[...].astype(BF16)
        ri = lax.broadcasted_iota(jnp.int32, (q, q), 0)
        cj = lax.broadcasted_iota(jnp.int32, (q, q), 1)
        self.lower = ri >= cj
        self.upper = cj >= ri
        self.dtc = dtc_ref[...]
        self.dtr = dtr_ref[...]
        self.arow = arow_ref[...]
        self.acol = acol_ref[...]
        self.drow = drow_ref[...]
        da_col = self.dtc * self.arow
        self.a_col = _exact_left(self.lower.astype(BF16), da_col)
        self.a_row = _exact_right_t(self.dtr * self.acol, self.lower.astype(BF16))
        self.a_last = jnp.sum(da_col, axis=0, keepdims=True)
        self.lane_q = lax.broadcasted_iota(jnp.int32, (q, LANES), 1)
        self.lane_1 = lax.broadcasted_iota(jnp.int32, (1, LANES), 1)
        self.sub_8 = lax.broadcasted_iota(jnp.int32, (SUBLANES, q), 0)
        self.first_half = self.lane_q < HEAD_DIM
        self.first_rows = lax.broadcasted_iota(jnp.int32, (LANES, 1), 0) < HEAD_DIM

    def col(self, v, r):
        return jnp.sum(jnp.where(self.lane_q == r, v, 0.0), axis=1, keepdims=True)

    def row(self, v, r):
        return jnp.sum(jnp.where(self.sub_8 == r, v, 0.0), axis=0, keepdims=True)

    def scalar(self, v, r):
        return jnp.sum(jnp.where(self.lane_1 == r, v, 0.0), axis=1, keepdims=True)

    def pair(self, v0, v1):
        return jnp.where(self.first_half, v0, v1)

    def half_rowsum(self, t, h):
        keep = self.first_half if h == 0 else jnp.logical_not(self.first_half)
        return jnp.sum(jnp.where(keep, t, 0.0), axis=1, keepdims=True)


def ssd_scan_fwd(name, xbc, dt_col, dt_row, a_row, a_col, d_row, d_inner):
    rows = xbc.shape[0]
    q, n, g_cnt = CHUNK, D_STATE, SSD_GROUPS
    nc = rows // q
    rp = d_inner // g_cnt
    n_pairs = rp // LANES
    b_off = d_inner // n

    def body(x_ref, b_ref, c_ref, dtc_ref, dtr_ref, arow_ref, acol_ref, drow_ref, y_ref, sprev_ref, s_ref):
        @pl.when(pl.program_id(1) == 0)
        def _():
            s_ref[...] = jnp.zeros_like(s_ref)

        sprev_ref[...] = s_ref[...]
        k = _ScanCommon(b_ref, c_ref, dtc_ref, dtr_ref, arow_ref, acol_ref, drow_ref)
        cb_mat = _dot(k.cb, k.bb, 1, 1)
        for pr in range(n_pairs):
            sl = slice(pr * LANES, (pr + 1) * LANES)
            heads = (2 * pr, 2 * pr + 1)
            xp = x_ref[:, sl]
            ac = [k.col(k.a_col, r) for r in heads]
            ar = [k.row(k.a_row, r) for r in heads]
            al = [k.scalar(k.a_last, r) for r in heads]
            xd = xp * k.pair(k.col(k.dtc, heads[0]), k.col(k.dtc, heads[1]))
            xdb = xd.astype(BF16)
            ys = []
            for h in range(2):
                lm = jnp.exp(jnp.where(k.lower, ac[h] - ar[h], -jnp.inf))
                ys.append(_dot((cb_mat * lm).astype(BF16), xdb, 1, 0))
            y = jnp.where(k.first_half, ys[0], ys[1])
            sp = s_ref[sl, :]
            y = y + k.pair(jnp.exp(ac[0]), jnp.exp(ac[1])) * _dot(k.cb, sp.astype(BF16), 1, 1)
            y = y + k.pair(k.scalar(k.drow, heads[0]), k.scalar(k.drow, heads[1])) * xp
            y_ref[:, sl] = y
            wb = (xd * k.pair(jnp.exp(al[0] - ac[0]), jnp.exp(al[1] - ac[1]))).astype(BF16)
            decay = jnp.where(k.first_rows, jnp.exp(al[0]), jnp.exp(al[1]))
            s_ref[sl, :] = decay * sp + _dot(wb, k.bb, 0, 0)

    return pl.pallas_call(
        body, name=name, grid=(g_cnt, nc),
        in_specs=[
            pl.BlockSpec((q, rp), lambda g, c: (c, g)),
            pl.BlockSpec((q, n), lambda g, c: (c, b_off + g)),
            pl.BlockSpec((q, n), lambda g, c: (c, b_off + g_cnt + g)),
            pl.BlockSpec((None, q, LANES), lambda g, c: (g, c, 0)),
            pl.BlockSpec((None, SUBLANES, q), lambda g, c: (g, 0, c)),
            pl.BlockSpec((None, 1, LANES), lambda g, c: (g, 0, 0)),
            pl.BlockSpec((None, SUBLANES, 1), lambda g, c: (g, 0, 0)),
            pl.BlockSpec((None, 1, LANES), lambda g, c: (g, 0, 0)),
        ],
        out_specs=[pl.BlockSpec((q, rp), lambda g, c: (c, g)),
                   pl.BlockSpec((None, None, rp, n), lambda g, c: (c, g, 0, 0))],
        out_shape=[jax.ShapeDtypeStruct((rows, d_inner), F32), jax.ShapeDtypeStruct((nc, g_cnt, rp, n), F32)],
        scratch_shapes=[pltpu.VMEM((rp, n), F32)],
        compiler_params=_params(("parallel", "arbitrary")),
    )(xbc, xbc, xbc, dt_col, dt_row, a_row, a_col, d_row)


def ssd_scan_bwd(name, xbc, dt_col, dt_row, a_row, a_col, d_row, sprev, dy, d_inner):
    rows = xbc.shape[0]
    q, n, g_cnt = CHUNK, D_STATE, SSD_GROUPS
    nc = rows // q
    rp = d_inner // g_cnt
    n_pairs = rp // LANES
    b_off = d_inner // n

    def body(x_ref, b_ref, c_ref, dtc_ref, dtr_ref, arow_ref, acol_ref, drow_ref, sprev_ref, dy_ref,
             dx_ref, db_ref, dc_ref, ddtc_ref, ddtr_ref, dar_ref, dac_ref, dd_ref, ds_ref):
        @pl.when(pl.program_id(1) == 0)
        def _():
            ds_ref[...] = jnp.zeros_like(ds_ref)
            dar_ref[...] = jnp.zeros_like(dar_ref)
            dac_ref[...] = jnp.zeros_like(dac_ref)
            dd_ref[...] = jnp.zeros_like(dd_ref)

        k = _ScanCommon(b_ref, c_ref, dtc_ref, dtr_ref, arow_ref, acol_ref, drow_ref)
        cb_mat = _dot(k.cb, k.bb, 1, 1)
        cbt_mat = _dot(k.bb, k.cb, 1, 1)
        d_cb = jnp.zeros((q, q), F32)
        d_b = jnp.zeros((q, n), F32)
        d_c = jnp.zeros((q, n), F32)
        da_col = jnp.zeros((q, LANES), F32)
        da_row = jnp.zeros((SUBLANES, q), F32)
        ddt_x = jnp.zeros((q, LANES), F32)
        d_alast = jnp.zeros((1, LANES), F32)
        d_dskip = jnp.zeros((1, LANES), F32)
        for pr in range(n_pairs):
            sl = slice(pr * LANES, (pr + 1) * LANES)
            heads = (2 * pr, 2 * pr + 1)
            xp = x_ref[:, sl]
            dyp = dy_ref[:, sl]
            dyb = dyp.astype(BF16)
            ac = [k.col(k.a_col, r) for r in heads]
            ar = [k.row(k.a_row, r) for r in heads]
            al = [k.scalar(k.a_last, r) for r in heads]
            dt_p = k.pair(k.col(k.dtc, heads[0]), k.col(k.dtc, heads[1]))
            xd = xp * dt_p
            xdb = xd.astype(BF16)
            sp = sprev_ref[sl, :]
            spb = sp.astype(BF16)
            dsp = ds_ref[sl, :]
            dspb = dsp.astype(BF16)
            dskip_p = k.pair(k.scalar(k.drow, heads[0]), k.scalar(k.drow, heads[1]))
            dxp = dskip_p * dyp
            dd_lane = jnp.sum(dyp * xp, axis=0, keepdims=True)
            e_p = k.pair(jnp.exp(ac[0]), jnp.exp(ac[1]))
            t_off = dyp * (e_p * _dot(k.cb, spb, 1, 1))
            dzb = (e_p * dyp).astype(BF16)
            d_c = d_c + _dot(dzb, spb, 1, 0)
            ds_in = _dot(dzb, k.cb, 0, 0)
            decay = jnp.where(k.first_rows, jnp.exp(al[0]), jnp.exp(al[1]))
            ds_in = ds_in + decay * dsp
            t_state = jnp.sum(dsp * sp, axis=1, keepdims=True) * decay
            dec_p = k.pair(jnp.exp(al[0] - ac[0]), jnp.exp(al[1] - ac[1]))
            dw = _dot(k.bb, dspb, 1, 1)
            d_b = d_b + _dot((xd * dec_p).astype(BF16), dspb, 1, 0)
            dxd = dw * dec_p
            t_dec = dw * xd * dec_p
            dxd_h = []
            for h in range(2):
                r = heads[h]
                keep = k.first_half if h == 0 else jnp.logical_not(k.first_half)
                lm = jnp.exp(jnp.where(k.lower, ac[h] - ar[h], -jnp.inf))
                m_mat = cb_mat * lm
                dm = _dot(jnp.where(keep, dyp, 0.0).astype(BF16), xdb, 1, 1)
                dseg = dm * m_mat
                d_cb = d_cb + dm * lm
                lmt = jnp.exp(jnp.where(k.upper, ar[h] - ac[h], -jnp.inf))
                dxd_h.append(_dot((cbt_mat * lmt).astype(BF16), dyb, 1, 0))
                tdec_h = k.half_rowsum(t_dec, h)
                da_h = k.half_rowsum(t_off, h) - tdec_h + jnp.sum(dseg, axis=1, keepdims=True)
                da_col = da_col + jnp.where(k.lane_q == r, da_h, 0.0)
                da_row = da_row - jnp.where(k.sub_8 == r, jnp.sum(dseg, axis=0, keepdims=True), 0.0)
                keep_rows = k.first_rows if h == 0 else jnp.logical_not(k.first_rows)
                dal_h = jnp.sum(tdec_h, axis=0, keepdims=True) + jnp.sum(jnp.where(keep_rows, t_state, 0.0), axis=0, keepdims=True)
                d_alast = d_alast + jnp.where(k.lane_1 == r, dal_h, 0.0)
                keep_1 = k.lane_1 < HEAD_DIM if h == 0 else k.lane_1 >= HEAD_DIM
                dd_h = jnp.sum(jnp.where(keep_1, dd_lane, 0.0), axis=1, keepdims=True)
                d_dskip = d_dskip + jnp.where(k.lane_1 == r, dd_h, 0.0)
            dxd = dxd + jnp.where(k.first_half, dxd_h[0], dxd_h[1])
            dx_ref[:, sl] = dxp + dt_p * dxd
            t_dt = dxd * xp
            for h in range(2):
                ddt_x = ddt_x + jnp.where(k.lane_q == heads[h], k.half_rowsum(t_dt, h), 0.0)
            ds_ref[sl, :] = ds_in
        d_cb_b = d_cb.astype(BF16)
        db_ref[...] = d_b + _dot(d_cb_b, k.cb, 0, 0)
        dc_ref[...] = d_c + _dot(d_cb_b, k.bb, 1, 0)
        rc_col = _exact_left(k.upper.astype(BF16), da_col) + d_alast
        rc_row = _exact_right_t(da_row, k.upper.astype(BF16))
        ddtc_ref[...] = ddt_x + k.arow * rc_col
        ddtr_ref[...] = k.acol * rc_row
        dar_ref[...] += jnp.sum(rc_col * k.dtc, axis=0, keepdims=True)
        dac_ref[...] += jnp.sum(rc_row * k.dtr, axis=1, keepdims=True)
        dd_ref[...] += d_dskip

    rc = lambda c: nc - 1 - c
    return pl.pallas_call(
        body, name=name, grid=(g_cnt, nc),
        in_specs=[
            pl.BlockSpec((q, rp), lambda g, c: (rc(c), g)),
            pl.BlockSpec((q, n), lambda g, c: (rc(c), b_off + g)),
            pl.BlockSpec((q, n), lambda g, c: (rc(c), b_off + g_cnt + g)),
            pl.BlockSpec((None, q, LANES), lambda g, c: (g, rc(c), 0)),
            pl.BlockSpec((None, SUBLANES, q), lambda g, c: (g, 0, rc(c))),
            pl.BlockSpec((None, 1, LANES), lambda g, c: (g, 0, 0)),
            pl.BlockSpec((None, SUBLANES, 1), lambda g, c: (g, 0, 0)),
            pl.BlockSpec((None, 1, LANES), lambda g, c: (g, 0, 0)),
            pl.BlockSpec((None, None, rp, n), lambda g, c: (rc(c), g, 0, 0)),
            pl.BlockSpec((q, rp), lambda g, c: (rc(c), g)),
        ],
        out_specs=[
            pl.BlockSpec((q, rp), lambda g, c: (rc(c), g)),
            pl.BlockSpec((q, n), lambda g, c: (rc(c), g)),
            pl.BlockSpec((q, n), lambda g, c: (rc(c), g)),
            pl.BlockSpec((None, q, LANES), lambda g, c: (g, rc(c), 0)),
            pl.BlockSpec((None, SUBLANES, q), lambda g, c: (g, 0, rc(c))),
            pl.BlockSpec((None, 1, LANES), lambda g, c: (g, 0, 0)),
            pl.BlockSpec((None, SUBLANES, 1), lambda g, c: (g, 0, 0)),
            pl.BlockSpec((None, 1, LANES), lambda g, c: (g, 0, 0)),
        ],
        out_shape=[
            jax.ShapeDtypeStruct((rows, d_inner), F32),
            jax.ShapeDtypeStruct((rows, g_cnt * n), F32),
            jax.ShapeDtypeStruct((rows, g_cnt * n), F32),
            jax.ShapeDtypeStruct((g_cnt, rows, LANES), F32),
            jax.ShapeDtypeStruct((g_cnt, SUBLANES, rows), F32),
            jax.ShapeDtypeStruct((g_cnt, 1, LANES), F32),
            jax.ShapeDtypeStruct((g_cnt, SUBLANES, 1), F32),
            jax.ShapeDtypeStruct((g_cnt, 1, LANES), F32),
        ],
        scratch_shapes=[pltpu.VMEM((rp, n), F32)],
        compiler_params=_params(("parallel", "arbitrary")),
    )(xbc, xbc, xbc, dt_col, dt_row, a_row, a_col, d_row, sprev, dy)


def gatenorm_fwd(name, y, zx, w, d_inner):
    rows = y.shape[0]
    tr, gw = ROW_TILE, d_inner // SSD_GROUPS

    def body(y_ref, z_ref, w_ref, o_ref, ot_ref):
        z = z_ref[...]
        v = y_ref[...] * (z * _sigmoid(z))
        out = v * lax.rsqrt(jnp.mean(v * v, axis=-1, keepdims=True) + EPS) * w_ref[...]
        o_ref[...] = out.astype(o_ref.dtype)
        ot_ref[...] = out.T.astype(ot_ref.dtype)

    blk = pl.BlockSpec((tr, gw), lambda i, j: (i, j))
    return pl.pallas_call(
        body, name=name, grid=(rows // tr, SSD_GROUPS),
        in_specs=[blk, blk, pl.BlockSpec((1, gw), lambda i, j: (0, j))],
        out_specs=[blk, pl.BlockSpec((gw, tr), lambda i, j: (j, i))],
        out_shape=[jax.ShapeDtypeStruct((rows, d_inner), BF16), jax.ShapeDtypeStruct((d_inner, rows), BF16)],
        compiler_params=_params(("parallel", "parallel")),
    )(y, zx, w.reshape(1, d_inner))


def gatenorm_bwd(name, y, zx, w, dyn, d_inner, after):
    rows = y.shape[0]
    tr, gw = ROW_TILE, d_inner // SSD_GROUPS

    def body(y_ref, z_ref, w_ref, dn_ref, after_ref, dy_ref, dz_ref, dw_ref):
        z = z_ref[...]
        yv = y_ref[...]
        s = _sigmoid(z)
        gate = z * s
        v = yv * gate
        r = lax.rsqrt(jnp.mean(v * v, axis=-1, keepdims=True) + EPS)
        vh = v * r
        dn = dn_ref[...]
        dvh = dn * w_ref[...]
        dv = r * (dvh - vh * jnp.mean(dvh * vh, axis=-1, keepdims=True))
        dy_ref[...] = dv * gate
        dz_ref[...] = (dv * yv * (s * (1.0 + z * (1.0 - s)))).astype(dz_ref.dtype)

        @pl.when(pl.program_id(1) == 0)
        def _():
            dw_ref[...] = jnp.zeros_like(dw_ref)

        dw_ref[...] += jnp.sum(dn * vh, axis=0, keepdims=True)

    blk = pl.BlockSpec((tr, gw), lambda j, i: (i, j))
    wblk = pl.BlockSpec((1, gw), lambda j, i: (0, j))
    dy, dz, dw = pl.pallas_call(
        body, name=name, grid=(SSD_GROUPS, rows // tr),
        in_specs=[blk, blk, wblk, blk, pl.BlockSpec((SUBLANES, LANES), lambda j, i: (0, 0))], out_specs=[blk, blk, wblk],
        out_shape=[jax.ShapeDtypeStruct((rows, d_inner), F32), jax.ShapeDtypeStruct((rows, d_inner), BF16),
                   jax.ShapeDtypeStruct((1, d_inner), F32)],
        compiler_params=_params(("parallel", "arbitrary")),
    )(y, zx, w.reshape(1, d_inner), dyn, after)
    return dy, dz, dw.reshape(d_inner)


def _pool_count(rows, g):
    t1 = lax.broadcasted_iota(jnp.int32, (rows, 1), 0) - (PAD_FRONT - 1)
    win = jnp.left_shift(jnp.int32(POOL_WINDOWS[0]), g)
    return jnp.clip(t1, 1, win).astype(F32)


def _pool_select(levels, g):
    out = levels[-1]
    for i in range(len(levels) - 2, -1, -1):
        out = jnp.where(g == i, levels[i], out)
    return out


def pool_sub(name, u, transpose):
    rows, d = u.shape
    gd = d // len(POOL_WINDOWS)
    assert all(w == POOL_WINDOWS[0] << i for i, w in enumerate(POOL_WINDOWS))

    def body(u_ref, o_ref):
        g = pl.program_id(0)
        v = u_ref[...].astype(F32)
        cnt = _pool_count(rows, g)
        mask = _row_mask(v.shape, 0)
        s = v / cnt if transpose else v
        levels = []
        for i in range(len(POOL_WINDOWS)):
            step = 1 << i
            s = s + pltpu.roll(s, (rows - step) if transpose else step, axis=0)
            levels.append(s)
        sel = _pool_select(levels, g)
        out = (sel - v) if transpose else (sel / cnt - v)
        o_ref[...] = jnp.where(mask, out, 0.0).astype(o_ref.dtype)

    tc = POOL_TILE
    per = gd // tc
    blk = pl.BlockSpec((rows, tc), lambda g, j: (0, g * per + j))
    return pl.pallas_call(
        body, name=name, grid=(len(POOL_WINDOWS), per), in_specs=[blk], out_specs=blk,
        out_shape=jax.ShapeDtypeStruct((rows, d), F32 if transpose else BF16),
        compiler_params=_params(("parallel", "parallel")),
    )(u)


def pool_proj_fwd(name, mixed, w, b, scale):
    rows, d = mixed.shape
    ng = len(POOL_WINDOWS)
    gd = d // ng
    tr = ROW_TILE

    def body(m_ref, w_ref, b_ref, s_ref, pre_ref, mix_ref):
        pre = _dot(m_ref[...], w_ref[...], 1, 0) + b_ref[...]
        pre_ref[...] = pre
        mix_ref[...] = jnp.where(_row_mask(pre.shape, pl.program_id(1) * tr), pre * s_ref[...], 0.0)

    blk = pl.BlockSpec((tr, gd), lambda g, i: (i, g))
    vec = pl.BlockSpec((1, gd), lambda g, i: (0, g))
    return pl.pallas_call(
        body, name=name, grid=(ng, rows // tr),
        in_specs=[blk, pl.BlockSpec((None, gd, gd), lambda g, i: (g, 0, 0)), vec, vec], out_specs=[blk, blk],
        out_shape=[jax.ShapeDtypeStruct((rows, d), F32), jax.ShapeDtypeStruct((rows, d), F32)],
        compiler_params=_params(("parallel", "parallel")),
    )(mixed, w, b.reshape(1, d), scale.reshape(1, d))


def pool_proj_bwd(name, dmix, pre, mixed, w, scale):
    rows, d = dmix.shape
    ng = len(POOL_WINDOWS)
    gd = d // ng
    tr = ROW_TILE

    def body(dm_ref, pre_ref, mx_ref, w_ref, s_ref, dmx_ref, dw_ref, db_ref, ds_ref):
        @pl.when(pl.program_id(1) == 0)
        def _():
            dw_ref[...] = jnp.zeros_like(dw_ref)
            db_ref[...] = jnp.zeros_like(db_ref)
            ds_ref[...] = jnp.zeros_like(ds_ref)

        dmv = jnp.where(_row_mask(dm_ref.shape, pl.program_id(1) * tr), dm_ref[...], 0.0)
        dpre = dmv * s_ref[...]
        dpre_b = dpre.astype(BF16)
        ds_ref[...] += jnp.sum(dmv * pre_ref[...], axis=0, keepdims=True)
        db_ref[...] += jnp.sum(dpre, axis=0, keepdims=True)
        dmx_ref[...] = _dot(dpre_b, w_ref[...], 1, 1)
        dw_ref[...] += _dot(mx_ref[...], dpre_b, 0, 0)

    blk = pl.BlockSpec((tr, gd), lambda g, i: (i, g))
    vec = pl.BlockSpec((1, gd), lambda g, i: (0, g))
    wblk = pl.BlockSpec((None, gd, gd), lambda g, i: (g, 0, 0))
    dmixed, dw, db, ds = pl.pallas_call(
        body, name=name, grid=(ng, rows // tr),
        in_specs=[blk, blk, blk, wblk, vec], out_specs=[blk, wblk, vec, vec],
        out_shape=[jax.ShapeDtypeStruct((rows, d), F32), jax.ShapeDtypeStruct((ng, gd, gd), F32),
                   jax.ShapeDtypeStruct((1, d), F32), jax.ShapeDtypeStruct((1, d), F32)],
        compiler_params=_params(("parallel", "arbitrary")),
    )(dmix, pre, mixed, w, scale.reshape(1, d))
    return dmixed, dw, db.reshape(d), ds.reshape(d)


def _my_place():
    return lax.axis_index("x"), lax.axis_index("y"), lax.axis_index("c")


def _linear(place):
    return 4 * place[0] + 2 * place[1] + place[2]


def all_gather(name, shards):
    n_ops = len(shards)

    def body(*refs):
        ins, outs = refs[:n_ops], refs[n_ops:2 * n_ops]
        send_sems, recv_sems, local_sems = refs[2 * n_ops:]
        x, y, c = _my_place()
        me, sibling = (x, y, c), (x, y, 1 - c)
        chips = [(1 - x, y), (x, 1 - y), (1 - x, 1 - y)]

        def copy(t, k, block, to, src=None):
            dst = outs[t].at[_linear(block)]
            return pltpu.make_async_remote_copy(
                src_ref=dst if src is None else src, dst_ref=dst, send_sem=send_sems.at[t, k], recv_sem=recv_sems.at[t, k],
                device_id=to, device_id_type=MESH)

        mine = [pltpu.make_async_copy(ins[t], outs[t].at[_linear(me)], local_sems.at[t]) for t in range(n_ops)]
        for cp in mine:
            cp.start()
        first = []
        for t in range(n_ops):
            first.append(copy(t, 0, me, sibling, src=ins[t]))
            first += [copy(t, 1 + j, me, (*chip, c), src=ins[t]) for j, chip in enumerate(chips)]
        for cp in first:
            cp.start()
        passed = []
        for j, chip in enumerate(chips):
            for t in range(n_ops):
                copy(t, 1 + j, (*chip, c), me).wait_recv()
                fwd = copy(t, 4 + j, (*chip, c), sibling)
                fwd.start()
                passed.append(fwd)
        for t in range(n_ops):
            copy(t, 0, sibling, me).wait_recv()
            for j, chip in enumerate(chips):
                copy(t, 4 + j, (*chip, 1 - c), me).wait_recv()
        for cp in first + passed:
            cp.wait_send()
        for cp in mine:
            cp.wait()

    any_spec = pl.BlockSpec(memory_space=pl.ANY)
    outs = pl.pallas_call(
        body, name=name,
        in_specs=[any_spec] * n_ops, out_specs=[any_spec] * n_ops,
        out_shape=[jax.ShapeDtypeStruct((N_DEV, *s.shape), s.dtype) for s in shards],
        scratch_shapes=[pltpu.SemaphoreType.DMA((n_ops, 7)), pltpu.SemaphoreType.DMA((n_ops, 7)),
                        pltpu.SemaphoreType.DMA((n_ops,))],
    )(*shards)
    return list(outs)


_HBM = pl.BlockSpec(memory_space=pltpu.HBM)
_SEM = pl.BlockSpec(memory_space=pltpu.SEMAPHORE)
_EFFECT = pltpu.SideEffectType.DATAFLOW_SIDE_EFFECTING


def _hbm(a):
    return pltpu.with_memory_space_constraint(a, pltpu.HBM)


def _peers():
    x, y, c = _my_place()
    return [(x ^ (j >> 2), y ^ ((j >> 1) & 1), c ^ (j & 1)) for j in range(1, N_DEV)]


def exchange_start(name, groups, after):
    flat = [e for g in groups for e in g]
    n = len(flat)

    def body(*refs):
        srcs, lands = refs[:n], refs[n:2 * n]
        outs = refs[2 * n + 1:]
        sends, recvs, token = outs[:n], outs[n:2 * n], outs[4 * n]
        me = _linear(_my_place())
        for t, (_, _, src_view, land_view) in enumerate(flat):
            for peer in _peers():
                pltpu.make_async_remote_copy(
                    src_ref=src_view(srcs[t], _linear(peer)), dst_ref=land_view(lands[t], me),
                    send_sem=sends[t], recv_sem=recvs[t], device_id=peer, device_id_type=MESH).start()
        token[...] = jnp.zeros_like(token)

    sem_shapes = [pltpu.SemaphoreType.DMA(())] * (2 * n)
    thru = [pltpu.HBM(e[0].shape, e[0].dtype) for e in flat] + [pltpu.HBM(e[1].shape, e[1].dtype) for e in flat]
    outs = pl.pallas_call(
        body, name=name,
        out_shape=(*sem_shapes, *thru, jax.ShapeDtypeStruct((SUBLANES, LANES), F32)),
        in_specs=[*[_HBM] * (2 * n), pl.BlockSpec(memory_space=pl.ANY)],
        out_specs=(*[_SEM] * (2 * n), *[_HBM] * (2 * n), pl.BlockSpec(memory_space=pltpu.VMEM)),
        input_output_aliases={t: 2 * n + t for t in range(2 * n)},
        compiler_params=pltpu.CompilerParams(has_side_effects=_EFFECT),
    )(*[_hbm(e[0]) for e in flat], *[_hbm(e[1]) for e in flat], after)
    records, t = [], 0
    for g in groups:
        k = len(g)
        records.append((list(outs[t:t + k]), list(outs[n + t:n + t + k]), list(outs[2 * n + t:2 * n + t + k]),
                        list(outs[3 * n + t:3 * n + t + k])))
        t += k
    return records, outs[-1]


def exchange_wait(name, records, lands, land_of, seven_of, after):
    srcs = [s for r in records for s in r[2]]
    sends = [s for r in records for s in r[0]]
    recvs = [s for r in records for s in r[1]]
    where = [(ri, k) for ri, r in enumerate(records) for k in range(len(r[2]))]
    ns, nl = len(srcs), len(lands)

    def body(*refs):
        src_refs, land_refs = refs[:ns], refs[ns:ns + nl]
        send_refs, recv_refs = refs[ns + nl:2 * ns + nl], refs[2 * ns + nl:3 * ns + nl]
        for t, (ri, k) in enumerate(where):
            seven_ref = seven_of[ri][k](src_refs[t], land_refs[land_of[ri][k]])
            cp = pltpu.make_async_remote_copy(src_ref=seven_ref, dst_ref=seven_ref, send_sem=send_refs[t], recv_sem=recv_refs[t],
                                              device_id=_my_place(), device_id_type=MESH)
            cp.wait_send()
            cp.wait_recv()

    outs = pl.pallas_call(
        body, name=name,
        out_shape=tuple(pltpu.HBM(a.shape, a.dtype) for a in (*srcs, *lands)),
        in_specs=[*[_HBM] * (ns + nl), *[_SEM] * (2 * ns), pl.BlockSpec(memory_space=pl.ANY)], out_specs=tuple([_HBM] * (ns + nl)),
        input_output_aliases={t: t for t in range(ns + nl)},
        compiler_params=pltpu.CompilerParams(has_side_effects=_EFFECT),
    )(*srcs, *lands, *sends, *recvs, after)
    return list(outs[:ns]), list(outs[ns:])


def _seven_slots_of_land(src_ref, land_ref):
    return land_ref.at[pl.ds(0, N_DEV - 1)]


def _seven_blocks_of_src(src_ref, land_ref):
    return src_ref.at[pl.ds(0, N_DEV - 1)]


def _whole(ref, dev):
    return ref


def _slot(ref, dev):
    return ref.at[dev]


def _slot_of_layer(layer):
    return lambda ref, dev: ref.at[dev, layer]


def _as_rows(a, lead=0):
    return a.reshape(a.shape[:lead] + (-1, a.shape[-1]))


def sum_adamw(name, parts, w, m, v, half=None):
    shape = w.shape
    c1 = 1.0 - ADAM_B1 ** ADAM_STEP
    c2 = 1.0 - ADAM_B2 ** ADAM_STEP
    if half is not None:
        layers, r, cols = shape
        tr = _tile(r, ROW_TILE, 16)
        grid = (layers, r // tr)
        p_spec = pl.BlockSpec((N_DEV, None, None, tr, cols), lambda l, i: (0, l, half, i, 0))
        blk = pl.BlockSpec((None, tr, cols), lambda l, i: (l, i, 0))
        args, out_sds, sem = (parts, w, m, v), jax.ShapeDtypeStruct(shape, F32), ("parallel", "parallel")
    else:
        p2, w2, m2, v2 = _as_rows(parts, 1), _as_rows(w), _as_rows(m), _as_rows(v)
        rows, cols = w2.shape
        tr = _tile(rows, ROW_TILE, 16)
        grid = (rows // tr,)
        p_spec = pl.BlockSpec((N_DEV, tr, cols), lambda i: (0, i, 0))
        blk = pl.BlockSpec((tr, cols), lambda i: (i, 0))
        args, out_sds, sem = (p2, w2, m2, v2), jax.ShapeDtypeStruct((rows, cols), F32), ("parallel",)

    def body(p_ref, w_ref, m_ref, v_ref, g_ref, d_ref, nm_ref, nv_ref):
        g = p_ref[0].astype(F32)
        for k in range(1, N_DEV):
            g = g + p_ref[k].astype(F32)
        wv = w_ref[...]
        nm = ADAM_B1 * m_ref[...] + (1.0 - ADAM_B1) * g
        nv = ADAM_B2 * v_ref[...] + (1.0 - ADAM_B2) * (g * g)
        g_ref[...] = g
        nm_ref[...] = nm
        nv_ref[...] = nv
        d_ref[...] = -ADAM_LR * ((nm / c1) / (jnp.sqrt(nv / c2) + ADAM_EPS) + ADAM_WD * wv)

    outs = pl.pallas_call(
        body, name=name, grid=grid, in_specs=[p_spec, blk, blk, blk], out_specs=[blk] * 4,
        out_shape=[out_sds] * 4, compiler_params=_params(sem),
    )(*args)
    return [o.reshape(shape) for o in outs]


def _unblock_cols(g):
    g = jnp.moveaxis(g, 0, -2)
    return g.reshape(g.shape[:-2] + (g.shape[-2] * g.shape[-1],))


def _block_cols(a):
    r, c = a.shape
    return jnp.moveaxis(a.reshape(r, N_DEV, c // N_DEV), 1, 0)


def _my_cols(a, n):
    me = _linear(_my_place())
    return lax.dynamic_slice_in_dim(a, me * n, n, axis=a.ndim - 1)


def _pack(arrays):
    flat = jnp.concatenate([a.reshape(-1).astype(F32) for a in arrays])
    pad = (-flat.shape[0]) % (ROW_TILE * LANES)
    return jnp.pad(flat, (0, pad)).reshape(-1, LANES)


def _unpack(packed, like):
    flat = packed.reshape(-1)
    out, pos = [], 0
    for a in like:
        out.append(flat[pos:pos + a.size].reshape(a.shape))
        pos += a.size
    return out


def kernel(x, meta_tokens, norm_w, ssd_w_in, ssd_conv_w, ssd_conv_b, ssd_dt_bias, ssd_a_log, ssd_d, ssd_norm_w, ssd_w_out, pool_w, pool_b, pool_scale, ffn_w_gate, ffn_w_up, ffn_w_down, loss_target, m_meta_tokens, m_norm_w, m_ssd_w_in, m_ssd_conv_w, m_ssd_conv_b, m_ssd_dt_bias, m_ssd_a_log, m_ssd_d, m_ssd_norm_w, m_ssd_w_out, m_pool_w, m_pool_b, m_pool_scale, m_ffn_w_gate, m_ffn_w_up, m_ffn_w_down, v_meta_tokens, v_norm_w, v_ssd_w_in, v_ssd_conv_w, v_ssd_conv_b, v_ssd_dt_bias, v_ssd_a_log, v_ssd_d, v_ssd_norm_w, v_ssd_w_out, v_pool_w, v_pool_b, v_pool_scale, v_ffn_w_gate, v_ffn_w_up, v_ffn_w_down):
    seq, d = x.shape[1], x.shape[2]
    depth = norm_w.shape[0]
    n_ssd = ssd_w_in.shape[0]
    d_inner = ssd_norm_w.shape[1]
    heads = d_inner // HEAD_DIM
    rpg = heads // SSD_GROUPS
    conv_dim = ssd_conv_b.shape[1]
    zx_cols = d_inner + conv_dim
    d_in_proj = zx_cols + heads
    rows = PAD_FRONT + N_META + seq
    assert rows % CHUNK == 0 and rpg % 2 == 0 and heads <= LANES and rpg <= SUBLANES

    hidden = ffn_w_down.shape[1] * N_DEV
    n_pool = pool_w.shape[0]

    small = all_gather("gather_small", [meta_tokens, norm_w, ssd_conv_w, pool_b, pool_scale])
    meta_f, norm_f, convw_f, poolb_f, pools_f = [_unblock_cols(s) for s in small]

    gather_recs = {}

    def start_gather(i, after):
        def entry(shard):
            return (shard.astype(BF16), lax.empty((N_DEV, *shard.shape), BF16), _whole, _slot)
        mixer = [ssd_w_in[i // 2], ssd_w_out[i // 2]] if i % 2 == 0 else [pool_w[i // 2]]
        gate_up = jnp.stack([ffn_w_gate[i], ffn_w_up[i]])
        groups = [*[[entry(w)] for w in mixer], [entry(gate_up), entry(ffn_w_down[i])]]
        gather_recs[i], tok = exchange_start(f"gather_start_{i}", groups, after)
        return tok

    def gathered(name, rec, after):
        n_e = len(rec[2])
        srcs, lands_ = exchange_wait(name, [rec], rec[3], [list(range(n_e))], [[_seven_slots_of_land] * n_e], after)
        me = _linear(_my_place())
        return [lax.dynamic_update_slice_in_dim(land, src[None], me, axis=0) for src, land in zip(srcs, lands_)]

    pending = meta_f
    for i in range(min(GATHER_AHEAD, depth)):
        pending = start_gather(i, pending)

    pad_h = lambda a: jnp.pad(a.astype(F32), ((0, 0), (0, LANES - heads)))
    bias_pad = pad_h(ssd_dt_bias)

    def head_layouts(vec):
        g = vec.reshape(SSD_GROUPS, rpg)
        row = jnp.pad(g, ((0, 0), (0, LANES - rpg)))[:, None, :]
        col = jnp.pad(g, ((0, 0), (0, SUBLANES - rpg)))[:, :, None]
        return row, col

    def dt_layouts(dt):
        g = dt[:, :heads].reshape(rows, SSD_GROUPS, rpg)
        col = jnp.pad(jnp.moveaxis(g, 1, 0), ((0, 0), (0, 0), (0, LANES - rpg)))
        row = jnp.pad(jnp.transpose(g, (1, 2, 0)), ((0, 0), (0, SUBLANES - rpg), (0, 0)))
        return col, row

    h = jnp.concatenate([jnp.zeros((PAD_FRONT, d), F32), meta_f, x[0]], axis=0)
    saved = []
    for i in range(depth):
        j = i // 2
        s = {"h": h}
        w_pre_mix = norm_f[i, 0]
        if i + GATHER_AHEAD < depth:
            pending = start_gather(i + GATHER_AHEAD, pending if i == 0 else h)
        (mixer_w,) = gathered(f"gather_wait_mix_{i}", gather_recs[i][0], h)
        if i % 2 == 0:
            w_in = jnp.pad(_unblock_cols(mixer_w), ((0, 0), (0, LANES - heads)))
            u, u_t = rmsnorm_fwd(f"norm_pre_mix_{i}", h, w_pre_mix, out_dtype=BF16, transposed=True, after=pending)
            zx = matmul(f"ssd_in_{i}", u, w_in, **MM_ROWS_RESIDENT, tn=384 if w_in.shape[1] % 384 == 0 else 512)
            xbc = conv_fwd(f"ssd_conv_{i}", zx, convw_f[j], ssd_conv_b[j], d_inner, conv_dim)
            dt = dt_fwd(f"ssd_dt_{i}", zx, bias_pad[j:j + 1], zx_cols)
            dt_col, dt_row = dt_layouts(dt)
            a_neg = -jnp.exp(ssd_a_log[j].astype(F32))
            a_row, a_col = head_layouts(a_neg)
            d_row, _ = head_layouts(ssd_d[j].astype(F32))
            y, sprev = ssd_scan_fwd(f"ssd_scan_{i}", xbc, dt_col, dt_row, a_row, a_col, d_row, d_inner)
            yn, yn_t = gatenorm_fwd(f"ssd_gate_{i}", y, zx, ssd_norm_w[j], d_inner)
            (w_out,) = gathered(f"gather_wait_out_{i}", gather_recs[i][1], yn)
            w_out = w_out.reshape(d_inner, d)
            mix = matmul(f"ssd_out_{i}", yn, w_out, **MM_DEEP)
            s.update(u_t=u_t, zx=zx, xbc=xbc, dt_col=dt_col, dt_row=dt_row, a_row=a_row, a_col=a_col, d_row=d_row,
                     a_neg=a_neg, y=y, sprev=sprev, yn_t=yn_t, w_in=w_in, w_out=w_out)
        else:
            w_pool = jnp.moveaxis(mixer_w, 0, 1).reshape(len(POOL_WINDOWS), d // len(POOL_WINDOWS), -1)
            u = rmsnorm_fwd(f"norm_pre_mix_{i}", h, w_pre_mix, after=pending)
            mixed = pool_sub(f"pool_sub_{i}", u, False)
            pre, mix = pool_proj_fwd(f"pool_proj_{i}", mixed, w_pool, poolb_f[j], pools_f[j])
            s.update(mixed=mixed, pre=pre, w_pool=w_pool)
        pending = None
        h1 = rmsnorm_fwd(f"norm_post_mix_{i}", mix, norm_f[i, 1], res=h)
        w_gu, w_down = gathered(f"gather_wait_ffn_{i}", gather_recs[i][-1], h1)
        u2, u2_t = rmsnorm_fwd(f"norm_pre_ffn_{i}", h1, norm_f[i, 2], out_dtype=BF16, transposed=True)
        gp = ffn_in_fwd(f"ffn_in_{i}", u2, w_gu)
        act, act_t = swiglu_fwd(f"ffn_act_{i}", gp)
        f = ffn_out_fwd(f"ffn_out_{i}", act, w_down)
        h = rmsnorm_fwd(f"norm_post_ffn_{i}", f, norm_f[i, 3], res=h1)
        s.update(mix=mix, h1=h1, u2_t=u2_t, gp=gp, act_t=act_t, f=f, w_gu=w_gu, w_down=w_down)
        saved.append(s)

    dh, loss_local = loss_head("loss_head", h, loss_target[0])
    loss = lax.psum(loss_local, AXES)

    g_norm = [[None] * 4 for _ in range(depth)]
    g_convw, g_convb, g_dtb, g_alog, g_dskip, g_ssdnorm = ([None] * n_ssd for _ in range(6))
    g_poolb, g_pools = ([None] * n_pool for _ in range(2))
    hid_s = hidden // N_DEV
    lands = {"in": lax.empty((N_DEV, *ssd_w_in.shape), BF16), "out": lax.empty((N_DEV, *ssd_w_out.shape), BF16),
             "pool": lax.empty((N_DEV, *pool_w.shape), BF16), "down": lax.empty((N_DEV, *ffn_w_down.shape), BF16),
             "gate_up": lax.empty((N_DEV, depth, 2, d, hid_s), BF16)}
    scatter_recs, scatter_keys, scatter_views, scatter_layers = [], [], [], []

    def scatter(name, blocks, layer, after):
        view = _slot_of_layer(layer)
        (rec,), tok = exchange_start(name, [[(b, lands[key], _slot, view) for key, b in blocks]], after)
        for (key, _), thru in zip(blocks, rec[3]):
            lands[key] = thru
        scatter_recs.append(rec)
        scatter_keys.append([key for key, _ in blocks])
        scatter_views.append([_seven_blocks_of_src] * len(blocks))
        scatter_layers.append(layer)
        return tok

    tok = None
    for i in reversed(range(depth)):
        j = i // 2
        s = saved[i]
        df, g_norm[i][3] = rmsnorm_bwd(f"norm_post_ffn_bwd_{i}", s["f"], norm_f[i, 3], dh, after=tok)
        df_b = df.astype(BF16)
        dact = ffn_out_bwd_x(f"ffn_out_bwd_x_{i}", df_b, s["w_down"])
        g_down = ffn_out_bwd_w(f"ffn_out_bwd_w_{i}", s["act_t"], df_b)
        dgp = swiglu_bwd(f"ffn_act_bwd_{i}", s["gp"], dact)
        g_gu = ffn_in_bwd_w(f"ffn_in_bwd_w_{i}", s["u2_t"], dgp)
        tok = scatter(f"scatter_start_ffn_{i}", [("gate_up", g_gu), ("down", g_down)], i, g_gu)
        du2 = ffn_in_bwd_x(f"ffn_in_bwd_x_{i}", dgp, s["w_gu"])
        dh1, g_norm[i][2] = rmsnorm_bwd(f"norm_pre_ffn_bwd_{i}", s["h1"], norm_f[i, 2], du2, add=dh, after=tok)
        dmix, g_norm[i][1] = rmsnorm_bwd(f"norm_post_mix_bwd_{i}", s["mix"], norm_f[i, 1], dh1)
        if i % 2 == 0:
            dmix_b = dmix.astype(BF16)
            dyn = matmul(f"ssd_out_bwd_x_{i}", dmix_b, s["w_out"], tb=True, **MM_ROWS_RESIDENT)
            g_out = matmul(f"ssd_out_bwd_w_{i}", s["yn_t"], dmix_b, out_dtype=BF16, **MM_COLS_RESIDENT)
            tok = scatter(f"scatter_start_out_{i}", [("out", g_out.reshape(N_DEV, d_inner // N_DEV, d))], j, g_out)
            dy, dz, g_ssdnorm[j] = gatenorm_bwd(f"ssd_gate_bwd_{i}", s["y"], s["zx"], ssd_norm_w[j], dyn, d_inner, after=tok)
            dx, db, dc, ddt_col, ddt_row, dar, dac, ddsk = ssd_scan_bwd(
                f"ssd_scan_bwd_{i}", s["xbc"], s["dt_col"], s["dt_row"], s["a_row"], s["a_col"], s["d_row"], s["sprev"], dy, d_inner)
            ddt = (jnp.moveaxis(ddt_col[:, :, :rpg], 0, 1).reshape(rows, heads)
                   + jnp.transpose(ddt_row[:, :rpg, :], (2, 0, 1)).reshape(rows, heads))
            d_a = (dar[:, 0, :rpg] + dac[:, :rpg, 0]).reshape(heads)
            g_alog[j] = d_a * s["a_neg"]
            g_dskip[j] = ddsk[:, 0, :rpg].reshape(heads)
            ddtr, dbias = dt_bwd(f"ssd_dt_bwd_{i}", s["zx"], bias_pad[j:j + 1], pad_h(ddt), zx_cols)
            g_dtb[j] = dbias[0, :heads]
            dxbc_raw, g_convw[j], dconvb = conv_bwd(
                f"ssd_conv_bwd_{i}", s["zx"], convw_f[j], ssd_conv_b[j], jnp.concatenate([dx, db, dc], axis=1), d_inner, conv_dim)
            g_convb[j] = dconvb[0]
            dzx = jnp.concatenate([dz, dxbc_raw, ddtr], axis=1)
            g_in = matmul(f"ssd_in_bwd_w_{i}", s["u_t"], dzx, out_dtype=BF16, **MM_ROWS_RESIDENT,
                          tn=384 if dzx.shape[1] % 384 == 0 else 512)
            tok = scatter(f"scatter_start_in_{i}", [("in", _block_cols(g_in[:, :d_in_proj]))], j, g_in)
            du = matmul(f"ssd_in_bwd_x_{i}", dzx, s["w_in"], tb=True, **{**MM_DEEP, "tk": 1152 if dzx.shape[1] % 1152 == 0 else 512})
        else:
            dmixed, g_poolw, g_poolb[j], g_pools[j] = pool_proj_bwd(
                f"pool_proj_bwd_{i}", dmix, s["pre"], s["mixed"], s["w_pool"], pools_f[j])
            ng, gd = g_poolw.shape[0], g_poolw.shape[1]
            blk_pool = jnp.moveaxis(g_poolw.astype(BF16).reshape(ng, N_DEV, gd // N_DEV, gd), 1, 0)
            tok = scatter(f"scatter_start_pool_{i}", [("pool", blk_pool)], j, blk_pool)
            du = pool_sub(f"pool_sub_bwd_{i}", dmixed, True)
        dh, g_norm[i][0] = rmsnorm_bwd(f"norm_pre_mix_bwd_{i}", s["h"], norm_f[i, 0], du, add=dh1, after=tok)

    grad_x = dh[PAD_FRONT + N_META:][None]
    g_meta = dh[PAD_FRONT:PAD_FRONT + N_META]

    small_grads = [g_meta, jnp.stack([jnp.stack(r) for r in g_norm]), jnp.stack(g_convw), jnp.stack(g_convb), jnp.stack(g_dtb),
                   jnp.stack(g_alog), jnp.stack(g_dskip), jnp.stack(g_ssdnorm), jnp.stack(g_poolb), jnp.stack(g_pools)]
    packed_g = _pack(small_grads)
    (small_rec,), _ = exchange_start(
        "gather_small_grads_start", [[(packed_g, lax.empty((N_DEV, *packed_g.shape), F32), _whole, _slot)]], packed_g)
    mine = lambda a: _my_cols(a, a.shape[-1] // N_DEV)
    small_w = [meta_tokens, norm_w, ssd_conv_w, ssd_conv_b, ssd_dt_bias, ssd_a_log, ssd_d, ssd_norm_w, pool_b, pool_scale]
    small_m = [m_meta_tokens, m_norm_w, m_ssd_conv_w, m_ssd_conv_b, m_ssd_dt_bias, m_ssd_a_log, m_ssd_d, m_ssd_norm_w, m_pool_b, m_pool_scale]
    small_v = [v_meta_tokens, v_norm_w, v_ssd_conv_w, v_ssd_conv_b, v_ssd_dt_bias, v_ssd_a_log, v_ssd_d, v_ssd_norm_w, v_pool_b, v_pool_scale]
    sharded = [True, True, True, False, False, False, False, False, True, True]
    def widen(a, is_sharded, full):
        if not is_sharded:
            return a
        return lax.dynamic_update_slice_in_dim(jnp.zeros(full.shape, F32), a, _linear(_my_place()) * a.shape[-1], axis=a.ndim - 1)
    packed_w = _pack([widen(a, sh, g) for a, sh, g in zip(small_w, sharded, small_grads)])
    packed_m = _pack([widen(a, sh, g) for a, sh, g in zip(small_m, sharded, small_grads)])
    packed_v = _pack([widen(a, sh, g) for a, sh, g in zip(small_v, sharded, small_grads)])

    def landed(name, keys, after):
        rs = [r for r, ks in enumerate(scatter_keys) if set(ks) <= set(keys)]
        srcs, out = exchange_wait(name, [scatter_recs[r] for r in rs], [lands[k] for k in keys],
                                  [[keys.index(k) for k in scatter_keys[r]] for r in rs], [scatter_views[r] for r in rs], after)
        me = _linear(_my_place())
        src_of = iter(srcs)
        for r in rs:
            for key in scatter_keys[r]:
                own = lax.dynamic_index_in_dim(next(src_of), me, axis=0, keepdims=True)[:, None]
                slot = keys.index(key)
                start = (me, scatter_layers[r]) + (0,) * (own.ndim - 2)
                out[slot] = lax.dynamic_update_slice(out[slot], own, start)
        return out

    p_gu, p_down = landed("scatter_wait_ffn", ["gate_up", "down"], packed_g)
    a_gate = sum_adamw("adamw_ffn_w_gate", p_gu, ffn_w_gate, m_ffn_w_gate, v_ffn_w_gate, half=0)
    a_up = sum_adamw("adamw_ffn_w_up", p_gu, ffn_w_up, m_ffn_w_up, v_ffn_w_up, half=1)
    a_down = sum_adamw("adamw_ffn_w_down", p_down, ffn_w_down, m_ffn_w_down, v_ffn_w_down)
    (p_pool,) = landed("scatter_wait_pool", ["pool"], a_down[0])
    a_pool = sum_adamw("adamw_pool_w", p_pool, pool_w, m_pool_w, v_pool_w)
    (p_out,) = landed("scatter_wait_out", ["out"], a_pool[0])
    a_out = sum_adamw("adamw_ssd_w_out", p_out, ssd_w_out, m_ssd_w_out, v_ssd_w_out)
    (p_in,) = landed("scatter_wait_in", ["in"], a_out[0])
    a_in = sum_adamw("adamw_ssd_w_in", p_in, ssd_w_in, m_ssd_w_in, v_ssd_w_in)
    bg = [a_in, a_out, a_pool, a_gate, a_up, a_down]

    (small_parts,) = gathered("gather_small_grads_wait", small_rec, a_in[0])
    sm = [_unpack(o, small_grads) for o in sum_adamw("adamw_small", small_parts, packed_w, packed_m, packed_v)]
    sm = [[mine(a) if sh else a for a, sh in zip(group, sharded)] for group in sm]

    def ordered(kind):
        s_ = sm[kind]
        b_ = [o[kind] for o in bg]
        return [s_[0], s_[1], b_[0], s_[2], s_[3], s_[4], s_[5], s_[6], s_[7], b_[1], b_[2], s_[8], s_[9], b_[3], b_[4], b_[5]]

    return (loss, grad_x, *ordered(0), *ordered(1), *ordered(2), *ordered(3))
```

```python
import functools

import jax
import jax.numpy as jnp
from jax import lax
from jax.experimental import pallas as pl
from jax.experimental.pallas import tpu as pltpu

F32 = jnp.float32
BF16 = jnp.bfloat16
MESH = pl.DeviceIdType.MESH
AXES = ("x", "y", "c")
N_DEV = 8

N_META = 16
EPS = 1e-6
HEAD_DIM = 64
D_STATE = 128
SSD_GROUPS = 8
D_CONV = 4
CHUNK = 256
POOL_WINDOWS = (2, 4, 8, 16)
ADAM_LR, ADAM_B1, ADAM_B2, ADAM_EPS, ADAM_WD, ADAM_STEP = 0.001, 0.9, 0.999, 1e-08, 0.01, 10

PAD_FRONT = (-N_META) % CHUNK
LANES = 128
SUBLANES = 8
ROW_TILE = 256
CONV_TILE = 256
POOL_TILE = 128
VMEM_LIMIT = 56 * 1024 * 1024


def _params(sem=None):
    return pltpu.CompilerParams(dimension_semantics=sem, vmem_limit_bytes=VMEM_LIMIT)


def _tile(n, target, mult):
    if n <= target:
        return n
    best = None
    for t in range(mult, target + 1, mult):
        if n % t == 0:
            best = t
    assert best is not None, (n, target, mult)
    return best


def _dot(a, b, ca, cb):
    return lax.dot_general(a, b, (((ca,), (cb,)), ((), ())), preferred_element_type=F32)


def _sigmoid(x):
    return 1.0 / (1.0 + jnp.exp(-x))


def _row_mask(shape, first_row):
    rows = lax.broadcasted_iota(jnp.int32, shape, 0) + first_row
    return rows >= PAD_FRONT


MM_ROWS_RESIDENT = dict(tm=2304, tk=2304)
MM_COLS_RESIDENT = dict(tm=512, tn=2304, tk=2304)
MM_DEEP = dict(tm=1152, tn=1024, tk=512)
GATHER_AHEAD = 2

def matmul(name, a, b, *, tb=False, out_dtype=F32, tm=768, tn=512, tk=2048):
    m, kdim = a.shape
    if tb:
        n, k2 = b.shape
    else:
        k2, n = b.shape
    assert kdim == k2, (a.shape, b.shape, tb)
    tm = _tile(m, tm, 16)
    tn = _tile(n, tn, LANES)
    tk = _tile(kdim, tk, LANES)
    nk = kdim // tk
    a_spec = pl.BlockSpec((tm, tk), lambda i, j, k: (i, k))
    b_spec = pl.BlockSpec((tn, tk), lambda i, j, k: (j, k)) if tb else pl.BlockSpec((tk, tn), lambda i, j, k: (k, j))

    def body_single(a_ref, b_ref, o_ref):
        o_ref[...] = _dot(a_ref[...], b_ref[...], 1, 1 if tb else 0).astype(o_ref.dtype)

    def body_acc(a_ref, b_ref, o_ref, acc_ref):
        k = pl.program_id(2)

        @pl.when(k == 0)
        def _():
            acc_ref[...] = jnp.zeros_like(acc_ref)

        acc_ref[...] += _dot(a_ref[...], b_ref[...], 1, 1 if tb else 0)

        @pl.when(k == nk - 1)
        def _():
            o_ref[...] = acc_ref[...].astype(o_ref.dtype)

    return pl.pallas_call(
        body_single if nk == 1 else body_acc, name=name, grid=(m // tm, n // tn, nk),
        in_specs=[a_spec, b_spec], out_specs=pl.BlockSpec((tm, tn), lambda i, j, k: (i, j)),
        out_shape=jax.ShapeDtypeStruct((m, n), out_dtype),
        scratch_shapes=[] if nk == 1 else [pltpu.VMEM((tm, tn), F32)],
        compiler_params=_params(("parallel", "parallel", "arbitrary")),
    )(a, b)


_TOKEN_SPEC = pl.BlockSpec((SUBLANES, LANES), lambda i: (0, 0))


def rmsnorm_fwd(name, x, w, res=None, out_dtype=F32, transposed=False, after=None):
    rows, d = x.shape
    tr = ROW_TILE
    row_spec = pl.BlockSpec((tr, d), lambda i: (i, 0))
    w_spec = pl.BlockSpec((1, d), lambda i: (0, 0))
    n_in = 2 + (res is not None) + (after is not None)

    def body(*refs):
        x_ref, w_ref = refs[:2]
        outs = refs[n_in:]
        xv = x_ref[...]
        y = xv * lax.rsqrt(jnp.mean(xv * xv, axis=-1, keepdims=True) + EPS) * w_ref[...]
        if res is not None:
            y = refs[2][...] + y
        outs[0][...] = y.astype(out_dtype)
        if transposed:
            outs[1][...] = y.T.astype(out_dtype)

    args = [x, w.reshape(1, d)] + ([] if res is None else [res]) + ([] if after is None else [after])
    specs = [row_spec, w_spec] + ([] if res is None else [row_spec]) + ([] if after is None else [_TOKEN_SPEC])
    out_specs, out_shape = [row_spec], [jax.ShapeDtypeStruct((rows, d), out_dtype)]
    if transposed:
        out_specs.append(pl.BlockSpec((d, tr), lambda i: (0, i)))
        out_shape.append(jax.ShapeDtypeStruct((d, rows), out_dtype))
    outs = pl.pallas_call(
        body, name=name, grid=(rows // tr,), in_specs=specs, out_specs=out_specs, out_shape=out_shape,
        compiler_params=_params(("parallel",)),
    )(*args)
    return outs if transposed else outs[0]


def rmsnorm_bwd(name, x, w, dy, add=None, after=None):
    rows, d = x.shape
    tr = ROW_TILE
    row_spec = pl.BlockSpec((tr, d), lambda i: (i, 0))
    w_spec = pl.BlockSpec((1, d), lambda i: (0, 0))

    def body(*refs):
        x_ref, w_ref, dy_ref = refs[:3]
        add_ref = None if add is None else refs[3]
        dx_ref, dw_ref = refs[-2:]
        xv = x_ref[...]
        dyv = dy_ref[...].astype(F32)
        r = lax.rsqrt(jnp.mean(xv * xv, axis=-1, keepdims=True) + EPS)
        xh = xv * r
        dxh = dyv * w_ref[...]
        dx = r * (dxh - xh * jnp.mean(dxh * xh, axis=-1, keepdims=True))
        if add is not None:
            dx = dx + add_ref[...]
        dx_ref[...] = dx

        @pl.when(pl.program_id(0) == 0)
        def _():
            dw_ref[...] = jnp.zeros_like(dw_ref)

        dw_ref[...] += jnp.sum(dyv * xh, axis=0, keepdims=True)

    args = [x, w.reshape(1, d), dy] + ([] if add is None else [add]) + ([] if after is None else [after])
    specs = [row_spec, w_spec, row_spec] + ([] if add is None else [row_spec]) + ([] if after is None else [_TOKEN_SPEC])
    dx, dw = pl.pallas_call(
        body, name=name, grid=(rows // tr,), in_specs=specs, out_specs=[row_spec, w_spec],
        out_shape=[jax.ShapeDtypeStruct((rows, d), F32), jax.ShapeDtypeStruct((1, d), F32)],
        compiler_params=_params(("arbitrary",)),
    )(*args)
    return dx, dw.reshape(d)


def loss_head(name, h, target):
    rows, d = h.shape
    tr = ROW_TILE
    first = (PAD_FRONT + N_META) // tr
    assert (PAD_FRONT + N_META) % tr == 0 and target.shape[0] == rows - first * tr

    def body(h_ref, t_ref, dh_ref, loss_ref):
        i = pl.program_id(0)

        @pl.when(i == 0)
        def _():
            loss_ref[...] = jnp.zeros_like(loss_ref)

        keep = (i >= first).astype(F32)
        diff = (h_ref[...] - t_ref[...]) * keep
        dh_ref[...] = diff / d
        loss_ref[...] += 0.5 * jnp.sum(diff * diff) / d

    dh, loss = pl.pallas_call(
        body, name=name, grid=(rows // tr,),
        in_specs=[pl.BlockSpec((tr, d), lambda i: (i, 0)), pl.BlockSpec((tr, d), lambda i: (jnp.maximum(i - first, 0), 0))],
        out_specs=[pl.BlockSpec((tr, d), lambda i: (i, 0)), pl.BlockSpec((SUBLANES, LANES), lambda i: (0, 0))],
        out_shape=[jax.ShapeDtypeStruct((rows, d), F32), jax.ShapeDtypeStruct((SUBLANES, LANES), F32)],
        compiler_params=_params(("arbitrary",)),
    )(h, target)
    return dh, loss[0, 0]


def _mm_call(name, a, b, grid, a_spec, b_spec, o_spec, out_sds, ca, cb, red_axis=None):
    n_red = None if red_axis is None else grid[red_axis]

    def body_single(a_ref, b_ref, o_ref):
        o_ref[...] = _dot(a_ref[...], b_ref[...], ca, cb).astype(o_ref.dtype)

    def body_acc(a_ref, b_ref, o_ref, acc_ref):
        k = pl.program_id(red_axis)

        @pl.when(k == 0)
        def _():
            acc_ref[...] = jnp.zeros_like(acc_ref)

        acc_ref[...] += _dot(a_ref[...], b_ref[...], ca, cb)

        @pl.when(k == n_red - 1)
        def _():
            o_ref[...] = acc_ref[...].astype(o_ref.dtype)

    acc_shape = tuple(s for s in o_spec.block_shape if s is not None)
    sem = tuple("arbitrary" if ax == red_axis else "parallel" for ax in range(len(grid)))
    return pl.pallas_call(
        body_single if red_axis is None else body_acc, name=name, grid=grid, in_specs=[a_spec, b_spec], out_specs=o_spec,
        out_shape=out_sds, scratch_shapes=[] if red_axis is None else [pltpu.VMEM(acc_shape, F32)],
        compiler_params=_params(sem),
    )(a, b)


def ffn_in_fwd(name, u2, w_gu):
    rows, d = u2.shape
    hs = w_gu.shape[-1]
    return _mm_call(name, u2, w_gu, (N_DEV, 2), pl.BlockSpec((rows, d), lambda k, t: (0, 0)),
                    pl.BlockSpec((None, None, d, hs), lambda k, t: (k, t, 0, 0)),
                    pl.BlockSpec((None, None, rows, hs), lambda k, t: (k, t, 0, 0)),
                    jax.ShapeDtypeStruct((N_DEV, 2, rows, hs), F32), 1, 0)


def ffn_out_fwd(name, act, w_down):
    _, rows, hs = act.shape
    d = w_down.shape[-1]
    tm, tn = _tile(rows, 1152, 16), _tile(d, 1024, LANES)
    return _mm_call(name, act, w_down, (rows // tm, d // tn, N_DEV), pl.BlockSpec((None, tm, hs), lambda i, j, k: (k, i, 0)),
                    pl.BlockSpec((None, hs, tn), lambda i, j, k: (k, 0, j)), pl.BlockSpec((tm, tn), lambda i, j, k: (i, j)),
                    jax.ShapeDtypeStruct((rows, d), F32), 1, 0, red_axis=2)


def ffn_out_bwd_x(name, df, w_down):
    rows, d = df.shape
    hs = w_down.shape[1]
    return _mm_call(name, df, w_down, (N_DEV,), pl.BlockSpec((rows, d), lambda k: (0, 0)),
                    pl.BlockSpec((None, hs, d), lambda k: (k, 0, 0)), pl.BlockSpec((None, rows, hs), lambda k: (k, 0, 0)),
                    jax.ShapeDtypeStruct((N_DEV, rows, hs), F32), 1, 1)


def ffn_out_bwd_w(name, act_t, df):
    _, hs, rows = act_t.shape
    d = df.shape[1]
    return _mm_call(name, act_t, df, (N_DEV,), pl.BlockSpec((None, hs, rows), lambda k: (k, 0, 0)),
                    pl.BlockSpec((rows, d), lambda k: (0, 0)), pl.BlockSpec((None, hs, d), lambda k: (k, 0, 0)),
                    jax.ShapeDtypeStruct((N_DEV, hs, d), BF16), 1, 0)


def ffn_in_bwd_w(name, u2_t, dgp):
    d, rows = u2_t.shape
    hs = dgp.shape[-1]
    return _mm_call(name, u2_t, dgp, (N_DEV, 2), pl.BlockSpec((d, rows), lambda k, t: (0, 0)),
                    pl.BlockSpec((None, None, rows, hs), lambda k, t: (k, t, 0, 0)),
                    pl.BlockSpec((None, None, d, hs), lambda k, t: (k, t, 0, 0)),
                    jax.ShapeDtypeStruct((N_DEV, 2, d, hs), BF16), 1, 0)


def ffn_in_bwd_x(name, dgp, w_gu):
    _, _, rows, hs = dgp.shape
    d = w_gu.shape[2]
    tm, tn = _tile(rows, 1152, 16), _tile(d, 1024, LANES)
    return _mm_call(name, dgp.reshape(2 * N_DEV, rows, hs), w_gu.reshape(2 * N_DEV, d, hs), (rows // tm, d // tn, 2 * N_DEV),
                    pl.BlockSpec((None, tm, hs), lambda i, j, k: (k, i, 0)), pl.BlockSpec((None, tn, hs), lambda i, j, k: (k, j, 0)),
                    pl.BlockSpec((tm, tn), lambda i, j, k: (i, j)), jax.ShapeDtypeStruct((rows, d), F32), 1, 1, red_axis=2)


def swiglu_fwd(name, gp):
    _, _, rows, hs = gp.shape
    tr = _tile(rows, 768, ROW_TILE)

    def body(gp_ref, a_ref, at_ref):
        g = gp_ref[0]
        act = g * _sigmoid(g) * gp_ref[1]
        a_ref[...] = act.astype(a_ref.dtype)
        at_ref[...] = act.T.astype(at_ref.dtype)

    return pl.pallas_call(
        body, name=name, grid=(N_DEV, rows // tr),
        in_specs=[pl.BlockSpec((None, 2, tr, hs), lambda k, i: (k, 0, i, 0))],
        out_specs=[pl.BlockSpec((None, tr, hs), lambda k, i: (k, i, 0)), pl.BlockSpec((None, hs, tr), lambda k, i: (k, 0, i))],
        out_shape=[jax.ShapeDtypeStruct((N_DEV, rows, hs), BF16), jax.ShapeDtypeStruct((N_DEV, hs, rows), BF16)],
        compiler_params=_params(("parallel", "parallel")),
    )(gp)


def swiglu_bwd(name, gp, dact):
    _, _, rows, hs = gp.shape
    tr = _tile(rows, 768, ROW_TILE)

    def body(gp_ref, da_ref, o_ref):
        g = gp_ref[0]
        s = _sigmoid(g)
        dav = da_ref[...]
        o_ref[0] = (dav * gp_ref[1] * (s * (1.0 + g * (1.0 - s)))).astype(o_ref.dtype)
        o_ref[1] = (dav * (g * s)).astype(o_ref.dtype)

    blk = pl.BlockSpec((None, 2, tr, hs), lambda k, i: (k, 0, i, 0))
    return pl.pallas_call(
        body, name=name, grid=(N_DEV, rows // tr),
        in_specs=[blk, pl.BlockSpec((None, tr, hs), lambda k, i: (k, i, 0))], out_specs=blk,
        out_shape=jax.ShapeDtypeStruct(gp.shape, BF16), compiler_params=_params(("parallel", "parallel")),
    )(gp, dact)


def conv_fwd(name, zx, conv_w, conv_b, d_inner, conv_dim):
    rows = zx.shape[0]
    tc = CONV_TILE
    off = d_inner // tc
    assert d_inner % tc == 0 and conv_dim % tc == 0

    def body(u_ref, w_ref, b_ref, o_ref):
        u = u_ref[...]
        acc = u * w_ref[D_CONV - 1:D_CONV, :] + b_ref[...]
        for s in range(1, D_CONV):
            acc = acc + pltpu.roll(u, s, axis=0) * w_ref[D_CONV - 1 - s:D_CONV - s, :]
        y = acc * _sigmoid(acc)
        o_ref[...] = jnp.where(_row_mask(y.shape, 0), y, 0.0)

    return pl.pallas_call(
        body, name=name, grid=(conv_dim // tc,),
        in_specs=[pl.BlockSpec((rows, tc), lambda j: (0, off + j)), pl.BlockSpec((D_CONV, tc), lambda j: (0, j)),
                  pl.BlockSpec((1, tc), lambda j: (0, j))],
        out_specs=pl.BlockSpec((rows, tc), lambda j: (0, j)),
        out_shape=jax.ShapeDtypeStruct((rows, conv_dim), F32), compiler_params=_params(("parallel",)),
    )(zx, conv_w, conv_b.reshape(1, conv_dim))


def conv_bwd(name, zx, conv_w, conv_b, dxbc, d_inner, conv_dim):
    rows = zx.shape[0]
    tc = CONV_TILE
    off = d_inner // tc

    def body(u_ref, w_ref, b_ref, dy_ref, du_ref, dw_ref, db_ref):
        u = u_ref[...]
        wk = [w_ref[D_CONV - 1 - s:D_CONV - s, :] for s in range(D_CONV)]
        shifted = [u] + [pltpu.roll(u, s, axis=0) for s in range(1, D_CONV)]
        acc = u * wk[0] + b_ref[...]
        for s in range(1, D_CONV):
            acc = acc + shifted[s] * wk[s]
        sg = _sigmoid(acc)
        mask = _row_mask(acc.shape, 0)
        dpre = jnp.where(mask, dy_ref[...] * (sg * (1.0 + acc * (1.0 - sg))), 0.0)
        db_ref[...] = jnp.sum(dpre, axis=0, keepdims=True)
        du = dpre * wk[0]
        dw_ref[D_CONV - 1:D_CONV, :] = jnp.sum(dpre * u, axis=0, keepdims=True)
        for s in range(1, D_CONV):
            du = du + pltpu.roll(dpre, rows - s, axis=0) * wk[s]
            dw_ref[D_CONV - 1 - s:D_CONV - s, :] = jnp.sum(dpre * shifted[s], axis=0, keepdims=True)
        du_ref[...] = jnp.where(mask, du, 0.0).astype(du_ref.dtype)

    return pl.pallas_call(
        body, name=name, grid=(conv_dim // tc,),
        in_specs=[pl.BlockSpec((rows, tc), lambda j: (0, off + j)), pl.BlockSpec((D_CONV, tc), lambda j: (0, j)),
                  pl.BlockSpec((1, tc), lambda j: (0, j)), pl.BlockSpec((rows, tc), lambda j: (0, j))],
        out_specs=[pl.BlockSpec((rows, tc), lambda j: (0, j)), pl.BlockSpec((D_CONV, tc), lambda j: (0, j)),
                   pl.BlockSpec((1, tc), lambda j: (0, j))],
        out_shape=[jax.ShapeDtypeStruct((rows, conv_dim), BF16), jax.ShapeDtypeStruct((D_CONV, conv_dim), F32),
                   jax.ShapeDtypeStruct((1, conv_dim), F32)],
        compiler_params=_params(("parallel",)),
    )(zx, conv_w, conv_b.reshape(1, conv_dim), dxbc)


def dt_fwd(name, zx, bias_pad, zx_cols):
    rows = zx.shape[0]
    tr = ROW_TILE
    off = zx_cols // LANES

    def body(r_ref, b_ref, o_ref):
        v = r_ref[...] + b_ref[...]
        sp = jnp.maximum(v, 0.0) + jnp.log1p(jnp.exp(-jnp.abs(v)))
        o_ref[...] = jnp.where(_row_mask(v.shape, pl.program_id(0) * tr), sp, 0.0)

    return pl.pallas_call(
        body, name=name, grid=(rows // tr,),
        in_specs=[pl.BlockSpec((tr, LANES), lambda i: (i, off)), pl.BlockSpec((1, LANES), lambda i: (0, 0))],
        out_specs=pl.BlockSpec((tr, LANES), lambda i: (i, 0)),
        out_shape=jax.ShapeDtypeStruct((rows, LANES), F32), compiler_params=_params(("parallel",)),
    )(zx, bias_pad)


def dt_bwd(name, zx, bias_pad, ddt, zx_cols):
    rows = zx.shape[0]
    tr = ROW_TILE
    off = zx_cols // LANES

    def body(r_ref, b_ref, d_ref, o_ref, db_ref):
        v = r_ref[...] + b_ref[...]
        g = jnp.where(_row_mask(v.shape, pl.program_id(0) * tr), d_ref[...] * _sigmoid(v), 0.0)
        o_ref[...] = g.astype(o_ref.dtype)

        @pl.when(pl.program_id(0) == 0)
        def _():
            db_ref[...] = jnp.zeros_like(db_ref)

        db_ref[...] += jnp.sum(g, axis=0, keepdims=True)

    return pl.pallas_call(
        body, name=name, grid=(rows // tr,),
        in_specs=[pl.BlockSpec((tr, LANES), lambda i: (i, off)), pl.BlockSpec((1, LANES), lambda i: (0, 0)),
                  pl.BlockSpec((tr, LANES), lambda i: (i, 0))],
        out_specs=[pl.BlockSpec((tr, LANES), lambda i: (i, 0)), pl.BlockSpec((1, LANES), lambda i: (0, 0))],
        out_shape=[jax.ShapeDtypeStruct((rows, LANES), BF16), jax.ShapeDtypeStruct((1, LANES), F32)],
        compiler_params=_params(("arbitrary",)),
    )(zx, bias_pad, ddt)


def _split3(x):
    h1 = x.astype(BF16)
    r1 = x - h1.astype(F32)
    h2 = r1.astype(BF16)
    h3 = (r1 - h2.astype(F32)).astype(BF16)
    return h1, h2, h3


def _exact_left(ones_b, x):
    h1, h2, h3 = _split3(x)
    return _dot(ones_b, h1, 1, 0) + _dot(ones_b, h2, 1, 0) + _dot(ones_b, h3, 1, 0)


def _exact_right_t(x, ones_b):
    h1, h2, h3 = _split3(x)
    return _dot(h1, ones_b, 1, 1) + _dot(h2, ones_b, 1, 1) + _dot(h3, ones_b, 1, 1)


class _ScanCommon:
    def __init__(self, b_ref, c_ref, dtc_ref, dtr_ref, arow_ref, acol_ref, drow_ref):
        q = CHUNK
        self.bb = b_ref[...].astype(BF16)
        self.cb = c_ref[...].astype(BF16)
        ri = lax.broadcasted_iota(jnp.int32, (q, q), 0)
        cj = lax.broadcasted_iota(jnp.int32, (q, q), 1)
        self.lower = ri >= cj
        self.upper = cj >= ri
        self.dtc = dtc_ref[...]
        self.dtr = dtr_ref[...]
        self.arow = arow_ref[...]
        self.acol = acol_ref[...]
        self.drow = drow_ref[...]
        da_col = self.dtc * self.arow
        self.a_col = _exact_left(self.lower.astype(BF16), da_col)
        self.a_row = _exact_right_t(self.dtr * self.acol, self.lower.astype(BF16))
        self.a_last = jnp.sum(da_col, axis=0, keepdims=True)
        self.lane_q = lax.broadcasted_iota(jnp.int32, (q, LANES), 1)
        self.lane_1 = lax.broadcasted_iota(jnp.int32, (1, LANES), 1)
        self.sub_8 = lax.broadcasted_iota(jnp.int32, (SUBLANES, q), 0)
        self.first_half = self.lane_q < HEAD_DIM
        self.first_rows = lax.broadcasted_iota(jnp.int32, (LANES, 1), 0) < HEAD_DIM

    def col(self, v, r):
        return jnp.sum(jnp.where(self.lane_q == r, v, 0.0), axis=1, keepdims=True)

    def row(self, v, r):
        return jnp.sum(jnp.where(self.sub_8 == r, v, 0.0), axis=0, keepdims=True)

    def scalar(self, v, r):
        return jnp.sum(jnp.where(self.lane_1 == r, v, 0.0), axis=1, keepdims=True)

    def pair(self, v0, v1):
        return jnp.where(self.first_half, v0, v1)

    def half_rowsum(self, t, h):
        keep = self.first_half if h == 0 else jnp.logical_not(self.first_half)
        return jnp.sum(jnp.where(keep, t, 0.0), axis=1, keepdims=True)


def ssd_scan_fwd(name, xbc, dt_col, dt_row, a_row, a_col, d_row, d_inner):
    rows = xbc.shape[0]
    q, n, g_cnt = CHUNK, D_STATE, SSD_GROUPS
    nc = rows // q
    rp = d_inner // g_cnt
    n_pairs = rp // LANES
    b_off = d_inner // n

    def body(x_ref, b_ref, c_ref, dtc_ref, dtr_ref, arow_ref, acol_ref, drow_ref, y_ref, sprev_ref, s_ref):
        @pl.when(pl.program_id(1) == 0)
        def _():
            s_ref[...] = jnp.zeros_like(s_ref)

        sprev_ref[...] = s_ref[...]
        k = _ScanCommon(b_ref, c_ref, dtc_ref, dtr_ref, arow_ref, acol_ref, drow_ref)
        cb_mat = _dot(k.cb, k.bb, 1, 1)
        for pr in range(n_pairs):
            sl = slice(pr * LANES, (pr + 1) * LANES)
            heads = (2 * pr, 2 * pr + 1)
            xp = x_ref[:, sl]
            ac = [k.col(k.a_col, r) for r in heads]
            ar = [k.row(k.a_row, r) for r in heads]
            al = [k.scalar(k.a_last, r) for r in heads]
            xd = xp * k.pair(k.col(k.dtc, heads[0]), k.col(k.dtc, heads[1]))
            xdb = xd.astype(BF16)
            ys = []
            for h in range(2):
                lm = jnp.exp(jnp.where(k.lower, ac[h] - ar[h], -jnp.inf))
                ys.append(_dot((cb_mat * lm).astype(BF16), xdb, 1, 0))
            y = jnp.where(k.first_half, ys[0], ys[1])
            sp = s_ref[sl, :]
            y = y + k.pair(jnp.exp(ac[0]), jnp.exp(ac[1])) * _dot(k.cb, sp.astype(BF16), 1, 1)
            y = y + k.pair(k.scalar(k.drow, heads[0]), k.scalar(k.drow, heads[1])) * xp
            y_ref[:, sl] = y
            wb = (xd * k.pair(jnp.exp(al[0] - ac[0]), jnp.exp(al[1] - ac[1]))).astype(BF16)
            decay = jnp.where(k.first_rows, jnp.exp(al[0]), jnp.exp(al[1]))
            s_ref[sl, :] = decay * sp + _dot(wb, k.bb, 0, 0)

    return pl.pallas_call(
        body, name=name, grid=(g_cnt, nc),
        in_specs=[
            pl.BlockSpec((q, rp), lambda g, c: (c, g)),
            pl.BlockSpec((q, n), lambda g, c: (c, b_off + g)),
            pl.BlockSpec((q, n), lambda g, c: (c, b_off + g_cnt + g)),
            pl.BlockSpec((None, q, LANES), lambda g, c: (g, c, 0)),
            pl.BlockSpec((None, SUBLANES, q), lambda g, c: (g, 0, c)),
            pl.BlockSpec((None, 1, LANES), lambda g, c: (g, 0, 0)),
            pl.BlockSpec((None, SUBLANES, 1), lambda g, c: (g, 0, 0)),
            pl.BlockSpec((None, 1, LANES), lambda g, c: (g, 0, 0)),
        ],
        out_specs=[pl.BlockSpec((q, rp), lambda g, c: (c, g)),
                   pl.BlockSpec((None, None, rp, n), lambda g, c: (c, g, 0, 0))],
        out_shape=[jax.ShapeDtypeStruct((rows, d_inner), F32), jax.ShapeDtypeStruct((nc, g_cnt, rp, n), F32)],
        scratch_shapes=[pltpu.VMEM((rp, n), F32)],
        compiler_params=_params(("parallel", "arbitrary")),
    )(xbc, xbc, xbc, dt_col, dt_row, a_row, a_col, d_row)


def ssd_scan_bwd(name, xbc, dt_col, dt_row, a_row, a_col, d_row, sprev, dy, d_inner):
    rows = xbc.shape[0]
    q, n, g_cnt = CHUNK, D_STATE, SSD_GROUPS
    nc = rows // q
    rp = d_inner // g_cnt
    n_pairs = rp // LANES
    b_off = d_inner // n

    def body(x_ref, b_ref, c_ref, dtc_ref, dtr_ref, arow_ref, acol_ref, drow_ref, sprev_ref, dy_ref,
             dx_ref, db_ref, dc_ref, ddtc_ref, ddtr_ref, dar_ref, dac_ref, dd_ref, ds_ref):
        @pl.when(pl.program_id(1) == 0)
        def _():
            ds_ref[...] = jnp.zeros_like(ds_ref)
            dar_ref[...] = jnp.zeros_like(dar_ref)
            dac_ref[...] = jnp.zeros_like(dac_ref)
            dd_ref[...] = jnp.zeros_like(dd_ref)

        k = _ScanCommon(b_ref, c_ref, dtc_ref, dtr_ref, arow_ref, acol_ref, drow_ref)
        cb_mat = _dot(k.cb, k.bb, 1, 1)
        cbt_mat = _dot(k.bb, k.cb, 1, 1)
        d_cb = jnp.zeros((q, q), F32)
        d_b = jnp.zeros((q, n), F32)
        d_c = jnp.zeros((q, n), F32)
        da_col = jnp.zeros((q, LANES), F32)
        da_row = jnp.zeros((SUBLANES, q), F32)
        ddt_x = jnp.zeros((q, LANES), F32)
        d_alast = jnp.zeros((1, LANES), F32)
        d_dskip = jnp.zeros((1, LANES), F32)
        for pr in range(n_pairs):
            sl = slice(pr * LANES, (pr + 1) * LANES)
            heads = (2 * pr, 2 * pr + 1)
            xp = x_ref[:, sl]
            dyp = dy_ref[:, sl]
            dyb = dyp.astype(BF16)
            ac = [k.col(k.a_col, r) for r in heads]
            ar = [k.row(k.a_row, r) for r in heads]
            al = [k.scalar(k.a_last, r) for r in heads]
            dt_p = k.pair(k.col(k.dtc, heads[0]), k.col(k.dtc, heads[1]))
            xd = xp * dt_p
            xdb = xd.astype(BF16)
            sp = sprev_ref[sl, :]
            spb = sp.astype(BF16)
            dsp = ds_ref[sl, :]
            dspb = dsp.astype(BF16)
            dskip_p = k.pair(k.scalar(k.drow, heads[0]), k.scalar(k.drow, heads[1]))
            dxp = dskip_p * dyp
            dd_lane = jnp.sum(dyp * xp, axis=0, keepdims=True)
            e_p = k.pair(jnp.exp(ac[0]), jnp.exp(ac[1]))
            t_off = dyp * (e_p * _dot(k.cb, spb, 1, 1))
            dzb = (e_p * dyp).astype(BF16)
            d_c = d_c + _dot(dzb, spb, 1, 0)
            ds_in = _dot(dzb, k.cb, 0, 0)
            decay = jnp.where(k.first_rows, jnp.exp(al[0]), jnp.exp(al[1]))
            ds_in = ds_in + decay * dsp
            t_state = jnp.sum(dsp * sp, axis=1, keepdims=True) * decay
            dec_p = k.pair(jnp.exp(al[0] - ac[0]), jnp.exp(al[1] - ac[1]))
            dw = _dot(k.bb, dspb, 1, 1)
            d_b = d_b + _dot((xd * dec_p).astype(BF16), dspb, 1, 0)
            dxd = dw * dec_p
            t_dec = dw * xd * dec_p
            dxd_h = []
            for h in range(2):
                r = heads[h]
                keep = k.first_half if h == 0 else jnp.logical_not(k.first_half)
                lm = jnp.exp(jnp.where(k.lower, ac[h] - ar[h], -jnp.inf))
                m_mat = cb_mat * lm
                dm = _dot(jnp.where(keep, dyp, 0.0).astype(BF16), xdb, 1, 1)
                dseg = dm * m_mat
                d_cb = d_cb + dm * lm
                lmt = jnp.exp(jnp.where(k.upper, ar[h] - ac[h], -jnp.inf))
                dxd_h.append(_dot((cbt_mat * lmt).astype(BF16), dyb, 1, 0))
                tdec_h = k.half_rowsum(t_dec, h)
                da_h = k.half_rowsum(t_off, h) - tdec_h + jnp.sum(dseg, axis=1, keepdims=True)
                da_col = da_col + jnp.where(k.lane_q == r, da_h, 0.0)
                da_row = da_row - jnp.where(k.sub_8 == r, jnp.sum(dseg, axis=0, keepdims=True), 0.0)
                keep_rows = k.first_rows if h == 0 else jnp.logical_not(k.first_rows)
                dal_h = jnp.sum(tdec_h, axis=0, keepdims=True) + jnp.sum(jnp.where(keep_rows, t_state, 0.0), axis=0, keepdims=True)
                d_alast = d_alast + jnp.where(k.lane_1 == r, dal_h, 0.0)
                keep_1 = k.lane_1 < HEAD_DIM if h == 0 else k.lane_1 >= HEAD_DIM
                dd_h = jnp.sum(jnp.where(keep_1, dd_lane, 0.0), axis=1, keepdims=True)
                d_dskip = d_dskip + jnp.where(k.lane_1 == r, dd_h, 0.0)
            dxd = dxd + jnp.where(k.first_half, dxd_h[0], dxd_h[1])
            dx_ref[:, sl] = dxp + dt_p * dxd
            t_dt = dxd * xp
            for h in range(2):
                ddt_x = ddt_x + jnp.where(k.lane_q == heads[h], k.half_rowsum(t_dt, h), 0.0)
            ds_ref[sl, :] = ds_in
        d_cb_b = d_cb.astype(BF16)
        db_ref[...] = d_b + _dot(d_cb_b, k.cb, 0, 0)
        dc_ref[...] = d_c + _dot(d_cb_b, k.bb, 1, 0)
        rc_col = _exact_left(k.upper.astype(BF16), da_col) + d_alast
        rc_row = _exact_right_t(da_row, k.upper.astype(BF16))
        ddtc_ref[...] = ddt_x + k.arow * rc_col
        ddtr_ref[...] = k.acol * rc_row
        dar_ref[...] += jnp.sum(rc_col * k.dtc, axis=0, keepdims=True)
        dac_ref[...] += jnp.sum(rc_row * k.dtr, axis=1, keepdims=True)
        dd_ref[...] += d_dskip

    rc = lambda c: nc - 1 - c
    return pl.pallas_call(
        body, name=name, grid=(g_cnt, nc),
        in_specs=[
            pl.BlockSpec((q, rp), lambda g, c: (rc(c), g)),
            pl.BlockSpec((q, n), lambda g, c: (rc(c), b_off + g)),
            pl.BlockSpec((q, n), lambda g, c: (rc(c), b_off + g_cnt + g)),
            pl.BlockSpec((None, q, LANES), lambda g, c: (g, rc(c), 0)),
            pl.BlockSpec((None, SUBLANES, q), lambda g, c: (g, 0, rc(c))),
            pl.BlockSpec((None, 1, LANES), lambda g, c: (g, 0, 0)),
            pl.BlockSpec((None, SUBLANES, 1), lambda g, c: (g, 0, 0)),
            pl.BlockSpec((None, 1, LANES), lambda g, c: (g, 0, 0)),
            pl.BlockSpec((None, None, rp, n), lambda g, c: (rc(c), g, 0, 0)),
            pl.BlockSpec((q, rp), lambda g, c: (rc(c), g)),
        ],
        out_specs=[
            pl.BlockSpec((q, rp), lambda g, c: (rc(c), g)),
            pl.BlockSpec((q, n), lambda g, c: (rc(c), g)),
            pl.BlockSpec((q, n), lambda g, c: (rc(c), g)),
            pl.BlockSpec((None, q, LANES), lambda g, c: (g, rc(c), 0)),
            pl.BlockSpec((None, SUBLANES, q), lambda g, c: (g, 0, rc(c))),
            pl.BlockSpec((None, 1, LANES), lambda g, c: (g, 0, 0)),
            pl.BlockSpec((None, SUBLANES, 1), lambda g, c: (g, 0, 0)),
            pl.BlockSpec((None, 1, LANES), lambda g, c: (g, 0, 0)),
        ],
        out_shape=[
            jax.ShapeDtypeStruct((rows, d_inner), F32),
            jax.ShapeDtypeStruct((rows, g_cnt * n), F32),
            jax.ShapeDtypeStruct((rows, g_cnt * n), F32),
            jax.ShapeDtypeStruct((g_cnt, rows, LANES), F32),
            jax.ShapeDtypeStruct((g_cnt, SUBLANES, rows), F32),
            jax.ShapeDtypeStruct((g_cnt, 1, LANES), F32),
            jax.ShapeDtypeStruct((g_cnt, SUBLANES, 1), F32),
            jax.ShapeDtypeStruct((g_cnt, 1, LANES), F32),
        ],
        scratch_shapes=[pltpu.VMEM((rp, n), F32)],
        compiler_params=_params(("parallel", "arbitrary")),
    )(xbc, xbc, xbc, dt_col, dt_row, a_row, a_col, d_row, sprev, dy)


def gatenorm_fwd(name, y, zx, w, d_inner):
    rows = y.shape[0]
    tr, gw = ROW_TILE, d_inner // SSD_GROUPS

    def body(y_ref, z_ref, w_ref, o_ref, ot_ref):
        z = z_ref[...]
        v = y_ref[...] * (z * _sigmoid(z))
        out = v * lax.rsqrt(jnp.mean(v * v, axis=-1, keepdims=True) + EPS) * w_ref[...]
        o_ref[...] = out.astype(o_ref.dtype)
        ot_ref[...] = out.T.astype(ot_ref.dtype)

    blk = pl.BlockSpec((tr, gw), lambda i, j: (i, j))
    return pl.pallas_call(
        body, name=name, grid=(rows // tr, SSD_GROUPS),
        in_specs=[blk, blk, pl.BlockSpec((1, gw), lambda i, j: (0, j))],
        out_specs=[blk, pl.BlockSpec((gw, tr), lambda i, j: (j, i))],
        out_shape=[jax.ShapeDtypeStruct((rows, d_inner), BF16), jax.ShapeDtypeStruct((d_inner, rows), BF16)],
        compiler_params=_params(("parallel", "parallel")),
    )(y, zx, w.reshape(1, d_inner))


def gatenorm_bwd(name, y, zx, w, dyn, d_inner, after):
    rows = y.shape[0]
    tr, gw = ROW_TILE, d_inner // SSD_GROUPS

    def body(y_ref, z_ref, w_ref, dn_ref, after_ref, dy_ref, dz_ref, dw_ref):
        z = z_ref[...]
        yv = y_ref[...]
        s = _sigmoid(z)
        gate = z * s
        v = yv * gate
        r = lax.rsqrt(jnp.mean(v * v, axis=-1, keepdims=True) + EPS)
        vh = v * r
        dn = dn_ref[...]
        dvh = dn * w_ref[...]
        dv = r * (dvh - vh * jnp.mean(dvh * vh, axis=-1, keepdims=True))
        dy_ref[...] = dv * gate
        dz_ref[...] = (dv * yv * (s * (1.0 + z * (1.0 - s)))).astype(dz_ref.dtype)

        @pl.when(pl.program_id(1) == 0)
        def _():
            dw_ref[...] = jnp.zeros_like(dw_ref)

        dw_ref[...] += jnp.sum(dn * vh, axis=0, keepdims=True)

    blk = pl.BlockSpec((tr, gw), lambda j, i: (i, j))
    wblk = pl.BlockSpec((1, gw), lambda j, i: (0, j))
    dy, dz, dw = pl.pallas_call(
        body, name=name, grid=(SSD_GROUPS, rows // tr),
        in_specs=[blk, blk, wblk, blk, pl.BlockSpec((SUBLANES, LANES), lambda j, i: (0, 0))], out_specs=[blk, blk, wblk],
        out_shape=[jax.ShapeDtypeStruct((rows, d_inner), F32), jax.ShapeDtypeStruct((rows, d_inner), BF16),
                   jax.ShapeDtypeStruct((1, d_inner), F32)],
        compiler_params=_params(("parallel", "arbitrary")),
    )(y, zx, w.reshape(1, d_inner), dyn, after)
    return dy, dz, dw.reshape(d_inner)


def _pool_count(rows, g):
    t1 = lax.broadcasted_iota(jnp.int32, (rows, 1), 0) - (PAD_FRONT - 1)
    win = jnp.left_shift(jnp.int32(POOL_WINDOWS[0]), g)
    return jnp.clip(t1, 1, win).astype(F32)


def _pool_select(levels, g):
    out = levels[-1]
    for i in range(len(levels) - 2, -1, -1):
        out = jnp.where(g == i, levels[i], out)
    return out


def pool_sub(name, u, transpose):
    rows, d = u.shape
    gd = d // len(POOL_WINDOWS)
    assert all(w == POOL_WINDOWS[0] << i for i, w in enumerate(POOL_WINDOWS))

    def body(u_ref, o_ref):
        g = pl.program_id(0)
        v = u_ref[...].astype(F32)
        cnt = _pool_count(rows, g)
        mask = _row_mask(v.shape, 0)
        s = v / cnt if transpose else v
        levels = []
        for i in range(len(POOL_WINDOWS)):
            step = 1 << i
            s = s + pltpu.roll(s, (rows - step) if transpose else step, axis=0)
            levels.append(s)
        sel = _pool_select(levels, g)
        out = (sel - v) if transpose else (sel / cnt - v)
        o_ref[...] = jnp.where(mask, out, 0.0).astype(o_ref.dtype)

    tc = POOL_TILE
    per = gd // tc
    blk = pl.BlockSpec((rows, tc), lambda g, j: (0, g * per + j))
    return pl.pallas_call(
        body, name=name, grid=(len(POOL_WINDOWS), per), in_specs=[blk], out_specs=blk,
        out_shape=jax.ShapeDtypeStruct((rows, d), F32 if transpose else BF16),
        compiler_params=_params(("parallel", "parallel")),
    )(u)


def pool_proj_fwd(name, mixed, w, b, scale):
    rows, d = mixed.shape
    ng = len(POOL_WINDOWS)
    gd = d // ng
    tr = ROW_TILE

    def body(m_ref, w_ref, b_ref, s_ref, pre_ref, mix_ref):
        pre = _dot(m_ref[...], w_ref[...], 1, 0) + b_ref[...]
        pre_ref[...] = pre
        mix_ref[...] = jnp.where(_row_mask(pre.shape, pl.program_id(1) * tr), pre * s_ref[...], 0.0)

    blk = pl.BlockSpec((tr, gd), lambda g, i: (i, g))
    vec = pl.BlockSpec((1, gd), lambda g, i: (0, g))
    return pl.pallas_call(
        body, name=name, grid=(ng, rows // tr),
        in_specs=[blk, pl.BlockSpec((None, gd, gd), lambda g, i: (g, 0, 0)), vec, vec], out_specs=[blk, blk],
        out_shape=[jax.ShapeDtypeStruct((rows, d), F32), jax.ShapeDtypeStruct((rows, d), F32)],
        compiler_params=_params(("parallel", "parallel")),
    )(mixed, w, b.reshape(1, d), scale.reshape(1, d))


def pool_proj_bwd(name, dmix, pre, mixed, w, scale):
    rows, d = dmix.shape
    ng = len(POOL_WINDOWS)
    gd = d // ng
    tr = ROW_TILE

    def body(dm_ref, pre_ref, mx_ref, w_ref, s_ref, dmx_ref, dw_ref, db_ref, ds_ref):
        @pl.when(pl.program_id(1) == 0)
        def _():
            dw_ref[...] = jnp.zeros_like(dw_ref)
            db_ref[...] = jnp.zeros_like(db_ref)
            ds_ref[...] = jnp.zeros_like(ds_ref)

        dmv = jnp.where(_row_mask(dm_ref.shape, pl.program_id(1) * tr), dm_ref[...], 0.0)
        dpre = dmv * s_ref[...]
        dpre_b = dpre.astype(BF16)
        ds_ref[...] += jnp.sum(dmv * pre_ref[...], axis=0, keepdims=True)
        db_ref[...] += jnp.sum(dpre, axis=0, keepdims=True)
        dmx_ref[...] = _dot(dpre_b, w_ref[...], 1, 1)
        dw_ref[...] += _dot(mx_ref[...], dpre_b, 0, 0)

    blk = pl.BlockSpec((tr, gd), lambda g, i: (i, g))
    vec = pl.BlockSpec((1, gd), lambda g, i: (0, g))
    wblk = pl.BlockSpec((None, gd, gd), lambda g, i: (g, 0, 0))
    dmixed, dw, db, ds = pl.pallas_call(
        body, name=name, grid=(ng, rows // tr),
        in_specs=[blk, blk, blk, wblk, vec], out_specs=[blk, wblk, vec, vec],
        out_shape=[jax.ShapeDtypeStruct((rows, d), F32), jax.ShapeDtypeStruct((ng, gd, gd), F32),
                   jax.ShapeDtypeStruct((1, d), F32), jax.ShapeDtypeStruct((1, d), F32)],
        compiler_params=_params(("parallel", "arbitrary")),
    )(dmix, pre, mixed, w, scale.reshape(1, d))
    return dmixed, dw, db.reshape(d), ds.reshape(d)


def _my_place():
    return lax.axis_index("x"), lax.axis_index("y"), lax.axis_index("c")


def _linear(place):
    return 4 * place[0] + 2 * place[1] + place[2]


def all_gather(name, shards):
    n_ops = len(shards)

    def body(*refs):
        ins, outs = refs[:n_ops], refs[n_ops:2 * n_ops]
        send_sems, recv_sems, local_sems = refs[2 * n_ops:]
        x, y, c = _my_place()
        me, sibling = (x, y, c), (x, y, 1 - c)
        chips = [(1 - x, y), (x, 1 - y), (1 - x, 1 - y)]

        def copy(t, k, block, to, src=None):
            dst = outs[t].at[_linear(block)]
            return pltpu.make_async_remote_copy(
                src_ref=dst if src is None else src, dst_ref=dst, send_sem=send_sems.at[t, k], recv_sem=recv_sems.at[t, k],
                device_id=to, device_id_type=MESH)

        mine = [pltpu.make_async_copy(ins[t], outs[t].at[_linear(me)], local_sems.at[t]) for t in range(n_ops)]
        for cp in mine:
            cp.start()
        first = []
        for t in range(n_ops):
            first.append(copy(t, 0, me, sibling, src=ins[t]))
            first += [copy(t, 1 + j, me, (*chip, c), src=ins[t]) for j, chip in enumerate(chips)]
        for cp in first:
            cp.start()
        passed = []
        for j, chip in enumerate(chips):
            for t in range(n_ops):
                copy(t, 1 + j, (*chip, c), me).wait_recv()
                fwd = copy(t, 4 + j, (*chip, c), sibling)
                fwd.start()
                passed.append(fwd)
        for t in range(n_ops):
            copy(t, 0, sibling, me).wait_recv()
            for j, chip in enumerate(chips):
                copy(t, 4 + j, (*chip, 1 - c), me).wait_recv()
        for cp in first + passed:
            cp.wait_send()
        for cp in mine:
            cp.wait()

    any_spec = pl.BlockSpec(memory_space=pl.ANY)
    outs = pl.pallas_call(
        body, name=name,
        in_specs=[any_spec] * n_ops, out_specs=[any_spec] * n_ops,
        out_shape=[jax.ShapeDtypeStruct((N_DEV, *s.shape), s.dtype) for s in shards],
        scratch_shapes=[pltpu.SemaphoreType.DMA((n_ops, 7)), pltpu.SemaphoreType.DMA((n_ops, 7)),
                        pltpu.SemaphoreType.DMA((n_ops,))],
    )(*shards)
    return list(outs)


_HBM = pl.BlockSpec(memory_space=pltpu.HBM)
_SEM = pl.BlockSpec(memory_space=pltpu.SEMAPHORE)
_EFFECT = pltpu.SideEffectType.DATAFLOW_SIDE_EFFECTING


def _hbm(a):
    return pltpu.with_memory_space_constraint(a, pltpu.HBM)


def _peers():
    x, y, c = _my_place()
    return [(x ^ (j >> 2), y ^ ((j >> 1) & 1), c ^ (j & 1)) for j in range(1, N_DEV)]


def exchange_start(name, groups, after):
    flat = [e for g in groups for e in g]
    n = len(flat)

    def body(*refs):
        srcs, lands = refs[:n], refs[n:2 * n]
        outs = refs[2 * n + 1:]
        sends, recvs, token = outs[:n], outs[n:2 * n], outs[4 * n]
        me = _linear(_my_place())
        for t, (_, _, src_view, land_view) in enumerate(flat):
            for peer in _peers():
                pltpu.make_async_remote_copy(
                    src_ref=src_view(srcs[t], _linear(peer)), dst_ref=land_view(lands[t], me),
                    send_sem=sends[t], recv_sem=recvs[t], device_id=peer, device_id_type=MESH).start()
        token[...] = jnp.zeros_like(token)

    sem_shapes = [pltpu.SemaphoreType.DMA(())] * (2 * n)
    thru = [pltpu.HBM(e[0].shape, e[0].dtype) for e in flat] + [pltpu.HBM(e[1].shape, e[1].dtype) for e in flat]
    outs = pl.pallas_call(
        body, name=name,
        out_shape=(*sem_shapes, *thru, jax.ShapeDtypeStruct((SUBLANES, LANES), F32)),
        in_specs=[*[_HBM] * (2 * n), pl.BlockSpec(memory_space=pl.ANY)],
        out_specs=(*[_SEM] * (2 * n), *[_HBM] * (2 * n), pl.BlockSpec(memory_space=pltpu.VMEM)),
        input_output_aliases={t: 2 * n + t for t in range(2 * n)},
        compiler_params=pltpu.CompilerParams(has_side_effects=_EFFECT),
    )(*[_hbm(e[0]) for e in flat], *[_hbm(e[1]) for e in flat], after)
    records, t = [], 0
    for g in groups:
        k = len(g)
        records.append((list(outs[t:t + k]), list(outs[n + t:n + t + k]), list(outs[2 * n + t:2 * n + t + k]),
                        list(outs[3 * n + t:3 * n + t + k])))
        t += k
    return records, outs[-1]


def exchange_wait(name, records, lands, land_of, seven_of, after):
    srcs = [s for r in records for s in r[2]]
    sends = [s for r in records for s in r[0]]
    recvs = [s for r in records for s in r[1]]
    where = [(ri, k) for ri, r in enumerate(records) for k in range(len(r[2]))]
    ns, nl = len(srcs), len(lands)

    def body(*refs):
        src_refs, land_refs = refs[:ns], refs[ns:ns + nl]
        send_refs, recv_refs = refs[ns + nl:2 * ns + nl], refs[2 * ns + nl:3 * ns + nl]
        for t, (ri, k) in enumerate(where):
            seven_ref = seven_of[ri][k](src_refs[t], land_refs[land_of[ri][k]])
            cp = pltpu.make_async_remote_copy(src_ref=seven_ref, dst_ref=seven_ref, send_sem=send_refs[t], recv_sem=recv_refs[t],
                                              device_id=_my_place(), device_id_type=MESH)
            cp.wait_send()
            cp.wait_recv()

    outs = pl.pallas_call(
        body, name=name,
        out_shape=tuple(pltpu.HBM(a.shape, a.dtype) for a in (*srcs, *lands)),
        in_specs=[*[_HBM] * (ns + nl), *[_SEM] * (2 * ns), pl.BlockSpec(memory_space=pl.ANY)], out_specs=tuple([_HBM] * (ns + nl)),
        input_output_aliases={t: t for t in range(ns + nl)},
        compiler_params=pltpu.CompilerParams(has_side_effects=_EFFECT),
    )(*srcs, *lands, *sends, *recvs, after)
    return list(outs[:ns]), list(outs[ns:])


def _other_chips():
    x, y, c = _my_place()
    return [(1 - x, y, c), (x, 1 - y, c), (1 - x, 1 - y, c)]


def gather2_start(name, groups, after):
    flat = [e for g in groups for e in g]
    n = len(flat)

    def body(*refs):
        srcs, lands = refs[:n], refs[n:2 * n]
        outs = refs[2 * n + 1:]
        d_send, d_recv, i_send, i_recv, token = outs[:n], outs[n:2 * n], outs[2 * n:3 * n], outs[3 * n:4 * n], outs[6 * n]
        x, y, c = _my_place()
        me = _linear((x, y, c))
        for t in range(n):
            pltpu.make_async_remote_copy(src_ref=srcs[t], dst_ref=lands[t].at[me], send_sem=d_send[t], recv_sem=d_recv[t],
                                         device_id=(x, y, 1 - c), device_id_type=MESH).start()
            for peer in _other_chips():
                pltpu.make_async_remote_copy(src_ref=srcs[t], dst_ref=lands[t].at[me], send_sem=i_send[t], recv_sem=i_recv[t],
                                             device_id=peer, device_id_type=MESH).start()
        token[...] = jnp.zeros_like(token)

    thru = [pltpu.HBM(e[0].shape, e[0].dtype) for e in flat] + [pltpu.HBM(e[1].shape, e[1].dtype) for e in flat]
    outs = pl.pallas_call(
        body, name=name,
        out_shape=(*[pltpu.SemaphoreType.DMA(())] * (4 * n), *thru, jax.ShapeDtypeStruct((SUBLANES, LANES), F32)),
        in_specs=[*[_HBM] * (2 * n), pl.BlockSpec(memory_space=pl.ANY)],
        out_specs=(*[_SEM] * (4 * n), *[_HBM] * (2 * n), pl.BlockSpec(memory_space=pltpu.VMEM)),
        input_output_aliases={t: 4 * n + t for t in range(2 * n)},
        compiler_params=pltpu.CompilerParams(has_side_effects=_EFFECT),
    )(*[_hbm(e[0]) for e in flat], *[_hbm(e[1]) for e in flat], after)
    records, t = [], 0
    for g in groups:
        k = len(g)
        records.append({"d_send": list(outs[t:t + k]), "d_recv": list(outs[n + t:n + t + k]),
                        "i_send": list(outs[2 * n + t:2 * n + t + k]), "i_recv": list(outs[3 * n + t:3 * n + t + k]),
                        "srcs": list(outs[4 * n + t:4 * n + t + k]), "lands": list(outs[5 * n + t:5 * n + t + k])})
        t += k
    return records, outs[-1]


def gather2_relay(name, rec, after):
    lands, i_recv = rec["lands"], rec["i_recv"]
    n = len(lands)

    def body(*refs):
        land_refs, i_recv_refs = refs[:n], refs[n:2 * n]
        outs = refs[2 * n + 1:]
        f_send, f_recv, token = outs[n:2 * n], outs[2 * n:3 * n], outs[3 * n]
        x, y, c = _my_place()
        for t in range(n):
            three = land_refs[t].at[pl.ds(0, 3)]
            pltpu.make_async_remote_copy(src_ref=three, dst_ref=three, send_sem=f_send[t], recv_sem=i_recv_refs[t],
                                         device_id=(x, y, c), device_id_type=MESH).wait_recv()
            for peer in _other_chips():
                blk = land_refs[t].at[_linear(peer)]
                pltpu.make_async_remote_copy(src_ref=blk, dst_ref=blk, send_sem=f_send[t], recv_sem=f_recv[t],
                                             device_id=(x, y, 1 - c), device_id_type=MESH).start()
        token[...] = jnp.zeros_like(token)

    outs = pl.pallas_call(
        body, name=name,
        out_shape=(*[pltpu.HBM(a.shape, a.dtype) for a in lands], *[pltpu.SemaphoreType.DMA(())] * (2 * n),
                   jax.ShapeDtypeStruct((SUBLANES, LANES), F32)),
        in_specs=[*[_HBM] * n, *[_SEM] * n, pl.BlockSpec(memory_space=pl.ANY)],
        out_specs=(*[_HBM] * n, *[_SEM] * (2 * n), pl.BlockSpec(memory_space=pltpu.VMEM)),
        input_output_aliases={t: t for t in range(n)},
        compiler_params=pltpu.CompilerParams(has_side_effects=_EFFECT),
    )(*lands, *i_recv, after)
    return {**rec, "lands": list(outs[:n]), "f_send": list(outs[n:2 * n]), "f_recv": list(outs[2 * n:3 * n])}, outs[-1]


def gather2_wait(name, rec, after):
    n = len(rec["lands"])
    sem_names = ("d_send", "d_recv", "i_send", "f_send", "f_recv")

    def body(*refs):
        src_refs, land_refs = refs[:n], refs[n:2 * n]
        sems = {nm: refs[2 * n + k * n:2 * n + (k + 1) * n] for k, nm in enumerate(sem_names)}
        me = _my_place()
        for t in range(n):
            one, three = land_refs[t].at[0], land_refs[t].at[pl.ds(0, 3)]
            to_sibling = pltpu.make_async_remote_copy(src_ref=src_refs[t], dst_ref=one, send_sem=sems["d_send"][t],
                                                      recv_sem=sems["d_recv"][t], device_id=me, device_id_type=MESH)
            to_sibling.wait_send()
            to_sibling.wait_recv()
            pltpu.make_async_remote_copy(src_ref=three, dst_ref=three, send_sem=sems["i_send"][t], recv_sem=sems["d_recv"][t],
                                         device_id=me, device_id_type=MESH).wait_send()
            forwarded = pltpu.make_async_remote_copy(src_ref=three, dst_ref=three, send_sem=sems["f_send"][t],
                                                     recv_sem=sems["f_recv"][t], device_id=me, device_id_type=MESH)
            forwarded.wait_send()
            forwarded.wait_recv()

    arrays = (*rec["srcs"], *rec["lands"])
    outs = pl.pallas_call(
        body, name=name,
        out_shape=tuple(pltpu.HBM(a.shape, a.dtype) for a in arrays),
        in_specs=[*[_HBM] * (2 * n), *[_SEM] * (5 * n), *[pl.BlockSpec(memory_space=pl.ANY)] * 2], out_specs=tuple([_HBM] * (2 * n)),
        input_output_aliases={t: t for t in range(2 * n)},
        compiler_params=pltpu.CompilerParams(has_side_effects=_EFFECT),
    )(*arrays, *[s for nm in sem_names for s in rec[nm]], *after)
    return list(outs[:n]), list(outs[n:])


def _seven_slots_of_land(src_ref, land_ref):
    return land_ref.at[pl.ds(0, N_DEV - 1)]


def _seven_blocks_of_src(src_ref, land_ref):
    return src_ref.at[pl.ds(0, N_DEV - 1)]


def _whole(ref, dev):
    return ref


def _slot(ref, dev):
    return ref.at[dev]


def _slot_of_layer(layer):
    return lambda ref, dev: ref.at[dev, layer]


def _as_rows(a, lead=0):
    return a.reshape(a.shape[:lead] + (-1, a.shape[-1]))


def sum_adamw(name, parts, w, m, v, half=None):
    shape = w.shape
    c1 = 1.0 - ADAM_B1 ** ADAM_STEP
    c2 = 1.0 - ADAM_B2 ** ADAM_STEP
    if half is not None:
        layers, r, cols = shape
        tr = _tile(r, ROW_TILE, 16)
        grid = (layers, r // tr)
        p_spec = pl.BlockSpec((N_DEV, None, None, tr, cols), lambda l, i: (0, l, half, i, 0))
        blk = pl.BlockSpec((None, tr, cols), lambda l, i: (l, i, 0))
        args, out_sds, sem = (parts, w, m, v), jax.ShapeDtypeStruct(shape, F32), ("parallel", "parallel")
    else:
        p2, w2, m2, v2 = _as_rows(parts, 1), _as_rows(w), _as_rows(m), _as_rows(v)
        rows, cols = w2.shape
        tr = _tile(rows, ROW_TILE, 16)
        grid = (rows // tr,)
        p_spec = pl.BlockSpec((N_DEV, tr, cols), lambda i: (0, i, 0))
        blk = pl.BlockSpec((tr, cols), lambda i: (i, 0))
        args, out_sds, sem = (p2, w2, m2, v2), jax.ShapeDtypeStruct((rows, cols), F32), ("parallel",)

    def body(p_ref, w_ref, m_ref, v_ref, g_ref, d_ref, nm_ref, nv_ref):
        g = p_ref[0].astype(F32)
        for k in range(1, N_DEV):
            g = g + p_ref[k].astype(F32)
        wv = w_ref[...]
        nm = ADAM_B1 * m_ref[...] + (1.0 - ADAM_B1) * g
        nv = ADAM_B2 * v_ref[...] + (1.0 - ADAM_B2) * (g * g)
        g_ref[...] = g
        nm_ref[...] = nm
        nv_ref[...] = nv
        d_ref[...] = -ADAM_LR * ((nm / c1) / (jnp.sqrt(nv / c2) + ADAM_EPS) + ADAM_WD * wv)

    outs = pl.pallas_call(
        body, name=name, grid=grid, in_specs=[p_spec, blk, blk, blk], out_specs=[blk] * 4,
        out_shape=[out_sds] * 4, compiler_params=_params(sem),
    )(*args)
    return [o.reshape(shape) for o in outs]


def _unblock_cols(g):
    g = jnp.moveaxis(g, 0, -2)
    return g.reshape(g.shape[:-2] + (g.shape[-2] * g.shape[-1],))


def _block_cols(a):
    r, c = a.shape
    return jnp.moveaxis(a.reshape(r, N_DEV, c // N_DEV), 1, 0)


def _my_cols(a, n):
    me = _linear(_my_place())
    return lax.dynamic_slice_in_dim(a, me * n, n, axis=a.ndim - 1)


def _pack(arrays):
    flat = jnp.concatenate([a.reshape(-1).astype(F32) for a in arrays])
    pad = (-flat.shape[0]) % (ROW_TILE * LANES)
    return jnp.pad(flat, (0, pad)).reshape(-1, LANES)


def _unpack(packed, like):
    flat = packed.reshape(-1)
    out, pos = [], 0
    for a in like:
        out.append(flat[pos:pos + a.size].reshape(a.shape))
        pos += a.size
    return out


def kernel(x, meta_tokens, norm_w, ssd_w_in, ssd_conv_w, ssd_conv_b, ssd_dt_bias, ssd_a_log, ssd_d, ssd_norm_w, ssd_w_out, pool_w, pool_b, pool_scale, ffn_w_gate, ffn_w_up, ffn_w_down, loss_target, m_meta_tokens, m_norm_w, m_ssd_w_in, m_ssd_conv_w, m_ssd_conv_b, m_ssd_dt_bias, m_ssd_a_log, m_ssd_d, m_ssd_norm_w, m_ssd_w_out, m_pool_w, m_pool_b, m_pool_scale, m_ffn_w_gate, m_ffn_w_up, m_ffn_w_down, v_meta_tokens, v_norm_w, v_ssd_w_in, v_ssd_conv_w, v_ssd_conv_b, v_ssd_dt_bias, v_ssd_a_log, v_ssd_d, v_ssd_norm_w, v_ssd_w_out, v_pool_w, v_pool_b, v_pool_scale, v_ffn_w_gate, v_ffn_w_up, v_ffn_w_down):
    seq, d = x.shape[1], x.shape[2]
    depth = norm_w.shape[0]
    n_ssd = ssd_w_in.shape[0]
    d_inner = ssd_norm_w.shape[1]
    heads = d_inner // HEAD_DIM
    rpg = heads // SSD_GROUPS
    conv_dim = ssd_conv_b.shape[1]
    zx_cols = d_inner + conv_dim
    d_in_proj = zx_cols + heads
    rows = PAD_FRONT + N_META + seq
    assert rows % CHUNK == 0 and rpg % 2 == 0 and heads <= LANES and rpg <= SUBLANES

    hidden = ffn_w_down.shape[1] * N_DEV
    n_pool = pool_w.shape[0]

    small = all_gather("gather_small", [meta_tokens, norm_w, ssd_conv_w, pool_b, pool_scale])
    meta_f, norm_f, convw_f, poolb_f, pools_f = [_unblock_cols(s) for s in small]

    gather_recs = {}

    def start_gather(i, after):
        def entry(shard):
            return (shard.astype(BF16), lax.empty((N_DEV, *shard.shape), BF16))
        mixer = [ssd_w_in[i // 2], ssd_w_out[i // 2]] if i % 2 == 0 else [pool_w[i // 2]]
        gate_up = jnp.stack([ffn_w_gate[i], ffn_w_up[i]])
        groups = [*[[entry(w)] for w in mixer], [entry(gate_up), entry(ffn_w_down[i])]]
        gather_recs[i], tok = gather2_start(f"gather_start_{i}", groups, after)
        return tok

    gather_order = [(i, g) for i in range(depth) for g in range(3 if i % 2 == 0 else 2)]
    relayed = [0, None]

    def gathered(i, g, after):
        upto = min(gather_order.index((i, g)) + 2, len(gather_order))
        while relayed[0] < upto:
            li, lg = gather_order[relayed[0]]
            gather_recs[li][lg], relayed[1] = gather2_relay(f"gather_relay_{li}_{lg}", gather_recs[li][lg], after)
            relayed[0] += 1
        srcs, lands_ = gather2_wait(f"gather_wait_{i}_{g}", gather_recs[i][g], (after, relayed[1]))
        me = _linear(_my_place())
        return [lax.dynamic_update_slice_in_dim(land, src[None], me, axis=0) for src, land in zip(srcs, lands_)]

    def gathered_small(name, rec, after):
        srcs, lands_ = exchange_wait(name, [rec], rec[3], [[0]], [[_seven_slots_of_land]], after)
        return lax.dynamic_update_slice_in_dim(lands_[0], srcs[0][None], _linear(_my_place()), axis=0)

    pending = meta_f
    for i in range(min(GATHER_AHEAD, depth)):
        pending = start_gather(i, pending)

    pad_h = lambda a: jnp.pad(a.astype(F32), ((0, 0), (0, LANES - heads)))
    bias_pad = pad_h(ssd_dt_bias)

    def head_layouts(vec):
        g = vec.reshape(SSD_GROUPS, rpg)
        row = jnp.pad(g, ((0, 0), (0, LANES - rpg)))[:, None, :]
        col = jnp.pad(g, ((0, 0), (0, SUBLANES - rpg)))[:, :, None]
        return row, col

    def dt_layouts(dt):
        g = dt[:, :heads].reshape(rows, SSD_GROUPS, rpg)
        col = jnp.pad(jnp.moveaxis(g, 1, 0), ((0, 0), (0, 0), (0, LANES - rpg)))
        row = jnp.pad(jnp.transpose(g, (1, 2, 0)), ((0, 0), (0, SUBLANES - rpg), (0, 0)))
        return col, row

    h = jnp.concatenate([jnp.zeros((PAD_FRONT, d), F32), meta_f, x[0]], axis=0)
    saved = []
    for i in range(depth):
        j = i // 2
        s = {"h": h}
        w_pre_mix = norm_f[i, 0]
        if i + GATHER_AHEAD < depth:
            pending = start_gather(i + GATHER_AHEAD, pending if i == 0 else h)
        (mixer_w,) = gathered(i, 0, h)
        if i % 2 == 0:
            w_in = jnp.pad(_unblock_cols(mixer_w), ((0, 0), (0, LANES - heads)))
            u, u_t = rmsnorm_fwd(f"norm_pre_mix_{i}", h, w_pre_mix, out_dtype=BF16, transposed=True, after=pending)
            zx = matmul(f"ssd_in_{i}", u, w_in, **MM_ROWS_RESIDENT, tn=384 if w_in.shape[1] % 384 == 0 else 512)
            xbc = conv_fwd(f"ssd_conv_{i}", zx, convw_f[j], ssd_conv_b[j], d_inner, conv_dim)
            dt = dt_fwd(f"ssd_dt_{i}", zx, bias_pad[j:j + 1], zx_cols)
            dt_col, dt_row = dt_layouts(dt)
            a_neg = -jnp.exp(ssd_a_log[j].astype(F32))
            a_row, a_col = head_layouts(a_neg)
            d_row, _ = head_layouts(ssd_d[j].astype(F32))
            y, sprev = ssd_scan_fwd(f"ssd_scan_{i}", xbc, dt_col, dt_row, a_row, a_col, d_row, d_inner)
            yn, yn_t = gatenorm_fwd(f"ssd_gate_{i}", y, zx, ssd_norm_w[j], d_inner)
            (w_out,) = gathered(i, 1, yn)
            w_out = w_out.reshape(d_inner, d)
            mix = matmul(f"ssd_out_{i}", yn, w_out, **MM_DEEP)
            s.update(u_t=u_t, zx=zx, xbc=xbc, dt_col=dt_col, dt_row=dt_row, a_row=a_row, a_col=a_col, d_row=d_row,
                     a_neg=a_neg, y=y, sprev=sprev, yn_t=yn_t, w_in=w_in, w_out=w_out)
        else:
            w_pool = jnp.moveaxis(mixer_w, 0, 1).reshape(len(POOL_WINDOWS), d // len(POOL_WINDOWS), -1)
            u = rmsnorm_fwd(f"norm_pre_mix_{i}", h, w_pre_mix, after=pending)
            mixed = pool_sub(f"pool_sub_{i}", u, False)
            pre, mix = pool_proj_fwd(f"pool_proj_{i}", mixed, w_pool, poolb_f[j], pools_f[j])
            s.update(mixed=mixed, pre=pre, w_pool=w_pool)
        pending = None
        h1 = rmsnorm_fwd(f"norm_post_mix_{i}", mix, norm_f[i, 1], res=h)
        w_gu, w_down = gathered(i, 2 if i % 2 == 0 else 1, h1)
        u2, u2_t = rmsnorm_fwd(f"norm_pre_ffn_{i}", h1, norm_f[i, 2], out_dtype=BF16, transposed=True)
        gp = ffn_in_fwd(f"ffn_in_{i}", u2, w_gu)
        act, act_t = swiglu_fwd(f"ffn_act_{i}", gp)
        f = ffn_out_fwd(f"ffn_out_{i}", act, w_down)
        h = rmsnorm_fwd(f"norm_post_ffn_{i}", f, norm_f[i, 3], res=h1)
        s.update(mix=mix, h1=h1, u2_t=u2_t, gp=gp, act_t=act_t, f=f, w_gu=w_gu, w_down=w_down)
        saved.append(s)

    dh, loss_local = loss_head("loss_head", h, loss_target[0])
    loss = lax.psum(loss_local, AXES)

    g_norm = [[None] * 4 for _ in range(depth)]
    g_convw, g_convb, g_dtb, g_alog, g_dskip, g_ssdnorm = ([None] * n_ssd for _ in range(6))
    g_poolb, g_pools = ([None] * n_pool for _ in range(2))
    hid_s = hidden // N_DEV
    lands = {"in": lax.empty((N_DEV, *ssd_w_in.shape), BF16), "out": lax.empty((N_DEV, *ssd_w_out.shape), BF16),
             "pool": lax.empty((N_DEV, *pool_w.shape), BF16), "down": lax.empty((N_DEV, *ffn_w_down.shape), BF16),
             "gate_up": lax.empty((N_DEV, depth, 2, d, hid_s), BF16)}
    scatter_recs, scatter_keys, scatter_views, scatter_layers = [], [], [], []

    def scatter(name, blocks, layer, after):
        view = _slot_of_layer(layer)
        (rec,), tok = exchange_start(name, [[(b, lands[key], _slot, view) for key, b in blocks]], after)
        for (key, _), thru in zip(blocks, rec[3]):
            lands[key] = thru
        scatter_recs.append(rec)
        scatter_keys.append([key for key, _ in blocks])
        scatter_views.append([_seven_blocks_of_src] * len(blocks))
        scatter_layers.append(layer)
        return tok

    tok = None
    for i in reversed(range(depth)):
        j = i // 2
        s = saved[i]
        df, g_norm[i][3] = rmsnorm_bwd(f"norm_post_ffn_bwd_{i}", s["f"], norm_f[i, 3], dh, after=tok)
        df_b = df.astype(BF16)
        dact = ffn_out_bwd_x(f"ffn_out_bwd_x_{i}", df_b, s["w_down"])
        g_down = ffn_out_bwd_w(f"ffn_out_bwd_w_{i}", s["act_t"], df_b)
        dgp = swiglu_bwd(f"ffn_act_bwd_{i}", s["gp"], dact)
        g_gu = ffn_in_bwd_w(f"ffn_in_bwd_w_{i}", s["u2_t"], dgp)
        tok = scatter(f"scatter_start_ffn_{i}", [("gate_up", g_gu), ("down", g_down)], i, g_gu)
        du2 = ffn_in_bwd_x(f"ffn_in_bwd_x_{i}", dgp, s["w_gu"])
        dh1, g_norm[i][2] = rmsnorm_bwd(f"norm_pre_ffn_bwd_{i}", s["h1"], norm_f[i, 2], du2, add=dh, after=tok)
        dmix, g_norm[i][1] = rmsnorm_bwd(f"norm_post_mix_bwd_{i}", s["mix"], norm_f[i, 1], dh1)
        if i % 2 == 0:
            dmix_b = dmix.astype(BF16)
            dyn = matmul(f"ssd_out_bwd_x_{i}", dmix_b, s["w_out"], tb=True, **MM_ROWS_RESIDENT)
            g_out = matmul(f"ssd_out_bwd_w_{i}", s["yn_t"], dmix_b, out_dtype=BF16, **MM_COLS_RESIDENT)
            tok = scatter(f"scatter_start_out_{i}", [("out", g_out.reshape(N_DEV, d_inner // N_DEV, d))], j, g_out)
            dy, dz, g_ssdnorm[j] = gatenorm_bwd(f"ssd_gate_bwd_{i}", s["y"], s["zx"], ssd_norm_w[j], dyn, d_inner, after=tok)
            dx, db, dc, ddt_col, ddt_row, dar, dac, ddsk = ssd_scan_bwd(
                f"ssd_scan_bwd_{i}", s["xbc"], s["dt_col"], s["dt_row"], s["a_row"], s["a_col"], s["d_row"], s["sprev"], dy, d_inner)
            ddt = (jnp.moveaxis(ddt_col[:, :, :rpg], 0, 1).reshape(rows, heads)
                   + jnp.transpose(ddt_row[:, :rpg, :], (2, 0, 1)).reshape(rows, heads))
            d_a = (dar[:, 0, :rpg] + dac[:, :rpg, 0]).reshape(heads)
            g_alog[j] = d_a * s["a_neg"]
            g_dskip[j] = ddsk[:, 0, :rpg].reshape(heads)
            ddtr, dbias = dt_bwd(f"ssd_dt_bwd_{i}", s["zx"], bias_pad[j:j + 1], pad_h(ddt), zx_cols)
            g_dtb[j] = dbias[0, :heads]
            dxbc_raw, g_convw[j], dconvb = conv_bwd(
                f"ssd_conv_bwd_{i}", s["zx"], convw_f[j], ssd_conv_b[j], jnp.concatenate([dx, db, dc], axis=1), d_inner, conv_dim)
            g_convb[j] = dconvb[0]
            dzx = jnp.concatenate([dz, dxbc_raw, ddtr], axis=1)
            g_in = matmul(f"ssd_in_bwd_w_{i}", s["u_t"], dzx, out_dtype=BF16, **MM_ROWS_RESIDENT,
                          tn=384 if dzx.shape[1] % 384 == 0 else 512)
            tok = scatter(f"scatter_start_in_{i}", [("in", _block_cols(g_in[:, :d_in_proj]))], j, g_in)
            du = matmul(f"ssd_in_bwd_x_{i}", dzx, s["w_in"], tb=True, **{**MM_DEEP, "tk": 1152 if dzx.shape[1] % 1152 == 0 else 512})
        else:
            dmixed, g_poolw, g_poolb[j], g_pools[j] = pool_proj_bwd(
                f"pool_proj_bwd_{i}", dmix, s["pre"], s["mixed"], s["w_pool"], pools_f[j])
            ng, gd = g_poolw.shape[0], g_poolw.shape[1]
            blk_pool = jnp.moveaxis(g_poolw.astype(BF16).reshape(ng, N_DEV, gd // N_DEV, gd), 1, 0)
            tok = scatter(f"scatter_start_pool_{i}", [("pool", blk_pool)], j, blk_pool)
            du = pool_sub(f"pool_sub_bwd_{i}", dmixed, True)
        dh, g_norm[i][0] = rmsnorm_bwd(f"norm_pre_mix_bwd_{i}", s["h"], norm_f[i, 0], du, add=dh1, after=tok)

    grad_x = dh[PAD_FRONT + N_META:][None]
    g_meta = dh[PAD_FRONT:PAD_FRONT + N_META]

    small_grads = [g_meta, jnp.stack([jnp.stack(r) for r in g_norm]), jnp.stack(g_convw), jnp.stack(g_convb), jnp.stack(g_dtb),
                   jnp.stack(g_alog), jnp.stack(g_dskip), jnp.stack(g_ssdnorm), jnp.stack(g_poolb), jnp.stack(g_pools)]
    packed_g = _pack(small_grads)
    (small_rec,), _ = exchange_start(
        "gather_small_grads_start", [[(packed_g, lax.empty((N_DEV, *packed_g.shape), F32), _whole, _slot)]], packed_g)
    mine = lambda a: _my_cols(a, a.shape[-1] // N_DEV)
    small_w = [meta_tokens, norm_w, ssd_conv_w, ssd_conv_b, ssd_dt_bias, ssd_a_log, ssd_d, ssd_norm_w, pool_b, pool_scale]
    small_m = [m_meta_tokens, m_norm_w, m_ssd_conv_w, m_ssd_conv_b, m_ssd_dt_bias, m_ssd_a_log, m_ssd_d, m_ssd_norm_w, m_pool_b, m_pool_scale]
    small_v = [v_meta_tokens, v_norm_w, v_ssd_conv_w, v_ssd_conv_b, v_ssd_dt_bias, v_ssd_a_log, v_ssd_d, v_ssd_norm_w, v_pool_b, v_pool_scale]
    sharded = [True, True, True, False, False, False, False, False, True, True]
    def widen(a, is_sharded, full):
        if not is_sharded:
            return a
        return lax.dynamic_update_slice_in_dim(jnp.zeros(full.shape, F32), a, _linear(_my_place()) * a.shape[-1], axis=a.ndim - 1)
    packed_w = _pack([widen(a, sh, g) for a, sh, g in zip(small_w, sharded, small_grads)])
    packed_m = _pack([widen(a, sh, g) for a, sh, g in zip(small_m, sharded, small_grads)])
    packed_v = _pack([widen(a, sh, g) for a, sh, g in zip(small_v, sharded, small_grads)])

    def landed(name, keys, after):
        rs = [r for r, ks in enumerate(scatter_keys) if set(ks) <= set(keys)]
        srcs, out = exchange_wait(name, [scatter_recs[r] for r in rs], [lands[k] for k in keys],
                                  [[keys.index(k) for k in scatter_keys[r]] for r in rs], [scatter_views[r] for r in rs], after)
        me = _linear(_my_place())
        src_of = iter(srcs)
        for r in rs:
            for key in scatter_keys[r]:
                own = lax.dynamic_index_in_dim(next(src_of), me, axis=0, keepdims=True)[:, None]
                slot = keys.index(key)
                start = (me, scatter_layers[r]) + (0,) * (own.ndim - 2)
                out[slot] = lax.dynamic_update_slice(out[slot], own, start)
        return out

    p_gu, p_down = landed("scatter_wait_ffn", ["gate_up", "down"], packed_g)
    a_gate = sum_adamw("adamw_ffn_w_gate", p_gu, ffn_w_gate, m_ffn_w_gate, v_ffn_w_gate, half=0)
    a_up = sum_adamw("adamw_ffn_w_up", p_gu, ffn_w_up, m_ffn_w_up, v_ffn_w_up, half=1)
    a_down = sum_adamw("adamw_ffn_w_down", p_down, ffn_w_down, m_ffn_w_down, v_ffn_w_down)
    (p_pool,) = landed("scatter_wait_pool", ["pool"], a_down[0])
    a_pool = sum_adamw("adamw_pool_w", p_pool, pool_w, m_pool_w, v_pool_w)
    (p_out,) = landed("scatter_wait_out", ["out"], a_pool[0])
    a_out = sum_adamw("adamw_ssd_w_out", p_out, ssd_w_out, m_ssd_w_out, v_ssd_w_out)
    (p_in,) = landed("scatter_wait_in", ["in"], a_out[0])
    a_in = sum_adamw("adamw_ssd_w_in", p_in, ssd_w_in, m_ssd_w_in, v_ssd_w_in)
    bg = [a_in, a_out, a_pool, a_gate, a_up, a_down]

    small_parts = gathered_small("gather_small_grads_wait", small_rec, a_in[0])
    sm = [_unpack(o, small_grads) for o in sum_adamw("adamw_small", small_parts, packed_w, packed_m, packed_v)]
    sm = [[mine(a) if sh else a for a, sh in zip(group, sharded)] for group in sm]

    def ordered(kind):
        s_ = sm[kind]
        b_ = [o[kind] for o in bg]
        return [s_[0], s_[1], b_[0], s_[2], s_[3], s_[4], s_[5], s_[6], s_[7], b_[1], b_[2], s_[8], s_[9], b_[3], b_[4], b_[5]]

    return (loss, grad_x, *ordered(0), *ordered(1), *ordered(2), *ordered(3))
```

```python
import functools

import jax
import jax.numpy as jnp
from jax import lax
from jax.experimental import pallas as pl
from jax.experimental.pallas import tpu as pltpu

F32 = jnp.float32
BF16 = jnp.bfloat16
MESH = pl.DeviceIdType.MESH
AXES = ("x", "y", "c")
N_DEV = 8

N_META = 16
EPS = 1e-6
HEAD_DIM = 64
D_STATE = 128
SSD_GROUPS = 8
D_CONV = 4
CHUNK = 256
POOL_WINDOWS = (2, 4, 8, 16)
ADAM_LR, ADAM_B1, ADAM_B2, ADAM_EPS, ADAM_WD, ADAM_STEP = 0.001, 0.9, 0.999, 1e-08, 0.01, 10

PAD_FRONT = (-N_META) % CHUNK
LANES = 128
SUBLANES = 8
ROW_TILE = 256
CONV_TILE = 256
POOL_TILE = 128
VMEM_LIMIT = 56 * 1024 * 1024


def _params(sem=None):
    return pltpu.CompilerParams(dimension_semantics=sem, vmem_limit_bytes=VMEM_LIMIT)


def _tile(n, target, mult):
    if n <= target:
        return n
    best = None
    for t in range(mult, target + 1, mult):
        if n % t == 0:
            best = t
    assert best is not None, (n, target, mult)
    return best


def _dot(a, b, ca, cb):
    return lax.dot_general(a, b, (((ca,), (cb,)), ((), ())), preferred_element_type=F32)


def _sigmoid(x):
    return 1.0 / (1.0 + jnp.exp(-x))


def _row_mask(shape, first_row):
    rows = lax.broadcasted_iota(jnp.int32, shape, 0) + first_row
    return rows >= PAD_FRONT


MM_ROWS_RESIDENT = dict(tm=2304, tk=2304)
MM_COLS_RESIDENT = dict(tm=512, tn=2304, tk=2304)
MM_DEEP = dict(tm=1152, tn=1024, tk=512)
GATHER_AHEAD = 2

def matmul(name, a, b, *, tb=False, out_dtype=F32, tm=768, tn=512, tk=2048):
    m, kdim = a.shape
    if tb:
        n, k2 = b.shape
    else:
        k2, n = b.shape
    assert kdim == k2, (a.shape, b.shape, tb)
    tm = _tile(m, tm, 16)
    tn = _tile(n, tn, LANES)
    tk = _tile(kdim, tk, LANES)
    nk = kdim // tk
    a_spec = pl.BlockSpec((tm, tk), lambda i, j, k: (i, k))
    b_spec = pl.BlockSpec((tn, tk), lambda i, j, k: (j, k)) if tb else pl.BlockSpec((tk, tn), lambda i, j, k: (k, j))

    def body_single(a_ref, b_ref, o_ref):
        o_ref[...] = _dot(a_ref[...], b_ref[...], 1, 1 if tb else 0).astype(o_ref.dtype)

    def body_acc(a_ref, b_ref, o_ref, acc_ref):
        k = pl.program_id(2)

        @pl.when(k == 0)
        def _():
            acc_ref[...] = jnp.zeros_like(acc_ref)

        acc_ref[...] += _dot(a_ref[...], b_ref[...], 1, 1 if tb else 0)

        @pl.when(k == nk - 1)
        def _():
            o_ref[...] = acc_ref[...].astype(o_ref.dtype)

    return pl.pallas_call(
        body_single if nk == 1 else body_acc, name=name, grid=(m // tm, n // tn, nk),
        in_specs=[a_spec, b_spec], out_specs=pl.BlockSpec((tm, tn), lambda i, j, k: (i, j)),
        out_shape=jax.ShapeDtypeStruct((m, n), out_dtype),
        scratch_shapes=[] if nk == 1 else [pltpu.VMEM((tm, tn), F32)],
        compiler_params=_params(("parallel", "parallel", "arbitrary")),
    )(a, b)


_TOKEN_SPEC = pl.BlockSpec((SUBLANES, LANES), lambda i: (0, 0))


def rmsnorm_fwd(name, x, w, res=None, out_dtype=F32, transposed=False, after=None):
    rows, d = x.shape
    tr = ROW_TILE
    row_spec = pl.BlockSpec((tr, d), lambda i: (i, 0))
    w_spec = pl.BlockSpec((1, d), lambda i: (0, 0))
    n_in = 2 + (res is not None) + (after is not None)

    def body(*refs):
        x_ref, w_ref = refs[:2]
        outs = refs[n_in:]
        xv = x_ref[...]
        y = xv * lax.rsqrt(jnp.mean(xv * xv, axis=-1, keepdims=True) + EPS) * w_ref[...]
        if res is not None:
            y = refs[2][...] + y
        outs[0][...] = y.astype(out_dtype)
        if transposed:
            outs[1][...] = y.T.astype(out_dtype)

    args = [x, w.reshape(1, d)] + ([] if res is None else [res]) + ([] if after is None else [after])
    specs = [row_spec, w_spec] + ([] if res is None else [row_spec]) + ([] if after is None else [_TOKEN_SPEC])
    out_specs, out_shape = [row_spec], [jax.ShapeDtypeStruct((rows, d), out_dtype)]
    if transposed:
        out_specs.append(pl.BlockSpec((d, tr), lambda i: (0, i)))
        out_shape.append(jax.ShapeDtypeStruct((d, rows), out_dtype))
    outs = pl.pallas_call(
        body, name=name, grid=(rows // tr,), in_specs=specs, out_specs=out_specs, out_shape=out_shape,
        compiler_params=_params(("parallel",)),
    )(*args)
    return outs if transposed else outs[0]


def rmsnorm_bwd(name, x, w, dy, add=None, after=None):
    rows, d = x.shape
    tr = ROW_TILE
    row_spec = pl.BlockSpec((tr, d), lambda i: (i, 0))
    w_spec = pl.BlockSpec((1, d), lambda i: (0, 0))

    def body(*refs):
        x_ref, w_ref, dy_ref = refs[:3]
        add_ref = None if add is None else refs[3]
        dx_ref, dw_ref = refs[-2:]
        xv = x_ref[...]
        dyv = dy_ref[...].astype(F32)
        r = lax.rsqrt(jnp.mean(xv * xv, axis=-1, keepdims=True) + EPS)
        xh = xv * r
        dxh = dyv * w_ref[...]
        dx = r * (dxh - xh * jnp.mean(dxh * xh, axis=-1, keepdims=True))
        if add is not None:
            dx = dx + add_ref[...]
        dx_ref[...] = dx

        @pl.when(pl.program_id(0) == 0)
        def _():
            dw_ref[...] = jnp.zeros_like(dw_ref)

        dw_ref[...] += jnp.sum(dyv * xh, axis=0, keepdims=True)

    args = [x, w.reshape(1, d), dy] + ([] if add is None else [add]) + ([] if after is None else [after])
    specs = [row_spec, w_spec, row_spec] + ([] if add is None else [row_spec]) + ([] if after is None else [_TOKEN_SPEC])
    dx, dw = pl.pallas_call(
        body, name=name, grid=(rows // tr,), in_specs=specs, out_specs=[row_spec, w_spec],
        out_shape=[jax.ShapeDtypeStruct((rows, d), F32), jax.ShapeDtypeStruct((1, d), F32)],
        compiler_params=_params(("arbitrary",)),
    )(*args)
    return dx, dw.reshape(d)


def loss_head(name, h, target):
    rows, d = h.shape
    tr = ROW_TILE
    first = (PAD_FRONT + N_META) // tr
    assert (PAD_FRONT + N_META) % tr == 0 and target.shape[0] == rows - first * tr

    def body(h_ref, t_ref, dh_ref, loss_ref):
        i = pl.program_id(0)

        @pl.when(i == 0)
        def _():
            loss_ref[...] = jnp.zeros_like(loss_ref)

        keep = (i >= first).astype(F32)
        diff = (h_ref[...] - t_ref[...]) * keep
        dh_ref[...] = diff / d
        loss_ref[...] += 0.5 * jnp.sum(diff * diff) / d

    dh, loss = pl.pallas_call(
        body, name=name, grid=(rows // tr,),
        in_specs=[pl.BlockSpec((tr, d), lambda i: (i, 0)), pl.BlockSpec((tr, d), lambda i: (jnp.maximum(i - first, 0), 0))],
        out_specs=[pl.BlockSpec((tr, d), lambda i: (i, 0)), pl.BlockSpec((SUBLANES, LANES), lambda i: (0, 0))],
        out_shape=[jax.ShapeDtypeStruct((rows, d), F32), jax.ShapeDtypeStruct((SUBLANES, LANES), F32)],
        compiler_params=_params(("arbitrary",)),
    )(h, target)
    return dh, loss[0, 0]


def _mm_call(name, a, b, grid, a_spec, b_spec, o_spec, out_sds, ca, cb, red_axis=None):
    n_red = None if red_axis is None else grid[red_axis]

    def body_single(a_ref, b_ref, o_ref):
        o_ref[...] = _dot(a_ref[...], b_ref[...], ca, cb).astype(o_ref.dtype)

    def body_acc(a_ref, b_ref, o_ref, acc_ref):
        k = pl.program_id(red_axis)

        @pl.when(k == 0)
        def _():
            acc_ref[...] = jnp.zeros_like(acc_ref)

        acc_ref[...] += _dot(a_ref[...], b_ref[...], ca, cb)

        @pl.when(k == n_red - 1)
        def _():
            o_ref[...] = acc_ref[...].astype(o_ref.dtype)

    acc_shape = tuple(s for s in o_spec.block_shape if s is not None)
    sem = tuple("arbitrary" if ax == red_axis else "parallel" for ax in range(len(grid)))
    return pl.pallas_call(
        body_single if red_axis is None else body_acc, name=name, grid=grid, in_specs=[a_spec, b_spec], out_specs=o_spec,
        out_shape=out_sds, scratch_shapes=[] if red_axis is None else [pltpu.VMEM(acc_shape, F32)],
        compiler_params=_params(sem),
    )(a, b)


def ffn_in_fwd(name, u2, w_gu):
    rows, d = u2.shape
    hs = w_gu.shape[-1]
    return _mm_call(name, u2, w_gu, (N_DEV, 2), pl.BlockSpec((rows, d), lambda k, t: (0, 0)),
                    pl.BlockSpec((None, None, d, hs), lambda k, t: (k, t, 0, 0)),
                    pl.BlockSpec((None, None, rows, hs), lambda k, t: (k, t, 0, 0)),
                    jax.ShapeDtypeStruct((N_DEV, 2, rows, hs), F32), 1, 0)


def ffn_out_fwd(name, act, w_down):
    _, rows, hs = act.shape
    d = w_down.shape[-1]
    tm, tn = _tile(rows, 1152, 16), _tile(d, 1024, LANES)
    return _mm_call(name, act, w_down, (rows // tm, d // tn, N_DEV), pl.BlockSpec((None, tm, hs), lambda i, j, k: (k, i, 0)),
                    pl.BlockSpec((None, hs, tn), lambda i, j, k: (k, 0, j)), pl.BlockSpec((tm, tn), lambda i, j, k: (i, j)),
                    jax.ShapeDtypeStruct((rows, d), F32), 1, 0, red_axis=2)


def ffn_out_bwd_x(name, df, w_down):
    rows, d = df.shape
    hs = w_down.shape[1]
    return _mm_call(name, df, w_down, (N_DEV,), pl.BlockSpec((rows, d), lambda k: (0, 0)),
                    pl.BlockSpec((None, hs, d), lambda k: (k, 0, 0)), pl.BlockSpec((None, rows, hs), lambda k: (k, 0, 0)),
                    jax.ShapeDtypeStruct((N_DEV, rows, hs), F32), 1, 1)


def ffn_out_bwd_w(name, act_t, df):
    _, hs, rows = act_t.shape
    d = df.shape[1]
    return _mm_call(name, act_t, df, (N_DEV,), pl.BlockSpec((None, hs, rows), lambda k: (k, 0, 0)),
                    pl.BlockSpec((rows, d), lambda k: (0, 0)), pl.BlockSpec((None, hs, d), lambda k: (k, 0, 0)),
                    jax.ShapeDtypeStruct((N_DEV, hs, d), BF16), 1, 0)


def ffn_in_bwd_w(name, u2_t, dgp):
    d, rows = u2_t.shape
    hs = dgp.shape[-1]
    return _mm_call(name, u2_t, dgp, (N_DEV, 2), pl.BlockSpec((d, rows), lambda k, t: (0, 0)),
                    pl.BlockSpec((None, None, rows, hs), lambda k, t: (k, t, 0, 0)),
                    pl.BlockSpec((None, None, d, hs), lambda k, t: (k, t, 0, 0)),
                    jax.ShapeDtypeStruct((N_DEV, 2, d, hs), BF16), 1, 0)


def ffn_in_bwd_x(name, dgp, w_gu):
    _, _, rows, hs = dgp.shape
    d = w_gu.shape[2]
    tm, tn = _tile(rows, 1152, 16), _tile(d, 1024, LANES)
    return _mm_call(name, dgp.reshape(2 * N_DEV, rows, hs), w_gu.reshape(2 * N_DEV, d, hs), (rows // tm, d // tn, 2 * N_DEV),
                    pl.BlockSpec((None, tm, hs), lambda i, j, k: (k, i, 0)), pl.BlockSpec((None, tn, hs), lambda i, j, k: (k, j, 0)),
                    pl.BlockSpec((tm, tn), lambda i, j, k: (i, j)), jax.ShapeDtypeStruct((rows, d), F32), 1, 1, red_axis=2)


def swiglu_fwd(name, gp):
    _, _, rows, hs = gp.shape
    tr = _tile(rows, 768, ROW_TILE)

    def body(gp_ref, a_ref, at_ref):
        g = gp_ref[0]
        act = g * _sigmoid(g) * gp_ref[1]
        a_ref[...] = act.astype(a_ref.dtype)
        at_ref[...] = act.T.astype(at_ref.dtype)

    return pl.pallas_call(
        body, name=name, grid=(N_DEV, rows // tr),
        in_specs=[pl.BlockSpec((None, 2, tr, hs), lambda k, i: (k, 0, i, 0))],
        out_specs=[pl.BlockSpec((None, tr, hs), lambda k, i: (k, i, 0)), pl.BlockSpec((None, hs, tr), lambda k, i: (k, 0, i))],
        out_shape=[jax.ShapeDtypeStruct((N_DEV, rows, hs), BF16), jax.ShapeDtypeStruct((N_DEV, hs, rows), BF16)],
        compiler_params=_params(("parallel", "parallel")),
    )(gp)


def swiglu_bwd(name, gp, dact):
    _, _, rows, hs = gp.shape
    tr = _tile(rows, 768, ROW_TILE)

    def body(gp_ref, da_ref, o_ref):
        g = gp_ref[0]
        s = _sigmoid(g)
        dav = da_ref[...]
        o_ref[0] = (dav * gp_ref[1] * (s * (1.0 + g * (1.0 - s)))).astype(o_ref.dtype)
        o_ref[1] = (dav * (g * s)).astype(o_ref.dtype)

    blk = pl.BlockSpec((None, 2, tr, hs), lambda k, i: (k, 0, i, 0))
    return pl.pallas_call(
        body, name=name, grid=(N_DEV, rows // tr),
        in_specs=[blk, pl.BlockSpec((None, tr, hs), lambda k, i: (k, i, 0))], out_specs=blk,
        out_shape=jax.ShapeDtypeStruct(gp.shape, BF16), compiler_params=_params(("parallel", "parallel")),
    )(gp, dact)


def conv_fwd(name, zx, conv_w, conv_b, d_inner, conv_dim):
    rows = zx.shape[0]
    tc = CONV_TILE
    off = d_inner // tc
    assert d_inner % tc == 0 and conv_dim % tc == 0

    def body(u_ref, w_ref, b_ref, o_ref):
        u = u_ref[...]
        acc = u * w_ref[D_CONV - 1:D_CONV, :] + b_ref[...]
        for s in range(1, D_CONV):
            acc = acc + pltpu.roll(u, s, axis=0) * w_ref[D_CONV - 1 - s:D_CONV - s, :]
        y = acc * _sigmoid(acc)
        o_ref[...] = jnp.where(_row_mask(y.shape, 0), y, 0.0)

    return pl.pallas_call(
        body, name=name, grid=(conv_dim // tc,),
        in_specs=[pl.BlockSpec((rows, tc), lambda j: (0, off + j)), pl.BlockSpec((D_CONV, tc), lambda j: (0, j)),
                  pl.BlockSpec((1, tc), lambda j: (0, j))],
        out_specs=pl.BlockSpec((rows, tc), lambda j: (0, j)),
        out_shape=jax.ShapeDtypeStruct((rows, conv_dim), F32), compiler_params=_params(("parallel",)),
    )(zx, conv_w, conv_b.reshape(1, conv_dim))


def conv_bwd(name, zx, conv_w, conv_b, dxbc, d_inner, conv_dim):
    rows = zx.shape[0]
    tc = CONV_TILE
    off = d_inner // tc

    def body(u_ref, w_ref, b_ref, dy_ref, du_ref, dw_ref, db_ref):
        u = u_ref[...]
        wk = [w_ref[D_CONV - 1 - s:D_CONV - s, :] for s in range(D_CONV)]
        shifted = [u] + [pltpu.roll(u, s, axis=0) for s in range(1, D_CONV)]
        acc = u * wk[0] + b_ref[...]
        for s in range(1, D_CONV):
            acc = acc + shifted[s] * wk[s]
        sg = _sigmoid(acc)
        mask = _row_mask(acc.shape, 0)
        dpre = jnp.where(mask, dy_ref[...] * (sg * (1.0 + acc * (1.0 - sg))), 0.0)
        db_ref[...] = jnp.sum(dpre, axis=0, keepdims=True)
        du = dpre * wk[0]
        dw_ref[D_CONV - 1:D_CONV, :] = jnp.sum(dpre * u, axis=0, keepdims=True)
        for s in range(1, D_CONV):
            du = du + pltpu.roll(dpre, rows - s, axis=0) * wk[s]
            dw_ref[D_CONV - 1 - s:D_CONV - s, :] = jnp.sum(dpre * shifted[s], axis=0, keepdims=True)
        du_ref[...] = jnp.where(mask, du, 0.0).astype(du_ref.dtype)

    return pl.pallas_call(
        body, name=name, grid=(conv_dim // tc,),
        in_specs=[pl.BlockSpec((rows, tc), lambda j: (0, off + j)), pl.BlockSpec((D_CONV, tc), lambda j: (0, j)),
                  pl.BlockSpec((1, tc), lambda j: (0, j)), pl.BlockSpec((rows, tc), lambda j: (0, j))],
        out_specs=[pl.BlockSpec((rows, tc), lambda j: (0, j)), pl.BlockSpec((D_CONV, tc), lambda j: (0, j)),
                   pl.BlockSpec((1, tc), lambda j: (0, j))],
        out_shape=[jax.ShapeDtypeStruct((rows, conv_dim), BF16), jax.ShapeDtypeStruct((D_CONV, conv_dim), F32),
                   jax.ShapeDtypeStruct((1, conv_dim), F32)],
        compiler_params=_params(("parallel",)),
    )(zx, conv_w, conv_b.reshape(1, conv_dim), dxbc)


def dt_fwd(name, zx, bias_pad, zx_cols):
    rows = zx.shape[0]
    tr = ROW_TILE
    off = zx_cols // LANES

    def body(r_ref, b_ref, o_ref):
        v = r_ref[...] + b_ref[...]
        sp = jnp.maximum(v, 0.0) + jnp.log1p(jnp.exp(-jnp.abs(v)))
        o_ref[...] = jnp.where(_row_mask(v.shape, pl.program_id(0) * tr), sp, 0.0)

    return pl.pallas_call(
        body, name=name, grid=(rows // tr,),
        in_specs=[pl.BlockSpec((tr, LANES), lambda i: (i, off)), pl.BlockSpec((1, LANES), lambda i: (0, 0))],
        out_specs=pl.BlockSpec((tr, LANES), lambda i: (i, 0)),
        out_shape=jax.ShapeDtypeStruct((rows, LANES), F32), compiler_params=_params(("parallel",)),
    )(zx, bias_pad)


def dt_bwd(name, zx, bias_pad, ddt, zx_cols):
    rows = zx.shape[0]
    tr = ROW_TILE
    off = zx_cols // LANES

    def body(r_ref, b_ref, d_ref, o_ref, db_ref):
        v = r_ref[...] + b_ref[...]
        g = jnp.where(_row_mask(v.shape, pl.program_id(0) * tr), d_ref[...] * _sigmoid(v), 0.0)
        o_ref[...] = g.astype(o_ref.dtype)

        @pl.when(pl.program_id(0) == 0)
        def _():
            db_ref[...] = jnp.zeros_like(db_ref)

        db_ref[...] += jnp.sum(g, axis=0, keepdims=True)

    return pl.pallas_call(
        body, name=name, grid=(rows // tr,),
        in_specs=[pl.BlockSpec((tr, LANES), lambda i: (i, off)), pl.BlockSpec((1, LANES), lambda i: (0, 0)),
                  pl.BlockSpec((tr, LANES), lambda i: (i, 0))],
        out_specs=[pl.BlockSpec((tr, LANES), lambda i: (i, 0)), pl.BlockSpec((1, LANES), lambda i: (0, 0))],
        out_shape=[jax.ShapeDtypeStruct((rows, LANES), BF16), jax.ShapeDtypeStruct((1, LANES), F32)],
        compiler_params=_params(("arbitrary",)),
    )(zx, bias_pad, ddt)


def _split3(x):
    h1 = x.astype(BF16)
    r1 = x - h1.astype(F32)
    h2 = r1.astype(BF16)
    h3 = (r1 - h2.astype(F32)).astype(BF16)
    return h1, h2, h3


def _exact_left(ones_b, x):
    h1, h2, h3 = _split3(x)
    return _dot(ones_b, h1, 1, 0) + _dot(ones_b, h2, 1, 0) + _dot(ones_b, h3, 1, 0)


def _exact_right_t(x, ones_b):
    h1, h2, h3 = _split3(x)
    return _dot(h1, ones_b, 1, 1) + _dot(h2, ones_b, 1, 1) + _dot(h3, ones_b, 1, 1)


class _ScanCommon:
    def __init__(self, b_ref, c_ref, dtc_ref, dtr_ref, arow_ref, acol_ref, drow_ref):
        q = CHUNK
        self.bb = b_ref[...].astype(BF16)
        self.cb = c_ref[...].astype(BF16)
        ri = lax.broadcasted_iota(jnp.int32, (q, q), 0)
        cj = lax.broadcasted_iota(jnp.int32, (q, q), 1)
        self.lower = ri >= cj
        self.upper = cj >= ri
        self.dtc = dtc_ref[...]
        self.dtr = dtr_ref[...]
        self.arow = arow_ref[...]
        self.acol = acol_ref[...]
        self.drow = drow_ref[...]
        da_col = self.dtc * self.arow
        self.a_col = _exact_left(self.lower.astype(BF16), da_col)
        self.a_row = _exact_right_t(self.dtr * self.acol, self.lower.astype(BF16))
        self.a_last = jnp.sum(da_col, axis=0, keepdims=True)
        self.lane_q = lax.broadcasted_iota(jnp.int32, (q, LANES), 1)
        self.lane_1 = lax.broadcasted_iota(jnp.int32, (1, LANES), 1)
        self.sub_8 = lax.broadcasted_iota(jnp.int32, (SUBLANES, q), 0)
        self.first_half = self.lane_q < HEAD_DIM
        self.first_rows = lax.broadcasted_iota(jnp.int32, (LANES, 1), 0) < HEAD_DIM

    def col(self, v, r):
        return jnp.sum(jnp.where(self.lane_q == r, v, 0.0), axis=1, keepdims=True)

    def row(self, v, r):
        return jnp.sum(jnp.where(self.sub_8 == r, v, 0.0), axis=0, keepdims=True)

    def scalar(self, v, r):
        return jnp.sum(jnp.where(self.lane_1 == r, v, 0.0), axis=1, keepdims=True)

    def pair(self, v0, v1):
        return jnp.where(self.first_half, v0, v1)

    def half_rowsum(self, t, h):
        keep = self.first_half if h == 0 else jnp.logical_not(self.first_half)
        return jnp.sum(jnp.where(keep, t, 0.0), axis=1, keepdims=True)


def ssd_scan_fwd(name, xbc, dt_col, dt_row, a_row, a_col, d_row, d_inner):
    rows = xbc.shape[0]
    q, n, g_cnt = CHUNK, D_STATE, SSD_GROUPS
    nc = rows // q
    rp = d_inner // g_cnt
    n_pairs = rp // LANES
    b_off = d_inner // n

    def body(x_ref, b_ref, c_ref, dtc_ref, dtr_ref, arow_ref, acol_ref, drow_ref, y_ref, sprev_ref, s_ref):
        @pl.when(pl.program_id(1) == 0)
        def _():
            s_ref[...] = jnp.zeros_like(s_ref)

        sprev_ref[...] = s_ref[...]
        k = _ScanCommon(b_ref, c_ref, dtc_ref, dtr_ref, arow_ref, acol_ref, drow_ref)
        cb_mat = _dot(k.cb, k.bb, 1, 1)
        for pr in range(n_pairs):
            sl = slice(pr * LANES, (pr + 1) * LANES)
            heads = (2 * pr, 2 * pr + 1)
            xp = x_ref[:, sl]
            ac = [k.col(k.a_col, r) for r in heads]
            ar = [k.row(k.a_row, r) for r in heads]
            al = [k.scalar(k.a_last, r) for r in heads]
            xd = xp * k.pair(k.col(k.dtc, heads[0]), k.col(k.dtc, heads[1]))
            xdb = xd.astype(BF16)
            ys = []
            for h in range(2):
                lm = jnp.exp(jnp.where(k.lower, ac[h] - ar[h], -jnp.inf))
                ys.append(_dot((cb_mat * lm).astype(BF16), xdb, 1, 0))
            y = jnp.where(k.first_half, ys[0], ys[1])
            sp = s_ref[sl, :]
            y = y + k.pair(jnp.exp(ac[0]), jnp.exp(ac[1])) * _dot(k.cb, sp.astype(BF16), 1, 1)
            y = y + k.pair(k.scalar(k.drow, heads[0]), k.scalar(k.drow, heads[1])) * xp
            y_ref[:, sl] = y
            wb = (xd * k.pair(jnp.exp(al[0] - ac[0]), jnp.exp(al[1] - ac[1]))).astype(BF16)
            decay = jnp.where(k.first_rows, jnp.exp(al[0]), jnp.exp(al[1]))
            s_ref[sl, :] = decay * sp + _dot(wb, k.bb, 0, 0)

    return pl.pallas_call(
        body, name=name, grid=(g_cnt, nc),
        in_specs=[
            pl.BlockSpec((q, rp), lambda g, c: (c, g)),
            pl.BlockSpec((q, n), lambda g, c: (c, b_off + g)),
            pl.BlockSpec((q, n), lambda g, c: (c, b_off + g_cnt + g)),
            pl.BlockSpec((None, q, LANES), lambda g, c: (g, c, 0)),
            pl.BlockSpec((None, SUBLANES, q), lambda g, c: (g, 0, c)),
            pl.BlockSpec((None, 1, LANES), lambda g, c: (g, 0, 0)),
            pl.BlockSpec((None, SUBLANES, 1), lambda g, c: (g, 0, 0)),
            pl.BlockSpec((None, 1, LANES), lambda g, c: (g, 0, 0)),
        ],
        out_specs=[pl.BlockSpec((q, rp), lambda g, c: (c, g)),
                   pl.BlockSpec((None, None, rp, n), lambda g, c: (c, g, 0, 0))],
        out_shape=[jax.ShapeDtypeStruct((rows, d_inner), F32), jax.ShapeDtypeStruct((nc, g_cnt, rp, n), F32)],
        scratch_shapes=[pltpu.VMEM((rp, n), F32)],
        compiler_params=_params(("parallel", "arbitrary")),
    )(xbc, xbc, xbc, dt_col, dt_row, a_row, a_col, d_row)


def ssd_scan_bwd(name, xbc, dt_col, dt_row, a_row, a_col, d_row, sprev, dy, d_inner):
    rows = xbc.shape[0]
    q, n, g_cnt = CHUNK, D_STATE, SSD_GROUPS
    nc = rows // q
    rp = d_inner // g_cnt
    n_pairs = rp // LANES
    b_off = d_inner // n

    def body(x_ref, b_ref, c_ref, dtc_ref, dtr_ref, arow_ref, acol_ref, drow_ref, sprev_ref, dy_ref,
             dx_ref, db_ref, dc_ref, ddtc_ref, ddtr_ref, dar_ref, dac_ref, dd_ref, ds_ref):
        @pl.when(pl.program_id(1) == 0)
        def _():
            ds_ref[...] = jnp.zeros_like(ds_ref)
            dar_ref[...] = jnp.zeros_like(dar_ref)
            dac_ref[...] = jnp.zeros_like(dac_ref)
            dd_ref[...] = jnp.zeros_like(dd_ref)

        k = _ScanCommon(b_ref, c_ref, dtc_ref, dtr_ref, arow_ref, acol_ref, drow_ref)
        cb_mat = _dot(k.cb, k.bb, 1, 1)
        cbt_mat = _dot(k.bb, k.cb, 1, 1)
        d_cb = jnp.zeros((q, q), F32)
        d_b = jnp.zeros((q, n), F32)
        d_c = jnp.zeros((q, n), F32)
        da_col = jnp.zeros((q, LANES), F32)
        da_row = jnp.zeros((SUBLANES, q), F32)
        ddt_x = jnp.zeros((q, LANES), F32)
        d_alast = jnp.zeros((1, LANES), F32)
        d_dskip = jnp.zeros((1, LANES), F32)
        for pr in range(n_pairs):
            sl = slice(pr * LANES, (pr + 1) * LANES)
            heads = (2 * pr, 2 * pr + 1)
            xp = x_ref[:, sl]
            dyp = dy_ref[:, sl]
            dyb = dyp.astype(BF16)
            ac = [k.col(k.a_col, r) for r in heads]
            ar = [k.row(k.a_row, r) for r in heads]
            al = [k.scalar(k.a_last, r) for r in heads]
            dt_p = k.pair(k.col(k.dtc, heads[0]), k.col(k.dtc, heads[1]))
            xd = xp * dt_p
            xdb = xd.astype(BF16)
            sp = sprev_ref[sl, :]
            spb = sp.astype(BF16)
            dsp = ds_ref[sl, :]
            dspb = dsp.astype(BF16)
            dskip_p = k.pair(k.scalar(k.drow, heads[0]), k.scalar(k.drow, heads[1]))
            dxp = dskip_p * dyp
            dd_lane = jnp.sum(dyp * xp, axis=0, keepdims=True)
            e_p = k.pair(jnp.exp(ac[0]), jnp.exp(ac[1]))
            t_off = dyp * (e_p * _dot(k.cb, spb, 1, 1))
            dzb = (e_p * dyp).astype(BF16)
            d_c = d_c + _dot(dzb, spb, 1, 0)
            ds_in = _dot(dzb, k.cb, 0, 0)
            decay = jnp.where(k.first_rows, jnp.exp(al[0]), jnp.exp(al[1]))
            ds_in = ds_in + decay * dsp
            t_state = jnp.sum(dsp * sp, axis=1, keepdims=True) * decay
            dec_p = k.pair(jnp.exp(al[0] - ac[0]), jnp.exp(al[1] - ac[1]))
            dw = _dot(k.bb, dspb, 1, 1)
            d_b = d_b + _dot((xd * dec_p).astype(BF16), dspb, 1, 0)
            dxd = dw * dec_p
            t_dec = dw * xd * dec_p
            dxd_h = []
            for h in range(2):
                r = heads[h]
                keep = k.first_half if h == 0 else jnp.logical_not(k.first_half)
                lm = jnp.exp(jnp.where(k.lower, ac[h] - ar[h], -jnp.inf))
                m_mat = cb_mat * lm
                dm = _dot(jnp.where(keep, dyp, 0.0).astype(BF16), xdb, 1, 1)
                dseg = dm * m_mat
                d_cb = d_cb + dm * lm
                lmt = jnp.exp(jnp.where(k.upper, ar[h] - ac[h], -jnp.inf))
                dxd_h.append(_dot((cbt_mat * lmt).astype(BF16), dyb, 1, 0))
                tdec_h = k.half_rowsum(t_dec, h)
                da_h = k.half_rowsum(t_off, h) - tdec_h + jnp.sum(dseg, axis=1, keepdims=True)
                da_col = da_col + jnp.where(k.lane_q == r, da_h, 0.0)
                da_row = da_row - jnp.where(k.sub_8 == r, jnp.sum(dseg, axis=0, keepdims=True), 0.0)
                keep_rows = k.first_rows if h == 0 else jnp.logical_not(k.first_rows)
                dal_h = jnp.sum(tdec_h, axis=0, keepdims=True) + jnp.sum(jnp.where(keep_rows, t_state, 0.0), axis=0, keepdims=True)
                d_alast = d_alast + jnp.where(k.lane_1 == r, dal_h, 0.0)
                keep_1 = k.lane_1 < HEAD_DIM if h == 0 else k.lane_1 >= HEAD_DIM
                dd_h = jnp.sum(jnp.where(keep_1, dd_lane, 0.0), axis=1, keepdims=True)
                d_dskip = d_dskip + jnp.where(k.lane_1 == r, dd_h, 0.0)
            dxd = dxd + jnp.where(k.first_half, dxd_h[0], dxd_h[1])
            dx_ref[:, sl] = dxp + dt_p * dxd
            t_dt = dxd * xp
            for h in range(2):
                ddt_x = ddt_x + jnp.where(k.lane_q == heads[h], k.half_rowsum(t_dt, h), 0.0)
            ds_ref[sl, :] = ds_in
        d_cb_b = d_cb.astype(BF16)
        db_ref[...] = d_b + _dot(d_cb_b, k.cb, 0, 0)
        dc_ref[...] = d_c + _dot(d_cb_b, k.bb, 1, 0)
        rc_col = _exact_left(k.upper.astype(BF16), da_col) + d_alast
        rc_row = _exact_right_t(da_row, k.upper.astype(BF16))
        ddtc_ref[...] = ddt_x + k.arow * rc_col
        ddtr_ref[...] = k.acol * rc_row
        dar_ref[...] += jnp.sum(rc_col * k.dtc, axis=0, keepdims=True)
        dac_ref[...] += jnp.sum(rc_row * k.dtr, axis=1, keepdims=True)
        dd_ref[...] += d_dskip

    rc = lambda c: nc - 1 - c
    return pl.pallas_call(
        body, name=name, grid=(g_cnt, nc),
        in_specs=[
            pl.BlockSpec((q, rp), lambda g, c: (rc(c), g)),
            pl.BlockSpec((q, n), lambda g, c: (rc(c), b_off + g)),
            pl.BlockSpec((q, n), lambda g, c: (rc(c), b_off + g_cnt + g)),
            pl.BlockSpec((None, q, LANES), lambda g, c: (g, rc(c), 0)),
            pl.BlockSpec((None, SUBLANES, q), lambda g, c: (g, 0, rc(c))),
            pl.BlockSpec((None, 1, LANES), lambda g, c: (g, 0, 0)),
            pl.BlockSpec((None, SUBLANES, 1), lambda g, c: (g, 0, 0)),
            pl.BlockSpec((None, 1, LANES), lambda g, c: (g, 0, 0)),
            pl.BlockSpec((None, None, rp, n), lambda g, c: (rc(c), g, 0, 0)),
            pl.BlockSpec((q, rp), lambda g, c: (rc(c), g)),
        ],
        out_specs=[
            pl.BlockSpec((q, rp), lambda g, c: (rc(c), g)),
            pl.BlockSpec((q, n), lambda g, c: (rc(c), g)),
            pl.BlockSpec((q, n), lambda g, c: (rc(c), g)),
            pl.BlockSpec((None, q, LANES), lambda g, c: (g, rc(c), 0)),
            pl.BlockSpec((None, SUBLANES, q), lambda g, c: (g, 0, rc(c))),
            pl.BlockSpec((None, 1, LANES), lambda g, c: (g, 0, 0)),
            pl.BlockSpec((None, SUBLANES, 1), lambda g, c: (g, 0, 0)),
            pl.BlockSpec((None, 1, LANES), lambda g, c: (g, 0, 0)),
        ],
        out_shape=[
            jax.ShapeDtypeStruct((rows, d_inner), F32),
            jax.ShapeDtypeStruct((rows, g_cnt * n), F32),
            jax.ShapeDtypeStruct((rows, g_cnt * n), F32),
            jax.ShapeDtypeStruct((g_cnt, rows, LANES), F32),
            jax.ShapeDtypeStruct((g_cnt, SUBLANES, rows), F32),
            jax.ShapeDtypeStruct((g_cnt, 1, LANES), F32),
            jax.ShapeDtypeStruct((g_cnt, SUBLANES, 1), F32),
            jax.ShapeDtypeStruct((g_cnt, 1, LANES), F32),
        ],
        scratch_shapes=[pltpu.VMEM((rp, n), F32)],
        compiler_params=_params(("parallel", "arbitrary")),
    )(xbc, xbc, xbc, dt_col, dt_row, a_row, a_col, d_row, sprev, dy)


def gatenorm_fwd(name, y, zx, w, d_inner):
    rows = y.shape[0]
    tr, gw = ROW_TILE, d_inner // SSD_GROUPS

    def body(y_ref, z_ref, w_ref, o_ref, ot_ref):
        z = z_ref[...]
        v = y_ref[...] * (z * _sigmoid(z))
        out = v * lax.rsqrt(jnp.mean(v * v, axis=-1, keepdims=True) + EPS) * w_ref[...]
        o_ref[...] = out.astype(o_ref.dtype)
        ot_ref[...] = out.T.astype(ot_ref.dtype)

    blk = pl.BlockSpec((tr, gw), lambda i, j: (i, j))
    return pl.pallas_call(
        body, name=name, grid=(rows // tr, SSD_GROUPS),
        in_specs=[blk, blk, pl.BlockSpec((1, gw), lambda i, j: (0, j))],
        out_specs=[blk, pl.BlockSpec((gw, tr), lambda i, j: (j, i))],
        out_shape=[jax.ShapeDtypeStruct((rows, d_inner), BF16), jax.ShapeDtypeStruct((d_inner, rows), BF16)],
        compiler_params=_params(("parallel", "parallel")),
    )(y, zx, w.reshape(1, d_inner))


def gatenorm_bwd(name, y, zx, w, dyn, d_inner, after):
    rows = y.shape[0]
    tr, gw = ROW_TILE, d_inner // SSD_GROUPS

    def body(y_ref, z_ref, w_ref, dn_ref, after_ref, dy_ref, dz_ref, dw_ref):
        z = z_ref[...]
        yv = y_ref[...]
        s = _sigmoid(z)
        gate = z * s
        v = yv * gate
        r = lax.rsqrt(jnp.mean(v * v, axis=-1, keepdims=True) + EPS)
        vh = v * r
        dn = dn_ref[...]
        dvh = dn * w_ref[...]
        dv = r * (dvh - vh * jnp.mean(dvh * vh, axis=-1, keepdims=True))
        dy_ref[...] = dv * gate
        dz_ref[...] = (dv * yv * (s * (1.0 + z * (1.0 - s)))).astype(dz_ref.dtype)

        @pl.when(pl.program_id(1) == 0)
        def _():
            dw_ref[...] = jnp.zeros_like(dw_ref)

        dw_ref[...] += jnp.sum(dn * vh, axis=0, keepdims=True)

    blk = pl.BlockSpec((tr, gw), lambda j, i: (i, j))
    wblk = pl.BlockSpec((1, gw), lambda j, i: (0, j))
    dy, dz, dw = pl.pallas_call(
        body, name=name, grid=(SSD_GROUPS, rows // tr),
        in_specs=[blk, blk, wblk, blk, pl.BlockSpec((SUBLANES, LANES), lambda j, i: (0, 0))], out_specs=[blk, blk, wblk],
        out_shape=[jax.ShapeDtypeStruct((rows, d_inner), F32), jax.ShapeDtypeStruct((rows, d_inner), BF16),
                   jax.ShapeDtypeStruct((1, d_inner), F32)],
        compiler_params=_params(("parallel", "arbitrary")),
    )(y, zx, w.reshape(1, d_inner), dyn, after)
    return dy, dz, dw.reshape(d_inner)


def _pool_count(rows, g):
    t1 = lax.broadcasted_iota(jnp.int32, (rows, 1), 0) - (PAD_FRONT - 1)
    win = jnp.left_shift(jnp.int32(POOL_WINDOWS[0]), g)
    return jnp.clip(t1, 1, win).astype(F32)


def _pool_select(levels, g):
    out = levels[-1]
    for i in range(len(levels) - 2, -1, -1):
        out = jnp.where(g == i, levels[i], out)
    return out


def pool_sub(name, u, transpose):
    rows, d = u.shape
    gd = d // len(POOL_WINDOWS)
    assert all(w == POOL_WINDOWS[0] << i for i, w in enumerate(POOL_WINDOWS))

    def body(u_ref, o_ref):
        g = pl.program_id(0)
        v = u_ref[...].astype(F32)
        cnt = _pool_count(rows, g)
        mask = _row_mask(v.shape, 0)
        s = v / cnt if transpose else v
        levels = []
        for i in range(len(POOL_WINDOWS)):
            step = 1 << i
            s = s + pltpu.roll(s, (rows - step) if transpose else step, axis=0)
            levels.append(s)
        sel = _pool_select(levels, g)
        out = (sel - v) if transpose else (sel / cnt - v)
        o_ref[...] = jnp.where(mask, out, 0.0).astype(o_ref.dtype)

    tc = POOL_TILE
    per = gd // tc
    blk = pl.BlockSpec((rows, tc), lambda g, j: (0, g * per + j))
    return pl.pallas_call(
        body, name=name, grid=(len(POOL_WINDOWS), per), in_specs=[blk], out_specs=blk,
        out_shape=jax.ShapeDtypeStruct((rows, d), F32 if transpose else BF16),
        compiler_params=_params(("parallel", "parallel")),
    )(u)


def pool_proj_fwd(name, mixed, w, b, scale):
    rows, d = mixed.shape
    ng = len(POOL_WINDOWS)
    gd = d // ng
    tr = ROW_TILE

    def body(m_ref, w_ref, b_ref, s_ref, pre_ref, mix_ref):
        pre = _dot(m_ref[...], w_ref[...], 1, 0) + b_ref[...]
        pre_ref[...] = pre
        mix_ref[...] = jnp.where(_row_mask(pre.shape, pl.program_id(1) * tr), pre * s_ref[...], 0.0)

    blk = pl.BlockSpec((tr, gd), lambda g, i: (i, g))
    vec = pl.BlockSpec((1, gd), lambda g, i: (0, g))
    return pl.pallas_call(
        body, name=name, grid=(ng, rows // tr),
        in_specs=[blk, pl.BlockSpec((None, gd, gd), lambda g, i: (g, 0, 0)), vec, vec], out_specs=[blk, blk],
        out_shape=[jax.ShapeDtypeStruct((rows, d), F32), jax.ShapeDtypeStruct((rows, d), F32)],
        compiler_params=_params(("parallel", "parallel")),
    )(mixed, w, b.reshape(1, d), scale.reshape(1, d))


def pool_proj_bwd(name, dmix, pre, mixed, w, scale):
    rows, d = dmix.shape
    ng = len(POOL_WINDOWS)
    gd = d // ng
    tr = ROW_TILE

    def body(dm_ref, pre_ref, mx_ref, w_ref, s_ref, dmx_ref, dw_ref, db_ref, ds_ref):
        @pl.when(pl.program_id(1) == 0)
        def _():
            dw_ref[...] = jnp.zeros_like(dw_ref)
            db_ref[...] = jnp.zeros_like(db_ref)
            ds_ref[...] = jnp.zeros_like(ds_ref)

        dmv = jnp.where(_row_mask(dm_ref.shape, pl.program_id(1) * tr), dm_ref[...], 0.0)
        dpre = dmv * s_ref[...]
        dpre_b = dpre.astype(BF16)
        ds_ref[...] += jnp.sum(dmv * pre_ref[...], axis=0, keepdims=True)
        db_ref[...] += jnp.sum(dpre, axis=0, keepdims=True)
        dmx_ref[...] = _dot(dpre_b, w_ref[...], 1, 1)
        dw_ref[...] += _dot(mx_ref[...], dpre_b, 0, 0)

    blk = pl.BlockSpec((tr, gd), lambda g, i: (i, g))
    vec = pl.BlockSpec((1, gd), lambda g, i: (0, g))
    wblk = pl.BlockSpec((None, gd, gd), lambda g, i: (g, 0, 0))
    dmixed, dw, db, ds = pl.pallas_call(
        body, name=name, grid=(ng, rows // tr),
        in_specs=[blk, blk, blk, wblk, vec], out_specs=[blk, wblk, vec, vec],
        out_shape=[jax.ShapeDtypeStruct((rows, d), F32), jax.ShapeDtypeStruct((ng, gd, gd), F32),
                   jax.ShapeDtypeStruct((1, d), F32), jax.ShapeDtypeStruct((1, d), F32)],
        compiler_params=_params(("parallel", "arbitrary")),
    )(dmix, pre, mixed, w, scale.reshape(1, d))
    return dmixed, dw, db.reshape(d), ds.reshape(d)


def _my_place():
    return lax.axis_index("x"), lax.axis_index("y"), lax.axis_index("c")


def _linear(place):
    return 4 * place[0] + 2 * place[1] + place[2]


def all_gather(name, shards):
    n_ops = len(shards)

    def body(*refs):
        ins, outs = refs[:n_ops], refs[n_ops:2 * n_ops]
        send_sems, recv_sems, local_sems = refs[2 * n_ops:]
        x, y, c = _my_place()
        me, sibling = (x, y, c), (x, y, 1 - c)
        chips = [(1 - x, y), (x, 1 - y), (1 - x, 1 - y)]

        def copy(t, k, block, to, src=None):
            dst = outs[t].at[_linear(block)]
            return pltpu.make_async_remote_copy(
                src_ref=dst if src is None else src, dst_ref=dst, send_sem=send_sems.at[t, k], recv_sem=recv_sems.at[t, k],
                device_id=to, device_id_type=MESH)

        mine = [pltpu.make_async_copy(ins[t], outs[t].at[_linear(me)], local_sems.at[t]) for t in range(n_ops)]
        for cp in mine:
            cp.start()
        first = []
        for t in range(n_ops):
            first.append(copy(t, 0, me, sibling, src=ins[t]))
            first += [copy(t, 1 + j, me, (*chip, c), src=ins[t]) for j, chip in enumerate(chips)]
        for cp in first:
            cp.start()
        passed = []
        for j, chip in enumerate(chips):
            for t in range(n_ops):
                copy(t, 1 + j, (*chip, c), me).wait_recv()
                fwd = copy(t, 4 + j, (*chip, c), sibling)
                fwd.start()
                passed.append(fwd)
        for t in range(n_ops):
            copy(t, 0, sibling, me).wait_recv()
            for j, chip in enumerate(chips):
                copy(t, 4 + j, (*chip, 1 - c), me).wait_recv()
        for cp in first + passed:
            cp.wait_send()
        for cp in mine:
            cp.wait()

    any_spec = pl.BlockSpec(memory_space=pl.ANY)
    outs = pl.pallas_call(
        body, name=name,
        in_specs=[any_spec] * n_ops, out_specs=[any_spec] * n_ops,
        out_shape=[jax.ShapeDtypeStruct((N_DEV, *s.shape), s.dtype) for s in shards],
        scratch_shapes=[pltpu.SemaphoreType.DMA((n_ops, 7)), pltpu.SemaphoreType.DMA((n_ops, 7)),
                        pltpu.SemaphoreType.DMA((n_ops,))],
    )(*shards)
    return list(outs)


_HBM = pl.BlockSpec(memory_space=pltpu.HBM)
_SEM = pl.BlockSpec(memory_space=pltpu.SEMAPHORE)
_EFFECT = pltpu.SideEffectType.DATAFLOW_SIDE_EFFECTING


def _hbm(a):
    return pltpu.with_memory_space_constraint(a, pltpu.HBM)


def _peers():
    x, y, c = _my_place()
    return [(x ^ (j >> 2), y ^ ((j >> 1) & 1), c ^ (j & 1)) for j in range(1, N_DEV)]


def exchange_start(name, groups, after):
    flat = [e for g in groups for e in g]
    n = len(flat)

    def body(*refs):
        srcs, lands = refs[:n], refs[n:2 * n]
        outs = refs[2 * n + 1:]
        sends, recvs, token = outs[:n], outs[n:2 * n], outs[4 * n]
        me = _linear(_my_place())
        for t, (_, _, src_view, land_view) in enumerate(flat):
            for peer in _peers():
                pltpu.make_async_remote_copy(
                    src_ref=src_view(srcs[t], _linear(peer)), dst_ref=land_view(lands[t], me),
                    send_sem=sends[t], recv_sem=recvs[t], device_id=peer, device_id_type=MESH).start()
        token[...] = jnp.zeros_like(token)

    sem_shapes = [pltpu.SemaphoreType.DMA(())] * (2 * n)
    thru = [pltpu.HBM(e[0].shape, e[0].dtype) for e in flat] + [pltpu.HBM(e[1].shape, e[1].dtype) for e in flat]
    outs = pl.pallas_call(
        body, name=name,
        out_shape=(*sem_shapes, *thru, jax.ShapeDtypeStruct((SUBLANES, LANES), F32)),
        in_specs=[*[_HBM] * (2 * n), pl.BlockSpec(memory_space=pl.ANY)],
        out_specs=(*[_SEM] * (2 * n), *[_HBM] * (2 * n), pl.BlockSpec(memory_space=pltpu.VMEM)),
        input_output_aliases={t: 2 * n + t for t in range(2 * n)},
        compiler_params=pltpu.CompilerParams(has_side_effects=_EFFECT),
    )(*[_hbm(e[0]) for e in flat], *[_hbm(e[1]) for e in flat], after)
    records, t = [], 0
    for g in groups:
        k = len(g)
        records.append((list(outs[t:t + k]), list(outs[n + t:n + t + k]), list(outs[2 * n + t:2 * n + t + k]),
                        list(outs[3 * n + t:3 * n + t + k])))
        t += k
    return records, outs[-1]


def exchange_wait(name, records, lands, land_of, seven_of, after):
    srcs = [s for r in records for s in r[2]]
    sends = [s for r in records for s in r[0]]
    recvs = [s for r in records for s in r[1]]
    where = [(ri, k) for ri, r in enumerate(records) for k in range(len(r[2]))]
    ns, nl = len(srcs), len(lands)

    def body(*refs):
        src_refs, land_refs = refs[:ns], refs[ns:ns + nl]
        send_refs, recv_refs = refs[ns + nl:2 * ns + nl], refs[2 * ns + nl:3 * ns + nl]
        for t, (ri, k) in enumerate(where):
            seven_ref = seven_of[ri][k](src_refs[t], land_refs[land_of[ri][k]])
            cp = pltpu.make_async_remote_copy(src_ref=seven_ref, dst_ref=seven_ref, send_sem=send_refs[t], recv_sem=recv_refs[t],
                                              device_id=_my_place(), device_id_type=MESH)
            cp.wait_send()
            cp.wait_recv()

    outs = pl.pallas_call(
        body, name=name,
        out_shape=tuple(pltpu.HBM(a.shape, a.dtype) for a in (*srcs, *lands)),
        in_specs=[*[_HBM] * (ns + nl), *[_SEM] * (2 * ns), pl.BlockSpec(memory_space=pl.ANY)], out_specs=tuple([_HBM] * (ns + nl)),
        input_output_aliases={t: t for t in range(ns + nl)},
        compiler_params=pltpu.CompilerParams(has_side_effects=_EFFECT),
    )(*srcs, *lands, *sends, *recvs, after)
    return list(outs[:ns]), list(outs[ns:])


def _other_chips():
    x, y, c = _my_place()
    return [(1 - x, y, c), (x, 1 - y, c), (1 - x, 1 - y, c)]


def gather2_start(name, groups, after):
    flat = [e for g in groups for e in g]
    n = len(flat)

    def body(*refs):
        srcs, lands = refs[:n], refs[n:2 * n]
        outs = refs[2 * n + 1:]
        d_send, d_recv, i_send, i_recv, token = outs[:n], outs[n:2 * n], outs[2 * n:3 * n], outs[3 * n:4 * n], outs[6 * n]
        x, y, c = _my_place()
        me = _linear((x, y, c))
        for t in range(n):
            pltpu.make_async_remote_copy(src_ref=srcs[t], dst_ref=lands[t].at[me], send_sem=d_send[t], recv_sem=d_recv[t],
                                         device_id=(x, y, 1 - c), device_id_type=MESH).start()
            for peer in _other_chips():
                pltpu.make_async_remote_copy(src_ref=srcs[t], dst_ref=lands[t].at[me], send_sem=i_send[t], recv_sem=i_recv[t],
                                             device_id=peer, device_id_type=MESH).start()
        token[...] = jnp.zeros_like(token)

    thru = [pltpu.HBM(e[0].shape, e[0].dtype) for e in flat] + [pltpu.HBM(e[1].shape, e[1].dtype) for e in flat]
    outs = pl.pallas_call(
        body, name=name,
        out_shape=(*[pltpu.SemaphoreType.DMA(())] * (4 * n), *thru, jax.ShapeDtypeStruct((SUBLANES, LANES), F32)),
        in_specs=[*[_HBM] * (2 * n), pl.BlockSpec(memory_space=pl.ANY)],
        out_specs=(*[_SEM] * (4 * n), *[_HBM] * (2 * n), pl.BlockSpec(memory_space=pltpu.VMEM)),
        input_output_aliases={t: 4 * n + t for t in range(2 * n)},
        compiler_params=pltpu.CompilerParams(has_side_effects=_EFFECT),
    )(*[_hbm(e[0]) for e in flat], *[_hbm(e[1]) for e in flat], after)
    records, t = [], 0
    for g in groups:
        k = len(g)
        records.append({"d_send": list(outs[t:t + k]), "d_recv": list(outs[n + t:n + t + k]),
                        "i_send": list(outs[2 * n + t:2 * n + t + k]), "i_recv": list(outs[3 * n + t:3 * n + t + k]),
                        "srcs": list(outs[4 * n + t:4 * n + t + k]), "lands": list(outs[5 * n + t:5 * n + t + k])})
        t += k
    return records, outs[-1]


def gather2_relay(name, rec, after):
    lands, i_recv = rec["lands"], rec["i_recv"]
    n = len(lands)

    def body(*refs):
        land_refs, i_recv_refs = refs[:n], refs[n:2 * n]
        outs = refs[2 * n + 1:]
        f_send, f_recv, token = outs[n:2 * n], outs[2 * n:3 * n], outs[3 * n]
        x, y, c = _my_place()
        for t in range(n):
            three = land_refs[t].at[pl.ds(0, 3)]
            pltpu.make_async_remote_copy(src_ref=three, dst_ref=three, send_sem=f_send[t], recv_sem=i_recv_refs[t],
                                         device_id=(x, y, c), device_id_type=MESH).wait_recv()
            for peer in _other_chips():
                blk = land_refs[t].at[_linear(peer)]
                pltpu.make_async_remote_copy(src_ref=blk, dst_ref=blk, send_sem=f_send[t], recv_sem=f_recv[t],
                                             device_id=(x, y, 1 - c), device_id_type=MESH).start()
        token[...] = jnp.zeros_like(token)

    outs = pl.pallas_call(
        body, name=name,
        out_shape=(*[pltpu.HBM(a.shape, a.dtype) for a in lands], *[pltpu.SemaphoreType.DMA(())] * (2 * n),
                   jax.ShapeDtypeStruct((SUBLANES, LANES), F32)),
        in_specs=[*[_HBM] * n, *[_SEM] * n, pl.BlockSpec(memory_space=pl.ANY)],
        out_specs=(*[_HBM] * n, *[_SEM] * (2 * n), pl.BlockSpec(memory_space=pltpu.VMEM)),
        input_output_aliases={t: t for t in range(n)},
        compiler_params=pltpu.CompilerParams(has_side_effects=_EFFECT),
    )(*lands, *i_recv, after)
    return {**rec, "lands": list(outs[:n]), "f_send": list(outs[n:2 * n]), "f_recv": list(outs[2 * n:3 * n])}, outs[-1]


def gather2_wait(name, rec, after):
    n = len(rec["lands"])
    sem_names = ("d_send", "d_recv", "i_send", "f_send", "f_recv")

    def body(*refs):
        src_refs, land_refs = refs[:n], refs[n:2 * n]
        sems = {nm: refs[2 * n + k * n:2 * n + (k + 1) * n] for k, nm in enumerate(sem_names)}
        me = _my_place()
        for t in range(n):
            one, three = land_refs[t].at[0], land_refs[t].at[pl.ds(0, 3)]
            to_sibling = pltpu.make_async_remote_copy(src_ref=src_refs[t], dst_ref=one, send_sem=sems["d_send"][t],
                                                      recv_sem=sems["d_recv"][t], device_id=me, device_id_type=MESH)
            to_sibling.wait_send()
            to_sibling.wait_recv()
            pltpu.make_async_remote_copy(src_ref=three, dst_ref=three, send_sem=sems["i_send"][t], recv_sem=sems["d_recv"][t],
                                         device_id=me, device_id_type=MESH).wait_send()
            forwarded = pltpu.make_async_remote_copy(src_ref=three, dst_ref=three, send_sem=sems["f_send"][t],
                                                     recv_sem=sems["f_recv"][t], device_id=me, device_id_type=MESH)
            forwarded.wait_send()
            forwarded.wait_recv()

    arrays = (*rec["srcs"], *rec["lands"])
    outs = pl.pallas_call(
        body, name=name,
        out_shape=tuple(pltpu.HBM(a.shape, a.dtype) for a in arrays),
        in_specs=[*[_HBM] * (2 * n), *[_SEM] * (5 * n), *[pl.BlockSpec(memory_space=pl.ANY)] * 2], out_specs=tuple([_HBM] * (2 * n)),
        input_output_aliases={t: t for t in range(2 * n)},
        compiler_params=pltpu.CompilerParams(has_side_effects=_EFFECT),
    )(*arrays, *[s for nm in sem_names for s in rec[nm]], *after)
    return list(outs[:n]), list(outs[n:])


def _seven_slots_of_land(src_ref, land_ref):
    return land_ref.at[pl.ds(0, N_DEV - 1)]


def _seven_blocks_of_src(src_ref, land_ref):
    return src_ref.at[pl.ds(0, N_DEV - 1)]


def _whole(ref, dev):
    return ref


def _slot(ref, dev):
    return ref.at[dev]


def _slot_of_layer(layer):
    return lambda ref, dev: ref.at[dev, layer]


def _as_rows(a, lead=0):
    return a.reshape(a.shape[:lead] + (-1, a.shape[-1]))


def sum_adamw(name, parts, w, m, v, half=None, column_major=False):
    shape = w.shape
    c1 = 1.0 - ADAM_B1 ** ADAM_STEP
    c2 = 1.0 - ADAM_B2 ** ADAM_STEP
    if column_major:
        layers, r, cols = shape
        tr = _tile(r, ROW_TILE, LANES)
        grid = (layers, r // tr)
        if half is None:
            p_spec = pl.BlockSpec((N_DEV, None, tr, cols), lambda l, i: (0, l, i, 0))
        else:
            p_spec = pl.BlockSpec((N_DEV, None, None, tr, cols), lambda l, i: (0, l, half, i, 0))
        blk = pl.BlockSpec((None, cols, tr), lambda l, i: (l, 0, i))
        args = (parts, *[jnp.swapaxes(a, 1, 2) for a in (w, m, v)])
        out_sds, sem = jax.ShapeDtypeStruct((layers, cols, r), F32), ("parallel", "parallel")
    elif half is not None:
        layers, r, cols = shape
        tr = _tile(r, ROW_TILE, 16)
        grid = (layers, r // tr)
        p_spec = pl.BlockSpec((N_DEV, None, None, tr, cols), lambda l, i: (0, l, half, i, 0))
        blk = pl.BlockSpec((None, tr, cols), lambda l, i: (l, i, 0))
        args, out_sds, sem = (parts, w, m, v), jax.ShapeDtypeStruct(shape, F32), ("parallel", "parallel")
    else:
        p2, w2, m2, v2 = _as_rows(parts, 1), _as_rows(w), _as_rows(m), _as_rows(v)
        rows, cols = w2.shape
        tr = _tile(rows, ROW_TILE, 16)
        grid = (rows // tr,)
        p_spec = pl.BlockSpec((N_DEV, tr, cols), lambda i: (0, i, 0))
        blk = pl.BlockSpec((tr, cols), lambda i: (i, 0))
        args, out_sds, sem = (p2, w2, m2, v2), jax.ShapeDtypeStruct((rows, cols), F32), ("parallel",)

    def body(p_ref, w_ref, m_ref, v_ref, g_ref, d_ref, nm_ref, nv_ref):
        g = p_ref[0].astype(F32)
        for k in range(1, N_DEV):
            g = g + p_ref[k].astype(F32)
        if column_major:
            g = g.T
        wv = w_ref[...]
        nm = ADAM_B1 * m_ref[...] + (1.0 - ADAM_B1) * g
        nv = ADAM_B2 * v_ref[...] + (1.0 - ADAM_B2) * (g * g)
        g_ref[...] = g
        nm_ref[...] = nm
        nv_ref[...] = nv
        d_ref[...] = -ADAM_LR * ((nm / c1) / (jnp.sqrt(nv / c2) + ADAM_EPS) + ADAM_WD * wv)

    outs = pl.pallas_call(
        body, name=name, grid=grid, in_specs=[p_spec, blk, blk, blk], out_specs=[blk] * 4,
        out_shape=[out_sds] * 4, compiler_params=_params(sem),
    )(*args)
    if column_major:
        return [jnp.swapaxes(o, 1, 2) for o in outs]
    return [o.reshape(shape) for o in outs]


def _unblock_cols(g):
    g = jnp.moveaxis(g, 0, -2)
    return g.reshape(g.shape[:-2] + (g.shape[-2] * g.shape[-1],))


def _block_cols(a):
    r, c = a.shape
    return jnp.moveaxis(a.reshape(r, N_DEV, c // N_DEV), 1, 0)


def _my_cols(a, n):
    me = _linear(_my_place())
    return lax.dynamic_slice_in_dim(a, me * n, n, axis=a.ndim - 1)


def _pack(arrays):
    flat = jnp.concatenate([a.reshape(-1).astype(F32) for a in arrays])
    pad = (-flat.shape[0]) % (ROW_TILE * LANES)
    return jnp.pad(flat, (0, pad)).reshape(-1, LANES)


def _unpack(packed, like):
    flat = packed.reshape(-1)
    out, pos = [], 0
    for a in like:
        out.append(flat[pos:pos + a.size].reshape(a.shape))
        pos += a.size
    return out


def kernel(x, meta_tokens, norm_w, ssd_w_in, ssd_conv_w, ssd_conv_b, ssd_dt_bias, ssd_a_log, ssd_d, ssd_norm_w, ssd_w_out, pool_w, pool_b, pool_scale, ffn_w_gate, ffn_w_up, ffn_w_down, loss_target, m_meta_tokens, m_norm_w, m_ssd_w_in, m_ssd_conv_w, m_ssd_conv_b, m_ssd_dt_bias, m_ssd_a_log, m_ssd_d, m_ssd_norm_w, m_ssd_w_out, m_pool_w, m_pool_b, m_pool_scale, m_ffn_w_gate, m_ffn_w_up, m_ffn_w_down, v_meta_tokens, v_norm_w, v_ssd_w_in, v_ssd_conv_w, v_ssd_conv_b, v_ssd_dt_bias, v_ssd_a_log, v_ssd_d, v_ssd_norm_w, v_ssd_w_out, v_pool_w, v_pool_b, v_pool_scale, v_ffn_w_gate, v_ffn_w_up, v_ffn_w_down):
    seq, d = x.shape[1], x.shape[2]
    depth = norm_w.shape[0]
    n_ssd = ssd_w_in.shape[0]
    d_inner = ssd_norm_w.shape[1]
    heads = d_inner // HEAD_DIM
    rpg = heads // SSD_GROUPS
    conv_dim = ssd_conv_b.shape[1]
    zx_cols = d_inner + conv_dim
    d_in_proj = zx_cols + heads
    rows = PAD_FRONT + N_META + seq
    assert rows % CHUNK == 0 and rpg % 2 == 0 and heads <= LANES and rpg <= SUBLANES

    hidden = ffn_w_down.shape[1] * N_DEV
    n_pool = pool_w.shape[0]

    small = all_gather("gather_small", [meta_tokens, norm_w, ssd_conv_w, pool_b, pool_scale])
    meta_f, norm_f, convw_f, poolb_f, pools_f = [_unblock_cols(s) for s in small]

    gather_recs = {}

    def start_gather(i, after):
        def entry(shard):
            return (shard.astype(BF16), lax.empty((N_DEV, *shard.shape), BF16))
        mixer = [ssd_w_in[i // 2], ssd_w_out[i // 2]] if i % 2 == 0 else [pool_w[i // 2]]
        gate_up = jnp.stack([ffn_w_gate[i], ffn_w_up[i]])
        groups = [*[[entry(w)] for w in mixer], [entry(gate_up), entry(ffn_w_down[i])]]
        gather_recs[i], tok = gather2_start(f"gather_start_{i}", groups, after)
        return tok

    gather_order = [(i, g) for i in range(depth) for g in range(3 if i % 2 == 0 else 2)]
    relayed = [0, None]

    def gathered(i, g, after):
        upto = min(gather_order.index((i, g)) + 2, len(gather_order))
        while relayed[0] < upto:
            li, lg = gather_order[relayed[0]]
            gather_recs[li][lg], relayed[1] = gather2_relay(f"gather_relay_{li}_{lg}", gather_recs[li][lg], after)
            relayed[0] += 1
        srcs, lands_ = gather2_wait(f"gather_wait_{i}_{g}", gather_recs[i][g], (after, relayed[1]))
        me = _linear(_my_place())
        return [lax.dynamic_update_slice_in_dim(land, src[None], me, axis=0) for src, land in zip(srcs, lands_)]

    def gathered_small(name, rec, after):
        srcs, lands_ = exchange_wait(name, [rec], rec[3], [[0]], [[_seven_slots_of_land]], after)
        return lax.dynamic_update_slice_in_dim(lands_[0], srcs[0][None], _linear(_my_place()), axis=0)

    pending = meta_f
    for i in range(min(GATHER_AHEAD, depth)):
        pending = start_gather(i, pending)

    pad_h = lambda a: jnp.pad(a.astype(F32), ((0, 0), (0, LANES - heads)))
    bias_pad = pad_h(ssd_dt_bias)

    def head_layouts(vec):
        g = vec.reshape(SSD_GROUPS, rpg)
        row = jnp.pad(g, ((0, 0), (0, LANES - rpg)))[:, None, :]
        col = jnp.pad(g, ((0, 0), (0, SUBLANES - rpg)))[:, :, None]
        return row, col

    def dt_layouts(dt):
        g = dt[:, :heads].reshape(rows, SSD_GROUPS, rpg)
        col = jnp.pad(jnp.moveaxis(g, 1, 0), ((0, 0), (0, 0), (0, LANES - rpg)))
        row = jnp.pad(jnp.transpose(g, (1, 2, 0)), ((0, 0), (0, SUBLANES - rpg), (0, 0)))
        return col, row

    h = jnp.concatenate([jnp.zeros((PAD_FRONT, d), F32), meta_f, x[0]], axis=0)
    saved = []
    for i in range(depth):
        j = i // 2
        s = {"h": h}
        w_pre_mix = norm_f[i, 0]
        if i + GATHER_AHEAD < depth:
            pending = start_gather(i + GATHER_AHEAD, pending if i == 0 else h)
        (mixer_w,) = gathered(i, 0, h)
        if i % 2 == 0:
            w_in = jnp.pad(_unblock_cols(mixer_w), ((0, 0), (0, LANES - heads)))
            u, u_t = rmsnorm_fwd(f"norm_pre_mix_{i}", h, w_pre_mix, out_dtype=BF16, transposed=True, after=pending)
            zx = matmul(f"ssd_in_{i}", u, w_in, **MM_ROWS_RESIDENT, tn=384 if w_in.shape[1] % 384 == 0 else 512)
            xbc = conv_fwd(f"ssd_conv_{i}", zx, convw_f[j], ssd_conv_b[j], d_inner, conv_dim)
            dt = dt_fwd(f"ssd_dt_{i}", zx, bias_pad[j:j + 1], zx_cols)
            dt_col, dt_row = dt_layouts(dt)
            a_neg = -jnp.exp(ssd_a_log[j].astype(F32))
            a_row, a_col = head_layouts(a_neg)
            d_row, _ = head_layouts(ssd_d[j].astype(F32))
            y, sprev = ssd_scan_fwd(f"ssd_scan_{i}", xbc, dt_col, dt_row, a_row, a_col, d_row, d_inner)
            yn, yn_t = gatenorm_fwd(f"ssd_gate_{i}", y, zx, ssd_norm_w[j], d_inner)
            (w_out,) = gathered(i, 1, yn)
            w_out = w_out.reshape(d_inner, d)
            mix = matmul(f"ssd_out_{i}", yn, w_out, **MM_DEEP)
            s.update(u_t=u_t, zx=zx, xbc=xbc, dt_col=dt_col, dt_row=dt_row, a_row=a_row, a_col=a_col, d_row=d_row,
                     a_neg=a_neg, y=y, sprev=sprev, yn_t=yn_t, w_in=w_in, w_out=w_out)
        else:
            w_pool = jnp.moveaxis(mixer_w, 0, 1).reshape(len(POOL_WINDOWS), d // len(POOL_WINDOWS), -1)
            u = rmsnorm_fwd(f"norm_pre_mix_{i}", h, w_pre_mix, after=pending)
            mixed = pool_sub(f"pool_sub_{i}", u, False)
            pre, mix = pool_proj_fwd(f"pool_proj_{i}", mixed, w_pool, poolb_f[j], pools_f[j])
            s.update(mixed=mixed, pre=pre, w_pool=w_pool)
        pending = None
        h1 = rmsnorm_fwd(f"norm_post_mix_{i}", mix, norm_f[i, 1], res=h)
        w_gu, w_down = gathered(i, 2 if i % 2 == 0 else 1, h1)
        u2, u2_t = rmsnorm_fwd(f"norm_pre_ffn_{i}", h1, norm_f[i, 2], out_dtype=BF16, transposed=True)
        gp = ffn_in_fwd(f"ffn_in_{i}", u2, w_gu)
        act, act_t = swiglu_fwd(f"ffn_act_{i}", gp)
        f = ffn_out_fwd(f"ffn_out_{i}", act, w_down)
        h = rmsnorm_fwd(f"norm_post_ffn_{i}", f, norm_f[i, 3], res=h1)
        s.update(mix=mix, h1=h1, u2_t=u2_t, gp=gp, act_t=act_t, f=f, w_gu=w_gu, w_down=w_down)
        saved.append(s)

    dh, loss_local = loss_head("loss_head", h, loss_target[0])
    loss = lax.psum(loss_local, AXES)

    g_norm = [[None] * 4 for _ in range(depth)]
    g_convw, g_convb, g_dtb, g_alog, g_dskip, g_ssdnorm = ([None] * n_ssd for _ in range(6))
    g_poolb, g_pools = ([None] * n_pool for _ in range(2))
    hid_s = hidden // N_DEV
    lands = {"in": lax.empty((N_DEV, *ssd_w_in.shape), BF16), "out": lax.empty((N_DEV, *ssd_w_out.shape), BF16),
             "pool": lax.empty((N_DEV, *pool_w.shape), BF16), "down": lax.empty((N_DEV, *ffn_w_down.shape), BF16),
             "gate_up": lax.empty((N_DEV, depth, 2, d, hid_s), BF16)}
    scatter_recs, scatter_keys, scatter_views, scatter_layers = [], [], [], []

    def scatter(name, blocks, layer, after):
        view = _slot_of_layer(layer)
        (rec,), tok = exchange_start(name, [[(b, lands[key], _slot, view) for key, b in blocks]], after)
        for (key, _), thru in zip(blocks, rec[3]):
            lands[key] = thru
        scatter_recs.append(rec)
        scatter_keys.append([key for key, _ in blocks])
        scatter_views.append([_seven_blocks_of_src] * len(blocks))
        scatter_layers.append(layer)
        return tok

    tok = None
    for i in reversed(range(depth)):
        j = i // 2
        s = saved[i]
        df, g_norm[i][3] = rmsnorm_bwd(f"norm_post_ffn_bwd_{i}", s["f"], norm_f[i, 3], dh, after=tok)
        df_b = df.astype(BF16)
        dact = ffn_out_bwd_x(f"ffn_out_bwd_x_{i}", df_b, s["w_down"])
        g_down = ffn_out_bwd_w(f"ffn_out_bwd_w_{i}", s["act_t"], df_b)
        dgp = swiglu_bwd(f"ffn_act_bwd_{i}", s["gp"], dact)
        g_gu = ffn_in_bwd_w(f"ffn_in_bwd_w_{i}", s["u2_t"], dgp)
        tok = scatter(f"scatter_start_ffn_{i}", [("gate_up", g_gu), ("down", g_down)], i, g_gu)
        du2 = ffn_in_bwd_x(f"ffn_in_bwd_x_{i}", dgp, s["w_gu"])
        dh1, g_norm[i][2] = rmsnorm_bwd(f"norm_pre_ffn_bwd_{i}", s["h1"], norm_f[i, 2], du2, add=dh, after=tok)
        dmix, g_norm[i][1] = rmsnorm_bwd(f"norm_post_mix_bwd_{i}", s["mix"], norm_f[i, 1], dh1)
        if i % 2 == 0:
            dmix_b = dmix.astype(BF16)
            dyn = matmul(f"ssd_out_bwd_x_{i}", dmix_b, s["w_out"], tb=True, **MM_ROWS_RESIDENT)
            g_out = matmul(f"ssd_out_bwd_w_{i}", s["yn_t"], dmix_b, out_dtype=BF16, **MM_COLS_RESIDENT)
            tok = scatter(f"scatter_start_out_{i}", [("out", g_out.reshape(N_DEV, d_inner // N_DEV, d))], j, g_out)
            dy, dz, g_ssdnorm[j] = gatenorm_bwd(f"ssd_gate_bwd_{i}", s["y"], s["zx"], ssd_norm_w[j], dyn, d_inner, after=tok)
            dx, db, dc, ddt_col, ddt_row, dar, dac, ddsk = ssd_scan_bwd(
                f"ssd_scan_bwd_{i}", s["xbc"], s["dt_col"], s["dt_row"], s["a_row"], s["a_col"], s["d_row"], s["sprev"], dy, d_inner)
            ddt = (jnp.moveaxis(ddt_col[:, :, :rpg], 0, 1).reshape(rows, heads)
                   + jnp.transpose(ddt_row[:, :rpg, :], (2, 0, 1)).reshape(rows, heads))
            d_a = (dar[:, 0, :rpg] + dac[:, :rpg, 0]).reshape(heads)
            g_alog[j] = d_a * s["a_neg"]
            g_dskip[j] = ddsk[:, 0, :rpg].reshape(heads)
            ddtr, dbias = dt_bwd(f"ssd_dt_bwd_{i}", s["zx"], bias_pad[j:j + 1], pad_h(ddt), zx_cols)
            g_dtb[j] = dbias[0, :heads]
            dxbc_raw, g_convw[j], dconvb = conv_bwd(
                f"ssd_conv_bwd_{i}", s["zx"], convw_f[j], ssd_conv_b[j], jnp.concatenate([dx, db, dc], axis=1), d_inner, conv_dim)
            g_convb[j] = dconvb[0]
            dzx = jnp.concatenate([dz, dxbc_raw, ddtr], axis=1)
            g_in = matmul(f"ssd_in_bwd_w_{i}", s["u_t"], dzx, out_dtype=BF16, **MM_ROWS_RESIDENT,
                          tn=384 if dzx.shape[1] % 384 == 0 else 512)
            tok = scatter(f"scatter_start_in_{i}", [("in", _block_cols(g_in[:, :d_in_proj]))], j, g_in)
            du = matmul(f"ssd_in_bwd_x_{i}", dzx, s["w_in"], tb=True, **{**MM_DEEP, "tk": 1152 if dzx.shape[1] % 1152 == 0 else 512})
        else:
            dmixed, g_poolw, g_poolb[j], g_pools[j] = pool_proj_bwd(
                f"pool_proj_bwd_{i}", dmix, s["pre"], s["mixed"], s["w_pool"], pools_f[j])
            ng, gd = g_poolw.shape[0], g_poolw.shape[1]
            blk_pool = jnp.moveaxis(g_poolw.astype(BF16).reshape(ng, N_DEV, gd // N_DEV, gd), 1, 0)
            tok = scatter(f"scatter_start_pool_{i}", [("pool", blk_pool)], j, blk_pool)
            du = pool_sub(f"pool_sub_bwd_{i}", dmixed, True)
        dh, g_norm[i][0] = rmsnorm_bwd(f"norm_pre_mix_bwd_{i}", s["h"], norm_f[i, 0], du, add=dh1, after=tok)

    grad_x = dh[PAD_FRONT + N_META:][None]
    g_meta = dh[PAD_FRONT:PAD_FRONT + N_META]

    small_grads = [g_meta, jnp.stack([jnp.stack(r) for r in g_norm]), jnp.stack(g_convw), jnp.stack(g_convb), jnp.stack(g_dtb),
                   jnp.stack(g_alog), jnp.stack(g_dskip), jnp.stack(g_ssdnorm), jnp.stack(g_poolb), jnp.stack(g_pools)]
    packed_g = _pack(small_grads)
    (small_rec,), _ = exchange_start(
        "gather_small_grads_start", [[(packed_g, lax.empty((N_DEV, *packed_g.shape), F32), _whole, _slot)]], packed_g)
    mine = lambda a: _my_cols(a, a.shape[-1] // N_DEV)
    small_w = [meta_tokens, norm_w, ssd_conv_w, ssd_conv_b, ssd_dt_bias, ssd_a_log, ssd_d, ssd_norm_w, pool_b, pool_scale]
    small_m = [m_meta_tokens, m_norm_w, m_ssd_conv_w, m_ssd_conv_b, m_ssd_dt_bias, m_ssd_a_log, m_ssd_d, m_ssd_norm_w, m_pool_b, m_pool_scale]
    small_v = [v_meta_tokens, v_norm_w, v_ssd_conv_w, v_ssd_conv_b, v_ssd_dt_bias, v_ssd_a_log, v_ssd_d, v_ssd_norm_w, v_pool_b, v_pool_scale]
    sharded = [True, True, True, False, False, False, False, False, True, True]
    def widen(a, is_sharded, full):
        if not is_sharded:
            return a
        return lax.dynamic_update_slice_in_dim(jnp.zeros(full.shape, F32), a, _linear(_my_place()) * a.shape[-1], axis=a.ndim - 1)
    packed_w = _pack([widen(a, sh, g) for a, sh, g in zip(small_w, sharded, small_grads)])
    packed_m = _pack([widen(a, sh, g) for a, sh, g in zip(small_m, sharded, small_grads)])
    packed_v = _pack([widen(a, sh, g) for a, sh, g in zip(small_v, sharded, small_grads)])

    def landed(name, keys, after):
        rs = [r for r, ks in enumerate(scatter_keys) if set(ks) <= set(keys)]
        srcs, out = exchange_wait(name, [scatter_recs[r] for r in rs], [lands[k] for k in keys],
                                  [[keys.index(k) for k in scatter_keys[r]] for r in rs], [scatter_views[r] for r in rs], after)
        me = _linear(_my_place())
        src_of = iter(srcs)
        for r in rs:
            for key in scatter_keys[r]:
                own = lax.dynamic_index_in_dim(next(src_of), me, axis=0, keepdims=True)[:, None]
                slot = keys.index(key)
                start = (me, scatter_layers[r]) + (0,) * (own.ndim - 2)
                out[slot] = lax.dynamic_update_slice(out[slot], own, start)
        return out

    p_gu, p_down = landed("scatter_wait_ffn", ["gate_up", "down"], packed_g)
    a_gate = sum_adamw("adamw_ffn_w_gate", p_gu, ffn_w_gate, m_ffn_w_gate, v_ffn_w_gate, half=0, column_major=True)
    a_up = sum_adamw("adamw_ffn_w_up", p_gu, ffn_w_up, m_ffn_w_up, v_ffn_w_up, half=1, column_major=True)
    a_down = sum_adamw("adamw_ffn_w_down", p_down, ffn_w_down, m_ffn_w_down, v_ffn_w_down)
    (p_pool,) = landed("scatter_wait_pool", ["pool"], a_down[0])
    a_pool = sum_adamw("adamw_pool_w", p_pool, pool_w, m_pool_w, v_pool_w)
    (p_out,) = landed("scatter_wait_out", ["out"], a_pool[0])
    a_out = sum_adamw("adamw_ssd_w_out", p_out, ssd_w_out, m_ssd_w_out, v_ssd_w_out)
    (p_in,) = landed("scatter_wait_in", ["in"], a_out[0])
    a_in = sum_adamw("adamw_ssd_w_in", p_in, ssd_w_in, m_ssd_w_in, v_ssd_w_in, column_major=True)
    bg = [a_in, a_out, a_pool, a_gate, a_up, a_down]

    small_parts = gathered_small("gather_small_grads_wait", small_rec, a_in[0])
    sm = [_unpack(o, small_grads) for o in sum_adamw("adamw_small", small_parts, packed_w, packed_m, packed_v)]
    sm = [[mine(a) if sh else a for a, sh in zip(group, sharded)] for group in sm]

    def ordered(kind):
        s_ = sm[kind]
        b_ = [o[kind] for o in bg]
        return [s_[0], s_[1], b_[0], s_[2], s_[3], s_[4], s_[5], s_[6], s_[7], b_[1], b_[2], s_[8], s_[9], b_[3], b_[4], b_[5]]

    return (loss, grad_x, *ordered(0), *ordered(1), *ordered(2), *ordered(3))
```

```python
import functools

import jax
import jax.numpy as jnp
from jax import lax
from jax.experimental import pallas as pl
from jax.experimental.pallas import tpu as pltpu

F32 = jnp.float32
BF16 = jnp.bfloat16
MESH = pl.DeviceIdType.MESH
AXES = ("x", "y", "c")
N_DEV = 8

N_META = 16
EPS = 1e-6
HEAD_DIM = 64
D_STATE = 128
SSD_GROUPS = 8
D_CONV = 4
CHUNK = 256
POOL_WINDOWS = (2, 4, 8, 16)
ADAM_LR, ADAM_B1, ADAM_B2, ADAM_EPS, ADAM_WD, ADAM_STEP = 0.001, 0.9, 0.999, 1e-08, 0.01, 10

PAD_FRONT = (-N_META) % CHUNK
LANES = 128
SUBLANES = 8
ROW_TILE = 256
CONV_TILE = 256
POOL_TILE = 128
VMEM_LIMIT = 56 * 1024 * 1024


def _params(sem=None):
    return pltpu.CompilerParams(dimension_semantics=sem, vmem_limit_bytes=VMEM_LIMIT)


def _tile(n, target, mult):
    if n <= target:
        return n
    best = None
    for t in range(mult, target + 1, mult):
        if n % t == 0:
            best = t
    assert best is not None, (n, target, mult)
    return best


def _dot(a, b, ca, cb):
    return lax.dot_general(a, b, (((ca,), (cb,)), ((), ())), preferred_element_type=F32)


def _sigmoid(x):
    return 1.0 / (1.0 + jnp.exp(-x))


def _row_mask(shape, first_row):
    rows = lax.broadcasted_iota(jnp.int32, shape, 0) + first_row
    return rows >= PAD_FRONT


MM_ROWS_RESIDENT = dict(tm=2304, tk=2304)
MM_COLS_RESIDENT = dict(tm=512, tn=2304, tk=2304)
MM_DEEP = dict(tm=1152, tn=1024, tk=512)
GATHER_AHEAD = 2

def matmul(name, a, b, *, tb=False, out_dtype=F32, tm=768, tn=512, tk=2048):
    m, kdim = a.shape
    if tb:
        n, k2 = b.shape
    else:
        k2, n = b.shape
    assert kdim == k2, (a.shape, b.shape, tb)
    tm = _tile(m, tm, 16)
    tn = _tile(n, tn, LANES)
    tk = _tile(kdim, tk, LANES)
    nk = kdim // tk
    a_spec = pl.BlockSpec((tm, tk), lambda i, j, k: (i, k))
    b_spec = pl.BlockSpec((tn, tk), lambda i, j, k: (j, k)) if tb else pl.BlockSpec((tk, tn), lambda i, j, k: (k, j))

    def body_single(a_ref, b_ref, o_ref):
        o_ref[...] = _dot(a_ref[...], b_ref[...], 1, 1 if tb else 0).astype(o_ref.dtype)

    def body_acc(a_ref, b_ref, o_ref, acc_ref):
        k = pl.program_id(2)

        @pl.when(k == 0)
        def _():
            acc_ref[...] = jnp.zeros_like(acc_ref)

        acc_ref[...] += _dot(a_ref[...], b_ref[...], 1, 1 if tb else 0)

        @pl.when(k == nk - 1)
        def _():
            o_ref[...] = acc_ref[...].astype(o_ref.dtype)

    return pl.pallas_call(
        body_single if nk == 1 else body_acc, name=name, grid=(m // tm, n // tn, nk),
        in_specs=[a_spec, b_spec], out_specs=pl.BlockSpec((tm, tn), lambda i, j, k: (i, j)),
        out_shape=jax.ShapeDtypeStruct((m, n), out_dtype),
        scratch_shapes=[] if nk == 1 else [pltpu.VMEM((tm, tn), F32)],
        compiler_params=_params(("parallel", "parallel", "arbitrary")),
    )(a, b)


_TOKEN_SPEC = pl.BlockSpec((SUBLANES, LANES), lambda i: (0, 0))


def rmsnorm_fwd(name, x, w, res=None, out_dtype=F32, transposed=False, after=None):
    rows, d = x.shape
    tr = ROW_TILE
    row_spec = pl.BlockSpec((tr, d), lambda i: (i, 0))
    w_spec = pl.BlockSpec((1, d), lambda i: (0, 0))
    n_in = 2 + (res is not None) + (after is not None)

    def body(*refs):
        x_ref, w_ref = refs[:2]
        outs = refs[n_in:]
        xv = x_ref[...]
        y = xv * lax.rsqrt(jnp.mean(xv * xv, axis=-1, keepdims=True) + EPS) * w_ref[...]
        if res is not None:
            y = refs[2][...] + y
        outs[0][...] = y.astype(out_dtype)
        if transposed:
            outs[1][...] = y.T.astype(out_dtype)

    args = [x, w.reshape(1, d)] + ([] if res is None else [res]) + ([] if after is None else [after])
    specs = [row_spec, w_spec] + ([] if res is None else [row_spec]) + ([] if after is None else [_TOKEN_SPEC])
    out_specs, out_shape = [row_spec], [jax.ShapeDtypeStruct((rows, d), out_dtype)]
    if transposed:
        out_specs.append(pl.BlockSpec((d, tr), lambda i: (0, i)))
        out_shape.append(jax.ShapeDtypeStruct((d, rows), out_dtype))
    outs = pl.pallas_call(
        body, name=name, grid=(rows // tr,), in_specs=specs, out_specs=out_specs, out_shape=out_shape,
        compiler_params=_params(("parallel",)),
    )(*args)
    return outs if transposed else outs[0]


def rmsnorm_bwd(name, x, w, dy, add=None, after=None):
    rows, d = x.shape
    tr = ROW_TILE
    row_spec = pl.BlockSpec((tr, d), lambda i: (i, 0))
    w_spec = pl.BlockSpec((1, d), lambda i: (0, 0))

    def body(*refs):
        x_ref, w_ref, dy_ref = refs[:3]
        add_ref = None if add is None else refs[3]
        dx_ref, dw_ref = refs[-2:]
        xv = x_ref[...]
        dyv = dy_ref[...].astype(F32)
        r = lax.rsqrt(jnp.mean(xv * xv, axis=-1, keepdims=True) + EPS)
        xh = xv * r
        dxh = dyv * w_ref[...]
        dx = r * (dxh - xh * jnp.mean(dxh * xh, axis=-1, keepdims=True))
        if add is not None:
            dx = dx + add_ref[...]
        dx_ref[...] = dx

        @pl.when(pl.program_id(0) == 0)
        def _():
            dw_ref[...] = jnp.zeros_like(dw_ref)

        dw_ref[...] += jnp.sum(dyv * xh, axis=0, keepdims=True)

    args = [x, w.reshape(1, d), dy] + ([] if add is None else [add]) + ([] if after is None else [after])
    specs = [row_spec, w_spec, row_spec] + ([] if add is None else [row_spec]) + ([] if after is None else [_TOKEN_SPEC])
    dx, dw = pl.pallas_call(
        body, name=name, grid=(rows // tr,), in_specs=specs, out_specs=[row_spec, w_spec],
        out_shape=[jax.ShapeDtypeStruct((rows, d), F32), jax.ShapeDtypeStruct((1, d), F32)],
        compiler_params=_params(("arbitrary",)),
    )(*args)
    return dx, dw.reshape(d)


def loss_head(name, h, target):
    rows, d = h.shape
    tr = ROW_TILE
    first = (PAD_FRONT + N_META) // tr
    assert (PAD_FRONT + N_META) % tr == 0 and target.shape[0] == rows - first * tr

    def body(h_ref, t_ref, dh_ref, loss_ref):
        i = pl.program_id(0)

        @pl.when(i == 0)
        def _():
            loss_ref[...] = jnp.zeros_like(loss_ref)

        keep = (i >= first).astype(F32)
        diff = (h_ref[...] - t_ref[...]) * keep
        dh_ref[...] = diff / d
        loss_ref[...] += 0.5 * jnp.sum(diff * diff) / d

    dh, loss = pl.pallas_call(
        body, name=name, grid=(rows // tr,),
        in_specs=[pl.BlockSpec((tr, d), lambda i: (i, 0)), pl.BlockSpec((tr, d), lambda i: (jnp.maximum(i - first, 0), 0))],
        out_specs=[pl.BlockSpec((tr, d), lambda i: (i, 0)), pl.BlockSpec((SUBLANES, LANES), lambda i: (0, 0))],
        out_shape=[jax.ShapeDtypeStruct((rows, d), F32), jax.ShapeDtypeStruct((SUBLANES, LANES), F32)],
        compiler_params=_params(("arbitrary",)),
    )(h, target)
    return dh, loss[0, 0]


def _mm_call(name, a, b, grid, a_spec, b_spec, o_spec, out_sds, ca, cb, red_axis=None, pairs=None):
    n_red = None if red_axis is None else grid[red_axis]

    def body_single(a_ref, b_ref, o_ref):
        o_ref[...] = _dot(a_ref[...], b_ref[...], ca, cb).astype(o_ref.dtype)

    def step_product(a_ref, b_ref):
        if pairs is None:
            return _dot(a_ref[...], b_ref[...], ca, cb)
        return functools.reduce(lambda s, t: s + t, [_dot(a_ref[p], b_ref[p], ca, cb) for p in range(pairs)])

    def body_acc(a_ref, b_ref, o_ref, acc_ref):
        k = pl.program_id(red_axis)

        @pl.when(k == 0)
        def _():
            acc_ref[...] = jnp.zeros_like(acc_ref)

        acc_ref[...] += step_product(a_ref, b_ref)

        @pl.when(k == n_red - 1)
        def _():
            o_ref[...] = acc_ref[...].astype(o_ref.dtype)

    acc_shape = tuple(s for s in o_spec.block_shape if s is not None)
    sem = tuple("arbitrary" if ax == red_axis else "parallel" for ax in range(len(grid)))
    return pl.pallas_call(
        body_single if red_axis is None else body_acc, name=name, grid=grid, in_specs=[a_spec, b_spec], out_specs=o_spec,
        out_shape=out_sds, scratch_shapes=[] if red_axis is None else [pltpu.VMEM(acc_shape, F32)],
        compiler_params=_params(sem),
    )(a, b)


def ffn_in_fwd(name, u2, w_gu):
    rows, d = u2.shape
    hs = w_gu.shape[-1]
    return _mm_call(name, u2, w_gu, (N_DEV, 2), pl.BlockSpec((rows, d), lambda k, t: (0, 0)),
                    pl.BlockSpec((None, None, d, hs), lambda k, t: (k, t, 0, 0)),
                    pl.BlockSpec((None, None, rows, hs), lambda k, t: (k, t, 0, 0)),
                    jax.ShapeDtypeStruct((N_DEV, 2, rows, hs), F32), 1, 0)


def ffn_out_fwd(name, act, w_down):
    _, rows, hs = act.shape
    d = w_down.shape[-1]
    tm, tn = _tile(rows, 1152, 16), _tile(d, 1024, LANES)
    half = N_DEV // 2
    return _mm_call(name, act.reshape(half, 2, rows, hs), w_down.reshape(half, 2, hs, d), (rows // tm, d // tn, half),
                    pl.BlockSpec((None, 2, tm, hs), lambda i, j, k: (k, 0, i, 0)),
                    pl.BlockSpec((None, 2, hs, tn), lambda i, j, k: (k, 0, 0, j)), pl.BlockSpec((tm, tn), lambda i, j, k: (i, j)),
                    jax.ShapeDtypeStruct((rows, d), F32), 1, 0, red_axis=2, pairs=2)


def ffn_out_bwd_x(name, df, w_down):
    rows, d = df.shape
    hs = w_down.shape[1]
    return _mm_call(name, df, w_down, (N_DEV,), pl.BlockSpec((rows, d), lambda k: (0, 0)),
                    pl.BlockSpec((None, hs, d), lambda k: (k, 0, 0)), pl.BlockSpec((None, rows, hs), lambda k: (k, 0, 0)),
                    jax.ShapeDtypeStruct((N_DEV, rows, hs), F32), 1, 1)


def ffn_out_bwd_w(name, act_t, df):
    _, hs, rows = act_t.shape
    d = df.shape[1]
    return _mm_call(name, act_t, df, (N_DEV,), pl.BlockSpec((None, hs, rows), lambda k: (k, 0, 0)),
                    pl.BlockSpec((rows, d), lambda k: (0, 0)), pl.BlockSpec((None, hs, d), lambda k: (k, 0, 0)),
                    jax.ShapeDtypeStruct((N_DEV, hs, d), BF16), 1, 0)


def ffn_in_bwd_w(name, u2_t, dgp):
    d, rows = u2_t.shape
    hs = dgp.shape[-1]
    return _mm_call(name, u2_t, dgp, (N_DEV, 2), pl.BlockSpec((d, rows), lambda k, t: (0, 0)),
                    pl.BlockSpec((None, None, rows, hs), lambda k, t: (k, t, 0, 0)),
                    pl.BlockSpec((None, None, d, hs), lambda k, t: (k, t, 0, 0)),
                    jax.ShapeDtypeStruct((N_DEV, 2, d, hs), BF16), 1, 0)


def ffn_in_bwd_x(name, dgp, w_gu):
    _, _, rows, hs = dgp.shape
    d = w_gu.shape[2]
    tm, tn = _tile(rows, 1152, 16), _tile(d, 1024, LANES)
    return _mm_call(name, dgp, w_gu, (rows // tm, d // tn, N_DEV),
                    pl.BlockSpec((None, 2, tm, hs), lambda i, j, k: (k, 0, i, 0)),
                    pl.BlockSpec((None, 2, tn, hs), lambda i, j, k: (k, 0, j, 0)),
                    pl.BlockSpec((tm, tn), lambda i, j, k: (i, j)), jax.ShapeDtypeStruct((rows, d), F32), 1, 1, red_axis=2, pairs=2)


def swiglu_fwd(name, gp):
    _, _, rows, hs = gp.shape
    tr = _tile(rows, 768, ROW_TILE)

    def body(gp_ref, a_ref, at_ref):
        g = gp_ref[0]
        act = g * _sigmoid(g) * gp_ref[1]
        a_ref[...] = act.astype(a_ref.dtype)
        at_ref[...] = act.T.astype(at_ref.dtype)

    return pl.pallas_call(
        body, name=name, grid=(N_DEV, rows // tr),
        in_specs=[pl.BlockSpec((None, 2, tr, hs), lambda k, i: (k, 0, i, 0))],
        out_specs=[pl.BlockSpec((None, tr, hs), lambda k, i: (k, i, 0)), pl.BlockSpec((None, hs, tr), lambda k, i: (k, 0, i))],
        out_shape=[jax.ShapeDtypeStruct((N_DEV, rows, hs), BF16), jax.ShapeDtypeStruct((N_DEV, hs, rows), BF16)],
        compiler_params=_params(("parallel", "parallel")),
    )(gp)


def swiglu_bwd(name, gp, dact):
    _, _, rows, hs = gp.shape
    tr = _tile(rows, 768, ROW_TILE)

    def body(gp_ref, da_ref, o_ref):
        g = gp_ref[0]
        s = _sigmoid(g)
        dav = da_ref[...]
        o_ref[0] = (dav * gp_ref[1] * (s * (1.0 + g * (1.0 - s)))).astype(o_ref.dtype)
        o_ref[1] = (dav * (g * s)).astype(o_ref.dtype)

    blk = pl.BlockSpec((None, 2, tr, hs), lambda k, i: (k, 0, i, 0))
    return pl.pallas_call(
        body, name=name, grid=(N_DEV, rows // tr),
        in_specs=[blk, pl.BlockSpec((None, tr, hs), lambda k, i: (k, i, 0))], out_specs=blk,
        out_shape=jax.ShapeDtypeStruct(gp.shape, BF16), compiler_params=_params(("parallel", "parallel")),
    )(gp, dact)


def conv_fwd(name, zx, conv_w, conv_b, d_inner, conv_dim):
    rows = zx.shape[0]
    tc = CONV_TILE
    off = d_inner // tc
    assert d_inner % tc == 0 and conv_dim % tc == 0

    def body(u_ref, w_ref, b_ref, o_ref):
        u = u_ref[...]
        acc = u * w_ref[D_CONV - 1:D_CONV, :] + b_ref[...]
        for s in range(1, D_CONV):
            acc = acc + pltpu.roll(u, s, axis=0) * w_ref[D_CONV - 1 - s:D_CONV - s, :]
        y = acc * _sigmoid(acc)
        o_ref[...] = jnp.where(_row_mask(y.shape, 0), y, 0.0)

    return pl.pallas_call(
        body, name=name, grid=(conv_dim // tc,),
        in_specs=[pl.BlockSpec((rows, tc), lambda j: (0, off + j)), pl.BlockSpec((D_CONV, tc), lambda j: (0, j)),
                  pl.BlockSpec((1, tc), lambda j: (0, j))],
        out_specs=pl.BlockSpec((rows, tc), lambda j: (0, j)),
        out_shape=jax.ShapeDtypeStruct((rows, conv_dim), F32), compiler_params=_params(("parallel",)),
    )(zx, conv_w, conv_b.reshape(1, conv_dim))


def conv_bwd(name, zx, conv_w, conv_b, dxbc, d_inner, conv_dim):
    rows = zx.shape[0]
    tc = CONV_TILE
    off = d_inner // tc

    def body(u_ref, w_ref, b_ref, dy_ref, du_ref, dw_ref, db_ref):
        u = u_ref[...]
        wk = [w_ref[D_CONV - 1 - s:D_CONV - s, :] for s in range(D_CONV)]
        shifted = [u] + [pltpu.roll(u, s, axis=0) for s in range(1, D_CONV)]
        acc = u * wk[0] + b_ref[...]
        for s in range(1, D_CONV):
            acc = acc + shifted[s] * wk[s]
        sg = _sigmoid(acc)
        mask = _row_mask(acc.shape, 0)
        dpre = jnp.where(mask, dy_ref[...] * (sg * (1.0 + acc * (1.0 - sg))), 0.0)
        db_ref[...] = jnp.sum(dpre, axis=0, keepdims=True)
        du = dpre * wk[0]
        dw_ref[D_CONV - 1:D_CONV, :] = jnp.sum(dpre * u, axis=0, keepdims=True)
        for s in range(1, D_CONV):
            du = du + pltpu.roll(dpre, rows - s, axis=0) * wk[s]
            dw_ref[D_CONV - 1 - s:D_CONV - s, :] = jnp.sum(dpre * shifted[s], axis=0, keepdims=True)
        du_ref[...] = jnp.where(mask, du, 0.0).astype(du_ref.dtype)

    return pl.pallas_call(
        body, name=name, grid=(conv_dim // tc,),
        in_specs=[pl.BlockSpec((rows, tc), lambda j: (0, off + j)), pl.BlockSpec((D_CONV, tc), lambda j: (0, j)),
                  pl.BlockSpec((1, tc), lambda j: (0, j)), pl.BlockSpec((rows, tc), lambda j: (0, j))],
        out_specs=[pl.BlockSpec((rows, tc), lambda j: (0, j)), pl.BlockSpec((D_CONV, tc), lambda j: (0, j)),
                   pl.BlockSpec((1, tc), lambda j: (0, j))],
        out_shape=[jax.ShapeDtypeStruct((rows, conv_dim), BF16), jax.ShapeDtypeStruct((D_CONV, conv_dim), F32),
                   jax.ShapeDtypeStruct((1, conv_dim), F32)],
        compiler_params=_params(("parallel",)),
    )(zx, conv_w, conv_b.reshape(1, conv_dim), dxbc)


def dt_fwd(name, zx, bias_pad, zx_cols):
    rows = zx.shape[0]
    tr = ROW_TILE
    off = zx_cols // LANES

    def body(r_ref, b_ref, o_ref):
        v = r_ref[...] + b_ref[...]
        sp = jnp.maximum(v, 0.0) + jnp.log1p(jnp.exp(-jnp.abs(v)))
        o_ref[...] = jnp.where(_row_mask(v.shape, pl.program_id(0) * tr), sp, 0.0)

    return pl.pallas_call(
        body, name=name, grid=(rows // tr,),
        in_specs=[pl.BlockSpec((tr, LANES), lambda i: (i, off)), pl.BlockSpec((1, LANES), lambda i: (0, 0))],
        out_specs=pl.BlockSpec((tr, LANES), lambda i: (i, 0)),
        out_shape=jax.ShapeDtypeStruct((rows, LANES), F32), compiler_params=_params(("parallel",)),
    )(zx, bias_pad)


def dt_bwd(name, zx, bias_pad, ddt, zx_cols):
    rows = zx.shape[0]
    tr = ROW_TILE
    off = zx_cols // LANES

    def body(r_ref, b_ref, d_ref, o_ref, db_ref):
        v = r_ref[...] + b_ref[...]
        g = jnp.where(_row_mask(v.shape, pl.program_id(0) * tr), d_ref[...] * _sigmoid(v), 0.0)
        o_ref[...] = g.astype(o_ref.dtype)

        @pl.when(pl.program_id(0) == 0)
        def _():
            db_ref[...] = jnp.zeros_like(db_ref)

        db_ref[...] += jnp.sum(g, axis=0, keepdims=True)

    return pl.pallas_call(
        body, name=name, grid=(rows // tr,),
        in_specs=[pl.BlockSpec((tr, LANES), lambda i: (i, off)), pl.BlockSpec((1, LANES), lambda i: (0, 0)),
                  pl.BlockSpec((tr, LANES), lambda i: (i, 0))],
        out_specs=[pl.BlockSpec((tr, LANES), lambda i: (i, 0)), pl.BlockSpec((1, LANES), lambda i: (0, 0))],
        out_shape=[jax.ShapeDtypeStruct((rows, LANES), BF16), jax.ShapeDtypeStruct((1, LANES), F32)],
        compiler_params=_params(("arbitrary",)),
    )(zx, bias_pad, ddt)


def _split3(x):
    h1 = x.astype(BF16)
    r1 = x - h1.astype(F32)
    h2 = r1.astype(BF16)
    h3 = (r1 - h2.astype(F32)).astype(BF16)
    return h1, h2, h3


def _exact_left(ones_b, x):
    h1, h2, h3 = _split3(x)
    return _dot(ones_b, h1, 1, 0) + _dot(ones_b, h2, 1, 0) + _dot(ones_b, h3, 1, 0)


def _exact_right_t(x, ones_b):
    h1, h2, h3 = _split3(x)
    return _dot(h1, ones_b, 1, 1) + _dot(h2, ones_b, 1, 1) + _dot(h3, ones_b, 1, 1)


class _ScanCommon:
    def __init__(self, b_ref, c_ref, dtc_ref, dtr_ref, arow_ref, acol_ref, drow_ref):
        q = CHUNK
        self.bb = b_ref[...].astype(BF16)
        self.cb = c_ref[...].astype(BF16)
        ri = lax.broadcasted_iota(jnp.int32, (q, q), 0)
        cj = lax.broadcasted_iota(jnp.int32, (q, q), 1)
        self.lower = ri >= cj
        self.upper = cj >= ri
        self.dtc = dtc_ref[...]
        self.dtr = dtr_ref[...]
        self.arow = arow_ref[...]
        self.acol = acol_ref[...]
        self.drow = drow_ref[...]
        da_col = self.dtc * self.arow
        self.a_col = _exact_left(self.lower.astype(BF16), da_col)
        self.a_row = _exact_right_t(self.dtr * self.acol, self.lower.astype(BF16))
        self.a_last = jnp.sum(da_col, axis=0, keepdims=True)
        self.lane_q = lax.broadcasted_iota(jnp.int32, (q, LANES), 1)
        self.lane_1 = lax.broadcasted_iota(jnp.int32, (1, LANES), 1)
        self.sub_8 = lax.broadcasted_iota(jnp.int32, (SUBLANES, q), 0)
        self.first_half = self.lane_q < HEAD_DIM
        self.first_rows = lax.broadcasted_iota(jnp.int32, (LANES, 1), 0) < HEAD_DIM

    def col(self, v, r):
        return jnp.sum(jnp.where(self.lane_q == r, v, 0.0), axis=1, keepdims=True)

    def row(self, v, r):
        return jnp.sum(jnp.where(self.sub_8 == r, v, 0.0), axis=0, keepdims=True)

    def scalar(self, v, r):
        return jnp.sum(jnp.where(self.lane_1 == r, v, 0.0), axis=1, keepdims=True)

    def pair(self, v0, v1):
        return jnp.where(self.first_half, v0, v1)

    def half_rowsum(self, t, h):
        keep = self.first_half if h == 0 else jnp.logical_not(self.first_half)
        return jnp.sum(jnp.where(keep, t, 0.0), axis=1, keepdims=True)


def ssd_scan_fwd(name, xbc, dt_col, dt_row, a_row, a_col, d_row, d_inner):
    rows = xbc.shape[0]
    q, n, g_cnt = CHUNK, D_STATE, SSD_GROUPS
    nc = rows // q
    rp = d_inner // g_cnt
    n_pairs = rp // LANES
    b_off = d_inner // n

    def body(x_ref, b_ref, c_ref, dtc_ref, dtr_ref, arow_ref, acol_ref, drow_ref, y_ref, sprev_ref, s_ref):
        @pl.when(pl.program_id(1) == 0)
        def _():
            s_ref[...] = jnp.zeros_like(s_ref)

        sprev_ref[...] = s_ref[...]
        k = _ScanCommon(b_ref, c_ref, dtc_ref, dtr_ref, arow_ref, acol_ref, drow_ref)
        cb_mat = _dot(k.cb, k.bb, 1, 1)
        for pr in range(n_pairs):
            sl = slice(pr * LANES, (pr + 1) * LANES)
            heads = (2 * pr, 2 * pr + 1)
            xp = x_ref[:, sl]
            ac = [k.col(k.a_col, r) for r in heads]
            ar = [k.row(k.a_row, r) for r in heads]
            al = [k.scalar(k.a_last, r) for r in heads]
            xd = xp * k.pair(k.col(k.dtc, heads[0]), k.col(k.dtc, heads[1]))
            xdb = xd.astype(BF16)
            ys = []
            for h in range(2):
                lm = jnp.exp(jnp.where(k.lower, ac[h] - ar[h], -jnp.inf))
                ys.append(_dot((cb_mat * lm).astype(BF16), xdb, 1, 0))
            y = jnp.where(k.first_half, ys[0], ys[1])
            sp = s_ref[sl, :]
            y = y + k.pair(jnp.exp(ac[0]), jnp.exp(ac[1])) * _dot(k.cb, sp.astype(BF16), 1, 1)
            y = y + k.pair(k.scalar(k.drow, heads[0]), k.scalar(k.drow, heads[1])) * xp
            y_ref[:, sl] = y
            wb = (xd * k.pair(jnp.exp(al[0] - ac[0]), jnp.exp(al[1] - ac[1]))).astype(BF16)
            decay = jnp.where(k.first_rows, jnp.exp(al[0]), jnp.exp(al[1]))
            s_ref[sl, :] = decay * sp + _dot(wb, k.bb, 0, 0)

    return pl.pallas_call(
        body, name=name, grid=(g_cnt, nc),
        in_specs=[
            pl.BlockSpec((q, rp), lambda g, c: (c, g)),
            pl.BlockSpec((q, n), lambda g, c: (c, b_off + g)),
            pl.BlockSpec((q, n), lambda g, c: (c, b_off + g_cnt + g)),
            pl.BlockSpec((None, q, LANES), lambda g, c: (g, c, 0)),
            pl.BlockSpec((None, SUBLANES, q), lambda g, c: (g, 0, c)),
            pl.BlockSpec((None, 1, LANES), lambda g, c: (g, 0, 0)),
            pl.BlockSpec((None, SUBLANES, 1), lambda g, c: (g, 0, 0)),
            pl.BlockSpec((None, 1, LANES), lambda g, c: (g, 0, 0)),
        ],
        out_specs=[pl.BlockSpec((q, rp), lambda g, c: (c, g)),
                   pl.BlockSpec((None, None, rp, n), lambda g, c: (c, g, 0, 0))],
        out_shape=[jax.ShapeDtypeStruct((rows, d_inner), F32), jax.ShapeDtypeStruct((nc, g_cnt, rp, n), F32)],
        scratch_shapes=[pltpu.VMEM((rp, n), F32)],
        compiler_params=_params(("parallel", "arbitrary")),
    )(xbc, xbc, xbc, dt_col, dt_row, a_row, a_col, d_row)


def ssd_scan_bwd(name, xbc, dt_col, dt_row, a_row, a_col, d_row, sprev, dy, d_inner):
    rows = xbc.shape[0]
    q, n, g_cnt = CHUNK, D_STATE, SSD_GROUPS
    nc = rows // q
    rp = d_inner // g_cnt
    n_pairs = rp // LANES
    b_off = d_inner // n

    def body(x_ref, b_ref, c_ref, dtc_ref, dtr_ref, arow_ref, acol_ref, drow_ref, sprev_ref, dy_ref,
             dx_ref, db_ref, dc_ref, ddtc_ref, ddtr_ref, dar_ref, dac_ref, dd_ref, ds_ref):
        @pl.when(pl.program_id(1) == 0)
        def _():
            ds_ref[...] = jnp.zeros_like(ds_ref)
            dar_ref[...] = jnp.zeros_like(dar_ref)
            dac_ref[...] = jnp.zeros_like(dac_ref)
            dd_ref[...] = jnp.zeros_like(dd_ref)

        k = _ScanCommon(b_ref, c_ref, dtc_ref, dtr_ref, arow_ref, acol_ref, drow_ref)
        cb_mat = _dot(k.cb, k.bb, 1, 1)
        cbt_mat = _dot(k.bb, k.cb, 1, 1)
        d_cb = jnp.zeros((q, q), F32)
        d_b = jnp.zeros((q, n), F32)
        d_c = jnp.zeros((q, n), F32)
        da_col = jnp.zeros((q, LANES), F32)
        da_row = jnp.zeros((SUBLANES, q), F32)
        ddt_x = jnp.zeros((q, LANES), F32)
        d_alast = jnp.zeros((1, LANES), F32)
        d_dskip = jnp.zeros((1, LANES), F32)
        for pr in range(n_pairs):
            sl = slice(pr * LANES, (pr + 1) * LANES)
            heads = (2 * pr, 2 * pr + 1)
            xp = x_ref[:, sl]
            dyp = dy_ref[:, sl]
            dyb = dyp.astype(BF16)
            ac = [k.col(k.a_col, r) for r in heads]
            ar = [k.row(k.a_row, r) for r in heads]
            al = [k.scalar(k.a_last, r) for r in heads]
            dt_p = k.pair(k.col(k.dtc, heads[0]), k.col(k.dtc, heads[1]))
            xd = xp * dt_p
            xdb = xd.astype(BF16)
            sp = sprev_ref[sl, :]
            spb = sp.astype(BF16)
            dsp = ds_ref[sl, :]
            dspb = dsp.astype(BF16)
            dskip_p = k.pair(k.scalar(k.drow, heads[0]), k.scalar(k.drow, heads[1]))
            dxp = dskip_p * dyp
            dd_lane = jnp.sum(dyp * xp, axis=0, keepdims=True)
            e_p = k.pair(jnp.exp(ac[0]), jnp.exp(ac[1]))
            t_off = dyp * (e_p * _dot(k.cb, spb, 1, 1))
            dzb = (e_p * dyp).astype(BF16)
            d_c = d_c + _dot(dzb, spb, 1, 0)
            ds_in = _dot(dzb, k.cb, 0, 0)
            decay = jnp.where(k.first_rows, jnp.exp(al[0]), jnp.exp(al[1]))
            ds_in = ds_in + decay * dsp
            t_state = jnp.sum(dsp * sp, axis=1, keepdims=True) * decay
            dec_p = k.pair(jnp.exp(al[0] - ac[0]), jnp.exp(al[1] - ac[1]))
            dw = _dot(k.bb, dspb, 1, 1)
            d_b = d_b + _dot((xd * dec_p).astype(BF16), dspb, 1, 0)
            dxd = dw * dec_p
            t_dec = dw * xd * dec_p
            dxd_h = []
            for h in range(2):
                r = heads[h]
                keep = k.first_half if h == 0 else jnp.logical_not(k.first_half)
                lm = jnp.exp(jnp.where(k.lower, ac[h] - ar[h], -jnp.inf))
                m_mat = cb_mat * lm
                dm = _dot(jnp.where(keep, dyp, 0.0).astype(BF16), xdb, 1, 1)
                dseg = dm * m_mat
                d_cb = d_cb + dm * lm
                lmt = jnp.exp(jnp.where(k.upper, ar[h] - ac[h], -jnp.inf))
                dxd_h.append(_dot((cbt_mat * lmt).astype(BF16), dyb, 1, 0))
                tdec_h = k.half_rowsum(t_dec, h)
                da_h = k.half_rowsum(t_off, h) - tdec_h + jnp.sum(dseg, axis=1, keepdims=True)
                da_col = da_col + jnp.where(k.lane_q == r, da_h, 0.0)
                da_row = da_row - jnp.where(k.sub_8 == r, jnp.sum(dseg, axis=0, keepdims=True), 0.0)
                keep_rows = k.first_rows if h == 0 else jnp.logical_not(k.first_rows)
                dal_h = jnp.sum(tdec_h, axis=0, keepdims=True) + jnp.sum(jnp.where(keep_rows, t_state, 0.0), axis=0, keepdims=True)
                d_alast = d_alast + jnp.where(k.lane_1 == r, dal_h, 0.0)
                keep_1 = k.lane_1 < HEAD_DIM if h == 0 else k.lane_1 >= HEAD_DIM
                dd_h = jnp.sum(jnp.where(keep_1, dd_lane, 0.0), axis=1, keepdims=True)
                d_dskip = d_dskip + jnp.where(k.lane_1 == r, dd_h, 0.0)
            dxd = dxd + jnp.where(k.first_half, dxd_h[0], dxd_h[1])
            dx_ref[:, sl] = dxp + dt_p * dxd
            t_dt = dxd * xp
            for h in range(2):
                ddt_x = ddt_x + jnp.where(k.lane_q == heads[h], k.half_rowsum(t_dt, h), 0.0)
            ds_ref[sl, :] = ds_in
        d_cb_b = d_cb.astype(BF16)
        db_ref[...] = d_b + _dot(d_cb_b, k.cb, 0, 0)
        dc_ref[...] = d_c + _dot(d_cb_b, k.bb, 1, 0)
        rc_col = _exact_left(k.upper.astype(BF16), da_col) + d_alast
        rc_row = _exact_right_t(da_row, k.upper.astype(BF16))
        ddtc_ref[...] = ddt_x + k.arow * rc_col
        ddtr_ref[...] = k.acol * rc_row
        dar_ref[...] += jnp.sum(rc_col * k.dtc, axis=0, keepdims=True)
        dac_ref[...] += jnp.sum(rc_row * k.dtr, axis=1, keepdims=True)
        dd_ref[...] += d_dskip

    rc = lambda c: nc - 1 - c
    return pl.pallas_call(
        body, name=name, grid=(g_cnt, nc),
        in_specs=[
            pl.BlockSpec((q, rp), lambda g, c: (rc(c), g)),
            pl.BlockSpec((q, n), lambda g, c: (rc(c), b_off + g)),
            pl.BlockSpec((q, n), lambda g, c: (rc(c), b_off + g_cnt + g)),
            pl.BlockSpec((None, q, LANES), lambda g, c: (g, rc(c), 0)),
            pl.BlockSpec((None, SUBLANES, q), lambda g, c: (g, 0, rc(c))),
            pl.BlockSpec((None, 1, LANES), lambda g, c: (g, 0, 0)),
            pl.BlockSpec((None, SUBLANES, 1), lambda g, c: (g, 0, 0)),
            pl.BlockSpec((None, 1, LANES), lambda g, c: (g, 0, 0)),
            pl.BlockSpec((None, None, rp, n), lambda g, c: (rc(c), g, 0, 0)),
            pl.BlockSpec((q, rp), lambda g, c: (rc(c), g)),
        ],
        out_specs=[
            pl.BlockSpec((q, rp), lambda g, c: (rc(c), g)),
            pl.BlockSpec((q, n), lambda g, c: (rc(c), g)),
            pl.BlockSpec((q, n), lambda g, c: (rc(c), g)),
            pl.BlockSpec((None, q, LANES), lambda g, c: (g, rc(c), 0)),
            pl.BlockSpec((None, SUBLANES, q), lambda g, c: (g, 0, rc(c))),
            pl.BlockSpec((None, 1, LANES), lambda g, c: (g, 0, 0)),
            pl.BlockSpec((None, SUBLANES, 1), lambda g, c: (g, 0, 0)),
            pl.BlockSpec((None, 1, LANES), lambda g, c: (g, 0, 0)),
        ],
        out_shape=[
            jax.ShapeDtypeStruct((rows, d_inner), F32),
            jax.ShapeDtypeStruct((rows, g_cnt * n), F32),
            jax.ShapeDtypeStruct((rows, g_cnt * n), F32),
            jax.ShapeDtypeStruct((g_cnt, rows, LANES), F32),
            jax.ShapeDtypeStruct((g_cnt, SUBLANES, rows), F32),
            jax.ShapeDtypeStruct((g_cnt, 1, LANES), F32),
            jax.ShapeDtypeStruct((g_cnt, SUBLANES, 1), F32),
            jax.ShapeDtypeStruct((g_cnt, 1, LANES), F32),
        ],
        scratch_shapes=[pltpu.VMEM((rp, n), F32)],
        compiler_params=_params(("parallel", "arbitrary")),
    )(xbc, xbc, xbc, dt_col, dt_row, a_row, a_col, d_row, sprev, dy)


def gatenorm_fwd(name, y, zx, w, d_inner):
    rows = y.shape[0]
    tr, gw = ROW_TILE, d_inner // SSD_GROUPS

    def body(y_ref, z_ref, w_ref, o_ref, ot_ref):
        z = z_ref[...]
        v = y_ref[...] * (z * _sigmoid(z))
        out = v * lax.rsqrt(jnp.mean(v * v, axis=-1, keepdims=True) + EPS) * w_ref[...]
        o_ref[...] = out.astype(o_ref.dtype)
        ot_ref[...] = out.T.astype(ot_ref.dtype)

    blk = pl.BlockSpec((tr, gw), lambda i, j: (i, j))
    return pl.pallas_call(
        body, name=name, grid=(rows // tr, SSD_GROUPS),
        in_specs=[blk, blk, pl.BlockSpec((1, gw), lambda i, j: (0, j))],
        out_specs=[blk, pl.BlockSpec((gw, tr), lambda i, j: (j, i))],
        out_shape=[jax.ShapeDtypeStruct((rows, d_inner), BF16), jax.ShapeDtypeStruct((d_inner, rows), BF16)],
        compiler_params=_params(("parallel", "parallel")),
    )(y, zx, w.reshape(1, d_inner))


def gatenorm_bwd(name, y, zx, w, dyn, d_inner, after):
    rows = y.shape[0]
    tr, gw = ROW_TILE, d_inner // SSD_GROUPS

    def body(y_ref, z_ref, w_ref, dn_ref, after_ref, dy_ref, dz_ref, dw_ref):
        z = z_ref[...]
        yv = y_ref[...]
        s = _sigmoid(z)
        gate = z * s
        v = yv * gate
        r = lax.rsqrt(jnp.mean(v * v, axis=-1, keepdims=True) + EPS)
        vh = v * r
        dn = dn_ref[...]
        dvh = dn * w_ref[...]
        dv = r * (dvh - vh * jnp.mean(dvh * vh, axis=-1, keepdims=True))
        dy_ref[...] = dv * gate
        dz_ref[...] = (dv * yv * (s * (1.0 + z * (1.0 - s)))).astype(dz_ref.dtype)

        @pl.when(pl.program_id(1) == 0)
        def _():
            dw_ref[...] = jnp.zeros_like(dw_ref)

        dw_ref[...] += jnp.sum(dn * vh, axis=0, keepdims=True)

    blk = pl.BlockSpec((tr, gw), lambda j, i: (i, j))
    wblk = pl.BlockSpec((1, gw), lambda j, i: (0, j))
    dy, dz, dw = pl.pallas_call(
        body, name=name, grid=(SSD_GROUPS, rows // tr),
        in_specs=[blk, blk, wblk, blk, pl.BlockSpec((SUBLANES, LANES), lambda j, i: (0, 0))], out_specs=[blk, blk, wblk],
        out_shape=[jax.ShapeDtypeStruct((rows, d_inner), F32), jax.ShapeDtypeStruct((rows, d_inner), BF16),
                   jax.ShapeDtypeStruct((1, d_inner), F32)],
        compiler_params=_params(("parallel", "arbitrary")),
    )(y, zx, w.reshape(1, d_inner), dyn, after)
    return dy, dz, dw.reshape(d_inner)


def _pool_count(rows, g):
    t1 = lax.broadcasted_iota(jnp.int32, (rows, 1), 0) - (PAD_FRONT - 1)
    win = jnp.left_shift(jnp.int32(POOL_WINDOWS[0]), g)
    return jnp.clip(t1, 1, win).astype(F32)


def _pool_select(levels, g):
    out = levels[-1]
    for i in range(len(levels) - 2, -1, -1):
        out = jnp.where(g == i, levels[i], out)
    return out


def pool_sub(name, u, transpose):
    rows, d = u.shape
    gd = d // len(POOL_WINDOWS)
    assert all(w == POOL_WINDOWS[0] << i for i, w in enumerate(POOL_WINDOWS))

    def body(u_ref, o_ref):
        g = pl.program_id(0)
        v = u_ref[...].astype(F32)
        cnt = _pool_count(rows, g)
        mask = _row_mask(v.shape, 0)
        s = v / cnt if transpose else v
        levels = []
        for i in range(len(POOL_WINDOWS)):
            step = 1 << i
            s = s + pltpu.roll(s, (rows - step) if transpose else step, axis=0)
            levels.append(s)
        sel = _pool_select(levels, g)
        out = (sel - v) if transpose else (sel / cnt - v)
        o_ref[...] = jnp.where(mask, out, 0.0).astype(o_ref.dtype)

    tc = POOL_TILE
    per = gd // tc
    blk = pl.BlockSpec((rows, tc), lambda g, j: (0, g * per + j))
    return pl.pallas_call(
        body, name=name, grid=(len(POOL_WINDOWS), per), in_specs=[blk], out_specs=blk,
        out_shape=jax.ShapeDtypeStruct((rows, d), F32 if transpose else BF16),
        compiler_params=_params(("parallel", "parallel")),
    )(u)


def pool_proj_fwd(name, mixed, w, b, scale):
    rows, d = mixed.shape
    ng = len(POOL_WINDOWS)
    gd = d // ng
    tr = ROW_TILE

    def body(m_ref, w_ref, b_ref, s_ref, pre_ref, mix_ref):
        pre = _dot(m_ref[...], w_ref[...], 1, 0) + b_ref[...]
        pre_ref[...] = pre
        mix_ref[...] = jnp.where(_row_mask(pre.shape, pl.program_id(1) * tr), pre * s_ref[...], 0.0)

    blk = pl.BlockSpec((tr, gd), lambda g, i: (i, g))
    vec = pl.BlockSpec((1, gd), lambda g, i: (0, g))
    return pl.pallas_call(
        body, name=name, grid=(ng, rows // tr),
        in_specs=[blk, pl.BlockSpec((None, gd, gd), lambda g, i: (g, 0, 0)), vec, vec], out_specs=[blk, blk],
        out_shape=[jax.ShapeDtypeStruct((rows, d), F32), jax.ShapeDtypeStruct((rows, d), F32)],
        compiler_params=_params(("parallel", "parallel")),
    )(mixed, w, b.reshape(1, d), scale.reshape(1, d))


def pool_proj_bwd(name, dmix, pre, mixed, w, scale):
    rows, d = dmix.shape
    ng = len(POOL_WINDOWS)
    gd = d // ng
    tr = ROW_TILE

    def body(dm_ref, pre_ref, mx_ref, w_ref, s_ref, dmx_ref, dw_ref, db_ref, ds_ref):
        @pl.when(pl.program_id(1) == 0)
        def _():
            dw_ref[...] = jnp.zeros_like(dw_ref)
            db_ref[...] = jnp.zeros_like(db_ref)
            ds_ref[...] = jnp.zeros_like(ds_ref)

        dmv = jnp.where(_row_mask(dm_ref.shape, pl.program_id(1) * tr), dm_ref[...], 0.0)
        dpre = dmv * s_ref[...]
        dpre_b = dpre.astype(BF16)
        ds_ref[...] += jnp.sum(dmv * pre_ref[...], axis=0, keepdims=True)
        db_ref[...] += jnp.sum(dpre, axis=0, keepdims=True)
        dmx_ref[...] = _dot(dpre_b, w_ref[...], 1, 1)
        dw_ref[...] += _dot(mx_ref[...], dpre_b, 0, 0)

    blk = pl.BlockSpec((tr, gd), lambda g, i: (i, g))
    vec = pl.BlockSpec((1, gd), lambda g, i: (0, g))
    wblk = pl.BlockSpec((None, gd, gd), lambda g, i: (g, 0, 0))
    dmixed, dw, db, ds = pl.pallas_call(
        body, name=name, grid=(ng, rows // tr),
        in_specs=[blk, blk, blk, wblk, vec], out_specs=[blk, wblk, vec, vec],
        out_shape=[jax.ShapeDtypeStruct((rows, d), F32), jax.ShapeDtypeStruct((ng, gd, gd), F32),
                   jax.ShapeDtypeStruct((1, d), F32), jax.ShapeDtypeStruct((1, d), F32)],
        compiler_params=_params(("parallel", "arbitrary")),
    )(dmix, pre, mixed, w, scale.reshape(1, d))
    return dmixed, dw, db.reshape(d), ds.reshape(d)


def _my_place():
    return lax.axis_index("x"), lax.axis_index("y"), lax.axis_index("c")


def _linear(place):
    return 4 * place[0] + 2 * place[1] + place[2]


def all_gather(name, shards, after):
    n_ops = len(shards)

    def body(*refs):
        ins, outs = refs[:n_ops], refs[n_ops + 1:2 * n_ops + 1]
        send_sems, recv_sems, local_sems = refs[2 * n_ops + 1:]
        x, y, c = _my_place()
        me, sibling = (x, y, c), (x, y, 1 - c)
        chips = [(1 - x, y), (x, 1 - y), (1 - x, 1 - y)]

        def copy(t, k, block, to, src=None):
            dst = outs[t].at[_linear(block)]
            return pltpu.make_async_remote_copy(
                src_ref=dst if src is None else src, dst_ref=dst, send_sem=send_sems.at[t, k], recv_sem=recv_sems.at[t, k],
                device_id=to, device_id_type=MESH)

        mine = [pltpu.make_async_copy(ins[t], outs[t].at[_linear(me)], local_sems.at[t]) for t in range(n_ops)]
        for cp in mine:
            cp.start()
        first = []
        for t in range(n_ops):
            first.append(copy(t, 0, me, sibling, src=ins[t]))
            first += [copy(t, 1 + j, me, (*chip, c), src=ins[t]) for j, chip in enumerate(chips)]
        for cp in first:
            cp.start()
        passed = []
        for j, chip in enumerate(chips):
            for t in range(n_ops):
                copy(t, 1 + j, (*chip, c), me).wait_recv()
                fwd = copy(t, 4 + j, (*chip, c), sibling)
                fwd.start()
                passed.append(fwd)
        for t in range(n_ops):
            copy(t, 0, sibling, me).wait_recv()
            for j, chip in enumerate(chips):
                copy(t, 4 + j, (*chip, 1 - c), me).wait_recv()
        for cp in first + passed:
            cp.wait_send()
        for cp in mine:
            cp.wait()

    any_spec = pl.BlockSpec(memory_space=pl.ANY)
    outs = pl.pallas_call(
        body, name=name,
        in_specs=[any_spec] * (n_ops + 1), out_specs=[any_spec] * n_ops,
        out_shape=[jax.ShapeDtypeStruct((N_DEV, *s.shape), s.dtype) for s in shards],
        scratch_shapes=[pltpu.SemaphoreType.DMA((n_ops, 7)), pltpu.SemaphoreType.DMA((n_ops, 7)),
                        pltpu.SemaphoreType.DMA((n_ops,))],
    )(*shards, after)
    return list(outs)


_HBM = pl.BlockSpec(memory_space=pltpu.HBM)
_SEM = pl.BlockSpec(memory_space=pltpu.SEMAPHORE)
_EFFECT = pltpu.SideEffectType.DATAFLOW_SIDE_EFFECTING


def _hbm(a):
    return pltpu.with_memory_space_constraint(a, pltpu.HBM)


def _peers():
    x, y, c = _my_place()
    return [(x ^ (j >> 2), y ^ ((j >> 1) & 1), c ^ (j & 1)) for j in range(1, N_DEV)]


def exchange_start(name, groups, after):
    flat = [e for g in groups for e in g]
    n = len(flat)

    def body(*refs):
        srcs, lands = refs[:n], refs[n:2 * n]
        outs = refs[2 * n + 1:]
        sends, recvs, token = outs[:n], outs[n:2 * n], outs[4 * n]
        me = _linear(_my_place())
        for t, (_, _, src_view, land_view) in enumerate(flat):
            for peer in _peers():
                pltpu.make_async_remote_copy(
                    src_ref=src_view(srcs[t], _linear(peer)), dst_ref=land_view(lands[t], me),
                    send_sem=sends[t], recv_sem=recvs[t], device_id=peer, device_id_type=MESH).start()
        token[...] = jnp.zeros_like(token)

    sem_shapes = [pltpu.SemaphoreType.DMA(())] * (2 * n)
    thru = [pltpu.HBM(e[0].shape, e[0].dtype) for e in flat] + [pltpu.HBM(e[1].shape, e[1].dtype) for e in flat]
    outs = pl.pallas_call(
        body, name=name,
        out_shape=(*sem_shapes, *thru, jax.ShapeDtypeStruct((SUBLANES, LANES), F32)),
        in_specs=[*[_HBM] * (2 * n), pl.BlockSpec(memory_space=pl.ANY)],
        out_specs=(*[_SEM] * (2 * n), *[_HBM] * (2 * n), pl.BlockSpec(memory_space=pltpu.VMEM)),
        input_output_aliases={t: 2 * n + t for t in range(2 * n)},
        compiler_params=pltpu.CompilerParams(has_side_effects=_EFFECT),
    )(*[_hbm(e[0]) for e in flat], *[_hbm(e[1]) for e in flat], after)
    records, t = [], 0
    for g in groups:
        k = len(g)
        records.append((list(outs[t:t + k]), list(outs[n + t:n + t + k]), list(outs[2 * n + t:2 * n + t + k]),
                        list(outs[3 * n + t:3 * n + t + k])))
        t += k
    return records, outs[-1]


def exchange_wait(name, records, lands, land_of, seven_of, after):
    srcs = [s for r in records for s in r[2]]
    sends = [s for r in records for s in r[0]]
    recvs = [s for r in records for s in r[1]]
    where = [(ri, k) for ri, r in enumerate(records) for k in range(len(r[2]))]
    ns, nl = len(srcs), len(lands)

    def body(*refs):
        src_refs, land_refs = refs[:ns], refs[ns:ns + nl]
        send_refs, recv_refs = refs[ns + nl:2 * ns + nl], refs[2 * ns + nl:3 * ns + nl]
        for t, (ri, k) in enumerate(where):
            seven_ref = seven_of[ri][k](src_refs[t], land_refs[land_of[ri][k]])
            cp = pltpu.make_async_remote_copy(src_ref=seven_ref, dst_ref=seven_ref, send_sem=send_refs[t], recv_sem=recv_refs[t],
                                              device_id=_my_place(), device_id_type=MESH)
            cp.wait_send()
            cp.wait_recv()

    outs = pl.pallas_call(
        body, name=name,
        out_shape=tuple(pltpu.HBM(a.shape, a.dtype) for a in (*srcs, *lands)),
        in_specs=[*[_HBM] * (ns + nl), *[_SEM] * (2 * ns), pl.BlockSpec(memory_space=pl.ANY)], out_specs=tuple([_HBM] * (ns + nl)),
        input_output_aliases={t: t for t in range(ns + nl)},
        compiler_params=pltpu.CompilerParams(has_side_effects=_EFFECT),
    )(*srcs, *lands, *sends, *recvs, after)
    return list(outs[:ns]), list(outs[ns:])


def _other_chips():
    x, y, c = _my_place()
    return [(1 - x, y, c), (x, 1 - y, c), (1 - x, 1 - y, c)]


def gather2_start(name, groups, after):
    flat = [e for g in groups for e in g]
    n = len(flat)

    def body(*refs):
        srcs, lands = refs[:n], refs[n:2 * n]
        outs = refs[2 * n + 1:]
        d_send, d_recv, i_send, i_recv, token = outs[:n], outs[n:2 * n], outs[2 * n:3 * n], outs[3 * n:4 * n], outs[6 * n]
        x, y, c = _my_place()
        me = _linear((x, y, c))
        for t in range(n):
            pltpu.make_async_remote_copy(src_ref=srcs[t], dst_ref=lands[t].at[me], send_sem=d_send[t], recv_sem=d_recv[t],
                                         device_id=(x, y, 1 - c), device_id_type=MESH).start()
            for peer in _other_chips():
                pltpu.make_async_remote_copy(src_ref=srcs[t], dst_ref=lands[t].at[me], send_sem=i_send[t], recv_sem=i_recv[t],
                                             device_id=peer, device_id_type=MESH).start()
        token[...] = jnp.zeros_like(token)

    thru = [pltpu.HBM(e[0].shape, e[0].dtype) for e in flat] + [pltpu.HBM(e[1].shape, e[1].dtype) for e in flat]
    outs = pl.pallas_call(
        body, name=name,
        out_shape=(*[pltpu.SemaphoreType.DMA(())] * (4 * n), *thru, jax.ShapeDtypeStruct((SUBLANES, LANES), F32)),
        in_specs=[*[_HBM] * (2 * n), pl.BlockSpec(memory_space=pl.ANY)],
        out_specs=(*[_SEM] * (4 * n), *[_HBM] * (2 * n), pl.BlockSpec(memory_space=pltpu.VMEM)),
        input_output_aliases={t: 4 * n + t for t in range(2 * n)},
        compiler_params=pltpu.CompilerParams(has_side_effects=_EFFECT),
    )(*[_hbm(e[0]) for e in flat], *[_hbm(e[1]) for e in flat], after)
    records, t = [], 0
    for g in groups:
        k = len(g)
        records.append({"d_send": list(outs[t:t + k]), "d_recv": list(outs[n + t:n + t + k]),
                        "i_send": list(outs[2 * n + t:2 * n + t + k]), "i_recv": list(outs[3 * n + t:3 * n + t + k]),
                        "srcs": list(outs[4 * n + t:4 * n + t + k]), "lands": list(outs[5 * n + t:5 * n + t + k])})
        t += k
    return records, outs[-1]


def gather2_relay(name, rec, after):
    lands, i_recv = rec["lands"], rec["i_recv"]
    n = len(lands)

    def body(*refs):
        land_refs, i_recv_refs = refs[:n], refs[n:2 * n]
        outs = refs[2 * n + 1:]
        f_send, f_recv, token = outs[n:2 * n], outs[2 * n:3 * n], outs[3 * n]
        x, y, c = _my_place()
        for t in range(n):
            three = land_refs[t].at[pl.ds(0, 3)]
            pltpu.make_async_remote_copy(src_ref=three, dst_ref=three, send_sem=f_send[t], recv_sem=i_recv_refs[t],
                                         device_id=(x, y, c), device_id_type=MESH).wait_recv()
            for peer in _other_chips():
                blk = land_refs[t].at[_linear(peer)]
                pltpu.make_async_remote_copy(src_ref=blk, dst_ref=blk, send_sem=f_send[t], recv_sem=f_recv[t],
                                             device_id=(x, y, 1 - c), device_id_type=MESH).start()
        token[...] = jnp.zeros_like(token)

    outs = pl.pallas_call(
        body, name=name,
        out_shape=(*[pltpu.HBM(a.shape, a.dtype) for a in lands], *[pltpu.SemaphoreType.DMA(())] * (2 * n),
                   jax.ShapeDtypeStruct((SUBLANES, LANES), F32)),
        in_specs=[*[_HBM] * n, *[_SEM] * n, pl.BlockSpec(memory_space=pl.ANY)],
        out_specs=(*[_HBM] * n, *[_SEM] * (2 * n), pl.BlockSpec(memory_space=pltpu.VMEM)),
        input_output_aliases={t: t for t in range(n)},
        compiler_params=pltpu.CompilerParams(has_side_effects=_EFFECT),
    )(*lands, *i_recv, after)
    return {**rec, "lands": list(outs[:n]), "f_send": list(outs[n:2 * n]), "f_recv": list(outs[2 * n:3 * n])}, outs[-1]


def gather2_wait(name, rec, after):
    n = len(rec["lands"])
    sem_names = ("d_send", "d_recv", "i_send", "f_send", "f_recv")

    def body(*refs):
        src_refs, land_refs = refs[:n], refs[n:2 * n]
        sems = {nm: refs[2 * n + k * n:2 * n + (k + 1) * n] for k, nm in enumerate(sem_names)}
        me = _my_place()
        for t in range(n):
            one, three = land_refs[t].at[0], land_refs[t].at[pl.ds(0, 3)]
            to_sibling = pltpu.make_async_remote_copy(src_ref=src_refs[t], dst_ref=one, send_sem=sems["d_send"][t],
                                                      recv_sem=sems["d_recv"][t], device_id=me, device_id_type=MESH)
            to_sibling.wait_send()
            to_sibling.wait_recv()
            pltpu.make_async_remote_copy(src_ref=three, dst_ref=three, send_sem=sems["i_send"][t], recv_sem=sems["d_recv"][t],
                                         device_id=me, device_id_type=MESH).wait_send()
            forwarded = pltpu.make_async_remote_copy(src_ref=three, dst_ref=three, send_sem=sems["f_send"][t],
                                                     recv_sem=sems["f_recv"][t], device_id=me, device_id_type=MESH)
            forwarded.wait_send()
            forwarded.wait_recv()

    arrays = (*rec["srcs"], *rec["lands"])
    outs = pl.pallas_call(
        body, name=name,
        out_shape=tuple(pltpu.HBM(a.shape, a.dtype) for a in arrays),
        in_specs=[*[_HBM] * (2 * n), *[_SEM] * (5 * n), *[pl.BlockSpec(memory_space=pl.ANY)] * 2], out_specs=tuple([_HBM] * (2 * n)),
        input_output_aliases={t: t for t in range(2 * n)},
        compiler_params=pltpu.CompilerParams(has_side_effects=_EFFECT),
    )(*arrays, *[s for nm in sem_names for s in rec[nm]], *after)
    return list(outs[:n]), list(outs[n:])


def _seven_slots_of_land(src_ref, land_ref):
    return land_ref.at[pl.ds(0, N_DEV - 1)]


def _seven_blocks_of_src(src_ref, land_ref):
    return src_ref.at[pl.ds(0, N_DEV - 1)]


def _whole(ref, dev):
    return ref


def _slot(ref, dev):
    return ref.at[dev]


def _slot_of_layer(layer):
    return lambda ref, dev: ref.at[dev, layer]


def _as_rows(a, lead=0):
    return a.reshape(a.shape[:lead] + (-1, a.shape[-1]))


def sum_adamw(name, parts, w, m, v, half=None, column_major=False):
    shape = w.shape
    c1 = 1.0 - ADAM_B1 ** ADAM_STEP
    c2 = 1.0 - ADAM_B2 ** ADAM_STEP
    if column_major:
        layers, r, cols = shape
        tr = _tile(r, ROW_TILE, LANES)
        grid = (layers, r // tr)
        if half is None:
            p_spec = pl.BlockSpec((N_DEV, None, tr, cols), lambda l, i: (0, l, i, 0))
        else:
            p_spec = pl.BlockSpec((N_DEV, None, None, tr, cols), lambda l, i: (0, l, half, i, 0))
        blk = pl.BlockSpec((None, cols, tr), lambda l, i: (l, 0, i))
        args = (parts, *[jnp.swapaxes(a, 1, 2) for a in (w, m, v)])
        out_sds, sem = jax.ShapeDtypeStruct((layers, cols, r), F32), ("parallel", "parallel")
    elif half is not None:
        layers, r, cols = shape
        tr = _tile(r, ROW_TILE, 16)
        grid = (layers, r // tr)
        p_spec = pl.BlockSpec((N_DEV, None, None, tr, cols), lambda l, i: (0, l, half, i, 0))
        blk = pl.BlockSpec((None, tr, cols), lambda l, i: (l, i, 0))
        args, out_sds, sem = (parts, w, m, v), jax.ShapeDtypeStruct(shape, F32), ("parallel", "parallel")
    else:
        p2, w2, m2, v2 = _as_rows(parts, 1), _as_rows(w), _as_rows(m), _as_rows(v)
        rows, cols = w2.shape
        tr = _tile(rows, ROW_TILE, 16)
        grid = (rows // tr,)
        p_spec = pl.BlockSpec((N_DEV, tr, cols), lambda i: (0, i, 0))
        blk = pl.BlockSpec((tr, cols), lambda i: (i, 0))
        args, out_sds, sem = (p2, w2, m2, v2), jax.ShapeDtypeStruct((rows, cols), F32), ("parallel",)

    def body(p_ref, w_ref, m_ref, v_ref, g_ref, d_ref, nm_ref, nv_ref):
        g = p_ref[0].astype(F32)
        for k in range(1, N_DEV):
            g = g + p_ref[k].astype(F32)
        if column_major:
            g = g.T
        wv = w_ref[...]
        nm = ADAM_B1 * m_ref[...] + (1.0 - ADAM_B1) * g
        nv = ADAM_B2 * v_ref[...] + (1.0 - ADAM_B2) * (g * g)
        g_ref[...] = g
        nm_ref[...] = nm
        nv_ref[...] = nv
        d_ref[...] = -ADAM_LR * ((nm / c1) / (jnp.sqrt(nv / c2) + ADAM_EPS) + ADAM_WD * wv)

    outs = pl.pallas_call(
        body, name=name, grid=grid, in_specs=[p_spec, blk, blk, blk], out_specs=[blk] * 4,
        out_shape=[out_sds] * 4, compiler_params=_params(sem),
    )(*args)
    if column_major:
        return [jnp.swapaxes(o, 1, 2) for o in outs]
    return [o.reshape(shape) for o in outs]


def _unblock_cols(g):
    g = jnp.moveaxis(g, 0, -2)
    return g.reshape(g.shape[:-2] + (g.shape[-2] * g.shape[-1],))


def _block_cols(a):
    r, c = a.shape
    return jnp.moveaxis(a.reshape(r, N_DEV, c // N_DEV), 1, 0)


def _my_cols(a, n):
    me = _linear(_my_place())
    return lax.dynamic_slice_in_dim(a, me * n, n, axis=a.ndim - 1)


def _pack(arrays):
    flat = jnp.concatenate([a.reshape(-1).astype(F32) for a in arrays])
    pad = (-flat.shape[0]) % (ROW_TILE * LANES)
    return jnp.pad(flat, (0, pad)).reshape(-1, LANES)


def _unpack(packed, like):
    flat = packed.reshape(-1)
    out, pos = [], 0
    for a in like:
        out.append(flat[pos:pos + a.size].reshape(a.shape))
        pos += a.size
    return out


def kernel(x, meta_tokens, norm_w, ssd_w_in, ssd_conv_w, ssd_conv_b, ssd_dt_bias, ssd_a_log, ssd_d, ssd_norm_w, ssd_w_out, pool_w, pool_b, pool_scale, ffn_w_gate, ffn_w_up, ffn_w_down, loss_target, m_meta_tokens, m_norm_w, m_ssd_w_in, m_ssd_conv_w, m_ssd_conv_b, m_ssd_dt_bias, m_ssd_a_log, m_ssd_d, m_ssd_norm_w, m_ssd_w_out, m_pool_w, m_pool_b, m_pool_scale, m_ffn_w_gate, m_ffn_w_up, m_ffn_w_down, v_meta_tokens, v_norm_w, v_ssd_w_in, v_ssd_conv_w, v_ssd_conv_b, v_ssd_dt_bias, v_ssd_a_log, v_ssd_d, v_ssd_norm_w, v_ssd_w_out, v_pool_w, v_pool_b, v_pool_scale, v_ffn_w_gate, v_ffn_w_up, v_ffn_w_down):
    seq, d = x.shape[1], x.shape[2]
    depth = norm_w.shape[0]
    n_ssd = ssd_w_in.shape[0]
    d_inner = ssd_norm_w.shape[1]
    heads = d_inner // HEAD_DIM
    rpg = heads // SSD_GROUPS
    conv_dim = ssd_conv_b.shape[1]
    zx_cols = d_inner + conv_dim
    d_in_proj = zx_cols + heads
    rows = PAD_FRONT + N_META + seq
    assert rows % CHUNK == 0 and rpg % 2 == 0 and heads <= LANES and rpg <= SUBLANES

    hidden = ffn_w_down.shape[1] * N_DEV
    n_pool = pool_w.shape[0]

    gather_recs = {}

    def start_gather(i, after, groups_of_layer=slice(None), tag=""):
        def entry(shard):
            return lambda: (shard.astype(BF16), lax.empty((N_DEV, *shard.shape), BF16))
        mixer = [ssd_w_in[i // 2], ssd_w_out[i // 2]] if i % 2 == 0 else [pool_w[i // 2]]
        groups = [*[[entry(w)] for w in mixer], [entry(jnp.stack([ffn_w_gate[i], ffn_w_up[i]])), entry(ffn_w_down[i])]]
        groups = [[make() for make in g] for g in groups[groups_of_layer]]
        recs, tok = gather2_start(f"gather_start_{i}{tag}", groups, after)
        gather_recs[i] = gather_recs.get(i, []) + recs
        return tok

    first = start_gather(0, meta_tokens, slice(0, 1), "_first")
    small = all_gather("gather_small", [meta_tokens, norm_w, ssd_conv_w, pool_b, pool_scale], first)
    meta_f, norm_f, convw_f, poolb_f, pools_f = [_unblock_cols(s) for s in small]

    gather_order = [(i, g) for i in range(depth) for g in range(3 if i % 2 == 0 else 2)]
    relayed = [0, None]

    def gathered(i, g, after):
        upto = min(gather_order.index((i, g)) + 2, len(gather_order))
        while relayed[0] < upto:
            li, lg = gather_order[relayed[0]]
            gather_recs[li][lg], relayed[1] = gather2_relay(f"gather_relay_{li}_{lg}", gather_recs[li][lg], after)
            relayed[0] += 1
        srcs, lands_ = gather2_wait(f"gather_wait_{i}_{g}", gather_recs[i][g], (after, relayed[1]))
        me = _linear(_my_place())
        return [lax.dynamic_update_slice_in_dim(land, src[None], me, axis=0) for src, land in zip(srcs, lands_)]

    def gathered_small(name, rec, after):
        srcs, lands_ = exchange_wait(name, [rec], rec[3], [[0]], [[_seven_slots_of_land]], after)
        return lax.dynamic_update_slice_in_dim(lands_[0], srcs[0][None], _linear(_my_place()), axis=0)

    pending = start_gather(0, meta_f, slice(1, None))
    for i in range(1, min(GATHER_AHEAD, depth)):
        pending = start_gather(i, pending)

    pad_h = lambda a: jnp.pad(a.astype(F32), ((0, 0), (0, LANES - heads)))
    bias_pad = pad_h(ssd_dt_bias)

    def head_layouts(vec):
        g = vec.reshape(SSD_GROUPS, rpg)
        row = jnp.pad(g, ((0, 0), (0, LANES - rpg)))[:, None, :]
        col = jnp.pad(g, ((0, 0), (0, SUBLANES - rpg)))[:, :, None]
        return row, col

    def dt_layouts(dt):
        g = dt[:, :heads].reshape(rows, SSD_GROUPS, rpg)
        col = jnp.pad(jnp.moveaxis(g, 1, 0), ((0, 0), (0, 0), (0, LANES - rpg)))
        row = jnp.pad(jnp.transpose(g, (1, 2, 0)), ((0, 0), (0, SUBLANES - rpg), (0, 0)))
        return col, row

    h = jnp.concatenate([jnp.zeros((PAD_FRONT, d), F32), meta_f, x[0]], axis=0)
    saved = []
    for i in range(depth):
        j = i // 2
        s = {"h": h}
        w_pre_mix = norm_f[i, 0]
        if i + GATHER_AHEAD < depth:
            pending = start_gather(i + GATHER_AHEAD, pending if i == 0 else h)
        (mixer_w,) = gathered(i, 0, h)
        if i % 2 == 0:
            w_in = jnp.pad(_unblock_cols(mixer_w), ((0, 0), (0, LANES - heads)))
            u, u_t = rmsnorm_fwd(f"norm_pre_mix_{i}", h, w_pre_mix, out_dtype=BF16, transposed=True, after=pending)
            zx = matmul(f"ssd_in_{i}", u, w_in, **MM_ROWS_RESIDENT, tn=384 if w_in.shape[1] % 384 == 0 else 512)
            xbc = conv_fwd(f"ssd_conv_{i}", zx, convw_f[j], ssd_conv_b[j], d_inner, conv_dim)
            dt = dt_fwd(f"ssd_dt_{i}", zx, bias_pad[j:j + 1], zx_cols)
            dt_col, dt_row = dt_layouts(dt)
            a_neg = -jnp.exp(ssd_a_log[j].astype(F32))
            a_row, a_col = head_layouts(a_neg)
            d_row, _ = head_layouts(ssd_d[j].astype(F32))
            y, sprev = ssd_scan_fwd(f"ssd_scan_{i}", xbc, dt_col, dt_row, a_row, a_col, d_row, d_inner)
            yn, yn_t = gatenorm_fwd(f"ssd_gate_{i}", y, zx, ssd_norm_w[j], d_inner)
            (w_out,) = gathered(i, 1, yn)
            w_out = w_out.reshape(d_inner, d)
            mix = matmul(f"ssd_out_{i}", yn, w_out, **MM_DEEP)
            s.update(u_t=u_t, zx=zx, xbc=xbc, dt_col=dt_col, dt_row=dt_row, a_row=a_row, a_col=a_col, d_row=d_row,
                     a_neg=a_neg, y=y, sprev=sprev, yn_t=yn_t, w_in=w_in, w_out=w_out)
        else:
            w_pool = jnp.moveaxis(mixer_w, 0, 1).reshape(len(POOL_WINDOWS), d // len(POOL_WINDOWS), -1)
            u = rmsnorm_fwd(f"norm_pre_mix_{i}", h, w_pre_mix, after=pending)
            mixed = pool_sub(f"pool_sub_{i}", u, False)
            pre, mix = pool_proj_fwd(f"pool_proj_{i}", mixed, w_pool, poolb_f[j], pools_f[j])
            s.update(mixed=mixed, pre=pre, w_pool=w_pool)
        pending = None
        h1 = rmsnorm_fwd(f"norm_post_mix_{i}", mix, norm_f[i, 1], res=h)
        w_gu, w_down = gathered(i, 2 if i % 2 == 0 else 1, h1)
        u2, u2_t = rmsnorm_fwd(f"norm_pre_ffn_{i}", h1, norm_f[i, 2], out_dtype=BF16, transposed=True)
        gp = ffn_in_fwd(f"ffn_in_{i}", u2, w_gu)
        act, act_t = swiglu_fwd(f"ffn_act_{i}", gp)
        f = ffn_out_fwd(f"ffn_out_{i}", act, w_down)
        h = rmsnorm_fwd(f"norm_post_ffn_{i}", f, norm_f[i, 3], res=h1)
        s.update(mix=mix, h1=h1, u2_t=u2_t, gp=gp, act_t=act_t, f=f, w_gu=w_gu, w_down=w_down)
        saved.append(s)

    dh, loss_local = loss_head("loss_head", h, loss_target[0])
    loss = lax.psum(loss_local, AXES)

    g_norm = [[None] * 4 for _ in range(depth)]
    g_convw, g_convb, g_dtb, g_alog, g_dskip, g_ssdnorm = ([None] * n_ssd for _ in range(6))
    g_poolb, g_pools = ([None] * n_pool for _ in range(2))
    hid_s = hidden // N_DEV
    lands = {"in": lax.empty((N_DEV, *ssd_w_in.shape), BF16), "out": lax.empty((N_DEV, *ssd_w_out.shape), BF16),
             "pool": lax.empty((N_DEV, *pool_w.shape), BF16), "down": lax.empty((N_DEV, *ffn_w_down.shape), BF16),
             "gate_up": lax.empty((N_DEV, depth, 2, d, hid_s), BF16)}
    scatter_recs, scatter_keys, scatter_views, scatter_layers = [], [], [], []

    def scatter(name, blocks, layer, after):
        view = _slot_of_layer(layer)
        (rec,), tok = exchange_start(name, [[(b, lands[key], _slot, view) for key, b in blocks]], after)
        for (key, _), thru in zip(blocks, rec[3]):
            lands[key] = thru
        scatter_recs.append(rec)
        scatter_keys.append([key for key, _ in blocks])
        scatter_views.append([_seven_blocks_of_src] * len(blocks))
        scatter_layers.append(layer)
        return tok

    tok = None
    for i in reversed(range(depth)):
        j = i // 2
        s = saved[i]
        df, g_norm[i][3] = rmsnorm_bwd(f"norm_post_ffn_bwd_{i}", s["f"], norm_f[i, 3], dh, after=tok)
        df_b = df.astype(BF16)
        dact = ffn_out_bwd_x(f"ffn_out_bwd_x_{i}", df_b, s["w_down"])
        g_down = ffn_out_bwd_w(f"ffn_out_bwd_w_{i}", s["act_t"], df_b)
        dgp = swiglu_bwd(f"ffn_act_bwd_{i}", s["gp"], dact)
        g_gu = ffn_in_bwd_w(f"ffn_in_bwd_w_{i}", s["u2_t"], dgp)
        tok = scatter(f"scatter_start_ffn_{i}", [("gate_up", g_gu), ("down", g_down)], i, g_gu)
        du2 = ffn_in_bwd_x(f"ffn_in_bwd_x_{i}", dgp, s["w_gu"])
        dh1, g_norm[i][2] = rmsnorm_bwd(f"norm_pre_ffn_bwd_{i}", s["h1"], norm_f[i, 2], du2, add=dh, after=tok)
        dmix, g_norm[i][1] = rmsnorm_bwd(f"norm_post_mix_bwd_{i}", s["mix"], norm_f[i, 1], dh1)
        if i % 2 == 0:
            dmix_b = dmix.astype(BF16)
            dyn = matmul(f"ssd_out_bwd_x_{i}", dmix_b, s["w_out"], tb=True, **MM_ROWS_RESIDENT)
            g_out = matmul(f"ssd_out_bwd_w_{i}", s["yn_t"], dmix_b, out_dtype=BF16, **MM_COLS_RESIDENT)
            tok = scatter(f"scatter_start_out_{i}", [("out", g_out.reshape(N_DEV, d_inner // N_DEV, d))], j, g_out)
            dy, dz, g_ssdnorm[j] = gatenorm_bwd(f"ssd_gate_bwd_{i}", s["y"], s["zx"], ssd_norm_w[j], dyn, d_inner, after=tok)
            dx, db, dc, ddt_col, ddt_row, dar, dac, ddsk = ssd_scan_bwd(
                f"ssd_scan_bwd_{i}", s["xbc"], s["dt_col"], s["dt_row"], s["a_row"], s["a_col"], s["d_row"], s["sprev"], dy, d_inner)
            ddt = (jnp.moveaxis(ddt_col[:, :, :rpg], 0, 1).reshape(rows, heads)
                   + jnp.transpose(ddt_row[:, :rpg, :], (2, 0, 1)).reshape(rows, heads))
            d_a = (dar[:, 0, :rpg] + dac[:, :rpg, 0]).reshape(heads)
            g_alog[j] = d_a * s["a_neg"]
            g_dskip[j] = ddsk[:, 0, :rpg].reshape(heads)
            ddtr, dbias = dt_bwd(f"ssd_dt_bwd_{i}", s["zx"], bias_pad[j:j + 1], pad_h(ddt), zx_cols)
            g_dtb[j] = dbias[0, :heads]
            dxbc_raw, g_convw[j], dconvb = conv_bwd(
                f"ssd_conv_bwd_{i}", s["zx"], convw_f[j], ssd_conv_b[j], jnp.concatenate([dx, db, dc], axis=1), d_inner, conv_dim)
            g_convb[j] = dconvb[0]
            dzx = jnp.concatenate([dz, dxbc_raw, ddtr], axis=1)
            g_in = matmul(f"ssd_in_bwd_w_{i}", s["u_t"], dzx, out_dtype=BF16, **MM_ROWS_RESIDENT,
                          tn=384 if dzx.shape[1] % 384 == 0 else 512)
            tok = scatter(f"scatter_start_in_{i}", [("in", _block_cols(g_in[:, :d_in_proj]))], j, g_in)
            du = matmul(f"ssd_in_bwd_x_{i}", dzx, s["w_in"], tb=True, **{**MM_DEEP, "tk": 1152 if dzx.shape[1] % 1152 == 0 else 512})
        else:
            dmixed, g_poolw, g_poolb[j], g_pools[j] = pool_proj_bwd(
                f"pool_proj_bwd_{i}", dmix, s["pre"], s["mixed"], s["w_pool"], pools_f[j])
            ng, gd = g_poolw.shape[0], g_poolw.shape[1]
            blk_pool = jnp.moveaxis(g_poolw.astype(BF16).reshape(ng, N_DEV, gd // N_DEV, gd), 1, 0)
            tok = scatter(f"scatter_start_pool_{i}", [("pool", blk_pool)], j, blk_pool)
            du = pool_sub(f"pool_sub_bwd_{i}", dmixed, True)
        dh, g_norm[i][0] = rmsnorm_bwd(f"norm_pre_mix_bwd_{i}", s["h"], norm_f[i, 0], du, add=dh1, after=tok)

    grad_x = dh[PAD_FRONT + N_META:][None]
    g_meta = dh[PAD_FRONT:PAD_FRONT + N_META]

    small_grads = [g_meta, jnp.stack([jnp.stack(r) for r in g_norm]), jnp.stack(g_convw), jnp.stack(g_convb), jnp.stack(g_dtb),
                   jnp.stack(g_alog), jnp.stack(g_dskip), jnp.stack(g_ssdnorm), jnp.stack(g_poolb), jnp.stack(g_pools)]
    packed_g = _pack(small_grads)
    (small_rec,), _ = exchange_start(
        "gather_small_grads_start", [[(packed_g, lax.empty((N_DEV, *packed_g.shape), F32), _whole, _slot)]], packed_g)
    mine = lambda a: _my_cols(a, a.shape[-1] // N_DEV)
    small_w = [meta_tokens, norm_w, ssd_conv_w, ssd_conv_b, ssd_dt_bias, ssd_a_log, ssd_d, ssd_norm_w, pool_b, pool_scale]
    small_m = [m_meta_tokens, m_norm_w, m_ssd_conv_w, m_ssd_conv_b, m_ssd_dt_bias, m_ssd_a_log, m_ssd_d, m_ssd_norm_w, m_pool_b, m_pool_scale]
    small_v = [v_meta_tokens, v_norm_w, v_ssd_conv_w, v_ssd_conv_b, v_ssd_dt_bias, v_ssd_a_log, v_ssd_d, v_ssd_norm_w, v_pool_b, v_pool_scale]
    sharded = [True, True, True, False, False, False, False, False, True, True]
    def widen(a, is_sharded, full):
        if not is_sharded:
            return a
        return lax.dynamic_update_slice_in_dim(jnp.zeros(full.shape, F32), a, _linear(_my_place()) * a.shape[-1], axis=a.ndim - 1)
    packed_w = _pack([widen(a, sh, g) for a, sh, g in zip(small_w, sharded, small_grads)])
    packed_m = _pack([widen(a, sh, g) for a, sh, g in zip(small_m, sharded, small_grads)])
    packed_v = _pack([widen(a, sh, g) for a, sh, g in zip(small_v, sharded, small_grads)])

    def landed(name, keys, after):
        rs = [r for r, ks in enumerate(scatter_keys) if set(ks) <= set(keys)]
        srcs, out = exchange_wait(name, [scatter_recs[r] for r in rs], [lands[k] for k in keys],
                                  [[keys.index(k) for k in scatter_keys[r]] for r in rs], [scatter_views[r] for r in rs], after)
        me = _linear(_my_place())
        src_of = iter(srcs)
        for r in rs:
            for key in scatter_keys[r]:
                own = lax.dynamic_index_in_dim(next(src_of), me, axis=0, keepdims=True)[:, None]
                slot = keys.index(key)
                start = (me, scatter_layers[r]) + (0,) * (own.ndim - 2)
                out[slot] = lax.dynamic_update_slice(out[slot], own, start)
        return out

    p_gu, p_down = landed("scatter_wait_ffn", ["gate_up", "down"], packed_g)
    a_gate = sum_adamw("adamw_ffn_w_gate", p_gu, ffn_w_gate, m_ffn_w_gate, v_ffn_w_gate, half=0, column_major=True)
    a_up = sum_adamw("adamw_ffn_w_up", p_gu, ffn_w_up, m_ffn_w_up, v_ffn_w_up, half=1, column_major=True)
    a_down = sum_adamw("adamw_ffn_w_down", p_down, ffn_w_down, m_ffn_w_down, v_ffn_w_down)
    (p_pool,) = landed("scatter_wait_pool", ["pool"], a_down[0])
    a_pool = sum_adamw("adamw_pool_w", p_pool, pool_w, m_pool_w, v_pool_w)
    (p_out,) = landed("scatter_wait_out", ["out"], a_pool[0])
    a_out = sum_adamw("adamw_ssd_w_out", p_out, ssd_w_out, m_ssd_w_out, v_ssd_w_out)
    (p_in,) = landed("scatter_wait_in", ["in"], a_out[0])
    a_in = sum_adamw("adamw_ssd_w_in", p_in, ssd_w_in, m_ssd_w_in, v_ssd_w_in, column_major=True)
    bg = [a_in, a_out, a_pool, a_gate, a_up, a_down]

    small_parts = gathered_small("gather_small_grads_wait", small_rec, a_in[0])
    sm = [_unpack(o, small_grads) for o in sum_adamw("adamw_small", small_parts, packed_w, packed_m, packed_v)]
    sm = [[mine(a) if sh else a for a, sh in zip(group, sharded)] for group in sm]

    def ordered(kind):
        s_ = sm[kind]
        b_ = [o[kind] for o in bg]
        return [s_[0], s_[1], b_[0], s_[2], s_[3], s_[4], s_[5], s_[6], s_[7], b_[1], b_[2], s_[8], s_[9], b_[3], b_[4], b_[5]]

    return (loss, grad_x, *ordered(0), *ordered(1), *ordered(2), *ordered(3))
```

```python
import functools

import jax
import jax.numpy as jnp
from jax import lax
from jax.experimental import pallas as pl
from jax.experimental.pallas import tpu as pltpu

F32 = jnp.float32
BF16 = jnp.bfloat16
MESH = pl.DeviceIdType.MESH
AXES = ("x", "y", "c")
N_DEV = 8

N_META = 16
EPS = 1e-6
HEAD_DIM = 64
D_STATE = 128
SSD_GROUPS = 8
D_CONV = 4
CHUNK = 256
POOL_WINDOWS = (2, 4, 8, 16)
ADAM_LR, ADAM_B1, ADAM_B2, ADAM_EPS, ADAM_WD, ADAM_STEP = 0.001, 0.9, 0.999, 1e-08, 0.01, 10

PAD_FRONT = (-N_META) % CHUNK
LANES = 128
SUBLANES = 8
ROW_TILE = 256
CONV_TILE = 256
POOL_TILE = 128
VMEM_LIMIT = 56 * 1024 * 1024


def _params(sem=None):
    return pltpu.CompilerParams(dimension_semantics=sem, vmem_limit_bytes=VMEM_LIMIT)


def _tile(n, target, mult):
    if n <= target:
        return n
    best = None
    for t in range(mult, target + 1, mult):
        if n % t == 0:
            best = t
    assert best is not None, (n, target, mult)
    return best


def _dot(a, b, ca, cb):
    return lax.dot_general(a, b, (((ca,), (cb,)), ((), ())), preferred_element_type=F32)


def _sigmoid(x):
    return 1.0 / (1.0 + jnp.exp(-x))


def _row_mask(shape, first_row):
    rows = lax.broadcasted_iota(jnp.int32, shape, 0) + first_row
    return rows >= PAD_FRONT


MM_ROWS_RESIDENT = dict(tm=2304, tk=2304)
MM_COLS_RESIDENT = dict(tm=512, tn=2304, tk=2304)
MM_DEEP = dict(tm=1152, tn=1024, tk=512)
GATHER_AHEAD = 2

def matmul(name, a, b, *, tb=False, out_dtype=F32, tm=768, tn=512, tk=2048):
    m, kdim = a.shape
    if tb:
        n, k2 = b.shape
    else:
        k2, n = b.shape
    assert kdim == k2, (a.shape, b.shape, tb)
    tm = _tile(m, tm, 16)
    tn = _tile(n, tn, LANES)
    tk = _tile(kdim, tk, LANES)
    nk = kdim // tk
    a_spec = pl.BlockSpec((tm, tk), lambda i, j, k: (i, k))
    b_spec = pl.BlockSpec((tn, tk), lambda i, j, k: (j, k)) if tb else pl.BlockSpec((tk, tn), lambda i, j, k: (k, j))

    def body_single(a_ref, b_ref, o_ref):
        o_ref[...] = _dot(a_ref[...], b_ref[...], 1, 1 if tb else 0).astype(o_ref.dtype)

    def body_acc(a_ref, b_ref, o_ref, acc_ref):
        k = pl.program_id(2)

        @pl.when(k == 0)
        def _():
            acc_ref[...] = jnp.zeros_like(acc_ref)

        acc_ref[...] += _dot(a_ref[...], b_ref[...], 1, 1 if tb else 0)

        @pl.when(k == nk - 1)
        def _():
            o_ref[...] = acc_ref[...].astype(o_ref.dtype)

    return pl.pallas_call(
        body_single if nk == 1 else body_acc, name=name, grid=(m // tm, n // tn, nk),
        in_specs=[a_spec, b_spec], out_specs=pl.BlockSpec((tm, tn), lambda i, j, k: (i, j)),
        out_shape=jax.ShapeDtypeStruct((m, n), out_dtype),
        scratch_shapes=[] if nk == 1 else [pltpu.VMEM((tm, tn), F32)],
        compiler_params=_params(("parallel", "parallel", "arbitrary")),
    )(a, b)


_TOKEN_SPEC = pl.BlockSpec((SUBLANES, LANES), lambda i: (0, 0))


def rmsnorm_fwd(name, x, w, res=None, out_dtype=F32, transposed=False, after=None):
    rows, d = x.shape
    tr = ROW_TILE
    row_spec = pl.BlockSpec((tr, d), lambda i: (i, 0))
    w_spec = pl.BlockSpec((1, d), lambda i: (0, 0))
    n_in = 2 + (res is not None) + (after is not None)

    def body(*refs):
        x_ref, w_ref = refs[:2]
        outs = refs[n_in:]
        xv = x_ref[...]
        y = xv * lax.rsqrt(jnp.mean(xv * xv, axis=-1, keepdims=True) + EPS) * w_ref[...]
        if res is not None:
            y = refs[2][...] + y
        outs[0][...] = y.astype(out_dtype)
        if transposed:
            outs[1][...] = y.T.astype(out_dtype)

    args = [x, w.reshape(1, d)] + ([] if res is None else [res]) + ([] if after is None else [after])
    specs = [row_spec, w_spec] + ([] if res is None else [row_spec]) + ([] if after is None else [_TOKEN_SPEC])
    out_specs, out_shape = [row_spec], [jax.ShapeDtypeStruct((rows, d), out_dtype)]
    if transposed:
        out_specs.append(pl.BlockSpec((d, tr), lambda i: (0, i)))
        out_shape.append(jax.ShapeDtypeStruct((d, rows), out_dtype))
    outs = pl.pallas_call(
        body, name=name, grid=(rows // tr,), in_specs=specs, out_specs=out_specs, out_shape=out_shape,
        compiler_params=_params(("parallel",)),
    )(*args)
    return outs if transposed else outs[0]


def rmsnorm_bwd(name, x, w, dy, add=None, after=None):
    rows, d = x.shape
    tr = ROW_TILE
    row_spec = pl.BlockSpec((tr, d), lambda i: (i, 0))
    w_spec = pl.BlockSpec((1, d), lambda i: (0, 0))

    def body(*refs):
        x_ref, w_ref, dy_ref = refs[:3]
        add_ref = None if add is None else refs[3]
        dx_ref, dw_ref = refs[-2:]
        xv = x_ref[...]
        dyv = dy_ref[...].astype(F32)
        r = lax.rsqrt(jnp.mean(xv * xv, axis=-1, keepdims=True) + EPS)
        xh = xv * r
        dxh = dyv * w_ref[...]
        dx = r * (dxh - xh * jnp.mean(dxh * xh, axis=-1, keepdims=True))
        if add is not None:
            dx = dx + add_ref[...]
        dx_ref[...] = dx

        @pl.when(pl.program_id(0) == 0)
        def _():
            dw_ref[...] = jnp.zeros_like(dw_ref)

        dw_ref[...] += jnp.sum(dyv * xh, axis=0, keepdims=True)

    args = [x, w.reshape(1, d), dy] + ([] if add is None else [add]) + ([] if after is None else [after])
    specs = [row_spec, w_spec, row_spec] + ([] if add is None else [row_spec]) + ([] if after is None else [_TOKEN_SPEC])
    dx, dw = pl.pallas_call(
        body, name=name, grid=(rows // tr,), in_specs=specs, out_specs=[row_spec, w_spec],
        out_shape=[jax.ShapeDtypeStruct((rows, d), F32), jax.ShapeDtypeStruct((1, d), F32)],
        compiler_params=_params(("arbitrary",)),
    )(*args)
    return dx, dw.reshape(d)


def loss_head(name, h, target):
    rows, d = h.shape
    tr = ROW_TILE
    first = (PAD_FRONT + N_META) // tr
    assert (PAD_FRONT + N_META) % tr == 0 and target.shape[0] == rows - first * tr

    def body(h_ref, t_ref, dh_ref, loss_ref):
        i = pl.program_id(0)

        @pl.when(i == 0)
        def _():
            loss_ref[...] = jnp.zeros_like(loss_ref)

        keep = (i >= first).astype(F32)
        diff = (h_ref[...] - t_ref[...]) * keep
        dh_ref[...] = diff / d
        loss_ref[...] += 0.5 * jnp.sum(diff * diff) / d

    dh, loss = pl.pallas_call(
        body, name=name, grid=(rows // tr,),
        in_specs=[pl.BlockSpec((tr, d), lambda i: (i, 0)), pl.BlockSpec((tr, d), lambda i: (jnp.maximum(i - first, 0), 0))],
        out_specs=[pl.BlockSpec((tr, d), lambda i: (i, 0)), pl.BlockSpec((SUBLANES, LANES), lambda i: (0, 0))],
        out_shape=[jax.ShapeDtypeStruct((rows, d), F32), jax.ShapeDtypeStruct((SUBLANES, LANES), F32)],
        compiler_params=_params(("arbitrary",)),
    )(h, target)
    return dh, loss[0, 0]


def _mm_call(name, a, b, grid, a_spec, b_spec, o_spec, out_sds, ca, cb, red_axis=None, pairs=None):
    n_red = None if red_axis is None else grid[red_axis]

    def body_single(a_ref, b_ref, o_ref):
        o_ref[...] = _dot(a_ref[...], b_ref[...], ca, cb).astype(o_ref.dtype)

    def step_product(a_ref, b_ref):
        if pairs is None:
            return _dot(a_ref[...], b_ref[...], ca, cb)
        return functools.reduce(lambda s, t: s + t, [_dot(a_ref[p], b_ref[p], ca, cb) for p in range(pairs)])

    def body_acc(a_ref, b_ref, o_ref, acc_ref):
        k = pl.program_id(red_axis)

        @pl.when(k == 0)
        def _():
            acc_ref[...] = jnp.zeros_like(acc_ref)

        acc_ref[...] += step_product(a_ref, b_ref)

        @pl.when(k == n_red - 1)
        def _():
            o_ref[...] = acc_ref[...].astype(o_ref.dtype)

    acc_shape = tuple(s for s in o_spec.block_shape if s is not None)
    sem = tuple("arbitrary" if ax == red_axis else "parallel" for ax in range(len(grid)))
    return pl.pallas_call(
        body_single if red_axis is None else body_acc, name=name, grid=grid, in_specs=[a_spec, b_spec], out_specs=o_spec,
        out_shape=out_sds, scratch_shapes=[] if red_axis is None else [pltpu.VMEM(acc_shape, F32)],
        compiler_params=_params(sem),
    )(a, b)


def ffn_in_fwd(name, u2, w_gu):
    rows, d = u2.shape
    hs = w_gu.shape[-1]
    return _mm_call(name, u2, w_gu, (N_DEV, 2), pl.BlockSpec((rows, d), lambda k, t: (0, 0)),
                    pl.BlockSpec((None, None, d, hs), lambda k, t: (k, t, 0, 0)),
                    pl.BlockSpec((None, None, rows, hs), lambda k, t: (k, t, 0, 0)),
                    jax.ShapeDtypeStruct((N_DEV, 2, rows, hs), F32), 1, 0)


def ffn_out_fwd(name, act, w_down):
    _, rows, hs = act.shape
    d = w_down.shape[-1]
    tm, tn = _tile(rows, 1152, 16), _tile(d, 1024, LANES)
    half = N_DEV // 2
    return _mm_call(name, act.reshape(half, 2, rows, hs), w_down.reshape(half, 2, hs, d), (rows // tm, d // tn, half),
                    pl.BlockSpec((None, 2, tm, hs), lambda i, j, k: (k, 0, i, 0)),
                    pl.BlockSpec((None, 2, hs, tn), lambda i, j, k: (k, 0, 0, j)), pl.BlockSpec((tm, tn), lambda i, j, k: (i, j)),
                    jax.ShapeDtypeStruct((rows, d), F32), 1, 0, red_axis=2, pairs=2)


def ffn_out_bwd_x(name, df, w_down):
    rows, d = df.shape
    hs = w_down.shape[1]
    return _mm_call(name, df, w_down, (N_DEV,), pl.BlockSpec((rows, d), lambda k: (0, 0)),
                    pl.BlockSpec((None, hs, d), lambda k: (k, 0, 0)), pl.BlockSpec((None, rows, hs), lambda k: (k, 0, 0)),
                    jax.ShapeDtypeStruct((N_DEV, rows, hs), F32), 1, 1)


def ffn_out_bwd_w(name, act_t, df):
    _, hs, rows = act_t.shape
    d = df.shape[1]
    return _mm_call(name, act_t, df, (N_DEV,), pl.BlockSpec((None, hs, rows), lambda k: (k, 0, 0)),
                    pl.BlockSpec((rows, d), lambda k: (0, 0)), pl.BlockSpec((None, hs, d), lambda k: (k, 0, 0)),
                    jax.ShapeDtypeStruct((N_DEV, hs, d), BF16), 1, 0)


def ffn_in_bwd_w(name, u2_t, dgp):
    d, rows = u2_t.shape
    hs = dgp.shape[-1]
    return _mm_call(name, u2_t, dgp, (N_DEV, 2), pl.BlockSpec((d, rows), lambda k, t: (0, 0)),
                    pl.BlockSpec((None, None, rows, hs), lambda k, t: (k, t, 0, 0)),
                    pl.BlockSpec((None, None, d, hs), lambda k, t: (k, t, 0, 0)),
                    jax.ShapeDtypeStruct((N_DEV, 2, d, hs), BF16), 1, 0)


def ffn_in_bwd_x(name, dgp, w_gu):
    _, _, rows, hs = dgp.shape
    d = w_gu.shape[2]
    tm, tn = _tile(rows, 1152, 16), _tile(d, 1024, LANES)
    return _mm_call(name, dgp, w_gu, (rows // tm, d // tn, N_DEV),
                    pl.BlockSpec((None, 2, tm, hs), lambda i, j, k: (k, 0, i, 0)),
                    pl.BlockSpec((None, 2, tn, hs), lambda i, j, k: (k, 0, j, 0)),
                    pl.BlockSpec((tm, tn), lambda i, j, k: (i, j)), jax.ShapeDtypeStruct((rows, d), F32), 1, 1, red_axis=2, pairs=2)


def swiglu_fwd(name, gp):
    _, _, rows, hs = gp.shape
    tr = _tile(rows, 768, ROW_TILE)

    def body(gp_ref, a_ref, at_ref):
        g = gp_ref[0]
        act = g * _sigmoid(g) * gp_ref[1]
        a_ref[...] = act.astype(a_ref.dtype)
        at_ref[...] = act.T.astype(at_ref.dtype)

    return pl.pallas_call(
        body, name=name, grid=(N_DEV, rows // tr),
        in_specs=[pl.BlockSpec((None, 2, tr, hs), lambda k, i: (k, 0, i, 0))],
        out_specs=[pl.BlockSpec((None, tr, hs), lambda k, i: (k, i, 0)), pl.BlockSpec((None, hs, tr), lambda k, i: (k, 0, i))],
        out_shape=[jax.ShapeDtypeStruct((N_DEV, rows, hs), BF16), jax.ShapeDtypeStruct((N_DEV, hs, rows), BF16)],
        compiler_params=_params(("parallel", "parallel")),
    )(gp)


def swiglu_bwd(name, gp, dact):
    _, _, rows, hs = gp.shape
    tr = _tile(rows, 768, ROW_TILE)

    def body(gp_ref, da_ref, o_ref):
        g = gp_ref[0]
        s = _sigmoid(g)
        dav = da_ref[...]
        o_ref[0] = (dav * gp_ref[1] * (s * (1.0 + g * (1.0 - s)))).astype(o_ref.dtype)
        o_ref[1] = (dav * (g * s)).astype(o_ref.dtype)

    blk = pl.BlockSpec((None, 2, tr, hs), lambda k, i: (k, 0, i, 0))
    return pl.pallas_call(
        body, name=name, grid=(N_DEV, rows // tr),
        in_specs=[blk, pl.BlockSpec((None, tr, hs), lambda k, i: (k, i, 0))], out_specs=blk,
        out_shape=jax.ShapeDtypeStruct(gp.shape, BF16), compiler_params=_params(("parallel", "parallel")),
    )(gp, dact)


def conv_fwd(name, zx, conv_w, conv_b, d_inner, conv_dim):
    rows = zx.shape[0]
    tc = CONV_TILE
    off = d_inner // tc
    assert d_inner % tc == 0 and conv_dim % tc == 0

    def body(u_ref, w_ref, b_ref, o_ref):
        u = u_ref[...]
        acc = u * w_ref[D_CONV - 1:D_CONV, :] + b_ref[...]
        for s in range(1, D_CONV):
            acc = acc + pltpu.roll(u, s, axis=0) * w_ref[D_CONV - 1 - s:D_CONV - s, :]
        y = acc * _sigmoid(acc)
        o_ref[...] = jnp.where(_row_mask(y.shape, 0), y, 0.0)

    return pl.pallas_call(
        body, name=name, grid=(conv_dim // tc,),
        in_specs=[pl.BlockSpec((rows, tc), lambda j: (0, off + j)), pl.BlockSpec((D_CONV, tc), lambda j: (0, j)),
                  pl.BlockSpec((1, tc), lambda j: (0, j))],
        out_specs=pl.BlockSpec((rows, tc), lambda j: (0, j)),
        out_shape=jax.ShapeDtypeStruct((rows, conv_dim), F32), compiler_params=_params(("parallel",)),
    )(zx, conv_w, conv_b.reshape(1, conv_dim))


def conv_bwd(name, zx, conv_w, conv_b, dxbc, d_inner, conv_dim):
    rows = zx.shape[0]
    tc = CONV_TILE
    off = d_inner // tc

    def body(u_ref, w_ref, b_ref, dy_ref, du_ref, dw_ref, db_ref):
        u = u_ref[...]
        wk = [w_ref[D_CONV - 1 - s:D_CONV - s, :] for s in range(D_CONV)]
        shifted = [u] + [pltpu.roll(u, s, axis=0) for s in range(1, D_CONV)]
        acc = u * wk[0] + b_ref[...]
        for s in range(1, D_CONV):
            acc = acc + shifted[s] * wk[s]
        sg = _sigmoid(acc)
        mask = _row_mask(acc.shape, 0)
        dpre = jnp.where(mask, dy_ref[...] * (sg * (1.0 + acc * (1.0 - sg))), 0.0)
        db_ref[...] = jnp.sum(dpre, axis=0, keepdims=True)
        du = dpre * wk[0]
        dw_ref[D_CONV - 1:D_CONV, :] = jnp.sum(dpre * u, axis=0, keepdims=True)
        for s in range(1, D_CONV):
            du = du + pltpu.roll(dpre, rows - s, axis=0) * wk[s]
            dw_ref[D_CONV - 1 - s:D_CONV - s, :] = jnp.sum(dpre * shifted[s], axis=0, keepdims=True)
        du_ref[...] = jnp.where(mask, du, 0.0).astype(du_ref.dtype)

    return pl.pallas_call(
        body, name=name, grid=(conv_dim // tc,),
        in_specs=[pl.BlockSpec((rows, tc), lambda j: (0, off + j)), pl.BlockSpec((D_CONV, tc), lambda j: (0, j)),
                  pl.BlockSpec((1, tc), lambda j: (0, j)), pl.BlockSpec((rows, tc), lambda j: (0, j))],
        out_specs=[pl.BlockSpec((rows, tc), lambda j: (0, j)), pl.BlockSpec((D_CONV, tc), lambda j: (0, j)),
                   pl.BlockSpec((1, tc), lambda j: (0, j))],
        out_shape=[jax.ShapeDtypeStruct((rows, conv_dim), BF16), jax.ShapeDtypeStruct((D_CONV, conv_dim), F32),
                   jax.ShapeDtypeStruct((1, conv_dim), F32)],
        compiler_params=_params(("parallel",)),
    )(zx, conv_w, conv_b.reshape(1, conv_dim), dxbc)


def dt_fwd(name, zx, bias_pad, zx_cols):
    rows = zx.shape[0]
    tr = ROW_TILE
    off = zx_cols // LANES

    def body(r_ref, b_ref, o_ref):
        v = r_ref[...] + b_ref[...]
        sp = jnp.maximum(v, 0.0) + jnp.log1p(jnp.exp(-jnp.abs(v)))
        o_ref[...] = jnp.where(_row_mask(v.shape, pl.program_id(0) * tr), sp, 0.0)

    return pl.pallas_call(
        body, name=name, grid=(rows // tr,),
        in_specs=[pl.BlockSpec((tr, LANES), lambda i: (i, off)), pl.BlockSpec((1, LANES), lambda i: (0, 0))],
        out_specs=pl.BlockSpec((tr, LANES), lambda i: (i, 0)),
        out_shape=jax.ShapeDtypeStruct((rows, LANES), F32), compiler_params=_params(("parallel",)),
    )(zx, bias_pad)


def dt_bwd(name, zx, bias_pad, ddt, zx_cols):
    rows = zx.shape[0]
    tr = ROW_TILE
    off = zx_cols // LANES

    def body(r_ref, b_ref, d_ref, o_ref, db_ref):
        v = r_ref[...] + b_ref[...]
        g = jnp.where(_row_mask(v.shape, pl.program_id(0) * tr), d_ref[...] * _sigmoid(v), 0.0)
        o_ref[...] = g.astype(o_ref.dtype)

        @pl.when(pl.program_id(0) == 0)
        def _():
            db_ref[...] = jnp.zeros_like(db_ref)

        db_ref[...] += jnp.sum(g, axis=0, keepdims=True)

    return pl.pallas_call(
        body, name=name, grid=(rows // tr,),
        in_specs=[pl.BlockSpec((tr, LANES), lambda i: (i, off)), pl.BlockSpec((1, LANES), lambda i: (0, 0)),
                  pl.BlockSpec((tr, LANES), lambda i: (i, 0))],
        out_specs=[pl.BlockSpec((tr, LANES), lambda i: (i, 0)), pl.BlockSpec((1, LANES), lambda i: (0, 0))],
        out_shape=[jax.ShapeDtypeStruct((rows, LANES), BF16), jax.ShapeDtypeStruct((1, LANES), F32)],
        compiler_params=_params(("arbitrary",)),
    )(zx, bias_pad, ddt)


def _split3(x):
    h1 = x.astype(BF16)
    r1 = x - h1.astype(F32)
    h2 = r1.astype(BF16)
    h3 = (r1 - h2.astype(F32)).astype(BF16)
    return h1, h2, h3


def _exact_left(ones_b, x):
    h1, h2, h3 = _split3(x)
    return _dot(ones_b, h1, 1, 0) + _dot(ones_b, h2, 1, 0) + _dot(ones_b, h3, 1, 0)


def _exact_right_t(x, ones_b):
    h1, h2, h3 = _split3(x)
    return _dot(h1, ones_b, 1, 1) + _dot(h2, ones_b, 1, 1) + _dot(h3, ones_b, 1, 1)


class _ScanCommon:
    def __init__(self, b_ref, c_ref, dtc_ref, dtr_ref, arow_ref, acol_ref, drow_ref):
        q = CHUNK
        self.bb = b_ref[...].astype(BF16)
        self.cb = c_ref[...].astype(BF16)
        ri = lax.broadcasted_iota(jnp.int32, (q, q), 0)
        cj = lax.broadcasted_iota(jnp.int32, (q, q), 1)
        self.lower = ri >= cj
        self.upper = cj >= ri
        self.dtc = dtc_ref[...]
        self.dtr = dtr_ref[...]
        self.arow = arow_ref[...]
        self.acol = acol_ref[...]
        self.drow = drow_ref[...]
        da_col = self.dtc * self.arow
        self.a_col = _exact_left(self.lower.astype(BF16), da_col)
        self.a_row = _exact_right_t(self.dtr * self.acol, self.lower.astype(BF16))
        self.a_last = jnp.sum(da_col, axis=0, keepdims=True)
        self.lane_q = lax.broadcasted_iota(jnp.int32, (q, LANES), 1)
        self.lane_1 = lax.broadcasted_iota(jnp.int32, (1, LANES), 1)
        self.sub_8 = lax.broadcasted_iota(jnp.int32, (SUBLANES, q), 0)
        self.first_half = self.lane_q < HEAD_DIM
        self.first_rows = lax.broadcasted_iota(jnp.int32, (LANES, 1), 0) < HEAD_DIM

    def col(self, v, r):
        return jnp.sum(jnp.where(self.lane_q == r, v, 0.0), axis=1, keepdims=True)

    def row(self, v, r):
        return jnp.sum(jnp.where(self.sub_8 == r, v, 0.0), axis=0, keepdims=True)

    def scalar(self, v, r):
        return jnp.sum(jnp.where(self.lane_1 == r, v, 0.0), axis=1, keepdims=True)

    def pair(self, v0, v1):
        return jnp.where(self.first_half, v0, v1)

    def half_rowsum(self, t, h):
        keep = self.first_half if h == 0 else jnp.logical_not(self.first_half)
        return jnp.sum(jnp.where(keep, t, 0.0), axis=1, keepdims=True)


def ssd_scan_fwd(name, xbc, dt_col, dt_row, a_row, a_col, d_row, d_inner):
    rows = xbc.shape[0]
    q, n, g_cnt = CHUNK, D_STATE, SSD_GROUPS
    nc = rows // q
    rp = d_inner // g_cnt
    n_pairs = rp // LANES
    b_off = d_inner // n

    def body(x_ref, b_ref, c_ref, dtc_ref, dtr_ref, arow_ref, acol_ref, drow_ref, y_ref, sprev_ref, s_ref):
        @pl.when(pl.program_id(1) == 0)
        def _():
            s_ref[...] = jnp.zeros_like(s_ref)

        sprev_ref[...] = s_ref[...]
        k = _ScanCommon(b_ref, c_ref, dtc_ref, dtr_ref, arow_ref, acol_ref, drow_ref)
        cb_mat = _dot(k.cb, k.bb, 1, 1)
        for pr in range(n_pairs):
            sl = slice(pr * LANES, (pr + 1) * LANES)
            heads = (2 * pr, 2 * pr + 1)
            xp = x_ref[:, sl]
            ac = [k.col(k.a_col, r) for r in heads]
            ar = [k.row(k.a_row, r) for r in heads]
            al = [k.scalar(k.a_last, r) for r in heads]
            xd = xp * k.pair(k.col(k.dtc, heads[0]), k.col(k.dtc, heads[1]))
            xdb = xd.astype(BF16)
            ys = []
            for h in range(2):
                lm = jnp.exp(jnp.where(k.lower, ac[h] - ar[h], -jnp.inf))
                ys.append(_dot((cb_mat * lm).astype(BF16), xdb, 1, 0))
            y = jnp.where(k.first_half, ys[0], ys[1])
            sp = s_ref[sl, :]
            y = y + k.pair(jnp.exp(ac[0]), jnp.exp(ac[1])) * _dot(k.cb, sp.astype(BF16), 1, 1)
            y = y + k.pair(k.scalar(k.drow, heads[0]), k.scalar(k.drow, heads[1])) * xp
            y_ref[:, sl] = y
            wb = (xd * k.pair(jnp.exp(al[0] - ac[0]), jnp.exp(al[1] - ac[1]))).astype(BF16)
            decay = jnp.where(k.first_rows, jnp.exp(al[0]), jnp.exp(al[1]))
            s_ref[sl, :] = decay * sp + _dot(wb, k.bb, 0, 0)

    return pl.pallas_call(
        body, name=name, grid=(g_cnt, nc),
        in_specs=[
            pl.BlockSpec((q, rp), lambda g, c: (c, g)),
            pl.BlockSpec((q, n), lambda g, c: (c, b_off + g)),
            pl.BlockSpec((q, n), lambda g, c: (c, b_off + g_cnt + g)),
            pl.BlockSpec((None, q, LANES), lambda g, c: (g, c, 0)),
            pl.BlockSpec((None, SUBLANES, q), lambda g, c: (g, 0, c)),
            pl.BlockSpec((None, 1, LANES), lambda g, c: (g, 0, 0)),
            pl.BlockSpec((None, SUBLANES, 1), lambda g, c: (g, 0, 0)),
            pl.BlockSpec((None, 1, LANES), lambda g, c: (g, 0, 0)),
        ],
        out_specs=[pl.BlockSpec((q, rp), lambda g, c: (c, g)),
                   pl.BlockSpec((None, None, rp, n), lambda g, c: (c, g, 0, 0))],
        out_shape=[jax.ShapeDtypeStruct((rows, d_inner), F32), jax.ShapeDtypeStruct((nc, g_cnt, rp, n), F32)],
        scratch_shapes=[pltpu.VMEM((rp, n), F32)],
        compiler_params=_params(("parallel", "arbitrary")),
    )(xbc, xbc, xbc, dt_col, dt_row, a_row, a_col, d_row)


def ssd_scan_bwd(name, xbc, dt_col, dt_row, a_row, a_col, d_row, sprev, dy, d_inner):
    rows = xbc.shape[0]
    q, n, g_cnt = CHUNK, D_STATE, SSD_GROUPS
    nc = rows // q
    rp = d_inner // g_cnt
    n_pairs = rp // LANES
    b_off = d_inner // n

    def body(x_ref, b_ref, c_ref, dtc_ref, dtr_ref, arow_ref, acol_ref, drow_ref, sprev_ref, dy_ref,
             dx_ref, db_ref, dc_ref, ddtc_ref, ddtr_ref, dar_ref, dac_ref, dd_ref, ds_ref):
        @pl.when(pl.program_id(1) == 0)
        def _():
            ds_ref[...] = jnp.zeros_like(ds_ref)
            dar_ref[...] = jnp.zeros_like(dar_ref)
            dac_ref[...] = jnp.zeros_like(dac_ref)
            dd_ref[...] = jnp.zeros_like(dd_ref)

        k = _ScanCommon(b_ref, c_ref, dtc_ref, dtr_ref, arow_ref, acol_ref, drow_ref)
        cb_mat = _dot(k.cb, k.bb, 1, 1)
        cbt_mat = _dot(k.bb, k.cb, 1, 1)
        d_cb = jnp.zeros((q, q), F32)
        d_b = jnp.zeros((q, n), F32)
        d_c = jnp.zeros((q, n), F32)
        da_col = jnp.zeros((q, LANES), F32)
        da_row = jnp.zeros((SUBLANES, q), F32)
        ddt_x = jnp.zeros((q, LANES), F32)
        d_alast = jnp.zeros((1, LANES), F32)
        d_dskip = jnp.zeros((1, LANES), F32)
        for pr in range(n_pairs):
            sl = slice(pr * LANES, (pr + 1) * LANES)
            heads = (2 * pr, 2 * pr + 1)
            xp = x_ref[:, sl]
            dyp = dy_ref[:, sl]
            dyb = dyp.astype(BF16)
            ac = [k.col(k.a_col, r) for r in heads]
            ar = [k.row(k.a_row, r) for r in heads]
            al = [k.scalar(k.a_last, r) for r in heads]
            dt_p = k.pair(k.col(k.dtc, heads[0]), k.col(k.dtc, heads[1]))
            xd = xp * dt_p
            xdb = xd.astype(BF16)
            sp = sprev_ref[sl, :]
            spb = sp.astype(BF16)
            dsp = ds_ref[sl, :]
            dspb = dsp.astype(BF16)
            dskip_p = k.pair(k.scalar(k.drow, heads[0]), k.scalar(k.drow, heads[1]))
            dxp = dskip_p * dyp
            dd_lane = jnp.sum(dyp * xp, axis=0, keepdims=True)
            e_p = k.pair(jnp.exp(ac[0]), jnp.exp(ac[1]))
            t_off = dyp * (e_p * _dot(k.cb, spb, 1, 1))
            dzb = (e_p * dyp).astype(BF16)
            d_c = d_c + _dot(dzb, spb, 1, 0)
            ds_in = _dot(dzb, k.cb, 0, 0)
            decay = jnp.where(k.first_rows, jnp.exp(al[0]), jnp.exp(al[1]))
            ds_in = ds_in + decay * dsp
            t_state = jnp.sum(dsp * sp, axis=1, keepdims=True) * decay
            dec_p = k.pair(jnp.exp(al[0] - ac[0]), jnp.exp(al[1] - ac[1]))
            dw = _dot(k.bb, dspb, 1, 1)
            d_b = d_b + _dot((xd * dec_p).astype(BF16), dspb, 1, 0)
            dxd = dw * dec_p
            t_dec = dw * xd * dec_p
            dxd_h = []
            for h in range(2):
                r = heads[h]
                keep = k.first_half if h == 0 else jnp.logical_not(k.first_half)
                lm = jnp.exp(jnp.where(k.lower, ac[h] - ar[h], -jnp.inf))
                m_mat = cb_mat * lm
                dm = _dot(jnp.where(keep, dyp, 0.0).astype(BF16), xdb, 1, 1)
                dseg = dm * m_mat
                d_cb = d_cb + dm * lm
                lmt = jnp.exp(jnp.where(k.upper, ar[h] - ac[h], -jnp.inf))
                dxd_h.append(_dot((cbt_mat * lmt).astype(BF16), dyb, 1, 0))
                tdec_h = k.half_rowsum(t_dec, h)
                da_h = k.half_rowsum(t_off, h) - tdec_h + jnp.sum(dseg, axis=1, keepdims=True)
                da_col = da_col + jnp.where(k.lane_q == r, da_h, 0.0)
                da_row = da_row - jnp.where(k.sub_8 == r, jnp.sum(dseg, axis=0, keepdims=True), 0.0)
                keep_rows = k.first_rows if h == 0 else jnp.logical_not(k.first_rows)
                dal_h = jnp.sum(tdec_h, axis=0, keepdims=True) + jnp.sum(jnp.where(keep_rows, t_state, 0.0), axis=0, keepdims=True)
                d_alast = d_alast + jnp.where(k.lane_1 == r, dal_h, 0.0)
                keep_1 = k.lane_1 < HEAD_DIM if h == 0 else k.lane_1 >= HEAD_DIM
                dd_h = jnp.sum(jnp.where(keep_1, dd_lane, 0.0), axis=1, keepdims=True)
                d_dskip = d_dskip + jnp.where(k.lane_1 == r, dd_h, 0.0)
            dxd = dxd + jnp.where(k.first_half, dxd_h[0], dxd_h[1])
            dx_ref[:, sl] = dxp + dt_p * dxd
            t_dt = dxd * xp
            for h in range(2):
                ddt_x = ddt_x + jnp.where(k.lane_q == heads[h], k.half_rowsum(t_dt, h), 0.0)
            ds_ref[sl, :] = ds_in
        d_cb_b = d_cb.astype(BF16)
        db_ref[...] = d_b + _dot(d_cb_b, k.cb, 0, 0)
        dc_ref[...] = d_c + _dot(d_cb_b, k.bb, 1, 0)
        rc_col = _exact_left(k.upper.astype(BF16), da_col) + d_alast
        rc_row = _exact_right_t(da_row, k.upper.astype(BF16))
        ddtc_ref[...] = ddt_x + k.arow * rc_col
        ddtr_ref[...] = k.acol * rc_row
        dar_ref[...] += jnp.sum(rc_col * k.dtc, axis=0, keepdims=True)
        dac_ref[...] += jnp.sum(rc_row * k.dtr, axis=1, keepdims=True)
        dd_ref[...] += d_dskip

    rc = lambda c: nc - 1 - c
    return pl.pallas_call(
        body, name=name, grid=(g_cnt, nc),
        in_specs=[
            pl.BlockSpec((q, rp), lambda g, c: (rc(c), g)),
            pl.BlockSpec((q, n), lambda g, c: (rc(c), b_off + g)),
            pl.BlockSpec((q, n), lambda g, c: (rc(c), b_off + g_cnt + g)),
            pl.BlockSpec((None, q, LANES), lambda g, c: (g, rc(c), 0)),
            pl.BlockSpec((None, SUBLANES, q), lambda g, c: (g, 0, rc(c))),
            pl.BlockSpec((None, 1, LANES), lambda g, c: (g, 0, 0)),
            pl.BlockSpec((None, SUBLANES, 1), lambda g, c: (g, 0, 0)),
            pl.BlockSpec((None, 1, LANES), lambda g, c: (g, 0, 0)),
            pl.BlockSpec((None, None, rp, n), lambda g, c: (rc(c), g, 0, 0)),
            pl.BlockSpec((q, rp), lambda g, c: (rc(c), g)),
        ],
        out_specs=[
            pl.BlockSpec((q, rp), lambda g, c: (rc(c), g)),
            pl.BlockSpec((q, n), lambda g, c: (rc(c), g)),
            pl.BlockSpec((q, n), lambda g, c: (rc(c), g)),
            pl.BlockSpec((None, q, LANES), lambda g, c: (g, rc(c), 0)),
            pl.BlockSpec((None, SUBLANES, q), lambda g, c: (g, 0, rc(c))),
            pl.BlockSpec((None, 1, LANES), lambda g, c: (g, 0, 0)),
            pl.BlockSpec((None, SUBLANES, 1), lambda g, c: (g, 0, 0)),
            pl.BlockSpec((None, 1, LANES), lambda g, c: (g, 0, 0)),
        ],
        out_shape=[
            jax.ShapeDtypeStruct((rows, d_inner), F32),
            jax.ShapeDtypeStruct((rows, g_cnt * n), F32),
            jax.ShapeDtypeStruct((rows, g_cnt * n), F32),
            jax.ShapeDtypeStruct((g_cnt, rows, LANES), F32),
            jax.ShapeDtypeStruct((g_cnt, SUBLANES, rows), F32),
            jax.ShapeDtypeStruct((g_cnt, 1, LANES), F32),
            jax.ShapeDtypeStruct((g_cnt, SUBLANES, 1), F32),
            jax.ShapeDtypeStruct((g_cnt, 1, LANES), F32),
        ],
        scratch_shapes=[pltpu.VMEM((rp, n), F32)],
        compiler_params=_params(("parallel", "arbitrary")),
    )(xbc, xbc, xbc, dt_col, dt_row, a_row, a_col, d_row, sprev, dy)


def gatenorm_fwd(name, y, zx, w, d_inner):
    rows = y.shape[0]
    tr, gw = ROW_TILE, d_inner // SSD_GROUPS

    def body(y_ref, z_ref, w_ref, o_ref, ot_ref):
        z = z_ref[...]
        v = y_ref[...] * (z * _sigmoid(z))
        out = v * lax.rsqrt(jnp.mean(v * v, axis=-1, keepdims=True) + EPS) * w_ref[...]
        o_ref[...] = out.astype(o_ref.dtype)
        ot_ref[...] = out.T.astype(ot_ref.dtype)

    blk = pl.BlockSpec((tr, gw), lambda i, j: (i, j))
    return pl.pallas_call(
        body, name=name, grid=(rows // tr, SSD_GROUPS),
        in_specs=[blk, blk, pl.BlockSpec((1, gw), lambda i, j: (0, j))],
        out_specs=[blk, pl.BlockSpec((gw, tr), lambda i, j: (j, i))],
        out_shape=[jax.ShapeDtypeStruct((rows, d_inner), BF16), jax.ShapeDtypeStruct((d_inner, rows), BF16)],
        compiler_params=_params(("parallel", "parallel")),
    )(y, zx, w.reshape(1, d_inner))


def gatenorm_bwd(name, y, zx, w, dyn, d_inner, after):
    rows = y.shape[0]
    tr, gw = ROW_TILE, d_inner // SSD_GROUPS

    def body(y_ref, z_ref, w_ref, dn_ref, after_ref, dy_ref, dz_ref, dw_ref):
        z = z_ref[...]
        yv = y_ref[...]
        s = _sigmoid(z)
        gate = z * s
        v = yv * gate
        r = lax.rsqrt(jnp.mean(v * v, axis=-1, keepdims=True) + EPS)
        vh = v * r
        dn = dn_ref[...]
        dvh = dn * w_ref[...]
        dv = r * (dvh - vh * jnp.mean(dvh * vh, axis=-1, keepdims=True))
        dy_ref[...] = dv * gate
        dz_ref[...] = (dv * yv * (s * (1.0 + z * (1.0 - s)))).astype(dz_ref.dtype)

        @pl.when(pl.program_id(1) == 0)
        def _():
            dw_ref[...] = jnp.zeros_like(dw_ref)

        dw_ref[...] += jnp.sum(dn * vh, axis=0, keepdims=True)

    blk = pl.BlockSpec((tr, gw), lambda j, i: (i, j))
    wblk = pl.BlockSpec((1, gw), lambda j, i: (0, j))
    dy, dz, dw = pl.pallas_call(
        body, name=name, grid=(SSD_GROUPS, rows // tr),
        in_specs=[blk, blk, wblk, blk, pl.BlockSpec((SUBLANES, LANES), lambda j, i: (0, 0))], out_specs=[blk, blk, wblk],
        out_shape=[jax.ShapeDtypeStruct((rows, d_inner), F32), jax.ShapeDtypeStruct((rows, d_inner), BF16),
                   jax.ShapeDtypeStruct((1, d_inner), F32)],
        compiler_params=_params(("parallel", "arbitrary")),
    )(y, zx, w.reshape(1, d_inner), dyn, after)
    return dy, dz, dw.reshape(d_inner)


def _pool_count(rows, g):
    t1 = lax.broadcasted_iota(jnp.int32, (rows, 1), 0) - (PAD_FRONT - 1)
    win = jnp.left_shift(jnp.int32(POOL_WINDOWS[0]), g)
    return jnp.clip(t1, 1, win).astype(F32)


def _pool_select(levels, g):
    out = levels[-1]
    for i in range(len(levels) - 2, -1, -1):
        out = jnp.where(g == i, levels[i], out)
    return out


def pool_sub(name, u, transpose):
    rows, d = u.shape
    gd = d // len(POOL_WINDOWS)
    assert all(w == POOL_WINDOWS[0] << i for i, w in enumerate(POOL_WINDOWS))

    def body(u_ref, o_ref):
        g = pl.program_id(0)
        v = u_ref[...].astype(F32)
        cnt = _pool_count(rows, g)
        mask = _row_mask(v.shape, 0)
        s = v / cnt if transpose else v
        levels = []
        for i in range(len(POOL_WINDOWS)):
            step = 1 << i
            s = s + pltpu.roll(s, (rows - step) if transpose else step, axis=0)
            levels.append(s)
        sel = _pool_select(levels, g)
        out = (sel - v) if transpose else (sel / cnt - v)
        o_ref[...] = jnp.where(mask, out, 0.0).astype(o_ref.dtype)

    tc = POOL_TILE
    per = gd // tc
    blk = pl.BlockSpec((rows, tc), lambda g, j: (0, g * per + j))
    return pl.pallas_call(
        body, name=name, grid=(len(POOL_WINDOWS), per), in_specs=[blk], out_specs=blk,
        out_shape=jax.ShapeDtypeStruct((rows, d), F32 if transpose else BF16),
        compiler_params=_params(("parallel", "parallel")),
    )(u)


def pool_proj_fwd(name, mixed, w, b, scale):
    rows, d = mixed.shape
    ng = len(POOL_WINDOWS)
    gd = d // ng
    tr = ROW_TILE

    def body(m_ref, w_ref, b_ref, s_ref, pre_ref, mix_ref):
        pre = _dot(m_ref[...], w_ref[...], 1, 0) + b_ref[...]
        pre_ref[...] = pre
        mix_ref[...] = jnp.where(_row_mask(pre.shape, pl.program_id(1) * tr), pre * s_ref[...], 0.0)

    blk = pl.BlockSpec((tr, gd), lambda g, i: (i, g))
    vec = pl.BlockSpec((1, gd), lambda g, i: (0, g))
    return pl.pallas_call(
        body, name=name, grid=(ng, rows // tr),
        in_specs=[blk, pl.BlockSpec((None, gd, gd), lambda g, i: (g, 0, 0)), vec, vec], out_specs=[blk, blk],
        out_shape=[jax.ShapeDtypeStruct((rows, d), F32), jax.ShapeDtypeStruct((rows, d), F32)],
        compiler_params=_params(("parallel", "parallel")),
    )(mixed, w, b.reshape(1, d), scale.reshape(1, d))


def pool_proj_bwd(name, dmix, pre, mixed, w, scale):
    rows, d = dmix.shape
    ng = len(POOL_WINDOWS)
    gd = d // ng
    tr = ROW_TILE

    def body(dm_ref, pre_ref, mx_ref, w_ref, s_ref, dmx_ref, dw_ref, db_ref, ds_ref):
        @pl.when(pl.program_id(1) == 0)
        def _():
            dw_ref[...] = jnp.zeros_like(dw_ref)
            db_ref[...] = jnp.zeros_like(db_ref)
            ds_ref[...] = jnp.zeros_like(ds_ref)

        dmv = jnp.where(_row_mask(dm_ref.shape, pl.program_id(1) * tr), dm_ref[...], 0.0)
        dpre = dmv * s_ref[...]
        dpre_b = dpre.astype(BF16)
        ds_ref[...] += jnp.sum(dmv * pre_ref[...], axis=0, keepdims=True)
        db_ref[...] += jnp.sum(dpre, axis=0, keepdims=True)
        dmx_ref[...] = _dot(dpre_b, w_ref[...], 1, 1)
        dw_ref[...] += _dot(mx_ref[...], dpre_b, 0, 0)

    blk = pl.BlockSpec((tr, gd), lambda g, i: (i, g))
    vec = pl.BlockSpec((1, gd), lambda g, i: (0, g))
    wblk = pl.BlockSpec((None, gd, gd), lambda g, i: (g, 0, 0))
    dmixed, dw, db, ds = pl.pallas_call(
        body, name=name, grid=(ng, rows // tr),
        in_specs=[blk, blk, blk, wblk, vec], out_specs=[blk, wblk, vec, vec],
        out_shape=[jax.ShapeDtypeStruct((rows, d), F32), jax.ShapeDtypeStruct((ng, gd, gd), F32),
                   jax.ShapeDtypeStruct((1, d), F32), jax.ShapeDtypeStruct((1, d), F32)],
        compiler_params=_params(("parallel", "arbitrary")),
    )(dmix, pre, mixed, w, scale.reshape(1, d))
    return dmixed, dw, db.reshape(d), ds.reshape(d)


def _my_place():
    return lax.axis_index("x"), lax.axis_index("y"), lax.axis_index("c")


def _linear(place):
    return 4 * place[0] + 2 * place[1] + place[2]


def all_gather(name, shards, after):
    n_ops = len(shards)

    def body(*refs):
        ins, outs = refs[:n_ops], refs[n_ops + 1:2 * n_ops + 1]
        send_sems, recv_sems, local_sems = refs[2 * n_ops + 1:]
        x, y, c = _my_place()
        me, sibling = (x, y, c), (x, y, 1 - c)
        chips = [(1 - x, y), (x, 1 - y), (1 - x, 1 - y)]

        def copy(t, k, block, to, src=None):
            dst = outs[t].at[_linear(block)]
            return pltpu.make_async_remote_copy(
                src_ref=dst if src is None else src, dst_ref=dst, send_sem=send_sems.at[t, k], recv_sem=recv_sems.at[t, k],
                device_id=to, device_id_type=MESH)

        mine = [pltpu.make_async_copy(ins[t], outs[t].at[_linear(me)], local_sems.at[t]) for t in range(n_ops)]
        for cp in mine:
            cp.start()
        first = []
        for t in range(n_ops):
            first.append(copy(t, 0, me, sibling, src=ins[t]))
            first += [copy(t, 1 + j, me, (*chip, c), src=ins[t]) for j, chip in enumerate(chips)]
        for cp in first:
            cp.start()
        passed = []
        for j, chip in enumerate(chips):
            for t in range(n_ops):
                copy(t, 1 + j, (*chip, c), me).wait_recv()
                fwd = copy(t, 4 + j, (*chip, c), sibling)
                fwd.start()
                passed.append(fwd)
        for t in range(n_ops):
            copy(t, 0, sibling, me).wait_recv()
            for j, chip in enumerate(chips):
                copy(t, 4 + j, (*chip, 1 - c), me).wait_recv()
        for cp in first + passed:
            cp.wait_send()
        for cp in mine:
            cp.wait()

    any_spec = pl.BlockSpec(memory_space=pl.ANY)
    outs = pl.pallas_call(
        body, name=name,
        in_specs=[any_spec] * (n_ops + 1), out_specs=[any_spec] * n_ops,
        out_shape=[jax.ShapeDtypeStruct((N_DEV, *s.shape), s.dtype) for s in shards],
        scratch_shapes=[pltpu.SemaphoreType.DMA((n_ops, 7)), pltpu.SemaphoreType.DMA((n_ops, 7)),
                        pltpu.SemaphoreType.DMA((n_ops,))],
    )(*shards, after)
    return list(outs)


_HBM = pl.BlockSpec(memory_space=pltpu.HBM)
_SEM = pl.BlockSpec(memory_space=pltpu.SEMAPHORE)
_EFFECT = pltpu.SideEffectType.DATAFLOW_SIDE_EFFECTING


def _hbm(a):
    return pltpu.with_memory_space_constraint(a, pltpu.HBM)


def _peers():
    x, y, c = _my_place()
    return [(x ^ (j >> 2), y ^ ((j >> 1) & 1), c ^ (j & 1)) for j in range(1, N_DEV)]


def exchange_start(name, groups, after):
    flat = [e for g in groups for e in g]
    n = len(flat)

    def body(*refs):
        srcs, lands = refs[:n], refs[n:2 * n]
        outs = refs[2 * n + 1:]
        sends, recvs, token = outs[:n], outs[n:2 * n], outs[4 * n]
        me = _linear(_my_place())
        for t, (_, _, src_view, land_view) in enumerate(flat):
            for peer in _peers():
                pltpu.make_async_remote_copy(
                    src_ref=src_view(srcs[t], _linear(peer)), dst_ref=land_view(lands[t], me),
                    send_sem=sends[t], recv_sem=recvs[t], device_id=peer, device_id_type=MESH).start()
        token[...] = jnp.zeros_like(token)

    sem_shapes = [pltpu.SemaphoreType.DMA(())] * (2 * n)
    thru = [pltpu.HBM(e[0].shape, e[0].dtype) for e in flat] + [pltpu.HBM(e[1].shape, e[1].dtype) for e in flat]
    outs = pl.pallas_call(
        body, name=name,
        out_shape=(*sem_shapes, *thru, jax.ShapeDtypeStruct((SUBLANES, LANES), F32)),
        in_specs=[*[_HBM] * (2 * n), pl.BlockSpec(memory_space=pl.ANY)],
        out_specs=(*[_SEM] * (2 * n), *[_HBM] * (2 * n), pl.BlockSpec(memory_space=pltpu.VMEM)),
        input_output_aliases={t: 2 * n + t for t in range(2 * n)},
        compiler_params=pltpu.CompilerParams(has_side_effects=_EFFECT),
    )(*[_hbm(e[0]) for e in flat], *[_hbm(e[1]) for e in flat], after)
    records, t = [], 0
    for g in groups:
        k = len(g)
        records.append((list(outs[t:t + k]), list(outs[n + t:n + t + k]), list(outs[2 * n + t:2 * n + t + k]),
                        list(outs[3 * n + t:3 * n + t + k])))
        t += k
    return records, outs[-1]


def exchange_wait(name, records, lands, land_of, seven_of, after):
    srcs = [s for r in records for s in r[2]]
    sends = [s for r in records for s in r[0]]
    recvs = [s for r in records for s in r[1]]
    where = [(ri, k) for ri, r in enumerate(records) for k in range(len(r[2]))]
    ns, nl = len(srcs), len(lands)

    def body(*refs):
        src_refs, land_refs = refs[:ns], refs[ns:ns + nl]
        send_refs, recv_refs = refs[ns + nl:2 * ns + nl], refs[2 * ns + nl:3 * ns + nl]
        for t, (ri, k) in enumerate(where):
            seven_ref = seven_of[ri][k](src_refs[t], land_refs[land_of[ri][k]])
            cp = pltpu.make_async_remote_copy(src_ref=seven_ref, dst_ref=seven_ref, send_sem=send_refs[t], recv_sem=recv_refs[t],
                                              device_id=_my_place(), device_id_type=MESH)
            cp.wait_send()
            cp.wait_recv()

    outs = pl.pallas_call(
        body, name=name,
        out_shape=tuple(pltpu.HBM(a.shape, a.dtype) for a in (*srcs, *lands)),
        in_specs=[*[_HBM] * (ns + nl), *[_SEM] * (2 * ns), pl.BlockSpec(memory_space=pl.ANY)], out_specs=tuple([_HBM] * (ns + nl)),
        input_output_aliases={t: t for t in range(ns + nl)},
        compiler_params=pltpu.CompilerParams(has_side_effects=_EFFECT),
    )(*srcs, *lands, *sends, *recvs, after)
    return list(outs[:ns]), list(outs[ns:])


def _other_chips():
    x, y, c = _my_place()
    return [(1 - x, y, c), (x, 1 - y, c), (1 - x, 1 - y, c)]


def gather2_start(name, groups, after):
    flat = [e for g in groups for e in g]
    n = len(flat)

    def body(*refs):
        srcs, lands = refs[:n], refs[n:2 * n]
        outs = refs[2 * n + 1:]
        d_send, d_recv, i_send, i_recv, token = outs[:n], outs[n:2 * n], outs[2 * n:3 * n], outs[3 * n:4 * n], outs[6 * n]
        x, y, c = _my_place()
        me = _linear((x, y, c))
        for t in range(n):
            pltpu.make_async_remote_copy(src_ref=srcs[t], dst_ref=lands[t].at[me], send_sem=d_send[t], recv_sem=d_recv[t],
                                         device_id=(x, y, 1 - c), device_id_type=MESH).start()
            for peer in _other_chips():
                pltpu.make_async_remote_copy(src_ref=srcs[t], dst_ref=lands[t].at[me], send_sem=i_send[t], recv_sem=i_recv[t],
                                             device_id=peer, device_id_type=MESH).start()
        token[...] = jnp.zeros_like(token)

    thru = [pltpu.HBM(e[0].shape, e[0].dtype) for e in flat] + [pltpu.HBM(e[1].shape, e[1].dtype) for e in flat]
    outs = pl.pallas_call(
        body, name=name,
        out_shape=(*[pltpu.SemaphoreType.DMA(())] * (4 * n), *thru, jax.ShapeDtypeStruct((SUBLANES, LANES), F32)),
        in_specs=[*[_HBM] * (2 * n), pl.BlockSpec(memory_space=pl.ANY)],
        out_specs=(*[_SEM] * (4 * n), *[_HBM] * (2 * n), pl.BlockSpec(memory_space=pltpu.VMEM)),
        input_output_aliases={t: 4 * n + t for t in range(2 * n)},
        compiler_params=pltpu.CompilerParams(has_side_effects=_EFFECT),
    )(*[_hbm(e[0]) for e in flat], *[_hbm(e[1]) for e in flat], after)
    records, t = [], 0
    for g in groups:
        k = len(g)
        records.append({"d_send": list(outs[t:t + k]), "d_recv": list(outs[n + t:n + t + k]),
                        "i_send": list(outs[2 * n + t:2 * n + t + k]), "i_recv": list(outs[3 * n + t:3 * n + t + k]),
                        "srcs": list(outs[4 * n + t:4 * n + t + k]), "lands": list(outs[5 * n + t:5 * n + t + k])})
        t += k
    return records, outs[-1]


def gather2_relay(name, rec, after):
    lands, i_recv = rec["lands"], rec["i_recv"]
    n = len(lands)

    def body(*refs):
        land_refs, i_recv_refs = refs[:n], refs[n:2 * n]
        outs = refs[2 * n + 1:]
        f_send, f_recv, token = outs[n:2 * n], outs[2 * n:3 * n], outs[3 * n]
        x, y, c = _my_place()
        for t in range(n):
            three = land_refs[t].at[pl.ds(0, 3)]
            pltpu.make_async_remote_copy(src_ref=three, dst_ref=three, send_sem=f_send[t], recv_sem=i_recv_refs[t],
                                         device_id=(x, y, c), device_id_type=MESH).wait_recv()
            for peer in _other_chips():
                blk = land_refs[t].at[_linear(peer)]
                pltpu.make_async_remote_copy(src_ref=blk, dst_ref=blk, send_sem=f_send[t], recv_sem=f_recv[t],
                                             device_id=(x, y, 1 - c), device_id_type=MESH).start()
        token[...] = jnp.zeros_like(token)

    outs = pl.pallas_call(
        body, name=name,
        out_shape=(*[pltpu.HBM(a.shape, a.dtype) for a in lands], *[pltpu.SemaphoreType.DMA(())] * (2 * n),
                   jax.ShapeDtypeStruct((SUBLANES, LANES), F32)),
        in_specs=[*[_HBM] * n, *[_SEM] * n, pl.BlockSpec(memory_space=pl.ANY)],
        out_specs=(*[_HBM] * n, *[_SEM] * (2 * n), pl.BlockSpec(memory_space=pltpu.VMEM)),
        input_output_aliases={t: t for t in range(n)},
        compiler_params=pltpu.CompilerParams(has_side_effects=_EFFECT),
    )(*lands, *i_recv, after)
    return {**rec, "lands": list(outs[:n]), "f_send": list(outs[n:2 * n]), "f_recv": list(outs[2 * n:3 * n])}, outs[-1]


def gather2_wait(name, rec, after):
    n = len(rec["lands"])
    sem_names = ("d_send", "d_recv", "i_send", "f_send", "f_recv")

    def body(*refs):
        src_refs, land_refs = refs[:n], refs[n:2 * n]
        sems = {nm: refs[2 * n + k * n:2 * n + (k + 1) * n] for k, nm in enumerate(sem_names)}
        me = _my_place()
        for t in range(n):
            one, three = land_refs[t].at[0], land_refs[t].at[pl.ds(0, 3)]
            to_sibling = pltpu.make_async_remote_copy(src_ref=src_refs[t], dst_ref=one, send_sem=sems["d_send"][t],
                                                      recv_sem=sems["d_recv"][t], device_id=me, device_id_type=MESH)
            to_sibling.wait_send()
            to_sibling.wait_recv()
            pltpu.make_async_remote_copy(src_ref=three, dst_ref=three, send_sem=sems["i_send"][t], recv_sem=sems["d_recv"][t],
                                         device_id=me, device_id_type=MESH).wait_send()
            forwarded = pltpu.make_async_remote_copy(src_ref=three, dst_ref=three, send_sem=sems["f_send"][t],
                                                     recv_sem=sems["f_recv"][t], device_id=me, device_id_type=MESH)
            forwarded.wait_send()
            forwarded.wait_recv()

    arrays = (*rec["srcs"], *rec["lands"])
    outs = pl.pallas_call(
        body, name=name,
        out_shape=tuple(pltpu.HBM(a.shape, a.dtype) for a in arrays),
        in_specs=[*[_HBM] * (2 * n), *[_SEM] * (5 * n), *[pl.BlockSpec(memory_space=pl.ANY)] * 2], out_specs=tuple([_HBM] * (2 * n)),
        input_output_aliases={t: t for t in range(2 * n)},
        compiler_params=pltpu.CompilerParams(has_side_effects=_EFFECT),
    )(*arrays, *[s for nm in sem_names for s in rec[nm]], *after)
    return list(outs[:n]), list(outs[n:])


def _seven_slots_of_land(src_ref, land_ref):
    return land_ref.at[pl.ds(0, N_DEV - 1)]


def _seven_blocks_of_src(src_ref, land_ref):
    return src_ref.at[pl.ds(0, N_DEV - 1)]


def _whole(ref, dev):
    return ref


def _slot(ref, dev):
    return ref.at[dev]


def _slot_of_layer(layer):
    return lambda ref, dev: ref.at[dev, layer]


def _as_rows(a, lead=0):
    return a.reshape(a.shape[:lead] + (-1, a.shape[-1]))


def sum_adamw(name, parts, w, m, v, half=None, column_major=False):
    shape = w.shape
    c1 = 1.0 - ADAM_B1 ** ADAM_STEP
    c2 = 1.0 - ADAM_B2 ** ADAM_STEP
    if column_major:
        layers, r, cols = shape
        tr = _tile(r, ROW_TILE, LANES)
        grid = (layers, r // tr)
        if half is None:
            p_spec = pl.BlockSpec((N_DEV, None, tr, cols), lambda l, i: (0, l, i, 0))
        else:
            p_spec = pl.BlockSpec((N_DEV, None, None, tr, cols), lambda l, i: (0, l, half, i, 0))
        blk = pl.BlockSpec((None, cols, tr), lambda l, i: (l, 0, i))
        args = (parts, *[jnp.swapaxes(a, 1, 2) for a in (w, m, v)])
        out_sds, sem = jax.ShapeDtypeStruct((layers, cols, r), F32), ("parallel", "parallel")
    elif half is not None:
        layers, r, cols = shape
        tr = _tile(r, ROW_TILE, 16)
        grid = (layers, r // tr)
        p_spec = pl.BlockSpec((N_DEV, None, None, tr, cols), lambda l, i: (0, l, half, i, 0))
        blk = pl.BlockSpec((None, tr, cols), lambda l, i: (l, i, 0))
        args, out_sds, sem = (parts, w, m, v), jax.ShapeDtypeStruct(shape, F32), ("parallel", "parallel")
    else:
        p2, w2, m2, v2 = _as_rows(parts, 1), _as_rows(w), _as_rows(m), _as_rows(v)
        rows, cols = w2.shape
        tr = _tile(rows, ROW_TILE, 16)
        grid = (rows // tr,)
        p_spec = pl.BlockSpec((N_DEV, tr, cols), lambda i: (0, i, 0))
        blk = pl.BlockSpec((tr, cols), lambda i: (i, 0))
        args, out_sds, sem = (p2, w2, m2, v2), jax.ShapeDtypeStruct((rows, cols), F32), ("parallel",)

    def body(p_ref, w_ref, m_ref, v_ref, g_ref, d_ref, nm_ref, nv_ref):
        g = p_ref[0].astype(F32)
        for k in range(1, N_DEV):
            g = g + p_ref[k].astype(F32)
        if column_major:
            g = g.T
        wv = w_ref[...]
        nm = ADAM_B1 * m_ref[...] + (1.0 - ADAM_B1) * g
        nv = ADAM_B2 * v_ref[...] + (1.0 - ADAM_B2) * (g * g)
        g_ref[...] = g
        nm_ref[...] = nm
        nv_ref[...] = nv
        d_ref[...] = -ADAM_LR * ((nm / c1) / (jnp.sqrt(nv / c2) + ADAM_EPS) + ADAM_WD * wv)

    outs = pl.pallas_call(
        body, name=name, grid=grid, in_specs=[p_spec, blk, blk, blk], out_specs=[blk] * 4,
        out_shape=[out_sds] * 4, compiler_params=_params(sem),
    )(*args)
    if column_major:
        return [jnp.swapaxes(o, 1, 2) for o in outs]
    return [o.reshape(shape) for o in outs]


def _unblock_cols(g):
    g = jnp.moveaxis(g, 0, -2)
    return g.reshape(g.shape[:-2] + (g.shape[-2] * g.shape[-1],))


def _block_cols(a):
    r, c = a.shape
    return jnp.moveaxis(a.reshape(r, N_DEV, c // N_DEV), 1, 0)


def _my_cols(a, n):
    me = _linear(_my_place())
    return lax.dynamic_slice_in_dim(a, me * n, n, axis=a.ndim - 1)


def _pack(arrays):
    flat = jnp.concatenate([a.reshape(-1).astype(F32) for a in arrays])
    pad = (-flat.shape[0]) % (ROW_TILE * LANES)
    return jnp.pad(flat, (0, pad)).reshape(-1, LANES)


def _unpack(packed, like):
    flat = packed.reshape(-1)
    out, pos = [], 0
    for a in like:
        out.append(flat[pos:pos + a.size].reshape(a.shape))
        pos += a.size
    return out


def kernel(x, meta_tokens, norm_w, ssd_w_in, ssd_conv_w, ssd_conv_b, ssd_dt_bias, ssd_a_log, ssd_d, ssd_norm_w, ssd_w_out, pool_w, pool_b, pool_scale, ffn_w_gate, ffn_w_up, ffn_w_down, loss_target, m_meta_tokens, m_norm_w, m_ssd_w_in, m_ssd_conv_w, m_ssd_conv_b, m_ssd_dt_bias, m_ssd_a_log, m_ssd_d, m_ssd_norm_w, m_ssd_w_out, m_pool_w, m_pool_b, m_pool_scale, m_ffn_w_gate, m_ffn_w_up, m_ffn_w_down, v_meta_tokens, v_norm_w, v_ssd_w_in, v_ssd_conv_w, v_ssd_conv_b, v_ssd_dt_bias, v_ssd_a_log, v_ssd_d, v_ssd_norm_w, v_ssd_w_out, v_pool_w, v_pool_b, v_pool_scale, v_ffn_w_gate, v_ffn_w_up, v_ffn_w_down):
    seq, d = x.shape[1], x.shape[2]
    depth = norm_w.shape[0]
    n_ssd = ssd_w_in.shape[0]
    d_inner = ssd_norm_w.shape[1]
    heads = d_inner // HEAD_DIM
    rpg = heads // SSD_GROUPS
    conv_dim = ssd_conv_b.shape[1]
    zx_cols = d_inner + conv_dim
    d_in_proj = zx_cols + heads
    rows = PAD_FRONT + N_META + seq
    assert rows % CHUNK == 0 and rpg % 2 == 0 and heads <= LANES and rpg <= SUBLANES

    hidden = ffn_w_down.shape[1] * N_DEV
    n_pool = pool_w.shape[0]

    gather_recs = {}

    def start_gather(i, after, groups_of_layer=slice(None), tag=""):
        def entry(shard):
            return lambda: (shard.astype(BF16), lax.empty((N_DEV, *shard.shape), BF16))
        mixer = [ssd_w_in[i // 2], ssd_w_out[i // 2]] if i % 2 == 0 else [pool_w[i // 2]]
        groups = [*[[entry(w)] for w in mixer], [entry(jnp.stack([ffn_w_gate[i], ffn_w_up[i]])), entry(ffn_w_down[i])]]
        groups = [[make() for make in g] for g in groups[groups_of_layer]]
        recs, tok = gather2_start(f"gather_start_{i}{tag}", groups, after)
        gather_recs[i] = gather_recs.get(i, []) + recs
        return tok

    first = start_gather(0, meta_tokens, slice(0, 1), "_first")
    small = all_gather("gather_small", [meta_tokens, norm_w, ssd_conv_w, pool_b, pool_scale], first)
    meta_f, norm_f, convw_f, poolb_f, pools_f = [_unblock_cols(s) for s in small]

    gather_order = [(i, g) for i in range(depth) for g in range(3 if i % 2 == 0 else 2)]
    relayed = [0, None]

    def gathered(i, g, after):
        idx = gather_order.index((i, g))
        upto = min(idx + (1 if idx == 0 else 2), len(gather_order))
        while relayed[0] < upto:
            li, lg = gather_order[relayed[0]]
            gather_recs[li][lg], relayed[1] = gather2_relay(f"gather_relay_{li}_{lg}", gather_recs[li][lg], after)
            relayed[0] += 1
        srcs, lands_ = gather2_wait(f"gather_wait_{i}_{g}", gather_recs[i][g], (after, relayed[1]))
        me = _linear(_my_place())
        return [lax.dynamic_update_slice_in_dim(land, src[None], me, axis=0) for src, land in zip(srcs, lands_)]

    def gathered_small(name, rec, after):
        srcs, lands_ = exchange_wait(name, [rec], rec[3], [[0]], [[_seven_slots_of_land]], after)
        return lax.dynamic_update_slice_in_dim(lands_[0], srcs[0][None], _linear(_my_place()), axis=0)

    pending = start_gather(0, meta_f, slice(1, None))
    for i in range(1, min(GATHER_AHEAD, depth)):
        pending = start_gather(i, pending)

    pad_h = lambda a: jnp.pad(a.astype(F32), ((0, 0), (0, LANES - heads)))
    bias_pad = pad_h(ssd_dt_bias)

    def head_layouts(vec):
        g = vec.reshape(SSD_GROUPS, rpg)
        row = jnp.pad(g, ((0, 0), (0, LANES - rpg)))[:, None, :]
        col = jnp.pad(g, ((0, 0), (0, SUBLANES - rpg)))[:, :, None]
        return row, col

    def dt_layouts(dt):
        g = dt[:, :heads].reshape(rows, SSD_GROUPS, rpg)
        col = jnp.pad(jnp.moveaxis(g, 1, 0), ((0, 0), (0, 0), (0, LANES - rpg)))
        row = jnp.pad(jnp.transpose(g, (1, 2, 0)), ((0, 0), (0, SUBLANES - rpg), (0, 0)))
        return col, row

    h = jnp.concatenate([jnp.zeros((PAD_FRONT, d), F32), meta_f, x[0]], axis=0)
    saved = []
    for i in range(depth):
        j = i // 2
        s = {"h": h}
        w_pre_mix = norm_f[i, 0]
        if i + GATHER_AHEAD < depth:
            pending = start_gather(i + GATHER_AHEAD, pending if i == 0 else h)
        (mixer_w,) = gathered(i, 0, h)
        if i % 2 == 0:
            w_in = jnp.pad(_unblock_cols(mixer_w), ((0, 0), (0, LANES - heads)))
            u, u_t = rmsnorm_fwd(f"norm_pre_mix_{i}", h, w_pre_mix, out_dtype=BF16, transposed=True, after=pending)
            zx = matmul(f"ssd_in_{i}", u, w_in, **MM_ROWS_RESIDENT, tn=384 if w_in.shape[1] % 384 == 0 else 512)
            xbc = conv_fwd(f"ssd_conv_{i}", zx, convw_f[j], ssd_conv_b[j], d_inner, conv_dim)
            dt = dt_fwd(f"ssd_dt_{i}", zx, bias_pad[j:j + 1], zx_cols)
            dt_col, dt_row = dt_layouts(dt)
            a_neg = -jnp.exp(ssd_a_log[j].astype(F32))
            a_row, a_col = head_layouts(a_neg)
            d_row, _ = head_layouts(ssd_d[j].astype(F32))
            y, sprev = ssd_scan_fwd(f"ssd_scan_{i}", xbc, dt_col, dt_row, a_row, a_col, d_row, d_inner)
            yn, yn_t = gatenorm_fwd(f"ssd_gate_{i}", y, zx, ssd_norm_w[j], d_inner)
            (w_out,) = gathered(i, 1, yn)
            w_out = w_out.reshape(d_inner, d)
            mix = matmul(f"ssd_out_{i}", yn, w_out, **MM_DEEP)
            s.update(u_t=u_t, zx=zx, xbc=xbc, dt_col=dt_col, dt_row=dt_row, a_row=a_row, a_col=a_col, d_row=d_row,
                     a_neg=a_neg, y=y, sprev=sprev, yn_t=yn_t, w_in=w_in, w_out=w_out)
        else:
            w_pool = jnp.moveaxis(mixer_w, 0, 1).reshape(len(POOL_WINDOWS), d // len(POOL_WINDOWS), -1)
            u = rmsnorm_fwd(f"norm_pre_mix_{i}", h, w_pre_mix, after=pending)
            mixed = pool_sub(f"pool_sub_{i}", u, False)
            pre, mix = pool_proj_fwd(f"pool_proj_{i}", mixed, w_pool, poolb_f[j], pools_f[j])
            s.update(mixed=mixed, pre=pre, w_pool=w_pool)
        pending = None
        h1 = rmsnorm_fwd(f"norm_post_mix_{i}", mix, norm_f[i, 1], res=h)
        w_gu, w_down = gathered(i, 2 if i % 2 == 0 else 1, h1)
        u2, u2_t = rmsnorm_fwd(f"norm_pre_ffn_{i}", h1, norm_f[i, 2], out_dtype=BF16, transposed=True)
        gp = ffn_in_fwd(f"ffn_in_{i}", u2, w_gu)
        act, act_t = swiglu_fwd(f"ffn_act_{i}", gp)
        f = ffn_out_fwd(f"ffn_out_{i}", act, w_down)
        h = rmsnorm_fwd(f"norm_post_ffn_{i}", f, norm_f[i, 3], res=h1)
        s.update(mix=mix, h1=h1, u2_t=u2_t, gp=gp, act_t=act_t, f=f, w_gu=w_gu, w_down=w_down)
        saved.append(s)

    dh, loss_local = loss_head("loss_head", h, loss_target[0])
    loss = lax.psum(loss_local, AXES)

    g_norm = [[None] * 4 for _ in range(depth)]
    g_convw, g_convb, g_dtb, g_alog, g_dskip, g_ssdnorm = ([None] * n_ssd for _ in range(6))
    g_poolb, g_pools = ([None] * n_pool for _ in range(2))
    hid_s = hidden // N_DEV
    lands = {"in": lax.empty((N_DEV, *ssd_w_in.shape), BF16), "out": lax.empty((N_DEV, *ssd_w_out.shape), BF16),
             "pool": lax.empty((N_DEV, *pool_w.shape), BF16), "down": lax.empty((N_DEV, *ffn_w_down.shape), BF16),
             "gate_up": lax.empty((N_DEV, depth, 2, d, hid_s), BF16)}
    scatter_recs, scatter_keys, scatter_views, scatter_layers = [], [], [], []

    def scatter(name, blocks, layer, after):
        view = _slot_of_layer(layer)
        (rec,), tok = exchange_start(name, [[(b, lands[key], _slot, view) for key, b in blocks]], after)
        for (key, _), thru in zip(blocks, rec[3]):
            lands[key] = thru
        scatter_recs.append(rec)
        scatter_keys.append([key for key, _ in blocks])
        scatter_views.append([_seven_blocks_of_src] * len(blocks))
        scatter_layers.append(layer)
        return tok

    tok = None
    for i in reversed(range(depth)):
        j = i // 2
        s = saved[i]
        df, g_norm[i][3] = rmsnorm_bwd(f"norm_post_ffn_bwd_{i}", s["f"], norm_f[i, 3], dh, after=tok)
        df_b = df.astype(BF16)
        dact = ffn_out_bwd_x(f"ffn_out_bwd_x_{i}", df_b, s["w_down"])
        g_down = ffn_out_bwd_w(f"ffn_out_bwd_w_{i}", s["act_t"], df_b)
        dgp = swiglu_bwd(f"ffn_act_bwd_{i}", s["gp"], dact)
        g_gu = ffn_in_bwd_w(f"ffn_in_bwd_w_{i}", s["u2_t"], dgp)
        tok = scatter(f"scatter_start_ffn_{i}", [("gate_up", g_gu), ("down", g_down)], i, g_gu)
        du2 = ffn_in_bwd_x(f"ffn_in_bwd_x_{i}", dgp, s["w_gu"])
        dh1, g_norm[i][2] = rmsnorm_bwd(f"norm_pre_ffn_bwd_{i}", s["h1"], norm_f[i, 2], du2, add=dh, after=tok)
        dmix, g_norm[i][1] = rmsnorm_bwd(f"norm_post_mix_bwd_{i}", s["mix"], norm_f[i, 1], dh1)
        if i % 2 == 0:
            dmix_b = dmix.astype(BF16)
            dyn = matmul(f"ssd_out_bwd_x_{i}", dmix_b, s["w_out"], tb=True, **MM_ROWS_RESIDENT)
            g_out = matmul(f"ssd_out_bwd_w_{i}", s["yn_t"], dmix_b, out_dtype=BF16, **MM_COLS_RESIDENT)
            tok = scatter(f"scatter_start_out_{i}", [("out", g_out.reshape(N_DEV, d_inner // N_DEV, d))], j, g_out)
            dy, dz, g_ssdnorm[j] = gatenorm_bwd(f"ssd_gate_bwd_{i}", s["y"], s["zx"], ssd_norm_w[j], dyn, d_inner, after=tok)
            dx, db, dc, ddt_col, ddt_row, dar, dac, ddsk = ssd_scan_bwd(
                f"ssd_scan_bwd_{i}", s["xbc"], s["dt_col"], s["dt_row"], s["a_row"], s["a_col"], s["d_row"], s["sprev"], dy, d_inner)
            ddt = (jnp.moveaxis(ddt_col[:, :, :rpg], 0, 1).reshape(rows, heads)
                   + jnp.transpose(ddt_row[:, :rpg, :], (2, 0, 1)).reshape(rows, heads))
            d_a = (dar[:, 0, :rpg] + dac[:, :rpg, 0]).reshape(heads)
            g_alog[j] = d_a * s["a_neg"]
            g_dskip[j] = ddsk[:, 0, :rpg].reshape(heads)
            ddtr, dbias = dt_bwd(f"ssd_dt_bwd_{i}", s["zx"], bias_pad[j:j + 1], pad_h(ddt), zx_cols)
            g_dtb[j] = dbias[0, :heads]
            dxbc_raw, g_convw[j], dconvb = conv_bwd(
                f"ssd_conv_bwd_{i}", s["zx"], convw_f[j], ssd_conv_b[j], jnp.concatenate([dx, db, dc], axis=1), d_inner, conv_dim)
            g_convb[j] = dconvb[0]
            dzx = jnp.concatenate([dz, dxbc_raw, ddtr], axis=1)
            g_in = matmul(f"ssd_in_bwd_w_{i}", s["u_t"], dzx, out_dtype=BF16, **MM_ROWS_RESIDENT,
                          tn=384 if dzx.shape[1] % 384 == 0 else 512)
            tok = scatter(f"scatter_start_in_{i}", [("in", _block_cols(g_in[:, :d_in_proj]))], j, g_in)
            du = matmul(f"ssd_in_bwd_x_{i}", dzx, s["w_in"], tb=True, **{**MM_DEEP, "tk": 1152 if dzx.shape[1] % 1152 == 0 else 512})
        else:
            dmixed, g_poolw, g_poolb[j], g_pools[j] = pool_proj_bwd(
                f"pool_proj_bwd_{i}", dmix, s["pre"], s["mixed"], s["w_pool"], pools_f[j])
            ng, gd = g_poolw.shape[0], g_poolw.shape[1]
            blk_pool = jnp.moveaxis(g_poolw.astype(BF16).reshape(ng, N_DEV, gd // N_DEV, gd), 1, 0)
            tok = scatter(f"scatter_start_pool_{i}", [("pool", blk_pool)], j, blk_pool)
            du = pool_sub(f"pool_sub_bwd_{i}", dmixed, True)
        dh, g_norm[i][0] = rmsnorm_bwd(f"norm_pre_mix_bwd_{i}", s["h"], norm_f[i, 0], du, add=dh1, after=tok)

    grad_x = dh[PAD_FRONT + N_META:][None]
    g_meta = dh[PAD_FRONT:PAD_FRONT + N_META]

    small_grads = [g_meta, jnp.stack([jnp.stack(r) for r in g_norm]), jnp.stack(g_convw), jnp.stack(g_convb), jnp.stack(g_dtb),
                   jnp.stack(g_alog), jnp.stack(g_dskip), jnp.stack(g_ssdnorm), jnp.stack(g_poolb), jnp.stack(g_pools)]
    packed_g = _pack(small_grads)
    (small_rec,), small_tok = exchange_start(
        "gather_small_grads_start", [[(packed_g, lax.empty((N_DEV, *packed_g.shape), F32), _whole, _slot)]], packed_g)
    mine = lambda a: _my_cols(a, a.shape[-1] // N_DEV)
    small_w = [meta_tokens, norm_w, ssd_conv_w, ssd_conv_b, ssd_dt_bias, ssd_a_log, ssd_d, ssd_norm_w, pool_b, pool_scale]
    small_m = [m_meta_tokens, m_norm_w, m_ssd_conv_w, m_ssd_conv_b, m_ssd_dt_bias, m_ssd_a_log, m_ssd_d, m_ssd_norm_w, m_pool_b, m_pool_scale]
    small_v = [v_meta_tokens, v_norm_w, v_ssd_conv_w, v_ssd_conv_b, v_ssd_dt_bias, v_ssd_a_log, v_ssd_d, v_ssd_norm_w, v_pool_b, v_pool_scale]
    sharded = [True, True, True, False, False, False, False, False, True, True]
    def widen(a, is_sharded, full):
        if not is_sharded:
            return a
        return lax.dynamic_update_slice_in_dim(jnp.zeros(full.shape, F32), a, _linear(_my_place()) * a.shape[-1], axis=a.ndim - 1)
    packed_w = _pack([widen(a, sh, g) for a, sh, g in zip(small_w, sharded, small_grads)])
    packed_m = _pack([widen(a, sh, g) for a, sh, g in zip(small_m, sharded, small_grads)])
    packed_v = _pack([widen(a, sh, g) for a, sh, g in zip(small_v, sharded, small_grads)])

    def landed(name, keys, after):
        rs = [r for r, ks in enumerate(scatter_keys) if set(ks) <= set(keys)]
        srcs, out = exchange_wait(name, [scatter_recs[r] for r in rs], [lands[k] for k in keys],
                                  [[keys.index(k) for k in scatter_keys[r]] for r in rs], [scatter_views[r] for r in rs], after)
        me = _linear(_my_place())
        src_of = iter(srcs)
        for r in rs:
            for key in scatter_keys[r]:
                own = lax.dynamic_index_in_dim(next(src_of), me, axis=0, keepdims=True)[:, None]
                slot = keys.index(key)
                start = (me, scatter_layers[r]) + (0,) * (own.ndim - 2)
                out[slot] = lax.dynamic_update_slice(out[slot], own, start)
        return out

    p_gu, p_down = landed("scatter_wait_ffn", ["gate_up", "down"], small_tok)
    a_gate = sum_adamw("adamw_ffn_w_gate", p_gu, ffn_w_gate, m_ffn_w_gate, v_ffn_w_gate, half=0, column_major=True)
    a_up = sum_adamw("adamw_ffn_w_up", p_gu, ffn_w_up, m_ffn_w_up, v_ffn_w_up, half=1, column_major=True)
    a_down = sum_adamw("adamw_ffn_w_down", p_down, ffn_w_down, m_ffn_w_down, v_ffn_w_down)
    (p_pool,) = landed("scatter_wait_pool", ["pool"], a_gate[0])
    a_pool = sum_adamw("adamw_pool_w", p_pool, pool_w, m_pool_w, v_pool_w)
    (p_out,) = landed("scatter_wait_out", ["out"], a_up[0])
    a_out = sum_adamw("adamw_ssd_w_out", p_out, ssd_w_out, m_ssd_w_out, v_ssd_w_out)
    (p_in,) = landed("scatter_wait_in", ["in"], a_down[0])
    a_in = sum_adamw("adamw_ssd_w_in", p_in, ssd_w_in, m_ssd_w_in, v_ssd_w_in, column_major=True)
    bg = [a_in, a_out, a_pool, a_gate, a_up, a_down]

    small_parts = gathered_small("gather_small_grads_wait", small_rec, a_in[0])
    sm = [_unpack(o, small_grads) for o in sum_adamw("adamw_small", small_parts, packed_w, packed_m, packed_v)]
    sm = [[mine(a) if sh else a for a, sh in zip(group, sharded)] for group in sm]

    def ordered(kind):
        s_ = sm[kind]
        b_ = [o[kind] for o in bg]
        return [s_[0], s_[1], b_[0], s_[2], s_[3], s_[4], s_[5], s_[6], s_[7], b_[1], b_[2], s_[8], s_[9], b_[3], b_[4], b_[5]]

    return (loss, grad_x, *ordered(0), *ordered(1), *ordered(2), *ordered(3))
```

```python
import functools

import jax
import jax.numpy as jnp
from jax import lax
from jax.experimental import pallas as pl
from jax.experimental.pallas import tpu as pltpu

F32 = jnp.float32
BF16 = jnp.bfloat16
MESH = pl.DeviceIdType.MESH
AXES = ("x", "y", "c")
N_DEV = 8

N_META = 16
EPS = 1e-6
HEAD_DIM = 64
D_STATE = 128
SSD_GROUPS = 8
D_CONV = 4
CHUNK = 256
POOL_WINDOWS = (2, 4, 8, 16)
ADAM_LR, ADAM_B1, ADAM_B2, ADAM_EPS, ADAM_WD, ADAM_STEP = 0.001, 0.9, 0.999, 1e-08, 0.01, 10

PAD_FRONT = (-N_META) % CHUNK
LANES = 128
SUBLANES = 8
ROW_TILE = 256
CONV_TILE = 256
POOL_TILE = 128
VMEM_LIMIT = 56 * 1024 * 1024


def _params(sem=None):
    return pltpu.CompilerParams(dimension_semantics=sem, vmem_limit_bytes=VMEM_LIMIT)


def _tile(n, target, mult):
    if n <= target:
        return n
    best = None
    for t in range(mult, target + 1, mult):
        if n % t == 0:
            best = t
    assert best is not None, (n, target, mult)
    return best


def _dot(a, b, ca, cb):
    return lax.dot_general(a, b, (((ca,), (cb,)), ((), ())), preferred_element_type=F32)


def _sigmoid(x):
    return 1.0 / (1.0 + jnp.exp(-x))


def _row_mask(shape, first_row):
    rows = lax.broadcasted_iota(jnp.int32, shape, 0) + first_row
    return rows >= PAD_FRONT


MM_ROWS_RESIDENT = dict(tm=2304, tk=2304)
MM_COLS_RESIDENT = dict(tm=512, tn=2304, tk=2304)
MM_DEEP = dict(tm=1152, tn=1024, tk=512)
GATHER_AHEAD = 2

def matmul(name, a, b, *, tb=False, out_dtype=F32, tm=768, tn=512, tk=2048):
    m, kdim = a.shape
    if tb:
        n, k2 = b.shape
    else:
        k2, n = b.shape
    assert kdim == k2, (a.shape, b.shape, tb)
    tm = _tile(m, tm, 16)
    tn = _tile(n, tn, LANES)
    tk = _tile(kdim, tk, LANES)
    nk = kdim // tk
    a_spec = pl.BlockSpec((tm, tk), lambda i, j, k: (i, k))
    b_spec = pl.BlockSpec((tn, tk), lambda i, j, k: (j, k)) if tb else pl.BlockSpec((tk, tn), lambda i, j, k: (k, j))

    def body_single(a_ref, b_ref, o_ref):
        o_ref[...] = _dot(a_ref[...], b_ref[...], 1, 1 if tb else 0).astype(o_ref.dtype)

    def body_acc(a_ref, b_ref, o_ref, acc_ref):
        k = pl.program_id(2)

        @pl.when(k == 0)
        def _():
            acc_ref[...] = jnp.zeros_like(acc_ref)

        acc_ref[...] += _dot(a_ref[...], b_ref[...], 1, 1 if tb else 0)

        @pl.when(k == nk - 1)
        def _():
            o_ref[...] = acc_ref[...].astype(o_ref.dtype)

    return pl.pallas_call(
        body_single if nk == 1 else body_acc, name=name, grid=(m // tm, n // tn, nk),
        in_specs=[a_spec, b_spec], out_specs=pl.BlockSpec((tm, tn), lambda i, j, k: (i, j)),
        out_shape=jax.ShapeDtypeStruct((m, n), out_dtype),
        scratch_shapes=[] if nk == 1 else [pltpu.VMEM((tm, tn), F32)],
        compiler_params=_params(("parallel", "parallel", "arbitrary")),
    )(a, b)


_TOKEN_SPEC = pl.BlockSpec((SUBLANES, LANES), lambda i: (0, 0))


def rmsnorm_fwd(name, x, w, res=None, out_dtype=F32, transposed=False, after=None):
    rows, d = x.shape
    tr = ROW_TILE
    row_spec = pl.BlockSpec((tr, d), lambda i: (i, 0))
    w_spec = pl.BlockSpec((1, d), lambda i: (0, 0))
    n_in = 2 + (res is not None) + (after is not None)

    def body(*refs):
        x_ref, w_ref = refs[:2]
        outs = refs[n_in:]
        xv = x_ref[...]
        y = xv * lax.rsqrt(jnp.mean(xv * xv, axis=-1, keepdims=True) + EPS) * w_ref[...]
        if res is not None:
            y = refs[2][...] + y
        outs[0][...] = y.astype(out_dtype)
        if transposed:
            outs[1][...] = y.T.astype(out_dtype)

    args = [x, w.reshape(1, d)] + ([] if res is None else [res]) + ([] if after is None else [after])
    specs = [row_spec, w_spec] + ([] if res is None else [row_spec]) + ([] if after is None else [_TOKEN_SPEC])
    out_specs, out_shape = [row_spec], [jax.ShapeDtypeStruct((rows, d), out_dtype)]
    if transposed:
        out_specs.append(pl.BlockSpec((d, tr), lambda i: (0, i)))
        out_shape.append(jax.ShapeDtypeStruct((d, rows), out_dtype))
    outs = pl.pallas_call(
        body, name=name, grid=(rows // tr,), in_specs=specs, out_specs=out_specs, out_shape=out_shape,
        compiler_params=_params(("parallel",)),
    )(*args)
    return outs if transposed else outs[0]


def rmsnorm_bwd(name, x, w, dy, add=None, after=None):
    rows, d = x.shape
    tr = ROW_TILE
    row_spec = pl.BlockSpec((tr, d), lambda i: (i, 0))
    w_spec = pl.BlockSpec((1, d), lambda i: (0, 0))

    def body(*refs):
        x_ref, w_ref, dy_ref = refs[:3]
        add_ref = None if add is None else refs[3]
        dx_ref, dw_ref = refs[-2:]
        xv = x_ref[...]
        dyv = dy_ref[...].astype(F32)
        r = lax.rsqrt(jnp.mean(xv * xv, axis=-1, keepdims=True) + EPS)
        xh = xv * r
        dxh = dyv * w_ref[...]
        dx = r * (dxh - xh * jnp.mean(dxh * xh, axis=-1, keepdims=True))
        if add is not None:
            dx = dx + add_ref[...]
        dx_ref[...] = dx

        @pl.when(pl.program_id(0) == 0)
        def _():
            dw_ref[...] = jnp.zeros_like(dw_ref)

        dw_ref[...] += jnp.sum(dyv * xh, axis=0, keepdims=True)

    args = [x, w.reshape(1, d), dy] + ([] if add is None else [add]) + ([] if after is None else [after])
    specs = [row_spec, w_spec, row_spec] + ([] if add is None else [row_spec]) + ([] if after is None else [_TOKEN_SPEC])
    dx, dw = pl.pallas_call(
        body, name=name, grid=(rows // tr,), in_specs=specs, out_specs=[row_spec, w_spec],
        out_shape=[jax.ShapeDtypeStruct((rows, d), F32), jax.ShapeDtypeStruct((1, d), F32)],
        compiler_params=_params(("arbitrary",)),
    )(*args)
    return dx, dw.reshape(d)


def loss_head(name, h, target):
    rows, d = h.shape
    tr = ROW_TILE
    first = (PAD_FRONT + N_META) // tr
    assert (PAD_FRONT + N_META) % tr == 0 and target.shape[0] == rows - first * tr

    def body(h_ref, t_ref, dh_ref, loss_ref):
        i = pl.program_id(0)

        @pl.when(i == 0)
        def _():
            loss_ref[...] = jnp.zeros_like(loss_ref)

        keep = (i >= first).astype(F32)
        diff = (h_ref[...] - t_ref[...]) * keep
        dh_ref[...] = diff / d
        loss_ref[...] += 0.5 * jnp.sum(diff * diff) / d

    dh, loss = pl.pallas_call(
        body, name=name, grid=(rows // tr,),
        in_specs=[pl.BlockSpec((tr, d), lambda i: (i, 0)), pl.BlockSpec((tr, d), lambda i: (jnp.maximum(i - first, 0), 0))],
        out_specs=[pl.BlockSpec((tr, d), lambda i: (i, 0)), pl.BlockSpec((SUBLANES, LANES), lambda i: (0, 0))],
        out_shape=[jax.ShapeDtypeStruct((rows, d), F32), jax.ShapeDtypeStruct((SUBLANES, LANES), F32)],
        compiler_params=_params(("arbitrary",)),
    )(h, target)
    return dh, loss[0, 0]


def _mm_call(name, a, b, grid, a_spec, b_spec, o_spec, out_sds, ca, cb, red_axis=None, pairs=None):
    n_red = None if red_axis is None else grid[red_axis]

    def body_single(a_ref, b_ref, o_ref):
        o_ref[...] = _dot(a_ref[...], b_ref[...], ca, cb).astype(o_ref.dtype)

    def step_product(a_ref, b_ref):
        if pairs is None:
            return _dot(a_ref[...], b_ref[...], ca, cb)
        return functools.reduce(lambda s, t: s + t, [_dot(a_ref[p], b_ref[p], ca, cb) for p in range(pairs)])

    def body_acc(a_ref, b_ref, o_ref, acc_ref):
        k = pl.program_id(red_axis)

        @pl.when(k == 0)
        def _():
            acc_ref[...] = jnp.zeros_like(acc_ref)

        acc_ref[...] += step_product(a_ref, b_ref)

        @pl.when(k == n_red - 1)
        def _():
            o_ref[...] = acc_ref[...].astype(o_ref.dtype)

    acc_shape = tuple(s for s in o_spec.block_shape if s is not None)
    sem = tuple("arbitrary" if ax == red_axis else "parallel" for ax in range(len(grid)))
    return pl.pallas_call(
        body_single if red_axis is None else body_acc, name=name, grid=grid, in_specs=[a_spec, b_spec], out_specs=o_spec,
        out_shape=out_sds, scratch_shapes=[] if red_axis is None else [pltpu.VMEM(acc_shape, F32)],
        compiler_params=_params(sem),
    )(a, b)


def ffn_in_fwd(name, u2, w_gu):
    rows, d = u2.shape
    hs = w_gu.shape[-1]
    return _mm_call(name, u2, w_gu, (N_DEV, 2), pl.BlockSpec((rows, d), lambda k, t: (0, 0)),
                    pl.BlockSpec((None, None, d, hs), lambda k, t: (k, t, 0, 0)),
                    pl.BlockSpec((None, None, rows, hs), lambda k, t: (k, t, 0, 0)),
                    jax.ShapeDtypeStruct((N_DEV, 2, rows, hs), F32), 1, 0)


def ffn_out_fwd(name, act, w_down):
    _, rows, hs = act.shape
    d = w_down.shape[-1]
    tm, tn = _tile(rows, 1152, 16), _tile(d, 1024, LANES)
    half = N_DEV // 2
    return _mm_call(name, act.reshape(half, 2, rows, hs), w_down.reshape(half, 2, hs, d), (rows // tm, d // tn, half),
                    pl.BlockSpec((None, 2, tm, hs), lambda i, j, k: (k, 0, i, 0)),
                    pl.BlockSpec((None, 2, hs, tn), lambda i, j, k: (k, 0, 0, j)), pl.BlockSpec((tm, tn), lambda i, j, k: (i, j)),
                    jax.ShapeDtypeStruct((rows, d), F32), 1, 0, red_axis=2, pairs=2)


def ffn_out_bwd_x(name, df, w_down):
    rows, d = df.shape
    hs = w_down.shape[1]
    return _mm_call(name, df, w_down, (N_DEV,), pl.BlockSpec((rows, d), lambda k: (0, 0)),
                    pl.BlockSpec((None, hs, d), lambda k: (k, 0, 0)), pl.BlockSpec((None, rows, hs), lambda k: (k, 0, 0)),
                    jax.ShapeDtypeStruct((N_DEV, rows, hs), F32), 1, 1)


def ffn_out_bwd_w(name, act_t, df):
    _, hs, rows = act_t.shape
    d = df.shape[1]
    return _mm_call(name, act_t, df, (N_DEV,), pl.BlockSpec((None, hs, rows), lambda k: (k, 0, 0)),
                    pl.BlockSpec((rows, d), lambda k: (0, 0)), pl.BlockSpec((None, hs, d), lambda k: (k, 0, 0)),
                    jax.ShapeDtypeStruct((N_DEV, hs, d), BF16), 1, 0)


def ffn_in_bwd_w(name, u2_t, dgp):
    d, rows = u2_t.shape
    hs = dgp.shape[-1]
    return _mm_call(name, u2_t, dgp, (N_DEV, 2), pl.BlockSpec((d, rows), lambda k, t: (0, 0)),
                    pl.BlockSpec((None, None, rows, hs), lambda k, t: (k, t, 0, 0)),
                    pl.BlockSpec((None, None, d, hs), lambda k, t: (k, t, 0, 0)),
                    jax.ShapeDtypeStruct((N_DEV, 2, d, hs), BF16), 1, 0)


def ffn_in_bwd_x(name, dgp, w_gu):
    _, _, rows, hs = dgp.shape
    d = w_gu.shape[2]
    tm, tn = _tile(rows, 1152, 16), _tile(d, 1024, LANES)
    return _mm_call(name, dgp, w_gu, (rows // tm, d // tn, N_DEV),
                    pl.BlockSpec((None, 2, tm, hs), lambda i, j, k: (k, 0, i, 0)),
                    pl.BlockSpec((None, 2, tn, hs), lambda i, j, k: (k, 0, j, 0)),
                    pl.BlockSpec((tm, tn), lambda i, j, k: (i, j)), jax.ShapeDtypeStruct((rows, d), F32), 1, 1, red_axis=2, pairs=2)


def swiglu_fwd(name, gp):
    _, _, rows, hs = gp.shape
    tr = _tile(rows, 768, ROW_TILE)

    def body(gp_ref, a_ref, at_ref):
        g = gp_ref[0]
        act = g * _sigmoid(g) * gp_ref[1]
        a_ref[...] = act.astype(a_ref.dtype)
        at_ref[...] = act.T.astype(at_ref.dtype)

    return pl.pallas_call(
        body, name=name, grid=(N_DEV, rows // tr),
        in_specs=[pl.BlockSpec((None, 2, tr, hs), lambda k, i: (k, 0, i, 0))],
        out_specs=[pl.BlockSpec((None, tr, hs), lambda k, i: (k, i, 0)), pl.BlockSpec((None, hs, tr), lambda k, i: (k, 0, i))],
        out_shape=[jax.ShapeDtypeStruct((N_DEV, rows, hs), BF16), jax.ShapeDtypeStruct((N_DEV, hs, rows), BF16)],
        compiler_params=_params(("parallel", "parallel")),
    )(gp)


def swiglu_bwd(name, gp, dact):
    _, _, rows, hs = gp.shape
    tr = _tile(rows, 768, ROW_TILE)

    def body(gp_ref, da_ref, o_ref):
        g = gp_ref[0]
        s = _sigmoid(g)
        dav = da_ref[...]
        o_ref[0] = (dav * gp_ref[1] * (s * (1.0 + g * (1.0 - s)))).astype(o_ref.dtype)
        o_ref[1] = (dav * (g * s)).astype(o_ref.dtype)

    blk = pl.BlockSpec((None, 2, tr, hs), lambda k, i: (k, 0, i, 0))
    return pl.pallas_call(
        body, name=name, grid=(N_DEV, rows // tr),
        in_specs=[blk, pl.BlockSpec((None, tr, hs), lambda k, i: (k, i, 0))], out_specs=blk,
        out_shape=jax.ShapeDtypeStruct(gp.shape, BF16), compiler_params=_params(("parallel", "parallel")),
    )(gp, dact)


def conv_fwd(name, zx, conv_w, conv_b, d_inner, conv_dim):
    rows = zx.shape[0]
    tc = CONV_TILE
    off = d_inner // tc
    assert d_inner % tc == 0 and conv_dim % tc == 0

    def body(u_ref, w_ref, b_ref, o_ref):
        u = u_ref[...]
        acc = u * w_ref[D_CONV - 1:D_CONV, :] + b_ref[...]
        for s in range(1, D_CONV):
            acc = acc + pltpu.roll(u, s, axis=0) * w_ref[D_CONV - 1 - s:D_CONV - s, :]
        y = acc * _sigmoid(acc)
        o_ref[...] = jnp.where(_row_mask(y.shape, 0), y, 0.0)

    return pl.pallas_call(
        body, name=name, grid=(conv_dim // tc,),
        in_specs=[pl.BlockSpec((rows, tc), lambda j: (0, off + j)), pl.BlockSpec((D_CONV, tc), lambda j: (0, j)),
                  pl.BlockSpec((1, tc), lambda j: (0, j))],
        out_specs=pl.BlockSpec((rows, tc), lambda j: (0, j)),
        out_shape=jax.ShapeDtypeStruct((rows, conv_dim), F32), compiler_params=_params(("parallel",)),
    )(zx, conv_w, conv_b.reshape(1, conv_dim))


def conv_bwd(name, zx, conv_w, conv_b, dxbc, d_inner, conv_dim):
    rows = zx.shape[0]
    tc = CONV_TILE
    off = d_inner // tc

    def body(u_ref, w_ref, b_ref, dy_ref, du_ref, dw_ref, db_ref):
        u = u_ref[...]
        wk = [w_ref[D_CONV - 1 - s:D_CONV - s, :] for s in range(D_CONV)]
        shifted = [u] + [pltpu.roll(u, s, axis=0) for s in range(1, D_CONV)]
        acc = u * wk[0] + b_ref[...]
        for s in range(1, D_CONV):
            acc = acc + shifted[s] * wk[s]
        sg = _sigmoid(acc)
        mask = _row_mask(acc.shape, 0)
        dpre = jnp.where(mask, dy_ref[...] * (sg * (1.0 + acc * (1.0 - sg))), 0.0)
        db_ref[...] = jnp.sum(dpre, axis=0, keepdims=True)
        du = dpre * wk[0]
        dw_ref[D_CONV - 1:D_CONV, :] = jnp.sum(dpre * u, axis=0, keepdims=True)
        for s in range(1, D_CONV):
            du = du + pltpu.roll(dpre, rows - s, axis=0) * wk[s]
            dw_ref[D_CONV - 1 - s:D_CONV - s, :] = jnp.sum(dpre * shifted[s], axis=0, keepdims=True)
        du_ref[...] = jnp.where(mask, du, 0.0).astype(du_ref.dtype)

    return pl.pallas_call(
        body, name=name, grid=(conv_dim // tc,),
        in_specs=[pl.BlockSpec((rows, tc), lambda j: (0, off + j)), pl.BlockSpec((D_CONV, tc), lambda j: (0, j)),
                  pl.BlockSpec((1, tc), lambda j: (0, j)), pl.BlockSpec((rows, tc), lambda j: (0, j))],
        out_specs=[pl.BlockSpec((rows, tc), lambda j: (0, j)), pl.BlockSpec((D_CONV, tc), lambda j: (0, j)),
                   pl.BlockSpec((1, tc), lambda j: (0, j))],
        out_shape=[jax.ShapeDtypeStruct((rows, conv_dim), BF16), jax.ShapeDtypeStruct((D_CONV, conv_dim), F32),
                   jax.ShapeDtypeStruct((1, conv_dim), F32)],
        compiler_params=_params(("parallel",)),
    )(zx, conv_w, conv_b.reshape(1, conv_dim), dxbc)


def dt_fwd(name, zx, bias_pad, zx_cols):
    rows = zx.shape[0]
    tr = ROW_TILE
    off = zx_cols // LANES

    def body(r_ref, b_ref, o_ref):
        v = r_ref[...] + b_ref[...]
        sp = jnp.maximum(v, 0.0) + jnp.log1p(jnp.exp(-jnp.abs(v)))
        o_ref[...] = jnp.where(_row_mask(v.shape, pl.program_id(0) * tr), sp, 0.0)

    return pl.pallas_call(
        body, name=name, grid=(rows // tr,),
        in_specs=[pl.BlockSpec((tr, LANES), lambda i: (i, off)), pl.BlockSpec((1, LANES), lambda i: (0, 0))],
        out_specs=pl.BlockSpec((tr, LANES), lambda i: (i, 0)),
        out_shape=jax.ShapeDtypeStruct((rows, LANES), F32), compiler_params=_params(("parallel",)),
    )(zx, bias_pad)


def dt_bwd(name, zx, bias_pad, ddt, zx_cols):
    rows = zx.shape[0]
    tr = ROW_TILE
    off = zx_cols // LANES

    def body(r_ref, b_ref, d_ref, o_ref, db_ref):
        v = r_ref[...] + b_ref[...]
        g = jnp.where(_row_mask(v.shape, pl.program_id(0) * tr), d_ref[...] * _sigmoid(v), 0.0)
        o_ref[...] = g.astype(o_ref.dtype)

        @pl.when(pl.program_id(0) == 0)
        def _():
            db_ref[...] = jnp.zeros_like(db_ref)

        db_ref[...] += jnp.sum(g, axis=0, keepdims=True)

    return pl.pallas_call(
        body, name=name, grid=(rows // tr,),
        in_specs=[pl.BlockSpec((tr, LANES), lambda i: (i, off)), pl.BlockSpec((1, LANES), lambda i: (0, 0)),
                  pl.BlockSpec((tr, LANES), lambda i: (i, 0))],
        out_specs=[pl.BlockSpec((tr, LANES), lambda i: (i, 0)), pl.BlockSpec((1, LANES), lambda i: (0, 0))],
        out_shape=[jax.ShapeDtypeStruct((rows, LANES), BF16), jax.ShapeDtypeStruct((1, LANES), F32)],
        compiler_params=_params(("arbitrary",)),
    )(zx, bias_pad, ddt)


def _split3(x):
    h1 = x.astype(BF16)
    r1 = x - h1.astype(F32)
    h2 = r1.astype(BF16)
    h3 = (r1 - h2.astype(F32)).astype(BF16)
    return h1, h2, h3


def _exact_left(ones_b, x):
    h1, h2, h3 = _split3(x)
    return _dot(ones_b, h1, 1, 0) + _dot(ones_b, h2, 1, 0) + _dot(ones_b, h3, 1, 0)


def _exact_right_t(x, ones_b):
    h1, h2, h3 = _split3(x)
    return _dot(h1, ones_b, 1, 1) + _dot(h2, ones_b, 1, 1) + _dot(h3, ones_b, 1, 1)


class _ScanCommon:
    def __init__(self, b_ref, c_ref, dtc_ref, dtr_ref, arow_ref, acol_ref, drow_ref):
        q = CHUNK
        self.bb = b_ref[...].astype(BF16)
        self.cb = c_ref[...].astype(BF16)
        ri = lax.broadcasted_iota(jnp.int32, (q, q), 0)
        cj = lax.broadcasted_iota(jnp.int32, (q, q), 1)
        self.lower = ri >= cj
        self.upper = cj >= ri
        self.dtc = dtc_ref[...]
        self.dtr = dtr_ref[...]
        self.arow = arow_ref[...]
        self.acol = acol_ref[...]
        self.drow = drow_ref[...]
        da_col = self.dtc * self.arow
        self.a_col = _exact_left(self.lower.astype(BF16), da_col)
        self.a_row = _exact_right_t(self.dtr * self.acol, self.lower.astype(BF16))
        self.a_last = jnp.sum(da_col, axis=0, keepdims=True)
        self.lane_q = lax.broadcasted_iota(jnp.int32, (q, LANES), 1)
        self.lane_1 = lax.broadcasted_iota(jnp.int32, (1, LANES), 1)
        self.sub_8 = lax.broadcasted_iota(jnp.int32, (SUBLANES, q), 0)
        self.first_half = self.lane_q < HEAD_DIM
        self.first_rows = lax.broadcasted_iota(jnp.int32, (LANES, 1), 0) < HEAD_DIM

    def col(self, v, r):
        return jnp.sum(jnp.where(self.lane_q == r, v, 0.0), axis=1, keepdims=True)

    def row(self, v, r):
        return jnp.sum(jnp.where(self.sub_8 == r, v, 0.0), axis=0, keepdims=True)

    def scalar(self, v, r):
        return jnp.sum(jnp.where(self.lane_1 == r, v, 0.0), axis=1, keepdims=True)

    def pair(self, v0, v1):
        return jnp.where(self.first_half, v0, v1)

    def half_rowsum(self, t, h):
        keep = self.first_half if h == 0 else jnp.logical_not(self.first_half)
        return jnp.sum(jnp.where(keep, t, 0.0), axis=1, keepdims=True)


def ssd_scan_fwd(name, xbc, dt_col, dt_row, a_row, a_col, d_row, d_inner):
    rows = xbc.shape[0]
    q, n, g_cnt = CHUNK, D_STATE, SSD_GROUPS
    nc = rows // q
    rp = d_inner // g_cnt
    n_pairs = rp // LANES
    b_off = d_inner // n

    def body(x_ref, b_ref, c_ref, dtc_ref, dtr_ref, arow_ref, acol_ref, drow_ref, y_ref, sprev_ref, s_ref):
        @pl.when(pl.program_id(1) == 0)
        def _():
            s_ref[...] = jnp.zeros_like(s_ref)

        sprev_ref[...] = s_ref[...]
        k = _ScanCommon(b_ref, c_ref, dtc_ref, dtr_ref, arow_ref, acol_ref, drow_ref)
        cb_mat = _dot(k.cb, k.bb, 1, 1)
        for pr in range(n_pairs):
            sl = slice(pr * LANES, (pr + 1) * LANES)
            heads = (2 * pr, 2 * pr + 1)
            xp = x_ref[:, sl]
            ac = [k.col(k.a_col, r) for r in heads]
            ar = [k.row(k.a_row, r) for r in heads]
            al = [k.scalar(k.a_last, r) for r in heads]
            xd = xp * k.pair(k.col(k.dtc, heads[0]), k.col(k.dtc, heads[1]))
            xdb = xd.astype(BF16)
            ys = []
            for h in range(2):
                lm = jnp.exp(jnp.where(k.lower, ac[h] - ar[h], -jnp.inf))
                ys.append(_dot((cb_mat * lm).astype(BF16), xdb, 1, 0))
            y = jnp.where(k.first_half, ys[0], ys[1])
            sp = s_ref[sl, :]
            y = y + k.pair(jnp.exp(ac[0]), jnp.exp(ac[1])) * _dot(k.cb, sp.astype(BF16), 1, 1)
            y = y + k.pair(k.scalar(k.drow, heads[0]), k.scalar(k.drow, heads[1])) * xp
            y_ref[:, sl] = y
            wb = (xd * k.pair(jnp.exp(al[0] - ac[0]), jnp.exp(al[1] - ac[1]))).astype(BF16)
            decay = jnp.where(k.first_rows, jnp.exp(al[0]), jnp.exp(al[1]))
            s_ref[sl, :] = decay * sp + _dot(wb, k.bb, 0, 0)

    return pl.pallas_call(
        body, name=name, grid=(g_cnt, nc),
        in_specs=[
            pl.BlockSpec((q, rp), lambda g, c: (c, g)),
            pl.BlockSpec((q, n), lambda g, c: (c, b_off + g)),
            pl.BlockSpec((q, n), lambda g, c: (c, b_off + g_cnt + g)),
            pl.BlockSpec((None, q, LANES), lambda g, c: (g, c, 0)),
            pl.BlockSpec((None, SUBLANES, q), lambda g, c: (g, 0, c)),
            pl.BlockSpec((None, 1, LANES), lambda g, c: (g, 0, 0)),
            pl.BlockSpec((None, SUBLANES, 1), lambda g, c: (g, 0, 0)),
            pl.BlockSpec((None, 1, LANES), lambda g, c: (g, 0, 0)),
        ],
        out_specs=[pl.BlockSpec((q, rp), lambda g, c: (c, g)),
                   pl.BlockSpec((None, None, rp, n), lambda g, c: (c, g, 0, 0))],
        out_shape=[jax.ShapeDtypeStruct((rows, d_inner), F32), jax.ShapeDtypeStruct((nc, g_cnt, rp, n), F32)],
        scratch_shapes=[pltpu.VMEM((rp, n), F32)],
        compiler_params=_params(("parallel", "arbitrary")),
    )(xbc, xbc, xbc, dt_col, dt_row, a_row, a_col, d_row)


def ssd_scan_bwd(name, xbc, dt_col, dt_row, a_row, a_col, d_row, sprev, dy, d_inner):
    rows = xbc.shape[0]
    q, n, g_cnt = CHUNK, D_STATE, SSD_GROUPS
    nc = rows // q
    rp = d_inner // g_cnt
    n_pairs = rp // LANES
    b_off = d_inner // n

    def body(x_ref, b_ref, c_ref, dtc_ref, dtr_ref, arow_ref, acol_ref, drow_ref, sprev_ref, dy_ref,
             dx_ref, db_ref, dc_ref, ddtc_ref, ddtr_ref, dar_ref, dac_ref, dd_ref, ds_ref):
        @pl.when(pl.program_id(1) == 0)
        def _():
            ds_ref[...] = jnp.zeros_like(ds_ref)
            dar_ref[...] = jnp.zeros_like(dar_ref)
            dac_ref[...] = jnp.zeros_like(dac_ref)
            dd_ref[...] = jnp.zeros_like(dd_ref)

        k = _ScanCommon(b_ref, c_ref, dtc_ref, dtr_ref, arow_ref, acol_ref, drow_ref)
        cb_mat = _dot(k.cb, k.bb, 1, 1)
        cbt_mat = _dot(k.bb, k.cb, 1, 1)
        d_cb = jnp.zeros((q, q), F32)
        d_b = jnp.zeros((q, n), F32)
        d_c = jnp.zeros((q, n), F32)
        da_col = jnp.zeros((q, LANES), F32)
        da_row = jnp.zeros((SUBLANES, q), F32)
        ddt_x = jnp.zeros((q, LANES), F32)
        d_alast = jnp.zeros((1, LANES), F32)
        d_dskip = jnp.zeros((1, LANES), F32)
        for pr in range(n_pairs):
            sl = slice(pr * LANES, (pr + 1) * LANES)
            heads = (2 * pr, 2 * pr + 1)
            xp = x_ref[:, sl]
            dyp = dy_ref[:, sl]
            dyb = dyp.astype(BF16)
            ac = [k.col(k.a_col, r) for r in heads]
            ar = [k.row(k.a_row, r) for r in heads]
            al = [k.scalar(k.a_last, r) for r in heads]
            dt_p = k.pair(k.col(k.dtc, heads[0]), k.col(k.dtc, heads[1]))
            xd = xp * dt_p
            xdb = xd.astype(BF16)
            sp = sprev_ref[sl, :]
            spb = sp.astype(BF16)
            dsp = ds_ref[sl, :]
            dspb = dsp.astype(BF16)
            dskip_p = k.pair(k.scalar(k.drow, heads[0]), k.scalar(k.drow, heads[1]))
            dxp = dskip_p * dyp
            dd_lane = jnp.sum(dyp * xp, axis=0, keepdims=True)
            e_p = k.pair(jnp.exp(ac[0]), jnp.exp(ac[1]))
            t_off = dyp * (e_p * _dot(k.cb, spb, 1, 1))
            dzb = (e_p * dyp).astype(BF16)
            d_c = d_c + _dot(dzb, spb, 1, 0)
            ds_in = _dot(dzb, k.cb, 0, 0)
            decay = jnp.where(k.first_rows, jnp.exp(al[0]), jnp.exp(al[1]))
            ds_in = ds_in + decay * dsp
            t_state = jnp.sum(dsp * sp, axis=1, keepdims=True) * decay
            dec_p = k.pair(jnp.exp(al[0] - ac[0]), jnp.exp(al[1] - ac[1]))
            dw = _dot(k.bb, dspb, 1, 1)
            d_b = d_b + _dot((xd * dec_p).astype(BF16), dspb, 1, 0)
            dxd = dw * dec_p
            t_dec = dw * xd * dec_p
            dxd_h = []
            for h in range(2):
                r = heads[h]
                keep = k.first_half if h == 0 else jnp.logical_not(k.first_half)
                lm = jnp.exp(jnp.where(k.lower, ac[h] - ar[h], -jnp.inf))
                m_mat = cb_mat * lm
                dm = _dot(jnp.where(keep, dyp, 0.0).astype(BF16), xdb, 1, 1)
                dseg = dm * m_mat
                d_cb = d_cb + dm * lm
                lmt = jnp.exp(jnp.where(k.upper, ar[h] - ac[h], -jnp.inf))
                dxd_h.append(_dot((cbt_mat * lmt).astype(BF16), dyb, 1, 0))
                tdec_h = k.half_rowsum(t_dec, h)
                da_h = k.half_rowsum(t_off, h) - tdec_h + jnp.sum(dseg, axis=1, keepdims=True)
                da_col = da_col + jnp.where(k.lane_q == r, da_h, 0.0)
                da_row = da_row - jnp.where(k.sub_8 == r, jnp.sum(dseg, axis=0, keepdims=True), 0.0)
                keep_rows = k.first_rows if h == 0 else jnp.logical_not(k.first_rows)
                dal_h = jnp.sum(tdec_h, axis=0, keepdims=True) + jnp.sum(jnp.where(keep_rows, t_state, 0.0), axis=0, keepdims=True)
                d_alast = d_alast + jnp.where(k.lane_1 == r, dal_h, 0.0)
                keep_1 = k.lane_1 < HEAD_DIM if h == 0 else k.lane_1 >= HEAD_DIM
                dd_h = jnp.sum(jnp.where(keep_1, dd_lane, 0.0), axis=1, keepdims=True)
                d_dskip = d_dskip + jnp.where(k.lane_1 == r, dd_h, 0.0)
            dxd = dxd + jnp.where(k.first_half, dxd_h[0], dxd_h[1])
            dx_ref[:, sl] = dxp + dt_p * dxd
            t_dt = dxd * xp
            for h in range(2):
                ddt_x = ddt_x + jnp.where(k.lane_q == heads[h], k.half_rowsum(t_dt, h), 0.0)
            ds_ref[sl, :] = ds_in
        d_cb_b = d_cb.astype(BF16)
        db_ref[...] = d_b + _dot(d_cb_b, k.cb, 0, 0)
        dc_ref[...] = d_c + _dot(d_cb_b, k.bb, 1, 0)
        rc_col = _exact_left(k.upper.astype(BF16), da_col) + d_alast
        rc_row = _exact_right_t(da_row, k.upper.astype(BF16))
        ddtc_ref[...] = ddt_x + k.arow * rc_col
        ddtr_ref[...] = k.acol * rc_row
        dar_ref[...] += jnp.sum(rc_col * k.dtc, axis=0, keepdims=True)
        dac_ref[...] += jnp.sum(rc_row * k.dtr, axis=1, keepdims=True)
        dd_ref[...] += d_dskip

    rc = lambda c: nc - 1 - c
    return pl.pallas_call(
        body, name=name, grid=(g_cnt, nc),
        in_specs=[
            pl.BlockSpec((q, rp), lambda g, c: (rc(c), g)),
            pl.BlockSpec((q, n), lambda g, c: (rc(c), b_off + g)),
            pl.BlockSpec((q, n), lambda g, c: (rc(c), b_off + g_cnt + g)),
            pl.BlockSpec((None, q, LANES), lambda g, c: (g, rc(c), 0)),
            pl.BlockSpec((None, SUBLANES, q), lambda g, c: (g, 0, rc(c))),
            pl.BlockSpec((None, 1, LANES), lambda g, c: (g, 0, 0)),
            pl.BlockSpec((None, SUBLANES, 1), lambda g, c: (g, 0, 0)),
            pl.BlockSpec((None, 1, LANES), lambda g, c: (g, 0, 0)),
            pl.BlockSpec((None, None, rp, n), lambda g, c: (rc(c), g, 0, 0)),
            pl.BlockSpec((q, rp), lambda g, c: (rc(c), g)),
        ],
        out_specs=[
            pl.BlockSpec((q, rp), lambda g, c: (rc(c), g)),
            pl.BlockSpec((q, n), lambda g, c: (rc(c), g)),
            pl.BlockSpec((q, n), lambda g, c: (rc(c), g)),
            pl.BlockSpec((None, q, LANES), lambda g, c: (g, rc(c), 0)),
            pl.BlockSpec((None, SUBLANES, q), lambda g, c: (g, 0, rc(c))),
            pl.BlockSpec((None, 1, LANES), lambda g, c: (g, 0, 0)),
            pl.BlockSpec((None, SUBLANES, 1), lambda g, c: (g, 0, 0)),
            pl.BlockSpec((None, 1, LANES), lambda g, c: (g, 0, 0)),
        ],
        out_shape=[
            jax.ShapeDtypeStruct((rows, d_inner), F32),
            jax.ShapeDtypeStruct((rows, g_cnt * n), F32),
            jax.ShapeDtypeStruct((rows, g_cnt * n), F32),
            jax.ShapeDtypeStruct((g_cnt, rows, LANES), F32),
            jax.ShapeDtypeStruct((g_cnt, SUBLANES, rows), F32),
            jax.ShapeDtypeStruct((g_cnt, 1, LANES), F32),
            jax.ShapeDtypeStruct((g_cnt, SUBLANES, 1), F32),
            jax.ShapeDtypeStruct((g_cnt, 1, LANES), F32),
        ],
        scratch_shapes=[pltpu.VMEM((rp, n), F32)],
        compiler_params=_params(("parallel", "arbitrary")),
    )(xbc, xbc, xbc, dt_col, dt_row, a_row, a_col, d_row, sprev, dy)


def gatenorm_fwd(name, y, zx, w, d_inner):
    rows = y.shape[0]
    tr, gw = ROW_TILE, d_inner // SSD_GROUPS

    def body(y_ref, z_ref, w_ref, o_ref, ot_ref):
        z = z_ref[...]
        v = y_ref[...] * (z * _sigmoid(z))
        out = v * lax.rsqrt(jnp.mean(v * v, axis=-1, keepdims=True) + EPS) * w_ref[...]
        o_ref[...] = out.astype(o_ref.dtype)
        ot_ref[...] = out.T.astype(ot_ref.dtype)

    blk = pl.BlockSpec((tr, gw), lambda i, j: (i, j))
    return pl.pallas_call(
        body, name=name, grid=(rows // tr, SSD_GROUPS),
        in_specs=[blk, blk, pl.BlockSpec((1, gw), lambda i, j: (0, j))],
        out_specs=[blk, pl.BlockSpec((gw, tr), lambda i, j: (j, i))],
        out_shape=[jax.ShapeDtypeStruct((rows, d_inner), BF16), jax.ShapeDtypeStruct((d_inner, rows), BF16)],
        compiler_params=_params(("parallel", "parallel")),
    )(y, zx, w.reshape(1, d_inner))


def gatenorm_bwd(name, y, zx, w, dyn, d_inner, after):
    rows = y.shape[0]
    tr, gw = ROW_TILE, d_inner // SSD_GROUPS

    def body(y_ref, z_ref, w_ref, dn_ref, after_ref, dy_ref, dz_ref, dw_ref):
        z = z_ref[...]
        yv = y_ref[...]
        s = _sigmoid(z)
        gate = z * s
        v = yv * gate
        r = lax.rsqrt(jnp.mean(v * v, axis=-1, keepdims=True) + EPS)
        vh = v * r
        dn = dn_ref[...]
        dvh = dn * w_ref[...]
        dv = r * (dvh - vh * jnp.mean(dvh * vh, axis=-1, keepdims=True))
        dy_ref[...] = dv * gate
        dz_ref[...] = (dv * yv * (s * (1.0 + z * (1.0 - s)))).astype(dz_ref.dtype)

        @pl.when(pl.program_id(1) == 0)
        def _():
            dw_ref[...] = jnp.zeros_like(dw_ref)

        dw_ref[...] += jnp.sum(dn * vh, axis=0, keepdims=True)

    blk = pl.BlockSpec((tr, gw), lambda j, i: (i, j))
    wblk = pl.BlockSpec((1, gw), lambda j, i: (0, j))
    dy, dz, dw = pl.pallas_call(
        body, name=name, grid=(SSD_GROUPS, rows // tr),
        in_specs=[blk, blk, wblk, blk, pl.BlockSpec((SUBLANES, LANES), lambda j, i: (0, 0))], out_specs=[blk, blk, wblk],
        out_shape=[jax.ShapeDtypeStruct((rows, d_inner), F32), jax.ShapeDtypeStruct((rows, d_inner), BF16),
                   jax.ShapeDtypeStruct((1, d_inner), F32)],
        compiler_params=_params(("parallel", "arbitrary")),
    )(y, zx, w.reshape(1, d_inner), dyn, after)
    return dy, dz, dw.reshape(d_inner)


def _pool_count(rows, g):
    t1 = lax.broadcasted_iota(jnp.int32, (rows, 1), 0) - (PAD_FRONT - 1)
    win = jnp.left_shift(jnp.int32(POOL_WINDOWS[0]), g)
    return jnp.clip(t1, 1, win).astype(F32)


def _pool_select(levels, g):
    out = levels[-1]
    for i in range(len(levels) - 2, -1, -1):
        out = jnp.where(g == i, levels[i], out)
    return out


def pool_sub(name, u, transpose):
    rows, d = u.shape
    gd = d // len(POOL_WINDOWS)
    assert all(w == POOL_WINDOWS[0] << i for i, w in enumerate(POOL_WINDOWS))

    def body(u_ref, o_ref):
        g = pl.program_id(0)
        v = u_ref[...].astype(F32)
        cnt = _pool_count(rows, g)
        mask = _row_mask(v.shape, 0)
        s = v / cnt if transpose else v
        levels = []
        for i in range(len(POOL_WINDOWS)):
            step = 1 << i
            s = s + pltpu.roll(s, (rows - step) if transpose else step, axis=0)
            levels.append(s)
        sel = _pool_select(levels, g)
        out = (sel - v) if transpose else (sel / cnt - v)
        o_ref[...] = jnp.where(mask, out, 0.0).astype(o_ref.dtype)

    tc = POOL_TILE
    per = gd // tc
    blk = pl.BlockSpec((rows, tc), lambda g, j: (0, g * per + j))
    return pl.pallas_call(
        body, name=name, grid=(len(POOL_WINDOWS), per), in_specs=[blk], out_specs=blk,
        out_shape=jax.ShapeDtypeStruct((rows, d), F32 if transpose else BF16),
        compiler_params=_params(("parallel", "parallel")),
    )(u)


def pool_proj_fwd(name, mixed, w, b, scale):
    rows, d = mixed.shape
    ng = len(POOL_WINDOWS)
    gd = d // ng
    tr = ROW_TILE

    def body(m_ref, w_ref, b_ref, s_ref, pre_ref, mix_ref):
        pre = _dot(m_ref[...], w_ref[...], 1, 0) + b_ref[...]
        pre_ref[...] = pre
        mix_ref[...] = jnp.where(_row_mask(pre.shape, pl.program_id(1) * tr), pre * s_ref[...], 0.0)

    blk = pl.BlockSpec((tr, gd), lambda g, i: (i, g))
    vec = pl.BlockSpec((1, gd), lambda g, i: (0, g))
    return pl.pallas_call(
        body, name=name, grid=(ng, rows // tr),
        in_specs=[blk, pl.BlockSpec((None, gd, gd), lambda g, i: (g, 0, 0)), vec, vec], out_specs=[blk, blk],
        out_shape=[jax.ShapeDtypeStruct((rows, d), F32), jax.ShapeDtypeStruct((rows, d), F32)],
        compiler_params=_params(("parallel", "parallel")),
    )(mixed, w, b.reshape(1, d), scale.reshape(1, d))


def pool_proj_bwd(name, dmix, pre, mixed, w, scale):
    rows, d = dmix.shape
    ng = len(POOL_WINDOWS)
    gd = d // ng
    tr = ROW_TILE

    def body(dm_ref, pre_ref, mx_ref, w_ref, s_ref, dmx_ref, dw_ref, db_ref, ds_ref):
        @pl.when(pl.program_id(1) == 0)
        def _():
            dw_ref[...] = jnp.zeros_like(dw_ref)
            db_ref[...] = jnp.zeros_like(db_ref)
            ds_ref[...] = jnp.zeros_like(ds_ref)

        dmv = jnp.where(_row_mask(dm_ref.shape, pl.program_id(1) * tr), dm_ref[...], 0.0)
        dpre = dmv * s_ref[...]
        dpre_b = dpre.astype(BF16)
        ds_ref[...] += jnp.sum(dmv * pre_ref[...], axis=0, keepdims=True)
        db_ref[...] += jnp.sum(dpre, axis=0, keepdims=True)
        dmx_ref[...] = _dot(dpre_b, w_ref[...], 1, 1)
        dw_ref[...] += _dot(mx_ref[...], dpre_b, 0, 0)

    blk = pl.BlockSpec((tr, gd), lambda g, i: (i, g))
    vec = pl.BlockSpec((1, gd), lambda g, i: (0, g))
    wblk = pl.BlockSpec((None, gd, gd), lambda g, i: (g, 0, 0))
    dmixed, dw, db, ds = pl.pallas_call(
        body, name=name, grid=(ng, rows // tr),
        in_specs=[blk, blk, blk, wblk, vec], out_specs=[blk, wblk, vec, vec],
        out_shape=[jax.ShapeDtypeStruct((rows, d), F32), jax.ShapeDtypeStruct((ng, gd, gd), F32),
                   jax.ShapeDtypeStruct((1, d), F32), jax.ShapeDtypeStruct((1, d), F32)],
        compiler_params=_params(("parallel", "arbitrary")),
    )(dmix, pre, mixed, w, scale.reshape(1, d))
    return dmixed, dw, db.reshape(d), ds.reshape(d)


def _my_place():
    return lax.axis_index("x"), lax.axis_index("y"), lax.axis_index("c")


def _linear(place):
    return 4 * place[0] + 2 * place[1] + place[2]


def all_gather(name, shards, after):
    n_ops = len(shards)

    def body(*refs):
        ins, outs = refs[:n_ops], refs[n_ops + 1:2 * n_ops + 1]
        send_sems, recv_sems, local_sems = refs[2 * n_ops + 1:]
        x, y, c = _my_place()
        me, sibling = (x, y, c), (x, y, 1 - c)
        chips = [(1 - x, y), (x, 1 - y), (1 - x, 1 - y)]

        def copy(t, k, block, to, src=None):
            dst = outs[t].at[_linear(block)]
            return pltpu.make_async_remote_copy(
                src_ref=dst if src is None else src, dst_ref=dst, send_sem=send_sems.at[t, k], recv_sem=recv_sems.at[t, k],
                device_id=to, device_id_type=MESH)

        mine = [pltpu.make_async_copy(ins[t], outs[t].at[_linear(me)], local_sems.at[t]) for t in range(n_ops)]
        for cp in mine:
            cp.start()
        first = []
        for t in range(n_ops):
            first.append(copy(t, 0, me, sibling, src=ins[t]))
            first += [copy(t, 1 + j, me, (*chip, c), src=ins[t]) for j, chip in enumerate(chips)]
        for cp in first:
            cp.start()
        passed = []
        for j, chip in enumerate(chips):
            for t in range(n_ops):
                copy(t, 1 + j, (*chip, c), me).wait_recv()
                fwd = copy(t, 4 + j, (*chip, c), sibling)
                fwd.start()
                passed.append(fwd)
        for t in range(n_ops):
            copy(t, 0, sibling, me).wait_recv()
            for j, chip in enumerate(chips):
                copy(t, 4 + j, (*chip, 1 - c), me).wait_recv()
        for cp in first + passed:
            cp.wait_send()
        for cp in mine:
            cp.wait()

    any_spec = pl.BlockSpec(memory_space=pl.ANY)
    outs = pl.pallas_call(
        body, name=name,
        in_specs=[any_spec] * (n_ops + 1), out_specs=[any_spec] * n_ops,
        out_shape=[jax.ShapeDtypeStruct((N_DEV, *s.shape), s.dtype) for s in shards],
        scratch_shapes=[pltpu.SemaphoreType.DMA((n_ops, 7)), pltpu.SemaphoreType.DMA((n_ops, 7)),
                        pltpu.SemaphoreType.DMA((n_ops,))],
    )(*shards, after)
    return list(outs)


_HBM = pl.BlockSpec(memory_space=pltpu.HBM)
_SEM = pl.BlockSpec(memory_space=pltpu.SEMAPHORE)
_EFFECT = pltpu.SideEffectType.DATAFLOW_SIDE_EFFECTING


def _hbm(a):
    return pltpu.with_memory_space_constraint(a, pltpu.HBM)


def _peers():
    x, y, c = _my_place()
    return [(x ^ (j >> 2), y ^ ((j >> 1) & 1), c ^ (j & 1)) for j in range(1, N_DEV)]


def exchange_start(name, groups, after):
    flat = [e for g in groups for e in g]
    n = len(flat)

    def body(*refs):
        srcs, lands = refs[:n], refs[n:2 * n]
        outs = refs[2 * n + 1:]
        sends, recvs, token = outs[:n], outs[n:2 * n], outs[4 * n]
        me = _linear(_my_place())
        for t, (_, _, src_view, land_view) in enumerate(flat):
            for peer in _peers():
                pltpu.make_async_remote_copy(
                    src_ref=src_view(srcs[t], _linear(peer)), dst_ref=land_view(lands[t], me),
                    send_sem=sends[t], recv_sem=recvs[t], device_id=peer, device_id_type=MESH).start()
        token[...] = jnp.zeros_like(token)

    sem_shapes = [pltpu.SemaphoreType.DMA(())] * (2 * n)
    thru = [pltpu.HBM(e[0].shape, e[0].dtype) for e in flat] + [pltpu.HBM(e[1].shape, e[1].dtype) for e in flat]
    outs = pl.pallas_call(
        body, name=name,
        out_shape=(*sem_shapes, *thru, jax.ShapeDtypeStruct((SUBLANES, LANES), F32)),
        in_specs=[*[_HBM] * (2 * n), pl.BlockSpec(memory_space=pl.ANY)],
        out_specs=(*[_SEM] * (2 * n), *[_HBM] * (2 * n), pl.BlockSpec(memory_space=pltpu.VMEM)),
        input_output_aliases={t: 2 * n + t for t in range(2 * n)},
        compiler_params=pltpu.CompilerParams(has_side_effects=_EFFECT),
    )(*[_hbm(e[0]) for e in flat], *[_hbm(e[1]) for e in flat], after)
    records, t = [], 0
    for g in groups:
        k = len(g)
        records.append((list(outs[t:t + k]), list(outs[n + t:n + t + k]), list(outs[2 * n + t:2 * n + t + k]),
                        list(outs[3 * n + t:3 * n + t + k])))
        t += k
    return records, outs[-1]


def exchange_wait(name, records, lands, land_of, seven_of, after):
    srcs = [s for r in records for s in r[2]]
    sends = [s for r in records for s in r[0]]
    recvs = [s for r in records for s in r[1]]
    where = [(ri, k) for ri, r in enumerate(records) for k in range(len(r[2]))]
    ns, nl = len(srcs), len(lands)

    def body(*refs):
        src_refs, land_refs = refs[:ns], refs[ns:ns + nl]
        send_refs, recv_refs = refs[ns + nl:2 * ns + nl], refs[2 * ns + nl:3 * ns + nl]
        for t, (ri, k) in enumerate(where):
            seven_ref = seven_of[ri][k](src_refs[t], land_refs[land_of[ri][k]])
            cp = pltpu.make_async_remote_copy(src_ref=seven_ref, dst_ref=seven_ref, send_sem=send_refs[t], recv_sem=recv_refs[t],
                                              device_id=_my_place(), device_id_type=MESH)
            cp.wait_send()
            cp.wait_recv()

    outs = pl.pallas_call(
        body, name=name,
        out_shape=tuple(pltpu.HBM(a.shape, a.dtype) for a in (*srcs, *lands)),
        in_specs=[*[_HBM] * (ns + nl), *[_SEM] * (2 * ns), *[pl.BlockSpec(memory_space=pl.ANY)] * len(after)],
        out_specs=tuple([_HBM] * (ns + nl)),
        input_output_aliases={t: t for t in range(ns + nl)},
        compiler_params=pltpu.CompilerParams(has_side_effects=_EFFECT),
    )(*srcs, *lands, *sends, *recvs, *after)
    return list(outs[:ns]), list(outs[ns:])


def _other_chips():
    x, y, c = _my_place()
    return [(1 - x, y, c), (x, 1 - y, c), (1 - x, 1 - y, c)]


def gather2_start(name, groups, after):
    flat = [e for g in groups for e in g]
    n = len(flat)

    def body(*refs):
        srcs, lands = refs[:n], refs[n:2 * n]
        outs = refs[2 * n + 1:]
        d_send, d_recv, i_send, i_recv, token = outs[:n], outs[n:2 * n], outs[2 * n:3 * n], outs[3 * n:4 * n], outs[6 * n]
        x, y, c = _my_place()
        me = _linear((x, y, c))
        for t in range(n):
            pltpu.make_async_remote_copy(src_ref=srcs[t], dst_ref=lands[t].at[me], send_sem=d_send[t], recv_sem=d_recv[t],
                                         device_id=(x, y, 1 - c), device_id_type=MESH).start()
            for peer in _other_chips():
                pltpu.make_async_remote_copy(src_ref=srcs[t], dst_ref=lands[t].at[me], send_sem=i_send[t], recv_sem=i_recv[t],
                                             device_id=peer, device_id_type=MESH).start()
        token[...] = jnp.zeros_like(token)

    thru = [pltpu.HBM(e[0].shape, e[0].dtype) for e in flat] + [pltpu.HBM(e[1].shape, e[1].dtype) for e in flat]
    outs = pl.pallas_call(
        body, name=name,
        out_shape=(*[pltpu.SemaphoreType.DMA(())] * (4 * n), *thru, jax.ShapeDtypeStruct((SUBLANES, LANES), F32)),
        in_specs=[*[_HBM] * (2 * n), pl.BlockSpec(memory_space=pl.ANY)],
        out_specs=(*[_SEM] * (4 * n), *[_HBM] * (2 * n), pl.BlockSpec(memory_space=pltpu.VMEM)),
        input_output_aliases={t: 4 * n + t for t in range(2 * n)},
        compiler_params=pltpu.CompilerParams(has_side_effects=_EFFECT),
    )(*[_hbm(e[0]) for e in flat], *[_hbm(e[1]) for e in flat], after)
    records, t = [], 0
    for g in groups:
        k = len(g)
        records.append({"d_send": list(outs[t:t + k]), "d_recv": list(outs[n + t:n + t + k]),
                        "i_send": list(outs[2 * n + t:2 * n + t + k]), "i_recv": list(outs[3 * n + t:3 * n + t + k]),
                        "srcs": list(outs[4 * n + t:4 * n + t + k]), "lands": list(outs[5 * n + t:5 * n + t + k])})
        t += k
    return records, outs[-1]


def gather2_relay(name, rec, after):
    lands, i_recv = rec["lands"], rec["i_recv"]
    n = len(lands)

    def body(*refs):
        land_refs, i_recv_refs = refs[:n], refs[n:2 * n]
        outs = refs[2 * n + 1:]
        f_send, f_recv, token = outs[n:2 * n], outs[2 * n:3 * n], outs[3 * n]
        x, y, c = _my_place()
        for t in range(n):
            three = land_refs[t].at[pl.ds(0, 3)]
            pltpu.make_async_remote_copy(src_ref=three, dst_ref=three, send_sem=f_send[t], recv_sem=i_recv_refs[t],
                                         device_id=(x, y, c), device_id_type=MESH).wait_recv()
            for peer in _other_chips():
                blk = land_refs[t].at[_linear(peer)]
                pltpu.make_async_remote_copy(src_ref=blk, dst_ref=blk, send_sem=f_send[t], recv_sem=f_recv[t],
                                             device_id=(x, y, 1 - c), device_id_type=MESH).start()
        token[...] = jnp.zeros_like(token)

    outs = pl.pallas_call(
        body, name=name,
        out_shape=(*[pltpu.HBM(a.shape, a.dtype) for a in lands], *[pltpu.SemaphoreType.DMA(())] * (2 * n),
                   jax.ShapeDtypeStruct((SUBLANES, LANES), F32)),
        in_specs=[*[_HBM] * n, *[_SEM] * n, pl.BlockSpec(memory_space=pl.ANY)],
        out_specs=(*[_HBM] * n, *[_SEM] * (2 * n), pl.BlockSpec(memory_space=pltpu.VMEM)),
        input_output_aliases={t: t for t in range(n)},
        compiler_params=pltpu.CompilerParams(has_side_effects=_EFFECT),
    )(*lands, *i_recv, after)
    return {**rec, "lands": list(outs[:n]), "f_send": list(outs[n:2 * n]), "f_recv": list(outs[2 * n:3 * n])}, outs[-1]


def gather2_wait(name, rec, after):
    n = len(rec["lands"])
    sem_names = ("d_send", "d_recv", "i_send", "f_send", "f_recv")

    def body(*refs):
        src_refs, land_refs = refs[:n], refs[n:2 * n]
        sems = {nm: refs[2 * n + k * n:2 * n + (k + 1) * n] for k, nm in enumerate(sem_names)}
        me = _my_place()
        for t in range(n):
            one, three = land_refs[t].at[0], land_refs[t].at[pl.ds(0, 3)]
            to_sibling = pltpu.make_async_remote_copy(src_ref=src_refs[t], dst_ref=one, send_sem=sems["d_send"][t],
                                                      recv_sem=sems["d_recv"][t], device_id=me, device_id_type=MESH)
            to_sibling.wait_send()
            to_sibling.wait_recv()
            pltpu.make_async_remote_copy(src_ref=three, dst_ref=three, send_sem=sems["i_send"][t], recv_sem=sems["d_recv"][t],
                                         device_id=me, device_id_type=MESH).wait_send()
            forwarded = pltpu.make_async_remote_copy(src_ref=three, dst_ref=three, send_sem=sems["f_send"][t],
                                                     recv_sem=sems["f_recv"][t], device_id=me, device_id_type=MESH)
            forwarded.wait_send()
            forwarded.wait_recv()

    arrays = (*rec["srcs"], *rec["lands"])
    outs = pl.pallas_call(
        body, name=name,
        out_shape=tuple(pltpu.HBM(a.shape, a.dtype) for a in arrays),
        in_specs=[*[_HBM] * (2 * n), *[_SEM] * (5 * n), *[pl.BlockSpec(memory_space=pl.ANY)] * 2], out_specs=tuple([_HBM] * (2 * n)),
        input_output_aliases={t: t for t in range(2 * n)},
        compiler_params=pltpu.CompilerParams(has_side_effects=_EFFECT),
    )(*arrays, *[s for nm in sem_names for s in rec[nm]], *after)
    return list(outs[:n]), list(outs[n:])


def _seven_slots_of_land(src_ref, land_ref):
    return land_ref.at[pl.ds(0, N_DEV - 1)]


def _seven_blocks_of_src(src_ref, land_ref):
    return src_ref.at[pl.ds(0, N_DEV - 1)]


def _whole(ref, dev):
    return ref


def _slot(ref, dev):
    return ref.at[dev]


def _slot_of_layer(layer):
    return lambda ref, dev: ref.at[dev, layer]


def _as_rows(a, lead=0):
    return a.reshape(a.shape[:lead] + (-1, a.shape[-1]))


def sum_adamw(name, parts, w, m, v, half=None, column_major=False):
    shape = w.shape
    c1 = 1.0 - ADAM_B1 ** ADAM_STEP
    c2 = 1.0 - ADAM_B2 ** ADAM_STEP
    if column_major:
        layers, r, cols = shape
        tr = _tile(r, ROW_TILE, LANES)
        grid = (layers, r // tr)
        if half is None:
            p_spec = pl.BlockSpec((N_DEV, None, tr, cols), lambda l, i: (0, l, i, 0))
        else:
            p_spec = pl.BlockSpec((N_DEV, None, None, tr, cols), lambda l, i: (0, l, half, i, 0))
        blk = pl.BlockSpec((None, cols, tr), lambda l, i: (l, 0, i))
        args = (parts, *[jnp.swapaxes(a, 1, 2) for a in (w, m, v)])
        out_sds, sem = jax.ShapeDtypeStruct((layers, cols, r), F32), ("parallel", "parallel")
    elif half is not None:
        layers, r, cols = shape
        tr = _tile(r, ROW_TILE, 16)
        grid = (layers, r // tr)
        p_spec = pl.BlockSpec((N_DEV, None, None, tr, cols), lambda l, i: (0, l, half, i, 0))
        blk = pl.BlockSpec((None, tr, cols), lambda l, i: (l, i, 0))
        args, out_sds, sem = (parts, w, m, v), jax.ShapeDtypeStruct(shape, F32), ("parallel", "parallel")
    else:
        p2, w2, m2, v2 = _as_rows(parts, 1), _as_rows(w), _as_rows(m), _as_rows(v)
        rows, cols = w2.shape
        tr = _tile(rows, ROW_TILE, 16)
        grid = (rows // tr,)
        p_spec = pl.BlockSpec((N_DEV, tr, cols), lambda i: (0, i, 0))
        blk = pl.BlockSpec((tr, cols), lambda i: (i, 0))
        args, out_sds, sem = (p2, w2, m2, v2), jax.ShapeDtypeStruct((rows, cols), F32), ("parallel",)

    def body(p_ref, w_ref, m_ref, v_ref, g_ref, d_ref, nm_ref, nv_ref):
        g = p_ref[0].astype(F32)
        for k in range(1, N_DEV):
            g = g + p_ref[k].astype(F32)
        if column_major:
            g = g.T
        wv = w_ref[...]
        nm = ADAM_B1 * m_ref[...] + (1.0 - ADAM_B1) * g
        nv = ADAM_B2 * v_ref[...] + (1.0 - ADAM_B2) * (g * g)
        g_ref[...] = g
        nm_ref[...] = nm
        nv_ref[...] = nv
        d_ref[...] = -ADAM_LR * ((nm / c1) / (jnp.sqrt(nv / c2) + ADAM_EPS) + ADAM_WD * wv)

    outs = pl.pallas_call(
        body, name=name, grid=grid, in_specs=[p_spec, blk, blk, blk], out_specs=[blk] * 4,
        out_shape=[out_sds] * 4, compiler_params=_params(sem),
    )(*args)
    if column_major:
        return [jnp.swapaxes(o, 1, 2) for o in outs]
    return [o.reshape(shape) for o in outs]


def _unblock_cols(g):
    g = jnp.moveaxis(g, 0, -2)
    return g.reshape(g.shape[:-2] + (g.shape[-2] * g.shape[-1],))


def _block_cols(a):
    r, c = a.shape
    return jnp.moveaxis(a.reshape(r, N_DEV, c // N_DEV), 1, 0)


def _my_cols(a, n):
    me = _linear(_my_place())
    return lax.dynamic_slice_in_dim(a, me * n, n, axis=a.ndim - 1)


def _pack(arrays):
    flat = jnp.concatenate([a.reshape(-1).astype(F32) for a in arrays])
    pad = (-flat.shape[0]) % (ROW_TILE * LANES)
    return jnp.pad(flat, (0, pad)).reshape(-1, LANES)


def _unpack(packed, like):
    flat = packed.reshape(-1)
    out, pos = [], 0
    for a in like:
        out.append(flat[pos:pos + a.size].reshape(a.shape))
        pos += a.size
    return out


def kernel(x, meta_tokens, norm_w, ssd_w_in, ssd_conv_w, ssd_conv_b, ssd_dt_bias, ssd_a_log, ssd_d, ssd_norm_w, ssd_w_out, pool_w, pool_b, pool_scale, ffn_w_gate, ffn_w_up, ffn_w_down, loss_target, m_meta_tokens, m_norm_w, m_ssd_w_in, m_ssd_conv_w, m_ssd_conv_b, m_ssd_dt_bias, m_ssd_a_log, m_ssd_d, m_ssd_norm_w, m_ssd_w_out, m_pool_w, m_pool_b, m_pool_scale, m_ffn_w_gate, m_ffn_w_up, m_ffn_w_down, v_meta_tokens, v_norm_w, v_ssd_w_in, v_ssd_conv_w, v_ssd_conv_b, v_ssd_dt_bias, v_ssd_a_log, v_ssd_d, v_ssd_norm_w, v_ssd_w_out, v_pool_w, v_pool_b, v_pool_scale, v_ffn_w_gate, v_ffn_w_up, v_ffn_w_down):
    seq, d = x.shape[1], x.shape[2]
    depth = norm_w.shape[0]
    n_ssd = ssd_w_in.shape[0]
    d_inner = ssd_norm_w.shape[1]
    heads = d_inner // HEAD_DIM
    rpg = heads // SSD_GROUPS
    conv_dim = ssd_conv_b.shape[1]
    zx_cols = d_inner + conv_dim
    d_in_proj = zx_cols + heads
    rows = PAD_FRONT + N_META + seq
    assert rows % CHUNK == 0 and rpg % 2 == 0 and heads <= LANES and rpg <= SUBLANES

    hidden = ffn_w_down.shape[1] * N_DEV
    n_pool = pool_w.shape[0]

    gather_recs = {}

    def start_gather(i, after, groups_of_layer=slice(None), tag=""):
        def entry(shard):
            return lambda: (shard.astype(BF16), lax.empty((N_DEV, *shard.shape), BF16))
        mixer = [ssd_w_in[i // 2], ssd_w_out[i // 2]] if i % 2 == 0 else [pool_w[i // 2]]
        groups = [*[[entry(w)] for w in mixer], [entry(jnp.stack([ffn_w_gate[i], ffn_w_up[i]])), entry(ffn_w_down[i])]]
        groups = [[make() for make in g] for g in groups[groups_of_layer]]
        recs, tok = gather2_start(f"gather_start_{i}{tag}", groups, after)
        gather_recs[i] = gather_recs.get(i, []) + recs
        return tok

    first = start_gather(0, meta_tokens, slice(0, 1), "_first")
    small = all_gather("gather_small", [meta_tokens, norm_w, ssd_conv_w, pool_b, pool_scale], first)
    meta_f, norm_f, convw_f, poolb_f, pools_f = [_unblock_cols(s) for s in small]

    gather_order = [(i, g) for i in range(depth) for g in range(3 if i % 2 == 0 else 2)]
    relayed = [0, None]

    def gathered(i, g, after):
        idx = gather_order.index((i, g))
        upto = min(idx + (1 if idx == 0 else 2), len(gather_order))
        while relayed[0] < upto:
            li, lg = gather_order[relayed[0]]
            gather_recs[li][lg], relayed[1] = gather2_relay(f"gather_relay_{li}_{lg}", gather_recs[li][lg], after)
            relayed[0] += 1
        srcs, lands_ = gather2_wait(f"gather_wait_{i}_{g}", gather_recs[i][g], (after, relayed[1]))
        me = _linear(_my_place())
        return [lax.dynamic_update_slice_in_dim(land, src[None], me, axis=0) for src, land in zip(srcs, lands_)]

    def gathered_small(name, rec, after):
        srcs, lands_ = exchange_wait(name, [rec], rec[3], [[0]], [[_seven_slots_of_land]], after)
        return lax.dynamic_update_slice_in_dim(lands_[0], srcs[0][None], _linear(_my_place()), axis=0)

    pending = start_gather(0, meta_f, slice(1, None))
    for i in range(1, min(GATHER_AHEAD, depth)):
        pending = start_gather(i, pending)

    pad_h = lambda a: jnp.pad(a.astype(F32), ((0, 0), (0, LANES - heads)))
    bias_pad = pad_h(ssd_dt_bias)

    def head_layouts(vec):
        g = vec.reshape(SSD_GROUPS, rpg)
        row = jnp.pad(g, ((0, 0), (0, LANES - rpg)))[:, None, :]
        col = jnp.pad(g, ((0, 0), (0, SUBLANES - rpg)))[:, :, None]
        return row, col

    def dt_layouts(dt):
        g = dt[:, :heads].reshape(rows, SSD_GROUPS, rpg)
        col = jnp.pad(jnp.moveaxis(g, 1, 0), ((0, 0), (0, 0), (0, LANES - rpg)))
        row = jnp.pad(jnp.transpose(g, (1, 2, 0)), ((0, 0), (0, SUBLANES - rpg), (0, 0)))
        return col, row

    h = jnp.concatenate([jnp.zeros((PAD_FRONT, d), F32), meta_f, x[0]], axis=0)
    saved = []
    for i in range(depth):
        j = i // 2
        s = {"h": h}
        w_pre_mix = norm_f[i, 0]
        if i + GATHER_AHEAD < depth:
            pending = start_gather(i + GATHER_AHEAD, pending if i == 0 else h)
        (mixer_w,) = gathered(i, 0, h)
        if i % 2 == 0:
            w_in = jnp.pad(_unblock_cols(mixer_w), ((0, 0), (0, LANES - heads)))
            u, u_t = rmsnorm_fwd(f"norm_pre_mix_{i}", h, w_pre_mix, out_dtype=BF16, transposed=True, after=pending)
            zx = matmul(f"ssd_in_{i}", u, w_in, **MM_ROWS_RESIDENT, tn=384 if w_in.shape[1] % 384 == 0 else 512)
            xbc = conv_fwd(f"ssd_conv_{i}", zx, convw_f[j], ssd_conv_b[j], d_inner, conv_dim)
            dt = dt_fwd(f"ssd_dt_{i}", zx, bias_pad[j:j + 1], zx_cols)
            dt_col, dt_row = dt_layouts(dt)
            a_neg = -jnp.exp(ssd_a_log[j].astype(F32))
            a_row, a_col = head_layouts(a_neg)
            d_row, _ = head_layouts(ssd_d[j].astype(F32))
            y, sprev = ssd_scan_fwd(f"ssd_scan_{i}", xbc, dt_col, dt_row, a_row, a_col, d_row, d_inner)
            yn, yn_t = gatenorm_fwd(f"ssd_gate_{i}", y, zx, ssd_norm_w[j], d_inner)
            (w_out,) = gathered(i, 1, yn)
            w_out = w_out.reshape(d_inner, d)
            mix = matmul(f"ssd_out_{i}", yn, w_out, **MM_DEEP)
            s.update(u_t=u_t, zx=zx, xbc=xbc, dt_col=dt_col, dt_row=dt_row, a_row=a_row, a_col=a_col, d_row=d_row,
                     a_neg=a_neg, y=y, sprev=sprev, yn_t=yn_t, w_in=w_in, w_out=w_out)
        else:
            w_pool = jnp.moveaxis(mixer_w, 0, 1).reshape(len(POOL_WINDOWS), d // len(POOL_WINDOWS), -1)
            u = rmsnorm_fwd(f"norm_pre_mix_{i}", h, w_pre_mix, after=pending)
            mixed = pool_sub(f"pool_sub_{i}", u, False)
            pre, mix = pool_proj_fwd(f"pool_proj_{i}", mixed, w_pool, poolb_f[j], pools_f[j])
            s.update(mixed=mixed, pre=pre, w_pool=w_pool)
        pending = None
        h1 = rmsnorm_fwd(f"norm_post_mix_{i}", mix, norm_f[i, 1], res=h)
        w_gu, w_down = gathered(i, 2 if i % 2 == 0 else 1, h1)
        u2, u2_t = rmsnorm_fwd(f"norm_pre_ffn_{i}", h1, norm_f[i, 2], out_dtype=BF16, transposed=True)
        gp = ffn_in_fwd(f"ffn_in_{i}", u2, w_gu)
        act, act_t = swiglu_fwd(f"ffn_act_{i}", gp)
        f = ffn_out_fwd(f"ffn_out_{i}", act, w_down)
        h = rmsnorm_fwd(f"norm_post_ffn_{i}", f, norm_f[i, 3], res=h1)
        s.update(mix=mix, h1=h1, u2_t=u2_t, gp=gp, act_t=act_t, f=f, w_gu=w_gu, w_down=w_down)
        saved.append(s)

    dh, loss_local = loss_head("loss_head", h, loss_target[0])
    loss = lax.psum(loss_local, AXES)

    g_norm = [[None] * 4 for _ in range(depth)]
    g_convw, g_convb, g_dtb, g_alog, g_dskip, g_ssdnorm = ([None] * n_ssd for _ in range(6))
    g_poolb, g_pools = ([None] * n_pool for _ in range(2))
    hid_s = hidden // N_DEV
    lands = {"in": lax.empty((N_DEV, *ssd_w_in.shape), BF16), "out": lax.empty((N_DEV, *ssd_w_out.shape), BF16),
             "pool": lax.empty((N_DEV, *pool_w.shape), BF16), "down": lax.empty((N_DEV, *ffn_w_down.shape), BF16),
             "gate_up": lax.empty((N_DEV, depth, 2, d, hid_s), BF16)}
    scatter_recs, scatter_keys, scatter_views, scatter_layers = [], [], [], []

    def scatter(name, blocks, layer, after):
        view = _slot_of_layer(layer)
        (rec,), tok = exchange_start(name, [[(b, lands[key], _slot, view) for key, b in blocks]], after)
        for (key, _), thru in zip(blocks, rec[3]):
            lands[key] = thru
        scatter_recs.append(rec)
        scatter_keys.append([key for key, _ in blocks])
        scatter_views.append([_seven_blocks_of_src] * len(blocks))
        scatter_layers.append(layer)
        return tok

    tok = None
    for i in reversed(range(depth)):
        j = i // 2
        s = saved[i]
        df, g_norm[i][3] = rmsnorm_bwd(f"norm_post_ffn_bwd_{i}", s["f"], norm_f[i, 3], dh, after=tok)
        df_b = df.astype(BF16)
        dact = ffn_out_bwd_x(f"ffn_out_bwd_x_{i}", df_b, s["w_down"])
        g_down = ffn_out_bwd_w(f"ffn_out_bwd_w_{i}", s["act_t"], df_b)
        dgp = swiglu_bwd(f"ffn_act_bwd_{i}", s["gp"], dact)
        g_gu = ffn_in_bwd_w(f"ffn_in_bwd_w_{i}", s["u2_t"], dgp)
        tok = scatter(f"scatter_start_ffn_{i}", [("gate_up", g_gu), ("down", g_down)], i, g_gu)
        du2 = ffn_in_bwd_x(f"ffn_in_bwd_x_{i}", dgp, s["w_gu"])
        dh1, g_norm[i][2] = rmsnorm_bwd(f"norm_pre_ffn_bwd_{i}", s["h1"], norm_f[i, 2], du2, add=dh, after=tok)
        dmix, g_norm[i][1] = rmsnorm_bwd(f"norm_post_mix_bwd_{i}", s["mix"], norm_f[i, 1], dh1)
        if i % 2 == 0:
            dmix_b = dmix.astype(BF16)
            dyn = matmul(f"ssd_out_bwd_x_{i}", dmix_b, s["w_out"], tb=True, **MM_ROWS_RESIDENT)
            g_out = matmul(f"ssd_out_bwd_w_{i}", s["yn_t"], dmix_b, out_dtype=BF16, **MM_COLS_RESIDENT)
            tok = scatter(f"scatter_start_out_{i}", [("out", g_out.reshape(N_DEV, d_inner // N_DEV, d))], j, g_out)
            dy, dz, g_ssdnorm[j] = gatenorm_bwd(f"ssd_gate_bwd_{i}", s["y"], s["zx"], ssd_norm_w[j], dyn, d_inner, after=tok)
            dx, db, dc, ddt_col, ddt_row, dar, dac, ddsk = ssd_scan_bwd(
                f"ssd_scan_bwd_{i}", s["xbc"], s["dt_col"], s["dt_row"], s["a_row"], s["a_col"], s["d_row"], s["sprev"], dy, d_inner)
            ddt = (jnp.moveaxis(ddt_col[:, :, :rpg], 0, 1).reshape(rows, heads)
                   + jnp.transpose(ddt_row[:, :rpg, :], (2, 0, 1)).reshape(rows, heads))
            d_a = (dar[:, 0, :rpg] + dac[:, :rpg, 0]).reshape(heads)
            g_alog[j] = d_a * s["a_neg"]
            g_dskip[j] = ddsk[:, 0, :rpg].reshape(heads)
            ddtr, dbias = dt_bwd(f"ssd_dt_bwd_{i}", s["zx"], bias_pad[j:j + 1], pad_h(ddt), zx_cols)
            g_dtb[j] = dbias[0, :heads]
            dxbc_raw, g_convw[j], dconvb = conv_bwd(
                f"ssd_conv_bwd_{i}", s["zx"], convw_f[j], ssd_conv_b[j], jnp.concatenate([dx, db, dc], axis=1), d_inner, conv_dim)
            g_convb[j] = dconvb[0]
            dzx = jnp.concatenate([dz, dxbc_raw, ddtr], axis=1)
            g_in = matmul(f"ssd_in_bwd_w_{i}", s["u_t"], dzx, out_dtype=BF16, **MM_ROWS_RESIDENT,
                          tn=384 if dzx.shape[1] % 384 == 0 else 512)
            tok = scatter(f"scatter_start_in_{i}", [("in", _block_cols(g_in[:, :d_in_proj]))], j, g_in)
            du = matmul(f"ssd_in_bwd_x_{i}", dzx, s["w_in"], tb=True, **{**MM_DEEP, "tk": 1152 if dzx.shape[1] % 1152 == 0 else 512})
        else:
            dmixed, g_poolw, g_poolb[j], g_pools[j] = pool_proj_bwd(
                f"pool_proj_bwd_{i}", dmix, s["pre"], s["mixed"], s["w_pool"], pools_f[j])
            ng, gd = g_poolw.shape[0], g_poolw.shape[1]
            blk_pool = jnp.moveaxis(g_poolw.astype(BF16).reshape(ng, N_DEV, gd // N_DEV, gd), 1, 0)
            tok = scatter(f"scatter_start_pool_{i}", [("pool", blk_pool)], j, blk_pool)
            du = pool_sub(f"pool_sub_bwd_{i}", dmixed, True)
        dh, g_norm[i][0] = rmsnorm_bwd(f"norm_pre_mix_bwd_{i}", s["h"], norm_f[i, 0], du, add=dh1, after=tok)

    grad_x = dh[PAD_FRONT + N_META:][None]
    g_meta = dh[PAD_FRONT:PAD_FRONT + N_META]

    small_grads = [g_meta, jnp.stack([jnp.stack(r) for r in g_norm]), jnp.stack(g_convw), jnp.stack(g_convb), jnp.stack(g_dtb),
                   jnp.stack(g_alog), jnp.stack(g_dskip), jnp.stack(g_ssdnorm), jnp.stack(g_poolb), jnp.stack(g_pools)]
    packed_g = _pack(small_grads)
    (small_rec,), small_tok = exchange_start(
        "gather_small_grads_start", [[(packed_g, lax.empty((N_DEV, *packed_g.shape), F32), _whole, _slot)]], packed_g)
    mine = lambda a: _my_cols(a, a.shape[-1] // N_DEV)
    small_w = [meta_tokens, norm_w, ssd_conv_w, ssd_conv_b, ssd_dt_bias, ssd_a_log, ssd_d, ssd_norm_w, pool_b, pool_scale]
    small_m = [m_meta_tokens, m_norm_w, m_ssd_conv_w, m_ssd_conv_b, m_ssd_dt_bias, m_ssd_a_log, m_ssd_d, m_ssd_norm_w, m_pool_b, m_pool_scale]
    small_v = [v_meta_tokens, v_norm_w, v_ssd_conv_w, v_ssd_conv_b, v_ssd_dt_bias, v_ssd_a_log, v_ssd_d, v_ssd_norm_w, v_pool_b, v_pool_scale]
    sharded = [True, True, True, False, False, False, False, False, True, True]
    def widen(a, is_sharded, full):
        if not is_sharded:
            return a
        return lax.dynamic_update_slice_in_dim(jnp.zeros(full.shape, F32), a, _linear(_my_place()) * a.shape[-1], axis=a.ndim - 1)
    packed_w = _pack([widen(a, sh, g) for a, sh, g in zip(small_w, sharded, small_grads)])
    packed_m = _pack([widen(a, sh, g) for a, sh, g in zip(small_m, sharded, small_grads)])
    packed_v = _pack([widen(a, sh, g) for a, sh, g in zip(small_v, sharded, small_grads)])

    def landed(name, keys, after):
        rs = [r for r, ks in enumerate(scatter_keys) if set(ks) <= set(keys)]
        srcs, out = exchange_wait(name, [scatter_recs[r] for r in rs], [lands[k] for k in keys],
                                  [[keys.index(k) for k in scatter_keys[r]] for r in rs], [scatter_views[r] for r in rs], after)
        me = _linear(_my_place())
        src_of = iter(srcs)
        for r in rs:
            for key in scatter_keys[r]:
                own = lax.dynamic_index_in_dim(next(src_of), me, axis=0, keepdims=True)[:, None]
                slot = keys.index(key)
                start = (me, scatter_layers[r]) + (0,) * (own.ndim - 2)
                out[slot] = lax.dynamic_update_slice(out[slot], own, start)
        return out

    p_gu, p_down = landed("scatter_wait_ffn", ["gate_up", "down"], (small_tok,))
    a_gate = sum_adamw("adamw_ffn_w_gate", p_gu, ffn_w_gate, m_ffn_w_gate, v_ffn_w_gate, half=0, column_major=True)
    a_up = sum_adamw("adamw_ffn_w_up", p_gu, ffn_w_up, m_ffn_w_up, v_ffn_w_up, half=1, column_major=True)
    a_down = sum_adamw("adamw_ffn_w_down", p_down, ffn_w_down, m_ffn_w_down, v_ffn_w_down)
    (p_pool,) = landed("scatter_wait_pool", ["pool"], (a_gate[0],))
    a_pool = sum_adamw("adamw_pool_w", p_pool, pool_w, m_pool_w, v_pool_w)
    (p_out,) = landed("scatter_wait_out", ["out"], (a_up[0], a_pool[0]))
    a_out = sum_adamw("adamw_ssd_w_out", p_out, ssd_w_out, m_ssd_w_out, v_ssd_w_out)
    (p_in,) = landed("scatter_wait_in", ["in"], (a_gate[0], a_up[0], a_down[0], a_pool[0], a_out[0]))
    a_in = sum_adamw("adamw_ssd_w_in", p_in, ssd_w_in, m_ssd_w_in, v_ssd_w_in, column_major=True)
    bg = [a_in, a_out, a_pool, a_gate, a_up, a_down]

    small_parts = gathered_small("gather_small_grads_wait", small_rec, (a_in[0],))
    sm = [_unpack(o, small_grads) for o in sum_adamw("adamw_small", small_parts, packed_w, packed_m, packed_v)]
    sm = [[mine(a) if sh else a for a, sh in zip(group, sharded)] for group in sm]

    def ordered(kind):
        s_ = sm[kind]
        b_ = [o[kind] for o in bg]
        return [s_[0], s_[1], b_[0], s_[2], s_[3], s_[4], s_[5], s_[6], s_[7], b_[1], b_[2], s_[8], s_[9], b_[3], b_[4], b_[5]]

    return (loss, grad_x, *ordered(0), *ordered(1), *ordered(2), *ordered(3))
```

```python
import functools

import jax
import jax.numpy as jnp
from jax import lax
from jax.experimental import pallas as pl
from jax.experimental.pallas import tpu as pltpu

F32 = jnp.float32
BF16 = jnp.bfloat16
MESH = pl.DeviceIdType.MESH
AXES = ("x", "y", "c")
N_DEV = 8

N_META = 16
EPS = 1e-6
HEAD_DIM = 64
D_STATE = 128
SSD_GROUPS = 8
D_CONV = 4
CHUNK = 256
POOL_WINDOWS = (2, 4, 8, 16)
ADAM_LR, ADAM_B1, ADAM_B2, ADAM_EPS, ADAM_WD, ADAM_STEP = 0.001, 0.9, 0.999, 1e-08, 0.01, 10

PAD_FRONT = (-N_META) % CHUNK
LANES = 128
SUBLANES = 8
ROW_TILE = 256
CONV_TILE = 256
POOL_TILE = 128
VMEM_LIMIT = 56 * 1024 * 1024


def _params(sem=None):
    return pltpu.CompilerParams(dimension_semantics=sem, vmem_limit_bytes=VMEM_LIMIT)


def _tile(n, target, mult):
    if n <= target:
        return n
    best = None
    for t in range(mult, target + 1, mult):
        if n % t == 0:
            best = t
    assert best is not None, (n, target, mult)
    return best


def _dot(a, b, ca, cb):
    return lax.dot_general(a, b, (((ca,), (cb,)), ((), ())), preferred_element_type=F32)


def _sigmoid(x):
    return 1.0 / (1.0 + jnp.exp(-x))


def _row_mask(shape, first_row):
    rows = lax.broadcasted_iota(jnp.int32, shape, 0) + first_row
    return rows >= PAD_FRONT


MM_ROWS_RESIDENT = dict(tm=2304, tk=2304)
MM_COLS_RESIDENT = dict(tm=512, tn=2304, tk=2304)
MM_DEEP = dict(tm=1152, tn=1024, tk=512)
GATHER_AHEAD = 2

def matmul(name, a, b, *, tb=False, out_dtype=F32, tm=768, tn=512, tk=2048):
    m, kdim = a.shape
    if tb:
        n, k2 = b.shape
    else:
        k2, n = b.shape
    assert kdim == k2, (a.shape, b.shape, tb)
    tm = _tile(m, tm, 16)
    tn = _tile(n, tn, LANES)
    tk = _tile(kdim, tk, LANES)
    nk = kdim // tk
    a_spec = pl.BlockSpec((tm, tk), lambda i, j, k: (i, k))
    b_spec = pl.BlockSpec((tn, tk), lambda i, j, k: (j, k)) if tb else pl.BlockSpec((tk, tn), lambda i, j, k: (k, j))

    def body_single(a_ref, b_ref, o_ref):
        o_ref[...] = _dot(a_ref[...], b_ref[...], 1, 1 if tb else 0).astype(o_ref.dtype)

    def body_acc(a_ref, b_ref, o_ref, acc_ref):
        k = pl.program_id(2)

        @pl.when(k == 0)
        def _():
            acc_ref[...] = jnp.zeros_like(acc_ref)

        acc_ref[...] += _dot(a_ref[...], b_ref[...], 1, 1 if tb else 0)

        @pl.when(k == nk - 1)
        def _():
            o_ref[...] = acc_ref[...].astype(o_ref.dtype)

    return pl.pallas_call(
        body_single if nk == 1 else body_acc, name=name, grid=(m // tm, n // tn, nk),
        in_specs=[a_spec, b_spec], out_specs=pl.BlockSpec((tm, tn), lambda i, j, k: (i, j)),
        out_shape=jax.ShapeDtypeStruct((m, n), out_dtype),
        scratch_shapes=[] if nk == 1 else [pltpu.VMEM((tm, tn), F32)],
        compiler_params=_params(("parallel", "parallel", "arbitrary")),
    )(a, b)


_TOKEN_SPEC = pl.BlockSpec((SUBLANES, LANES), lambda i: (0, 0))


def rmsnorm_fwd(name, x, w, res=None, out_dtype=F32, transposed=False, after=None):
    rows, d = x.shape
    tr = ROW_TILE
    row_spec = pl.BlockSpec((tr, d), lambda i: (i, 0))
    w_spec = pl.BlockSpec((1, d), lambda i: (0, 0))
    n_in = 2 + (res is not None) + (after is not None)

    def body(*refs):
        x_ref, w_ref = refs[:2]
        outs = refs[n_in:]
        xv = x_ref[...]
        y = xv * lax.rsqrt(jnp.mean(xv * xv, axis=-1, keepdims=True) + EPS) * w_ref[...]
        if res is not None:
            y = refs[2][...] + y
        outs[0][...] = y.astype(out_dtype)
        if transposed:
            outs[1][...] = y.T.astype(out_dtype)

    args = [x, w.reshape(1, d)] + ([] if res is None else [res]) + ([] if after is None else [after])
    specs = [row_spec, w_spec] + ([] if res is None else [row_spec]) + ([] if after is None else [_TOKEN_SPEC])
    out_specs, out_shape = [row_spec], [jax.ShapeDtypeStruct((rows, d), out_dtype)]
    if transposed:
        out_specs.append(pl.BlockSpec((d, tr), lambda i: (0, i)))
        out_shape.append(jax.ShapeDtypeStruct((d, rows), out_dtype))
    outs = pl.pallas_call(
        body, name=name, grid=(rows // tr,), in_specs=specs, out_specs=out_specs, out_shape=out_shape,
        compiler_params=_params(("parallel",)),
    )(*args)
    return outs if transposed else outs[0]


def rmsnorm_bwd(name, x, w, dy, add=None, after=None):
    rows, d = x.shape
    tr = ROW_TILE
    row_spec = pl.BlockSpec((tr, d), lambda i: (i, 0))
    w_spec = pl.BlockSpec((1, d), lambda i: (0, 0))

    def body(*refs):
        x_ref, w_ref, dy_ref = refs[:3]
        add_ref = None if add is None else refs[3]
        dx_ref, dw_ref = refs[-2:]
        xv = x_ref[...]
        dyv = dy_ref[...].astype(F32)
        r = lax.rsqrt(jnp.mean(xv * xv, axis=-1, keepdims=True) + EPS)
        xh = xv * r
        dxh = dyv * w_ref[...]
        dx = r * (dxh - xh * jnp.mean(dxh * xh, axis=-1, keepdims=True))
        if add is not None:
            dx = dx + add_ref[...]
        dx_ref[...] = dx

        @pl.when(pl.program_id(0) == 0)
        def _():
            dw_ref[...] = jnp.zeros_like(dw_ref)

        dw_ref[...] += jnp.sum(dyv * xh, axis=0, keepdims=True)

    args = [x, w.reshape(1, d), dy] + ([] if add is None else [add]) + ([] if after is None else [after])
    specs = [row_spec, w_spec, row_spec] + ([] if add is None else [row_spec]) + ([] if after is None else [_TOKEN_SPEC])
    dx, dw = pl.pallas_call(
        body, name=name, grid=(rows // tr,), in_specs=specs, out_specs=[row_spec, w_spec],
        out_shape=[jax.ShapeDtypeStruct((rows, d), F32), jax.ShapeDtypeStruct((1, d), F32)],
        compiler_params=_params(("arbitrary",)),
    )(*args)
    return dx, dw.reshape(d)


def loss_head(name, h, target):
    rows, d = h.shape
    tr = ROW_TILE
    first = (PAD_FRONT + N_META) // tr
    assert (PAD_FRONT + N_META) % tr == 0 and target.shape[0] == rows - first * tr

    def body(h_ref, t_ref, dh_ref, loss_ref):
        i = pl.program_id(0)

        @pl.when(i == 0)
        def _():
            loss_ref[...] = jnp.zeros_like(loss_ref)

        keep = (i >= first).astype(F32)
        diff = (h_ref[...] - t_ref[...]) * keep
        dh_ref[...] = diff / d
        loss_ref[...] += 0.5 * jnp.sum(diff * diff) / d

    dh, loss = pl.pallas_call(
        body, name=name, grid=(rows // tr,),
        in_specs=[pl.BlockSpec((tr, d), lambda i: (i, 0)), pl.BlockSpec((tr, d), lambda i: (jnp.maximum(i - first, 0), 0))],
        out_specs=[pl.BlockSpec((tr, d), lambda i: (i, 0)), pl.BlockSpec((SUBLANES, LANES), lambda i: (0, 0))],
        out_shape=[jax.ShapeDtypeStruct((rows, d), F32), jax.ShapeDtypeStruct((SUBLANES, LANES), F32)],
        compiler_params=_params(("arbitrary",)),
    )(h, target)
    return dh, loss[0, 0]


def _mm_call(name, a, b, grid, a_spec, b_spec, o_spec, out_sds, ca, cb, red_axis=None, pairs=None):
    n_red = None if red_axis is None else grid[red_axis]

    def body_single(a_ref, b_ref, o_ref):
        o_ref[...] = _dot(a_ref[...], b_ref[...], ca, cb).astype(o_ref.dtype)

    def step_product(a_ref, b_ref):
        if pairs is None:
            return _dot(a_ref[...], b_ref[...], ca, cb)
        return functools.reduce(lambda s, t: s + t, [_dot(a_ref[p], b_ref[p], ca, cb) for p in range(pairs)])

    def body_acc(a_ref, b_ref, o_ref, acc_ref):
        k = pl.program_id(red_axis)

        @pl.when(k == 0)
        def _():
            acc_ref[...] = jnp.zeros_like(acc_ref)

        acc_ref[...] += step_product(a_ref, b_ref)

        @pl.when(k == n_red - 1)
        def _():
            o_ref[...] = acc_ref[...].astype(o_ref.dtype)

    acc_shape = tuple(s for s in o_spec.block_shape if s is not None)
    sem = tuple("arbitrary" if ax == red_axis else "parallel" for ax in range(len(grid)))
    return pl.pallas_call(
        body_single if red_axis is None else body_acc, name=name, grid=grid, in_specs=[a_spec, b_spec], out_specs=o_spec,
        out_shape=out_sds, scratch_shapes=[] if red_axis is None else [pltpu.VMEM(acc_shape, F32)],
        compiler_params=_params(sem),
    )(a, b)


def ffn_in_fwd(name, u2, w_gu):
    rows, d = u2.shape
    hs = w_gu.shape[-1]
    return _mm_call(name, u2, w_gu, (N_DEV, 2), pl.BlockSpec((rows, d), lambda k, t: (0, 0)),
                    pl.BlockSpec((None, None, d, hs), lambda k, t: (k, t, 0, 0)),
                    pl.BlockSpec((None, None, rows, hs), lambda k, t: (k, t, 0, 0)),
                    jax.ShapeDtypeStruct((N_DEV, 2, rows, hs), BF16), 1, 0)


def ffn_out_fwd(name, act, w_down):
    _, rows, hs = act.shape
    d = w_down.shape[-1]
    tm, tn = _tile(rows, 1152, 16), _tile(d, 1024, LANES)
    half = N_DEV // 2
    return _mm_call(name, act.reshape(half, 2, rows, hs), w_down.reshape(half, 2, hs, d), (rows // tm, d // tn, half),
                    pl.BlockSpec((None, 2, tm, hs), lambda i, j, k: (k, 0, i, 0)),
                    pl.BlockSpec((None, 2, hs, tn), lambda i, j, k: (k, 0, 0, j)), pl.BlockSpec((tm, tn), lambda i, j, k: (i, j)),
                    jax.ShapeDtypeStruct((rows, d), F32), 1, 0, red_axis=2, pairs=2)


def ffn_out_bwd_x(name, df, w_down):
    rows, d = df.shape
    hs = w_down.shape[1]
    return _mm_call(name, df, w_down, (N_DEV,), pl.BlockSpec((rows, d), lambda k: (0, 0)),
                    pl.BlockSpec((None, hs, d), lambda k: (k, 0, 0)), pl.BlockSpec((None, rows, hs), lambda k: (k, 0, 0)),
                    jax.ShapeDtypeStruct((N_DEV, rows, hs), F32), 1, 1)


def ffn_out_bwd_w(name, act_t, df):
    _, hs, rows = act_t.shape
    d = df.shape[1]
    return _mm_call(name, act_t, df, (N_DEV,), pl.BlockSpec((None, hs, rows), lambda k: (k, 0, 0)),
                    pl.BlockSpec((rows, d), lambda k: (0, 0)), pl.BlockSpec((None, hs, d), lambda k: (k, 0, 0)),
                    jax.ShapeDtypeStruct((N_DEV, hs, d), BF16), 1, 0)


def ffn_in_bwd_w(name, u2_t, dgp):
    d, rows = u2_t.shape
    hs = dgp.shape[-1]
    return _mm_call(name, u2_t, dgp, (N_DEV, 2), pl.BlockSpec((d, rows), lambda k, t: (0, 0)),
                    pl.BlockSpec((None, None, rows, hs), lambda k, t: (k, t, 0, 0)),
                    pl.BlockSpec((None, None, d, hs), lambda k, t: (k, t, 0, 0)),
                    jax.ShapeDtypeStruct((N_DEV, 2, d, hs), BF16), 1, 0)


def ffn_in_bwd_x(name, dgp, w_gu):
    _, _, rows, hs = dgp.shape
    d = w_gu.shape[2]
    tm, tn = _tile(rows, 1152, 16), _tile(d, 1024, LANES)
    return _mm_call(name, dgp, w_gu, (rows // tm, d // tn, N_DEV),
                    pl.BlockSpec((None, 2, tm, hs), lambda i, j, k: (k, 0, i, 0)),
                    pl.BlockSpec((None, 2, tn, hs), lambda i, j, k: (k, 0, j, 0)),
                    pl.BlockSpec((tm, tn), lambda i, j, k: (i, j)), jax.ShapeDtypeStruct((rows, d), F32), 1, 1, red_axis=2, pairs=2)


def swiglu_fwd(name, gp):
    _, _, rows, hs = gp.shape
    tr = _tile(rows, 768, ROW_TILE)

    def body(gp_ref, a_ref, at_ref):
        g = gp_ref[0].astype(F32)
        act = g * _sigmoid(g) * gp_ref[1].astype(F32)
        a_ref[...] = act.astype(a_ref.dtype)
        at_ref[...] = act.T.astype(at_ref.dtype)

    return pl.pallas_call(
        body, name=name, grid=(N_DEV, rows // tr),
        in_specs=[pl.BlockSpec((None, 2, tr, hs), lambda k, i: (k, 0, i, 0))],
        out_specs=[pl.BlockSpec((None, tr, hs), lambda k, i: (k, i, 0)), pl.BlockSpec((None, hs, tr), lambda k, i: (k, 0, i))],
        out_shape=[jax.ShapeDtypeStruct((N_DEV, rows, hs), BF16), jax.ShapeDtypeStruct((N_DEV, hs, rows), BF16)],
        compiler_params=_params(("parallel", "parallel")),
    )(gp)


def swiglu_bwd(name, gp, dact):
    _, _, rows, hs = gp.shape
    tr = _tile(rows, 768, ROW_TILE)

    def body(gp_ref, da_ref, o_ref):
        g = gp_ref[0].astype(F32)
        s = _sigmoid(g)
        dav = da_ref[...]
        o_ref[0] = (dav * gp_ref[1].astype(F32) * (s * (1.0 + g * (1.0 - s)))).astype(o_ref.dtype)
        o_ref[1] = (dav * (g * s)).astype(o_ref.dtype)

    blk = pl.BlockSpec((None, 2, tr, hs), lambda k, i: (k, 0, i, 0))
    return pl.pallas_call(
        body, name=name, grid=(N_DEV, rows // tr),
        in_specs=[blk, pl.BlockSpec((None, tr, hs), lambda k, i: (k, i, 0))], out_specs=blk,
        out_shape=jax.ShapeDtypeStruct(gp.shape, BF16), compiler_params=_params(("parallel", "parallel")),
    )(gp, dact)


def conv_fwd(name, zx, conv_w, conv_b, d_inner, conv_dim):
    rows = zx.shape[0]
    tc = CONV_TILE
    off = d_inner // tc
    assert d_inner % tc == 0 and conv_dim % tc == 0

    def body(u_ref, w_ref, b_ref, o_ref):
        u = u_ref[...]
        acc = u * w_ref[D_CONV - 1:D_CONV, :] + b_ref[...]
        for s in range(1, D_CONV):
            acc = acc + pltpu.roll(u, s, axis=0) * w_ref[D_CONV - 1 - s:D_CONV - s, :]
        y = acc * _sigmoid(acc)
        o_ref[...] = jnp.where(_row_mask(y.shape, 0), y, 0.0)

    return pl.pallas_call(
        body, name=name, grid=(conv_dim // tc,),
        in_specs=[pl.BlockSpec((rows, tc), lambda j: (0, off + j)), pl.BlockSpec((D_CONV, tc), lambda j: (0, j)),
                  pl.BlockSpec((1, tc), lambda j: (0, j))],
        out_specs=pl.BlockSpec((rows, tc), lambda j: (0, j)),
        out_shape=jax.ShapeDtypeStruct((rows, conv_dim), F32), compiler_params=_params(("parallel",)),
    )(zx, conv_w, conv_b.reshape(1, conv_dim))


def conv_bwd(name, zx, conv_w, conv_b, dxbc, d_inner, conv_dim):
    rows = zx.shape[0]
    tc = CONV_TILE
    off = d_inner // tc

    def body(u_ref, w_ref, b_ref, dy_ref, du_ref, dw_ref, db_ref):
        u = u_ref[...]
        wk = [w_ref[D_CONV - 1 - s:D_CONV - s, :] for s in range(D_CONV)]
        shifted = [u] + [pltpu.roll(u, s, axis=0) for s in range(1, D_CONV)]
        acc = u * wk[0] + b_ref[...]
        for s in range(1, D_CONV):
            acc = acc + shifted[s] * wk[s]
        sg = _sigmoid(acc)
        mask = _row_mask(acc.shape, 0)
        dpre = jnp.where(mask, dy_ref[...] * (sg * (1.0 + acc * (1.0 - sg))), 0.0)
        db_ref[...] = jnp.sum(dpre, axis=0, keepdims=True)
        du = dpre * wk[0]
        dw_ref[D_CONV - 1:D_CONV, :] = jnp.sum(dpre * u, axis=0, keepdims=True)
        for s in range(1, D_CONV):
            du = du + pltpu.roll(dpre, rows - s, axis=0) * wk[s]
            dw_ref[D_CONV - 1 - s:D_CONV - s, :] = jnp.sum(dpre * shifted[s], axis=0, keepdims=True)
        du_ref[...] = jnp.where(mask, du, 0.0).astype(du_ref.dtype)

    return pl.pallas_call(
        body, name=name, grid=(conv_dim // tc,),
        in_specs=[pl.BlockSpec((rows, tc), lambda j: (0, off + j)), pl.BlockSpec((D_CONV, tc), lambda j: (0, j)),
                  pl.BlockSpec((1, tc), lambda j: (0, j)), pl.BlockSpec((rows, tc), lambda j: (0, j))],
        out_specs=[pl.BlockSpec((rows, tc), lambda j: (0, j)), pl.BlockSpec((D_CONV, tc), lambda j: (0, j)),
                   pl.BlockSpec((1, tc), lambda j: (0, j))],
        out_shape=[jax.ShapeDtypeStruct((rows, conv_dim), BF16), jax.ShapeDtypeStruct((D_CONV, conv_dim), F32),
                   jax.ShapeDtypeStruct((1, conv_dim), F32)],
        compiler_params=_params(("parallel",)),
    )(zx, conv_w, conv_b.reshape(1, conv_dim), dxbc)


def dt_fwd(name, zx, bias_pad, zx_cols):
    rows = zx.shape[0]
    tr = ROW_TILE
    off = zx_cols // LANES

    def body(r_ref, b_ref, o_ref):
        v = r_ref[...] + b_ref[...]
        sp = jnp.maximum(v, 0.0) + jnp.log1p(jnp.exp(-jnp.abs(v)))
        o_ref[...] = jnp.where(_row_mask(v.shape, pl.program_id(0) * tr), sp, 0.0)

    return pl.pallas_call(
        body, name=name, grid=(rows // tr,),
        in_specs=[pl.BlockSpec((tr, LANES), lambda i: (i, off)), pl.BlockSpec((1, LANES), lambda i: (0, 0))],
        out_specs=pl.BlockSpec((tr, LANES), lambda i: (i, 0)),
        out_shape=jax.ShapeDtypeStruct((rows, LANES), F32), compiler_params=_params(("parallel",)),
    )(zx, bias_pad)


def dt_bwd(name, zx, bias_pad, ddt, zx_cols):
    rows = zx.shape[0]
    tr = ROW_TILE
    off = zx_cols // LANES

    def body(r_ref, b_ref, d_ref, o_ref, db_ref):
        v = r_ref[...] + b_ref[...]
        g = jnp.where(_row_mask(v.shape, pl.program_id(0) * tr), d_ref[...] * _sigmoid(v), 0.0)
        o_ref[...] = g.astype(o_ref.dtype)

        @pl.when(pl.program_id(0) == 0)
        def _():
            db_ref[...] = jnp.zeros_like(db_ref)

        db_ref[...] += jnp.sum(g, axis=0, keepdims=True)

    return pl.pallas_call(
        body, name=name, grid=(rows // tr,),
        in_specs=[pl.BlockSpec((tr, LANES), lambda i: (i, off)), pl.BlockSpec((1, LANES), lambda i: (0, 0)),
                  pl.BlockSpec((tr, LANES), lambda i: (i, 0))],
        out_specs=[pl.BlockSpec((tr, LANES), lambda i: (i, 0)), pl.BlockSpec((1, LANES), lambda i: (0, 0))],
        out_shape=[jax.ShapeDtypeStruct((rows, LANES), BF16), jax.ShapeDtypeStruct((1, LANES), F32)],
        compiler_params=_params(("arbitrary",)),
    )(zx, bias_pad, ddt)


def _split3(x):
    h1 = x.astype(BF16)
    r1 = x - h1.astype(F32)
    h2 = r1.astype(BF16)
    h3 = (r1 - h2.astype(F32)).astype(BF16)
    return h1, h2, h3


def _exact_left(ones_b, x):
    h1, h2, h3 = _split3(x)
    return _dot(ones_b, h1, 1, 0) + _dot(ones_b, h2, 1, 0) + _dot(ones_b, h3, 1, 0)


def _exact_right_t(x, ones_b):
    h1, h2, h3 = _split3(x)
    return _dot(h1, ones_b, 1, 1) + _dot(h2, ones_b, 1, 1) + _dot(h3, ones_b, 1, 1)


class _ScanCommon:
    def __init__(self, b_ref, c_ref, dtc_ref, dtr_ref, arow_ref, acol_ref, drow_ref):
        q = CHUNK
        self.bb = b_ref[...].astype(BF16)
        self.cb = c_ref[...].astype(BF16)
        ri = lax.broadcasted_iota(jnp.int32, (q, q), 0)
        cj = lax.broadcasted_iota(jnp.int32, (q, q), 1)
        self.lower = ri >= cj
        self.upper = cj >= ri
        self.dtc = dtc_ref[...]
        self.dtr = dtr_ref[...]
        self.arow = arow_ref[...]
        self.acol = acol_ref[...]
        self.drow = drow_ref[...]
        da_col = self.dtc * self.arow
        self.a_col = _exact_left(self.lower.astype(BF16), da_col)
        self.a_row = _exact_right_t(self.dtr * self.acol, self.lower.astype(BF16))
        self.a_last = jnp.sum(da_col, axis=0, keepdims=True)
        self.lane_q = lax.broadcasted_iota(jnp.int32, (q, LANES), 1)
        self.lane_1 = lax.broadcasted_iota(jnp.int32, (1, LANES), 1)
        self.sub_8 = lax.broadcasted_iota(jnp.int32, (SUBLANES, q), 0)
        self.first_half = self.lane_q < HEAD_DIM
        self.first_rows = lax.broadcasted_iota(jnp.int32, (LANES, 1), 0) < HEAD_DIM

    def col(self, v, r):
        return jnp.sum(jnp.where(self.lane_q == r, v, 0.0), axis=1, keepdims=True)

    def row(self, v, r):
        return jnp.sum(jnp.where(self.sub_8 == r, v, 0.0), axis=0, keepdims=True)

    def scalar(self, v, r):
        return jnp.sum(jnp.where(self.lane_1 == r, v, 0.0), axis=1, keepdims=True)

    def pair(self, v0, v1):
        return jnp.where(self.first_half, v0, v1)

    def half_rowsum(self, t, h):
        keep = self.first_half if h == 0 else jnp.logical_not(self.first_half)
        return jnp.sum(jnp.where(keep, t, 0.0), axis=1, keepdims=True)


def ssd_scan_fwd(name, xbc, dt_col, dt_row, a_row, a_col, d_row, d_inner):
    rows = xbc.shape[0]
    q, n, g_cnt = CHUNK, D_STATE, SSD_GROUPS
    nc = rows // q
    rp = d_inner // g_cnt
    n_pairs = rp // LANES
    b_off = d_inner // n

    def body(x_ref, b_ref, c_ref, dtc_ref, dtr_ref, arow_ref, acol_ref, drow_ref, y_ref, sprev_ref, s_ref):
        @pl.when(pl.program_id(1) == 0)
        def _():
            s_ref[...] = jnp.zeros_like(s_ref)

        sprev_ref[...] = s_ref[...]
        k = _ScanCommon(b_ref, c_ref, dtc_ref, dtr_ref, arow_ref, acol_ref, drow_ref)
        cb_mat = _dot(k.cb, k.bb, 1, 1)
        for pr in range(n_pairs):
            sl = slice(pr * LANES, (pr + 1) * LANES)
            heads = (2 * pr, 2 * pr + 1)
            xp = x_ref[:, sl]
            ac = [k.col(k.a_col, r) for r in heads]
            ar = [k.row(k.a_row, r) for r in heads]
            al = [k.scalar(k.a_last, r) for r in heads]
            xd = xp * k.pair(k.col(k.dtc, heads[0]), k.col(k.dtc, heads[1]))
            xdb = xd.astype(BF16)
            ys = []
            for h in range(2):
                lm = jnp.exp(jnp.where(k.lower, ac[h] - ar[h], -jnp.inf))
                ys.append(_dot((cb_mat * lm).astype(BF16), xdb, 1, 0))
            y = jnp.where(k.first_half, ys[0], ys[1])
            sp = s_ref[sl, :]
            y = y + k.pair(jnp.exp(ac[0]), jnp.exp(ac[1])) * _dot(k.cb, sp.astype(BF16), 1, 1)
            y = y + k.pair(k.scalar(k.drow, heads[0]), k.scalar(k.drow, heads[1])) * xp
            y_ref[:, sl] = y
            wb = (xd * k.pair(jnp.exp(al[0] - ac[0]), jnp.exp(al[1] - ac[1]))).astype(BF16)
            decay = jnp.where(k.first_rows, jnp.exp(al[0]), jnp.exp(al[1]))
            s_ref[sl, :] = decay * sp + _dot(wb, k.bb, 0, 0)

    return pl.pallas_call(
        body, name=name, grid=(g_cnt, nc),
        in_specs=[
            pl.BlockSpec((q, rp), lambda g, c: (c, g)),
            pl.BlockSpec((q, n), lambda g, c: (c, b_off + g)),
            pl.BlockSpec((q, n), lambda g, c: (c, b_off + g_cnt + g)),
            pl.BlockSpec((None, q, LANES), lambda g, c: (g, c, 0)),
            pl.BlockSpec((None, SUBLANES, q), lambda g, c: (g, 0, c)),
            pl.BlockSpec((None, 1, LANES), lambda g, c: (g, 0, 0)),
            pl.BlockSpec((None, SUBLANES, 1), lambda g, c: (g, 0, 0)),
            pl.BlockSpec((None, 1, LANES), lambda g, c: (g, 0, 0)),
        ],
        out_specs=[pl.BlockSpec((q, rp), lambda g, c: (c, g)),
                   pl.BlockSpec((None, None, rp, n), lambda g, c: (c, g, 0, 0))],
        out_shape=[jax.ShapeDtypeStruct((rows, d_inner), F32), jax.ShapeDtypeStruct((nc, g_cnt, rp, n), F32)],
        scratch_shapes=[pltpu.VMEM((rp, n), F32)],
        compiler_params=_params(("parallel", "arbitrary")),
    )(xbc, xbc, xbc, dt_col, dt_row, a_row, a_col, d_row)


def ssd_scan_bwd(name, xbc, dt_col, dt_row, a_row, a_col, d_row, sprev, dy, d_inner):
    rows = xbc.shape[0]
    q, n, g_cnt = CHUNK, D_STATE, SSD_GROUPS
    nc = rows // q
    rp = d_inner // g_cnt
    n_pairs = rp // LANES
    b_off = d_inner // n

    def body(x_ref, b_ref, c_ref, dtc_ref, dtr_ref, arow_ref, acol_ref, drow_ref, sprev_ref, dy_ref,
             dx_ref, db_ref, dc_ref, ddtc_ref, ddtr_ref, dar_ref, dac_ref, dd_ref, ds_ref):
        @pl.when(pl.program_id(1) == 0)
        def _():
            ds_ref[...] = jnp.zeros_like(ds_ref)
            dar_ref[...] = jnp.zeros_like(dar_ref)
            dac_ref[...] = jnp.zeros_like(dac_ref)
            dd_ref[...] = jnp.zeros_like(dd_ref)

        k = _ScanCommon(b_ref, c_ref, dtc_ref, dtr_ref, arow_ref, acol_ref, drow_ref)
        cb_mat = _dot(k.cb, k.bb, 1, 1)
        cbt_mat = _dot(k.bb, k.cb, 1, 1)
        d_cb = jnp.zeros((q, q), F32)
        d_b = jnp.zeros((q, n), F32)
        d_c = jnp.zeros((q, n), F32)
        da_col = jnp.zeros((q, LANES), F32)
        da_row = jnp.zeros((SUBLANES, q), F32)
        ddt_x = jnp.zeros((q, LANES), F32)
        d_alast = jnp.zeros((1, LANES), F32)
        d_dskip = jnp.zeros((1, LANES), F32)
        for pr in range(n_pairs):
            sl = slice(pr * LANES, (pr + 1) * LANES)
            heads = (2 * pr, 2 * pr + 1)
            xp = x_ref[:, sl]
            dyp = dy_ref[:, sl]
            dyb = dyp.astype(BF16)
            ac = [k.col(k.a_col, r) for r in heads]
            ar = [k.row(k.a_row, r) for r in heads]
            al = [k.scalar(k.a_last, r) for r in heads]
            dt_p = k.pair(k.col(k.dtc, heads[0]), k.col(k.dtc, heads[1]))
            xd = xp * dt_p
            xdb = xd.astype(BF16)
            sp = sprev_ref[sl, :]
            spb = sp.astype(BF16)
            dsp = ds_ref[sl, :]
            dspb = dsp.astype(BF16)
            dskip_p = k.pair(k.scalar(k.drow, heads[0]), k.scalar(k.drow, heads[1]))
            dxp = dskip_p * dyp
            dd_lane = jnp.sum(dyp * xp, axis=0, keepdims=True)
            e_p = k.pair(jnp.exp(ac[0]), jnp.exp(ac[1]))
            t_off = dyp * (e_p * _dot(k.cb, spb, 1, 1))
            dzb = (e_p * dyp).astype(BF16)
            d_c = d_c + _dot(dzb, spb, 1, 0)
            ds_in = _dot(dzb, k.cb, 0, 0)
            decay = jnp.where(k.first_rows, jnp.exp(al[0]), jnp.exp(al[1]))
            ds_in = ds_in + decay * dsp
            t_state = jnp.sum(dsp * sp, axis=1, keepdims=True) * decay
            dec_p = k.pair(jnp.exp(al[0] - ac[0]), jnp.exp(al[1] - ac[1]))
            dw = _dot(k.bb, dspb, 1, 1)
            d_b = d_b + _dot((xd * dec_p).astype(BF16), dspb, 1, 0)
            dxd = dw * dec_p
            t_dec = dw * xd * dec_p
            dxd_h = []
            for h in range(2):
                r = heads[h]
                keep = k.first_half if h == 0 else jnp.logical_not(k.first_half)
                lm = jnp.exp(jnp.where(k.lower, ac[h] - ar[h], -jnp.inf))
                m_mat = cb_mat * lm
                dm = _dot(jnp.where(keep, dyp, 0.0).astype(BF16), xdb, 1, 1)
                dseg = dm * m_mat
                d_cb = d_cb + dm * lm
                lmt = jnp.exp(jnp.where(k.upper, ar[h] - ac[h], -jnp.inf))
                dxd_h.append(_dot((cbt_mat * lmt).astype(BF16), dyb, 1, 0))
                tdec_h = k.half_rowsum(t_dec, h)
                da_h = k.half_rowsum(t_off, h) - tdec_h + jnp.sum(dseg, axis=1, keepdims=True)
                da_col = da_col + jnp.where(k.lane_q == r, da_h, 0.0)
                da_row = da_row - jnp.where(k.sub_8 == r, jnp.sum(dseg, axis=0, keepdims=True), 0.0)
                keep_rows = k.first_rows if h == 0 else jnp.logical_not(k.first_rows)
                dal_h = jnp.sum(tdec_h, axis=0, keepdims=True) + jnp.sum(jnp.where(keep_rows, t_state, 0.0), axis=0, keepdims=True)
                d_alast = d_alast + jnp.where(k.lane_1 == r, dal_h, 0.0)
                keep_1 = k.lane_1 < HEAD_DIM if h == 0 else k.lane_1 >= HEAD_DIM
                dd_h = jnp.sum(jnp.where(keep_1, dd_lane, 0.0), axis=1, keepdims=True)
                d_dskip = d_dskip + jnp.where(k.lane_1 == r, dd_h, 0.0)
            dxd = dxd + jnp.where(k.first_half, dxd_h[0], dxd_h[1])
            dx_ref[:, sl] = dxp + dt_p * dxd
            t_dt = dxd * xp
            for h in range(2):
                ddt_x = ddt_x + jnp.where(k.lane_q == heads[h], k.half_rowsum(t_dt, h), 0.0)
            ds_ref[sl, :] = ds_in
        d_cb_b = d_cb.astype(BF16)
        db_ref[...] = d_b + _dot(d_cb_b, k.cb, 0, 0)
        dc_ref[...] = d_c + _dot(d_cb_b, k.bb, 1, 0)
        rc_col = _exact_left(k.upper.astype(BF16), da_col) + d_alast
        rc_row = _exact_right_t(da_row, k.upper.astype(BF16))
        ddtc_ref[...] = ddt_x + k.arow * rc_col
        ddtr_ref[...] = k.acol * rc_row
        dar_ref[...] += jnp.sum(rc_col * k.dtc, axis=0, keepdims=True)
        dac_ref[...] += jnp.sum(rc_row * k.dtr, axis=1, keepdims=True)
        dd_ref[...] += d_dskip

    rc = lambda c: nc - 1 - c
    return pl.pallas_call(
        body, name=name, grid=(g_cnt, nc),
        in_specs=[
            pl.BlockSpec((q, rp), lambda g, c: (rc(c), g)),
            pl.BlockSpec((q, n), lambda g, c: (rc(c), b_off + g)),
            pl.BlockSpec((q, n), lambda g, c: (rc(c), b_off + g_cnt + g)),
            pl.BlockSpec((None, q, LANES), lambda g, c: (g, rc(c), 0)),
            pl.BlockSpec((None, SUBLANES, q), lambda g, c: (g, 0, rc(c))),
            pl.BlockSpec((None, 1, LANES), lambda g, c: (g, 0, 0)),
            pl.BlockSpec((None, SUBLANES, 1), lambda g, c: (g, 0, 0)),
            pl.BlockSpec((None, 1, LANES), lambda g, c: (g, 0, 0)),
            pl.BlockSpec((None, None, rp, n), lambda g, c: (rc(c), g, 0, 0)),
            pl.BlockSpec((q, rp), lambda g, c: (rc(c), g)),
        ],
        out_specs=[
            pl.BlockSpec((q, rp), lambda g, c: (rc(c), g)),
            pl.BlockSpec((q, n), lambda g, c: (rc(c), g)),
            pl.BlockSpec((q, n), lambda g, c: (rc(c), g)),
            pl.BlockSpec((None, q, LANES), lambda g, c: (g, rc(c), 0)),
            pl.BlockSpec((None, SUBLANES, q), lambda g, c: (g, 0, rc(c))),
            pl.BlockSpec((None, 1, LANES), lambda g, c: (g, 0, 0)),
            pl.BlockSpec((None, SUBLANES, 1), lambda g, c: (g, 0, 0)),
            pl.BlockSpec((None, 1, LANES), lambda g, c: (g, 0, 0)),
        ],
        out_shape=[
            jax.ShapeDtypeStruct((rows, d_inner), F32),
            jax.ShapeDtypeStruct((rows, g_cnt * n), F32),
            jax.ShapeDtypeStruct((rows, g_cnt * n), F32),
            jax.ShapeDtypeStruct((g_cnt, rows, LANES), F32),
            jax.ShapeDtypeStruct((g_cnt, SUBLANES, rows), F32),
            jax.ShapeDtypeStruct((g_cnt, 1, LANES), F32),
            jax.ShapeDtypeStruct((g_cnt, SUBLANES, 1), F32),
            jax.ShapeDtypeStruct((g_cnt, 1, LANES), F32),
        ],
        scratch_shapes=[pltpu.VMEM((rp, n), F32)],
        compiler_params=_params(("parallel", "arbitrary")),
    )(xbc, xbc, xbc, dt_col, dt_row, a_row, a_col, d_row, sprev, dy)


def gatenorm_fwd(name, y, zx, w, d_inner):
    rows = y.shape[0]
    tr, gw = ROW_TILE, d_inner // SSD_GROUPS

    def body(y_ref, z_ref, w_ref, o_ref, ot_ref):
        z = z_ref[...]
        v = y_ref[...] * (z * _sigmoid(z))
        out = v * lax.rsqrt(jnp.mean(v * v, axis=-1, keepdims=True) + EPS) * w_ref[...]
        o_ref[...] = out.astype(o_ref.dtype)
        ot_ref[...] = out.T.astype(ot_ref.dtype)

    blk = pl.BlockSpec((tr, gw), lambda i, j: (i, j))
    return pl.pallas_call(
        body, name=name, grid=(rows // tr, SSD_GROUPS),
        in_specs=[blk, blk, pl.BlockSpec((1, gw), lambda i, j: (0, j))],
        out_specs=[blk, pl.BlockSpec((gw, tr), lambda i, j: (j, i))],
        out_shape=[jax.ShapeDtypeStruct((rows, d_inner), BF16), jax.ShapeDtypeStruct((d_inner, rows), BF16)],
        compiler_params=_params(("parallel", "parallel")),
    )(y, zx, w.reshape(1, d_inner))


def gatenorm_bwd(name, y, zx, w, dyn, d_inner, after):
    rows = y.shape[0]
    tr, gw = ROW_TILE, d_inner // SSD_GROUPS

    def body(y_ref, z_ref, w_ref, dn_ref, after_ref, dy_ref, dz_ref, dw_ref):
        z = z_ref[...]
        yv = y_ref[...]
        s = _sigmoid(z)
        gate = z * s
        v = yv * gate
        r = lax.rsqrt(jnp.mean(v * v, axis=-1, keepdims=True) + EPS)
        vh = v * r
        dn = dn_ref[...]
        dvh = dn * w_ref[...]
        dv = r * (dvh - vh * jnp.mean(dvh * vh, axis=-1, keepdims=True))
        dy_ref[...] = dv * gate
        dz_ref[...] = (dv * yv * (s * (1.0 + z * (1.0 - s)))).astype(dz_ref.dtype)

        @pl.when(pl.program_id(1) == 0)
        def _():
            dw_ref[...] = jnp.zeros_like(dw_ref)

        dw_ref[...] += jnp.sum(dn * vh, axis=0, keepdims=True)

    blk = pl.BlockSpec((tr, gw), lambda j, i: (i, j))
    wblk = pl.BlockSpec((1, gw), lambda j, i: (0, j))
    dy, dz, dw = pl.pallas_call(
        body, name=name, grid=(SSD_GROUPS, rows // tr),
        in_specs=[blk, blk, wblk, blk, pl.BlockSpec((SUBLANES, LANES), lambda j, i: (0, 0))], out_specs=[blk, blk, wblk],
        out_shape=[jax.ShapeDtypeStruct((rows, d_inner), F32), jax.ShapeDtypeStruct((rows, d_inner), BF16),
                   jax.ShapeDtypeStruct((1, d_inner), F32)],
        compiler_params=_params(("parallel", "arbitrary")),
    )(y, zx, w.reshape(1, d_inner), dyn, after)
    return dy, dz, dw.reshape(d_inner)


def _pool_count(rows, g):
    t1 = lax.broadcasted_iota(jnp.int32, (rows, 1), 0) - (PAD_FRONT - 1)
    win = jnp.left_shift(jnp.int32(POOL_WINDOWS[0]), g)
    return jnp.clip(t1, 1, win).astype(F32)


def _pool_select(levels, g):
    out = levels[-1]
    for i in range(len(levels) - 2, -1, -1):
        out = jnp.where(g == i, levels[i], out)
    return out


def pool_sub(name, u, transpose):
    rows, d = u.shape
    gd = d // len(POOL_WINDOWS)
    assert all(w == POOL_WINDOWS[0] << i for i, w in enumerate(POOL_WINDOWS))

    def body(u_ref, o_ref):
        g = pl.program_id(0)
        v = u_ref[...].astype(F32)
        cnt = _pool_count(rows, g)
        mask = _row_mask(v.shape, 0)
        s = v / cnt if transpose else v
        levels = []
        for i in range(len(POOL_WINDOWS)):
            step = 1 << i
            s = s + pltpu.roll(s, (rows - step) if transpose else step, axis=0)
            levels.append(s)
        sel = _pool_select(levels, g)
        out = (sel - v) if transpose else (sel / cnt - v)
        o_ref[...] = jnp.where(mask, out, 0.0).astype(o_ref.dtype)

    tc = POOL_TILE
    per = gd // tc
    blk = pl.BlockSpec((rows, tc), lambda g, j: (0, g * per + j))
    return pl.pallas_call(
        body, name=name, grid=(len(POOL_WINDOWS), per), in_specs=[blk], out_specs=blk,
        out_shape=jax.ShapeDtypeStruct((rows, d), F32 if transpose else BF16),
        compiler_params=_params(("parallel", "parallel")),
    )(u)


def pool_proj_fwd(name, mixed, w, b, scale):
    rows, d = mixed.shape
    ng = len(POOL_WINDOWS)
    gd = d // ng
    tr = ROW_TILE

    def body(m_ref, w_ref, b_ref, s_ref, pre_ref, mix_ref):
        pre = _dot(m_ref[...], w_ref[...], 1, 0) + b_ref[...]
        pre_ref[...] = pre
        mix_ref[...] = jnp.where(_row_mask(pre.shape, pl.program_id(1) * tr), pre * s_ref[...], 0.0)

    blk = pl.BlockSpec((tr, gd), lambda g, i: (i, g))
    vec = pl.BlockSpec((1, gd), lambda g, i: (0, g))
    return pl.pallas_call(
        body, name=name, grid=(ng, rows // tr),
        in_specs=[blk, pl.BlockSpec((None, gd, gd), lambda g, i: (g, 0, 0)), vec, vec], out_specs=[blk, blk],
        out_shape=[jax.ShapeDtypeStruct((rows, d), F32), jax.ShapeDtypeStruct((rows, d), F32)],
        compiler_params=_params(("parallel", "parallel")),
    )(mixed, w, b.reshape(1, d), scale.reshape(1, d))


def pool_proj_bwd(name, dmix, pre, mixed, w, scale):
    rows, d = dmix.shape
    ng = len(POOL_WINDOWS)
    gd = d // ng
    tr = ROW_TILE

    def body(dm_ref, pre_ref, mx_ref, w_ref, s_ref, dmx_ref, dw_ref, db_ref, ds_ref):
        @pl.when(pl.program_id(1) == 0)
        def _():
            dw_ref[...] = jnp.zeros_like(dw_ref)
            db_ref[...] = jnp.zeros_like(db_ref)
            ds_ref[...] = jnp.zeros_like(ds_ref)

        dmv = jnp.where(_row_mask(dm_ref.shape, pl.program_id(1) * tr), dm_ref[...], 0.0)
        dpre = dmv * s_ref[...]
        dpre_b = dpre.astype(BF16)
        ds_ref[...] += jnp.sum(dmv * pre_ref[...], axis=0, keepdims=True)
        db_ref[...] += jnp.sum(dpre, axis=0, keepdims=True)
        dmx_ref[...] = _dot(dpre_b, w_ref[...], 1, 1)
        dw_ref[...] += _dot(mx_ref[...], dpre_b, 0, 0)

    blk = pl.BlockSpec((tr, gd), lambda g, i: (i, g))
    vec = pl.BlockSpec((1, gd), lambda g, i: (0, g))
    wblk = pl.BlockSpec((None, gd, gd), lambda g, i: (g, 0, 0))
    dmixed, dw, db, ds = pl.pallas_call(
        body, name=name, grid=(ng, rows // tr),
        in_specs=[blk, blk, blk, wblk, vec], out_specs=[blk, wblk, vec, vec],
        out_shape=[jax.ShapeDtypeStruct((rows, d), F32), jax.ShapeDtypeStruct((ng, gd, gd), F32),
                   jax.ShapeDtypeStruct((1, d), F32), jax.ShapeDtypeStruct((1, d), F32)],
        compiler_params=_params(("parallel", "arbitrary")),
    )(dmix, pre, mixed, w, scale.reshape(1, d))
    return dmixed, dw, db.reshape(d), ds.reshape(d)


def _my_place():
    return lax.axis_index("x"), lax.axis_index("y"), lax.axis_index("c")


def _linear(place):
    return 4 * place[0] + 2 * place[1] + place[2]


def all_gather(name, shards, after):
    n_ops = len(shards)

    def body(*refs):
        ins, outs = refs[:n_ops], refs[n_ops + 1:2 * n_ops + 1]
        send_sems, recv_sems, local_sems = refs[2 * n_ops + 1:]
        x, y, c = _my_place()
        me, sibling = (x, y, c), (x, y, 1 - c)
        chips = [(1 - x, y), (x, 1 - y), (1 - x, 1 - y)]

        def copy(t, k, block, to, src=None):
            dst = outs[t].at[_linear(block)]
            return pltpu.make_async_remote_copy(
                src_ref=dst if src is None else src, dst_ref=dst, send_sem=send_sems.at[t, k], recv_sem=recv_sems.at[t, k],
                device_id=to, device_id_type=MESH)

        mine = [pltpu.make_async_copy(ins[t], outs[t].at[_linear(me)], local_sems.at[t]) for t in range(n_ops)]
        for cp in mine:
            cp.start()
        first = []
        for t in range(n_ops):
            first.append(copy(t, 0, me, sibling, src=ins[t]))
            first += [copy(t, 1 + j, me, (*chip, c), src=ins[t]) for j, chip in enumerate(chips)]
        for cp in first:
            cp.start()
        passed = []
        for j, chip in enumerate(chips):
            for t in range(n_ops):
                copy(t, 1 + j, (*chip, c), me).wait_recv()
                fwd = copy(t, 4 + j, (*chip, c), sibling)
                fwd.start()
                passed.append(fwd)
        for t in range(n_ops):
            copy(t, 0, sibling, me).wait_recv()
            for j, chip in enumerate(chips):
                copy(t, 4 + j, (*chip, 1 - c), me).wait_recv()
        for cp in first + passed:
            cp.wait_send()
        for cp in mine:
            cp.wait()

    any_spec = pl.BlockSpec(memory_space=pl.ANY)
    outs = pl.pallas_call(
        body, name=name,
        in_specs=[any_spec] * (n_ops + 1), out_specs=[any_spec] * n_ops,
        out_shape=[jax.ShapeDtypeStruct((N_DEV, *s.shape), s.dtype) for s in shards],
        scratch_shapes=[pltpu.SemaphoreType.DMA((n_ops, 7)), pltpu.SemaphoreType.DMA((n_ops, 7)),
                        pltpu.SemaphoreType.DMA((n_ops,))],
    )(*shards, after)
    return list(outs)


_HBM = pl.BlockSpec(memory_space=pltpu.HBM)
_SEM = pl.BlockSpec(memory_space=pltpu.SEMAPHORE)
_EFFECT = pltpu.SideEffectType.DATAFLOW_SIDE_EFFECTING


def _hbm(a):
    return pltpu.with_memory_space_constraint(a, pltpu.HBM)


def _peers():
    x, y, c = _my_place()
    return [(x ^ (j >> 2), y ^ ((j >> 1) & 1), c ^ (j & 1)) for j in range(1, N_DEV)]


def exchange_start(name, groups, after):
    flat = [e for g in groups for e in g]
    n = len(flat)

    def body(*refs):
        srcs, lands = refs[:n], refs[n:2 * n]
        outs = refs[2 * n + 1:]
        sends, recvs, token = outs[:n], outs[n:2 * n], outs[4 * n]
        me = _linear(_my_place())
        for t, (_, _, src_view, land_view) in enumerate(flat):
            for peer in _peers():
                pltpu.make_async_remote_copy(
                    src_ref=src_view(srcs[t], _linear(peer)), dst_ref=land_view(lands[t], me),
                    send_sem=sends[t], recv_sem=recvs[t], device_id=peer, device_id_type=MESH).start()
        token[...] = jnp.zeros_like(token)

    sem_shapes = [pltpu.SemaphoreType.DMA(())] * (2 * n)
    thru = [pltpu.HBM(e[0].shape, e[0].dtype) for e in flat] + [pltpu.HBM(e[1].shape, e[1].dtype) for e in flat]
    outs = pl.pallas_call(
        body, name=name,
        out_shape=(*sem_shapes, *thru, jax.ShapeDtypeStruct((SUBLANES, LANES), F32)),
        in_specs=[*[_HBM] * (2 * n), pl.BlockSpec(memory_space=pl.ANY)],
        out_specs=(*[_SEM] * (2 * n), *[_HBM] * (2 * n), pl.BlockSpec(memory_space=pltpu.VMEM)),
        input_output_aliases={t: 2 * n + t for t in range(2 * n)},
        compiler_params=pltpu.CompilerParams(has_side_effects=_EFFECT),
    )(*[_hbm(e[0]) for e in flat], *[_hbm(e[1]) for e in flat], after)
    records, t = [], 0
    for g in groups:
        k = len(g)
        records.append((list(outs[t:t + k]), list(outs[n + t:n + t + k]), list(outs[2 * n + t:2 * n + t + k]),
                        list(outs[3 * n + t:3 * n + t + k])))
        t += k
    return records, outs[-1]


def exchange_wait(name, records, lands, land_of, seven_of, after):
    srcs = [s for r in records for s in r[2]]
    sends = [s for r in records for s in r[0]]
    recvs = [s for r in records for s in r[1]]
    where = [(ri, k) for ri, r in enumerate(records) for k in range(len(r[2]))]
    ns, nl = len(srcs), len(lands)

    def body(*refs):
        src_refs, land_refs = refs[:ns], refs[ns:ns + nl]
        send_refs, recv_refs = refs[ns + nl:2 * ns + nl], refs[2 * ns + nl:3 * ns + nl]
        for t, (ri, k) in enumerate(where):
            seven_ref = seven_of[ri][k](src_refs[t], land_refs[land_of[ri][k]])
            cp = pltpu.make_async_remote_copy(src_ref=seven_ref, dst_ref=seven_ref, send_sem=send_refs[t], recv_sem=recv_refs[t],
                                              device_id=_my_place(), device_id_type=MESH)
            cp.wait_send()
            cp.wait_recv()

    outs = pl.pallas_call(
        body, name=name,
        out_shape=tuple(pltpu.HBM(a.shape, a.dtype) for a in (*srcs, *lands)),
        in_specs=[*[_HBM] * (ns + nl), *[_SEM] * (2 * ns), *[pl.BlockSpec(memory_space=pl.ANY)] * len(after)],
        out_specs=tuple([_HBM] * (ns + nl)),
        input_output_aliases={t: t for t in range(ns + nl)},
        compiler_params=pltpu.CompilerParams(has_side_effects=_EFFECT),
    )(*srcs, *lands, *sends, *recvs, *after)
    return list(outs[:ns]), list(outs[ns:])


def _other_chips():
    x, y, c = _my_place()
    return [(1 - x, y, c), (x, 1 - y, c), (1 - x, 1 - y, c)]


def gather2_start(name, groups, after):
    flat = [e for g in groups for e in g]
    n = len(flat)

    def body(*refs):
        srcs, lands = refs[:n], refs[n:2 * n]
        outs = refs[2 * n + 1:]
        d_send, d_recv, i_send, i_recv, token = outs[:n], outs[n:2 * n], outs[2 * n:3 * n], outs[3 * n:4 * n], outs[6 * n]
        x, y, c = _my_place()
        me = _linear((x, y, c))
        for t in range(n):
            pltpu.make_async_remote_copy(src_ref=srcs[t], dst_ref=lands[t].at[me], send_sem=d_send[t], recv_sem=d_recv[t],
                                         device_id=(x, y, 1 - c), device_id_type=MESH).start()
            for peer in _other_chips():
                pltpu.make_async_remote_copy(src_ref=srcs[t], dst_ref=lands[t].at[me], send_sem=i_send[t], recv_sem=i_recv[t],
                                             device_id=peer, device_id_type=MESH).start()
        token[...] = jnp.zeros_like(token)

    thru = [pltpu.HBM(e[0].shape, e[0].dtype) for e in flat] + [pltpu.HBM(e[1].shape, e[1].dtype) for e in flat]
    outs = pl.pallas_call(
        body, name=name,
        out_shape=(*[pltpu.SemaphoreType.DMA(())] * (4 * n), *thru, jax.ShapeDtypeStruct((SUBLANES, LANES), F32)),
        in_specs=[*[_HBM] * (2 * n), pl.BlockSpec(memory_space=pl.ANY)],
        out_specs=(*[_SEM] * (4 * n), *[_HBM] * (2 * n), pl.BlockSpec(memory_space=pltpu.VMEM)),
        input_output_aliases={t: 4 * n + t for t in range(2 * n)},
        compiler_params=pltpu.CompilerParams(has_side_effects=_EFFECT),
    )(*[_hbm(e[0]) for e in flat], *[_hbm(e[1]) for e in flat], after)
    records, t = [], 0
    for g in groups:
        k = len(g)
        records.append({"d_send": list(outs[t:t + k]), "d_recv": list(outs[n + t:n + t + k]),
                        "i_send": list(outs[2 * n + t:2 * n + t + k]), "i_recv": list(outs[3 * n + t:3 * n + t + k]),
                        "srcs": list(outs[4 * n + t:4 * n + t + k]), "lands": list(outs[5 * n + t:5 * n + t + k])})
        t += k
    return records, outs[-1]


def gather2_relay(name, rec, after):
    lands, i_recv = rec["lands"], rec["i_recv"]
    n = len(lands)

    def body(*refs):
        land_refs, i_recv_refs = refs[:n], refs[n:2 * n]
        outs = refs[2 * n + 1:]
        f_send, f_recv, token = outs[n:2 * n], outs[2 * n:3 * n], outs[3 * n]
        x, y, c = _my_place()
        for t in range(n):
            three = land_refs[t].at[pl.ds(0, 3)]
            pltpu.make_async_remote_copy(src_ref=three, dst_ref=three, send_sem=f_send[t], recv_sem=i_recv_refs[t],
                                         device_id=(x, y, c), device_id_type=MESH).wait_recv()
            for peer in _other_chips():
                blk = land_refs[t].at[_linear(peer)]
                pltpu.make_async_remote_copy(src_ref=blk, dst_ref=blk, send_sem=f_send[t], recv_sem=f_recv[t],
                                             device_id=(x, y, 1 - c), device_id_type=MESH).start()
        token[...] = jnp.zeros_like(token)

    outs = pl.pallas_call(
        body, name=name,
        out_shape=(*[pltpu.HBM(a.shape, a.dtype) for a in lands], *[pltpu.SemaphoreType.DMA(())] * (2 * n),
                   jax.ShapeDtypeStruct((SUBLANES, LANES), F32)),
        in_specs=[*[_HBM] * n, *[_SEM] * n, pl.BlockSpec(memory_space=pl.ANY)],
        out_specs=(*[_HBM] * n, *[_SEM] * (2 * n), pl.BlockSpec(memory_space=pltpu.VMEM)),
        input_output_aliases={t: t for t in range(n)},
        compiler_params=pltpu.CompilerParams(has_side_effects=_EFFECT),
    )(*lands, *i_recv, after)
    return {**rec, "lands": list(outs[:n]), "f_send": list(outs[n:2 * n]), "f_recv": list(outs[2 * n:3 * n])}, outs[-1]


def gather2_wait(name, rec, after):
    n = len(rec["lands"])
    sem_names = ("d_send", "d_recv", "i_send", "f_send", "f_recv")

    def body(*refs):
        src_refs, land_refs = refs[:n], refs[n:2 * n]
        sems = {nm: refs[2 * n + k * n:2 * n + (k + 1) * n] for k, nm in enumerate(sem_names)}
        me = _my_place()
        for t in range(n):
            one, three = land_refs[t].at[0], land_refs[t].at[pl.ds(0, 3)]
            to_sibling = pltpu.make_async_remote_copy(src_ref=src_refs[t], dst_ref=one, send_sem=sems["d_send"][t],
                                                      recv_sem=sems["d_recv"][t], device_id=me, device_id_type=MESH)
            to_sibling.wait_send()
            to_sibling.wait_recv()
            pltpu.make_async_remote_copy(src_ref=three, dst_ref=three, send_sem=sems["i_send"][t], recv_sem=sems["d_recv"][t],
                                         device_id=me, device_id_type=MESH).wait_send()
            forwarded = pltpu.make_async_remote_copy(src_ref=three, dst_ref=three, send_sem=sems["f_send"][t],
                                                     recv_sem=sems["f_recv"][t], device_id=me, device_id_type=MESH)
            forwarded.wait_send()
            forwarded.wait_recv()

    arrays = (*rec["srcs"], *rec["lands"])
    outs = pl.pallas_call(
        body, name=name,
        out_shape=tuple(pltpu.HBM(a.shape, a.dtype) for a in arrays),
        in_specs=[*[_HBM] * (2 * n), *[_SEM] * (5 * n), *[pl.BlockSpec(memory_space=pl.ANY)] * 2], out_specs=tuple([_HBM] * (2 * n)),
        input_output_aliases={t: t for t in range(2 * n)},
        compiler_params=pltpu.CompilerParams(has_side_effects=_EFFECT),
    )(*arrays, *[s for nm in sem_names for s in rec[nm]], *after)
    return list(outs[:n]), list(outs[n:])


def _seven_slots_of_land(src_ref, land_ref):
    return land_ref.at[pl.ds(0, N_DEV - 1)]


def _seven_blocks_of_src(src_ref, land_ref):
    return src_ref.at[pl.ds(0, N_DEV - 1)]


def _whole(ref, dev):
    return ref


def _slot(ref, dev):
    return ref.at[dev]


def _slot_of_layer(layer):
    return lambda ref, dev: ref.at[dev, layer]


def _as_rows(a, lead=0):
    return a.reshape(a.shape[:lead] + (-1, a.shape[-1]))


def sum_adamw(name, parts, w, m, v, half=None, column_major=False):
    shape = w.shape
    c1 = 1.0 - ADAM_B1 ** ADAM_STEP
    c2 = 1.0 - ADAM_B2 ** ADAM_STEP
    if column_major:
        layers, r, cols = shape
        tr = _tile(r, ROW_TILE, LANES)
        grid = (layers, r // tr)
        if half is None:
            p_spec = pl.BlockSpec((N_DEV, None, tr, cols), lambda l, i: (0, l, i, 0))
        else:
            p_spec = pl.BlockSpec((N_DEV, None, None, tr, cols), lambda l, i: (0, l, half, i, 0))
        blk = pl.BlockSpec((None, cols, tr), lambda l, i: (l, 0, i))
        args = (parts, *[jnp.swapaxes(a, 1, 2) for a in (w, m, v)])
        out_sds, sem = jax.ShapeDtypeStruct((layers, cols, r), F32), ("parallel", "parallel")
    elif half is not None:
        layers, r, cols = shape
        tr = _tile(r, ROW_TILE, 16)
        grid = (layers, r // tr)
        p_spec = pl.BlockSpec((N_DEV, None, None, tr, cols), lambda l, i: (0, l, half, i, 0))
        blk = pl.BlockSpec((None, tr, cols), lambda l, i: (l, i, 0))
        args, out_sds, sem = (parts, w, m, v), jax.ShapeDtypeStruct(shape, F32), ("parallel", "parallel")
    else:
        p2, w2, m2, v2 = _as_rows(parts, 1), _as_rows(w), _as_rows(m), _as_rows(v)
        rows, cols = w2.shape
        tr = _tile(rows, ROW_TILE, 16)
        grid = (rows // tr,)
        p_spec = pl.BlockSpec((N_DEV, tr, cols), lambda i: (0, i, 0))
        blk = pl.BlockSpec((tr, cols), lambda i: (i, 0))
        args, out_sds, sem = (p2, w2, m2, v2), jax.ShapeDtypeStruct((rows, cols), F32), ("parallel",)

    def body(p_ref, w_ref, m_ref, v_ref, g_ref, d_ref, nm_ref, nv_ref):
        g = p_ref[0].astype(F32)
        for k in range(1, N_DEV):
            g = g + p_ref[k].astype(F32)
        if column_major:
            g = g.T
        wv = w_ref[...]
        nm = ADAM_B1 * m_ref[...] + (1.0 - ADAM_B1) * g
        nv = ADAM_B2 * v_ref[...] + (1.0 - ADAM_B2) * (g * g)
        g_ref[...] = g
        nm_ref[...] = nm
        nv_ref[...] = nv
        d_ref[...] = -ADAM_LR * ((nm / c1) / (jnp.sqrt(nv / c2) + ADAM_EPS) + ADAM_WD * wv)

    outs = pl.pallas_call(
        body, name=name, grid=grid, in_specs=[p_spec, blk, blk, blk], out_specs=[blk] * 4,
        out_shape=[out_sds] * 4, compiler_params=_params(sem),
    )(*args)
    if column_major:
        return [jnp.swapaxes(o, 1, 2) for o in outs]
    return [o.reshape(shape) for o in outs]


def _unblock_cols(g):
    g = jnp.moveaxis(g, 0, -2)
    return g.reshape(g.shape[:-2] + (g.shape[-2] * g.shape[-1],))


def _block_cols(a):
    r, c = a.shape
    return jnp.moveaxis(a.reshape(r, N_DEV, c // N_DEV), 1, 0)


def _my_cols(a, n):
    me = _linear(_my_place())
    return lax.dynamic_slice_in_dim(a, me * n, n, axis=a.ndim - 1)


def _pack(arrays):
    flat = jnp.concatenate([a.reshape(-1).astype(F32) for a in arrays])
    pad = (-flat.shape[0]) % (ROW_TILE * LANES)
    return jnp.pad(flat, (0, pad)).reshape(-1, LANES)


def _unpack(packed, like):
    flat = packed.reshape(-1)
    out, pos = [], 0
    for a in like:
        out.append(flat[pos:pos + a.size].reshape(a.shape))
        pos += a.size
    return out


def kernel(x, meta_tokens, norm_w, ssd_w_in, ssd_conv_w, ssd_conv_b, ssd_dt_bias, ssd_a_log, ssd_d, ssd_norm_w, ssd_w_out, pool_w, pool_b, pool_scale, ffn_w_gate, ffn_w_up, ffn_w_down, loss_target, m_meta_tokens, m_norm_w, m_ssd_w_in, m_ssd_conv_w, m_ssd_conv_b, m_ssd_dt_bias, m_ssd_a_log, m_ssd_d, m_ssd_norm_w, m_ssd_w_out, m_pool_w, m_pool_b, m_pool_scale, m_ffn_w_gate, m_ffn_w_up, m_ffn_w_down, v_meta_tokens, v_norm_w, v_ssd_w_in, v_ssd_conv_w, v_ssd_conv_b, v_ssd_dt_bias, v_ssd_a_log, v_ssd_d, v_ssd_norm_w, v_ssd_w_out, v_pool_w, v_pool_b, v_pool_scale, v_ffn_w_gate, v_ffn_w_up, v_ffn_w_down):
    seq, d = x.shape[1], x.shape[2]
    depth = norm_w.shape[0]
    n_ssd = ssd_w_in.shape[0]
    d_inner = ssd_norm_w.shape[1]
    heads = d_inner // HEAD_DIM
    rpg = heads // SSD_GROUPS
    conv_dim = ssd_conv_b.shape[1]
    zx_cols = d_inner + conv_dim
    d_in_proj = zx_cols + heads
    rows = PAD_FRONT + N_META + seq
    assert rows % CHUNK == 0 and rpg % 2 == 0 and heads <= LANES and rpg <= SUBLANES

    hidden = ffn_w_down.shape[1] * N_DEV
    n_pool = pool_w.shape[0]

    gather_recs = {}

    def start_gather(i, after, groups_of_layer=slice(None), tag=""):
        def entry(shard):
            return lambda: (shard.astype(BF16), lax.empty((N_DEV, *shard.shape), BF16))
        mixer = [ssd_w_in[i // 2], ssd_w_out[i // 2]] if i % 2 == 0 else [pool_w[i // 2]]
        groups = [*[[entry(w)] for w in mixer], [entry(jnp.stack([ffn_w_gate[i], ffn_w_up[i]])), entry(ffn_w_down[i])]]
        groups = [[make() for make in g] for g in groups[groups_of_layer]]
        recs, tok = gather2_start(f"gather_start_{i}{tag}", groups, after)
        gather_recs[i] = gather_recs.get(i, []) + recs
        return tok

    first = start_gather(0, meta_tokens, slice(0, 1), "_first")
    small = all_gather("gather_small", [meta_tokens, norm_w, ssd_conv_w, pool_b, pool_scale], first)
    meta_f, norm_f, convw_f, poolb_f, pools_f = [_unblock_cols(s) for s in small]

    gather_order = [(i, g) for i in range(depth) for g in range(3 if i % 2 == 0 else 2)]
    relayed = [0, None]

    def gathered(i, g, after):
        idx = gather_order.index((i, g))
        upto = min(idx + (1 if idx == 0 else 2), len(gather_order))
        while relayed[0] < upto:
            li, lg = gather_order[relayed[0]]
            gather_recs[li][lg], relayed[1] = gather2_relay(f"gather_relay_{li}_{lg}", gather_recs[li][lg], after)
            relayed[0] += 1
        srcs, lands_ = gather2_wait(f"gather_wait_{i}_{g}", gather_recs[i][g], (after, relayed[1]))
        me = _linear(_my_place())
        return [lax.dynamic_update_slice_in_dim(land, src[None], me, axis=0) for src, land in zip(srcs, lands_)]

    def gathered_small(name, rec, after):
        srcs, lands_ = exchange_wait(name, [rec], rec[3], [[0]], [[_seven_slots_of_land]], after)
        return lax.dynamic_update_slice_in_dim(lands_[0], srcs[0][None], _linear(_my_place()), axis=0)

    pending = start_gather(0, meta_f, slice(1, None))
    for i in range(1, min(GATHER_AHEAD, depth)):
        pending = start_gather(i, pending)

    pad_h = lambda a: jnp.pad(a.astype(F32), ((0, 0), (0, LANES - heads)))
    bias_pad = pad_h(ssd_dt_bias)

    def head_layouts(vec):
        g = vec.reshape(SSD_GROUPS, rpg)
        row = jnp.pad(g, ((0, 0), (0, LANES - rpg)))[:, None, :]
        col = jnp.pad(g, ((0, 0), (0, SUBLANES - rpg)))[:, :, None]
        return row, col

    def dt_layouts(dt):
        g = dt[:, :heads].reshape(rows, SSD_GROUPS, rpg)
        col = jnp.pad(jnp.moveaxis(g, 1, 0), ((0, 0), (0, 0), (0, LANES - rpg)))
        row = jnp.pad(jnp.transpose(g, (1, 2, 0)), ((0, 0), (0, SUBLANES - rpg), (0, 0)))
        return col, row

    h = jnp.concatenate([jnp.zeros((PAD_FRONT, d), F32), meta_f, x[0]], axis=0)
    saved = []
    for i in range(depth):
        j = i // 2
        s = {"h": h}
        w_pre_mix = norm_f[i, 0]
        if i + GATHER_AHEAD < depth:
            pending = start_gather(i + GATHER_AHEAD, pending if i == 0 else h)
        (mixer_w,) = gathered(i, 0, h)
        if i % 2 == 0:
            w_in = jnp.pad(_unblock_cols(mixer_w), ((0, 0), (0, LANES - heads)))
            u, u_t = rmsnorm_fwd(f"norm_pre_mix_{i}", h, w_pre_mix, out_dtype=BF16, transposed=True, after=pending)
            zx = matmul(f"ssd_in_{i}", u, w_in, **MM_ROWS_RESIDENT, tn=384 if w_in.shape[1] % 384 == 0 else 512)
            xbc = conv_fwd(f"ssd_conv_{i}", zx, convw_f[j], ssd_conv_b[j], d_inner, conv_dim)
            dt = dt_fwd(f"ssd_dt_{i}", zx, bias_pad[j:j + 1], zx_cols)
            dt_col, dt_row = dt_layouts(dt)
            a_neg = -jnp.exp(ssd_a_log[j].astype(F32))
            a_row, a_col = head_layouts(a_neg)
            d_row, _ = head_layouts(ssd_d[j].astype(F32))
            y, sprev = ssd_scan_fwd(f"ssd_scan_{i}", xbc, dt_col, dt_row, a_row, a_col, d_row, d_inner)
            yn, yn_t = gatenorm_fwd(f"ssd_gate_{i}", y, zx, ssd_norm_w[j], d_inner)
            (w_out,) = gathered(i, 1, yn)
            w_out = w_out.reshape(d_inner, d)
            mix = matmul(f"ssd_out_{i}", yn, w_out, **MM_DEEP)
            s.update(u_t=u_t, zx=zx, xbc=xbc, dt_col=dt_col, dt_row=dt_row, a_row=a_row, a_col=a_col, d_row=d_row,
                     a_neg=a_neg, y=y, sprev=sprev, yn_t=yn_t, w_in=w_in, w_out=w_out)
        else:
            w_pool = jnp.moveaxis(mixer_w, 0, 1).reshape(len(POOL_WINDOWS), d // len(POOL_WINDOWS), -1)
            u = rmsnorm_fwd(f"norm_pre_mix_{i}", h, w_pre_mix, after=pending)
            mixed = pool_sub(f"pool_sub_{i}", u, False)
            pre, mix = pool_proj_fwd(f"pool_proj_{i}", mixed, w_pool, poolb_f[j], pools_f[j])
            s.update(mixed=mixed, pre=pre, w_pool=w_pool)
        pending = None
        h1 = rmsnorm_fwd(f"norm_post_mix_{i}", mix, norm_f[i, 1], res=h)
        w_gu, w_down = gathered(i, 2 if i % 2 == 0 else 1, h1)
        u2, u2_t = rmsnorm_fwd(f"norm_pre_ffn_{i}", h1, norm_f[i, 2], out_dtype=BF16, transposed=True)
        gp = ffn_in_fwd(f"ffn_in_{i}", u2, w_gu)
        act, act_t = swiglu_fwd(f"ffn_act_{i}", gp)
        f = ffn_out_fwd(f"ffn_out_{i}", act, w_down)
        h = rmsnorm_fwd(f"norm_post_ffn_{i}", f, norm_f[i, 3], res=h1)
        s.update(mix=mix, h1=h1, u2_t=u2_t, gp=gp, act_t=act_t, f=f, w_gu=w_gu, w_down=w_down)
        saved.append(s)

    dh, loss_local = loss_head("loss_head", h, loss_target[0])
    loss = lax.psum(loss_local, AXES)

    g_norm = [[None] * 4 for _ in range(depth)]
    g_convw, g_convb, g_dtb, g_alog, g_dskip, g_ssdnorm = ([None] * n_ssd for _ in range(6))
    g_poolb, g_pools = ([None] * n_pool for _ in range(2))
    hid_s = hidden // N_DEV
    lands = {"in": lax.empty((N_DEV, *ssd_w_in.shape), BF16), "out": lax.empty((N_DEV, *ssd_w_out.shape), BF16),
             "pool": lax.empty((N_DEV, *pool_w.shape), BF16), "down": lax.empty((N_DEV, *ffn_w_down.shape), BF16),
             "gate_up": lax.empty((N_DEV, depth, 2, d, hid_s), BF16)}
    scatter_recs, scatter_keys, scatter_views, scatter_layers = [], [], [], []

    def scatter(name, blocks, layer, after):
        view = _slot_of_layer(layer)
        (rec,), tok = exchange_start(name, [[(b, lands[key], _slot, view) for key, b in blocks]], after)
        for (key, _), thru in zip(blocks, rec[3]):
            lands[key] = thru
        scatter_recs.append(rec)
        scatter_keys.append([key for key, _ in blocks])
        scatter_views.append([_seven_blocks_of_src] * len(blocks))
        scatter_layers.append(layer)
        return tok

    tok = None
    for i in reversed(range(depth)):
        j = i // 2
        s = saved[i]
        df, g_norm[i][3] = rmsnorm_bwd(f"norm_post_ffn_bwd_{i}", s["f"], norm_f[i, 3], dh, after=tok)
        df_b = df.astype(BF16)
        dact = ffn_out_bwd_x(f"ffn_out_bwd_x_{i}", df_b, s["w_down"])
        g_down = ffn_out_bwd_w(f"ffn_out_bwd_w_{i}", s["act_t"], df_b)
        dgp = swiglu_bwd(f"ffn_act_bwd_{i}", s["gp"], dact)
        g_gu = ffn_in_bwd_w(f"ffn_in_bwd_w_{i}", s["u2_t"], dgp)
        tok = scatter(f"scatter_start_ffn_{i}", [("gate_up", g_gu), ("down", g_down)], i, g_gu)
        du2 = ffn_in_bwd_x(f"ffn_in_bwd_x_{i}", dgp, s["w_gu"])
        dh1, g_norm[i][2] = rmsnorm_bwd(f"norm_pre_ffn_bwd_{i}", s["h1"], norm_f[i, 2], du2, add=dh, after=tok)
        dmix, g_norm[i][1] = rmsnorm_bwd(f"norm_post_mix_bwd_{i}", s["mix"], norm_f[i, 1], dh1)
        if i % 2 == 0:
            dmix_b = dmix.astype(BF16)
            dyn = matmul(f"ssd_out_bwd_x_{i}", dmix_b, s["w_out"], tb=True, **MM_ROWS_RESIDENT)
            g_out = matmul(f"ssd_out_bwd_w_{i}", s["yn_t"], dmix_b, out_dtype=BF16, **MM_COLS_RESIDENT)
            tok = scatter(f"scatter_start_out_{i}", [("out", g_out.reshape(N_DEV, d_inner // N_DEV, d))], j, g_out)
            dy, dz, g_ssdnorm[j] = gatenorm_bwd(f"ssd_gate_bwd_{i}", s["y"], s["zx"], ssd_norm_w[j], dyn, d_inner, after=tok)
            dx, db, dc, ddt_col, ddt_row, dar, dac, ddsk = ssd_scan_bwd(
                f"ssd_scan_bwd_{i}", s["xbc"], s["dt_col"], s["dt_row"], s["a_row"], s["a_col"], s["d_row"], s["sprev"], dy, d_inner)
            ddt = (jnp.moveaxis(ddt_col[:, :, :rpg], 0, 1).reshape(rows, heads)
                   + jnp.transpose(ddt_row[:, :rpg, :], (2, 0, 1)).reshape(rows, heads))
            d_a = (dar[:, 0, :rpg] + dac[:, :rpg, 0]).reshape(heads)
            g_alog[j] = d_a * s["a_neg"]
            g_dskip[j] = ddsk[:, 0, :rpg].reshape(heads)
            ddtr, dbias = dt_bwd(f"ssd_dt_bwd_{i}", s["zx"], bias_pad[j:j + 1], pad_h(ddt), zx_cols)
            g_dtb[j] = dbias[0, :heads]
            dxbc_raw, g_convw[j], dconvb = conv_bwd(
                f"ssd_conv_bwd_{i}", s["zx"], convw_f[j], ssd_conv_b[j], jnp.concatenate([dx, db, dc], axis=1), d_inner, conv_dim)
            g_convb[j] = dconvb[0]
            dzx = jnp.concatenate([dz, dxbc_raw, ddtr], axis=1)
            g_in = matmul(f"ssd_in_bwd_w_{i}", s["u_t"], dzx, out_dtype=BF16, **MM_ROWS_RESIDENT,
                          tn=384 if dzx.shape[1] % 384 == 0 else 512)
            tok = scatter(f"scatter_start_in_{i}", [("in", _block_cols(g_in[:, :d_in_proj]))], j, g_in)
            du = matmul(f"ssd_in_bwd_x_{i}", dzx, s["w_in"], tb=True, **{**MM_DEEP, "tk": 1152 if dzx.shape[1] % 1152 == 0 else 512})
        else:
            dmixed, g_poolw, g_poolb[j], g_pools[j] = pool_proj_bwd(
                f"pool_proj_bwd_{i}", dmix, s["pre"], s["mixed"], s["w_pool"], pools_f[j])
            ng, gd = g_poolw.shape[0], g_poolw.shape[1]
            blk_pool = jnp.moveaxis(g_poolw.astype(BF16).reshape(ng, N_DEV, gd // N_DEV, gd), 1, 0)
            tok = scatter(f"scatter_start_pool_{i}", [("pool", blk_pool)], j, blk_pool)
            du = pool_sub(f"pool_sub_bwd_{i}", dmixed, True)
        dh, g_norm[i][0] = rmsnorm_bwd(f"norm_pre_mix_bwd_{i}", s["h"], norm_f[i, 0], du, add=dh1, after=tok)

    grad_x = dh[PAD_FRONT + N_META:][None]
    g_meta = dh[PAD_FRONT:PAD_FRONT + N_META]

    small_grads = [g_meta, jnp.stack([jnp.stack(r) for r in g_norm]), jnp.stack(g_convw), jnp.stack(g_convb), jnp.stack(g_dtb),
                   jnp.stack(g_alog), jnp.stack(g_dskip), jnp.stack(g_ssdnorm), jnp.stack(g_poolb), jnp.stack(g_pools)]
    packed_g = _pack(small_grads)
    (small_rec,), small_tok = exchange_start(
        "gather_small_grads_start", [[(packed_g, lax.empty((N_DEV, *packed_g.shape), F32), _whole, _slot)]], packed_g)
    mine = lambda a: _my_cols(a, a.shape[-1] // N_DEV)
    small_w = [meta_tokens, norm_w, ssd_conv_w, ssd_conv_b, ssd_dt_bias, ssd_a_log, ssd_d, ssd_norm_w, pool_b, pool_scale]
    small_m = [m_meta_tokens, m_norm_w, m_ssd_conv_w, m_ssd_conv_b, m_ssd_dt_bias, m_ssd_a_log, m_ssd_d, m_ssd_norm_w, m_pool_b, m_pool_scale]
    small_v = [v_meta_tokens, v_norm_w, v_ssd_conv_w, v_ssd_conv_b, v_ssd_dt_bias, v_ssd_a_log, v_ssd_d, v_ssd_norm_w, v_pool_b, v_pool_scale]
    sharded = [True, True, True, False, False, False, False, False, True, True]
    def widen(a, is_sharded, full):
        if not is_sharded:
            return a
        return lax.dynamic_update_slice_in_dim(jnp.zeros(full.shape, F32), a, _linear(_my_place()) * a.shape[-1], axis=a.ndim - 1)
    packed_w = _pack([widen(a, sh, g) for a, sh, g in zip(small_w, sharded, small_grads)])
    packed_m = _pack([widen(a, sh, g) for a, sh, g in zip(small_m, sharded, small_grads)])
    packed_v = _pack([widen(a, sh, g) for a, sh, g in zip(small_v, sharded, small_grads)])

    def landed(name, keys, after):
        rs = [r for r, ks in enumerate(scatter_keys) if set(ks) <= set(keys)]
        srcs, out = exchange_wait(name, [scatter_recs[r] for r in rs], [lands[k] for k in keys],
                                  [[keys.index(k) for k in scatter_keys[r]] for r in rs], [scatter_views[r] for r in rs], after)
        me = _linear(_my_place())
        src_of = iter(srcs)
        for r in rs:
            for key in scatter_keys[r]:
                own = lax.dynamic_index_in_dim(next(src_of), me, axis=0, keepdims=True)[:, None]
                slot = keys.index(key)
                start = (me, scatter_layers[r]) + (0,) * (own.ndim - 2)
                out[slot] = lax.dynamic_update_slice(out[slot], own, start)
        return out

    p_gu, p_down = landed("scatter_wait_ffn", ["gate_up", "down"], (small_tok,))
    a_gate = sum_adamw("adamw_ffn_w_gate", p_gu, ffn_w_gate, m_ffn_w_gate, v_ffn_w_gate, half=0, column_major=True)
    a_up = sum_adamw("adamw_ffn_w_up", p_gu, ffn_w_up, m_ffn_w_up, v_ffn_w_up, half=1, column_major=True)
    a_down = sum_adamw("adamw_ffn_w_down", p_down, ffn_w_down, m_ffn_w_down, v_ffn_w_down)
    (p_pool,) = landed("scatter_wait_pool", ["pool"], (a_gate[0],))
    a_pool = sum_adamw("adamw_pool_w", p_pool, pool_w, m_pool_w, v_pool_w)
    (p_out,) = landed("scatter_wait_out", ["out"], (a_up[0], a_pool[0]))
    a_out = sum_adamw("adamw_ssd_w_out", p_out, ssd_w_out, m_ssd_w_out, v_ssd_w_out)
    (p_in,) = landed("scatter_wait_in", ["in"], (a_gate[0], a_up[0], a_down[0], a_pool[0], a_out[0]))
    a_in = sum_adamw("adamw_ssd_w_in", p_in, ssd_w_in, m_ssd_w_in, v_ssd_w_in, column_major=True)
    bg = [a_in, a_out, a_pool, a_gate, a_up, a_down]

    small_parts = gathered_small("gather_small_grads_wait", small_rec, (a_in[0],))
    sm = [_unpack(o, small_grads) for o in sum_adamw("adamw_small", small_parts, packed_w, packed_m, packed_v)]
    sm = [[mine(a) if sh else a for a, sh in zip(group, sharded)] for group in sm]

    def ordered(kind):
        s_ = sm[kind]
        b_ = [o[kind] for o in bg]
        return [s_[0], s_[1], b_[0], s_[2], s_[3], s_[4], s_[5], s_[6], s_[7], b_[1], b_[2], s_[8], s_[9], b_[3], b_[4], b_[5]]

    return (loss, grad_x, *ordered(0), *ordered(1), *ordered(2), *ordered(3))
```
